```python
import jax, jax.numpy as jnp
from jax import lax
import numpy as np

D_MODEL = 1024
BATCH = 8
SEQ = 2048
DEPTH = 2
DEC_BATCH = 128
DEC_SEQ = 4
PAST_LEN = 16384
PAGE_SIZE = 128

D_MIX = D_MODEL
D_DELTA = D_MIX // 2
N_DHEADS = 4
HEAD_DIM = D_DELTA // N_DHEADS
QK_CONV = 4
CHUNK = 64
D_SCONV = D_MIX // 4
SCONV_W = 3
D_CONF = D_MIX - D_DELTA - D_SCONV
CONF_W = 31
EPS = 1e-6
SPLIT_WIDTHS = (3 * D_DELTA, N_DHEADS, N_DHEADS, D_DELTA,
                D_SCONV, D_SCONV, D_SCONV, D_SCONV,
                2 * D_CONF, D_CONF)
SPLIT_POINTS = tuple(int(s) for s in np.cumsum(SPLIT_WIDTHS)[:-1])
D_IN = int(sum(SPLIT_WIDTHS))

kernel_name = "hymba_delta_shortconv_conformer_step"


def rmsnorm(x, g):
    xf = x.astype(jnp.float32)
    y = xf * lax.rsqrt(jnp.mean(xf * xf, axis=-1, keepdims=True) + EPS)
    return (y * g.astype(jnp.float32)).astype(x.dtype)


def l2norm(x):
    return x * lax.rsqrt(jnp.sum(x * x, axis=-1, keepdims=True) + EPS)


def causal_dwconv(x, buf, w):
    xp = jnp.concatenate([buf.astype(x.dtype), x], axis=1)
    y = lax.conv_general_dilated(xp, w.astype(x.dtype)[:, None, :], (1,), 'VALID',
                                 dimension_numbers=('NWC', 'WIO', 'NWC'),
                                 feature_group_count=x.shape[-1])
    return y, xp[:, -(w.shape[0] - 1):]


def _to_chunks(a, C, pad):
    a = jnp.pad(a, [(0, 0), (0, pad)] + [(0, 0)] * (a.ndim - 2))
    b, tp, h = a.shape[:3]
    a = a.reshape((b, tp // C, C, h) + a.shape[3:])
    return jnp.moveaxis(a, (1, 3), (0, 2))


def gated_delta_rule(q, k, v, g, beta, S0):
    T = q.shape[1]
    dk = q.shape[-1]
    C = min(CHUNK, T)
    pad = (-T) % C
    q = q * (dk ** -0.5)
    qc, kc, vc = _to_chunks(q, C, pad), _to_chunks(k, C, pad), _to_chunks(v, C, pad)
    gcum = jnp.cumsum(_to_chunks(g, C, pad), axis=-1)
    bc = _to_chunks(beta, C, pad)
    kb = kc * bc[..., None]
    vb = vc * bc[..., None]
    tril = jnp.tril(jnp.ones((C, C), dtype=bool))
    strict = jnp.tril(jnp.ones((C, C), dtype=bool), -1)
    diff = gcum[..., :, None] - gcum[..., None, :]
    decay = jnp.where(tril, jnp.exp(jnp.where(tril, diff, 0.0)), 0.0)
    L = jnp.where(strict, jnp.einsum('nbhid,nbhjd->nbhij', kb, kc) * decay, 0.0)
    eye = jnp.eye(C, dtype=L.dtype)
    Tinv = lax.linalg.triangular_solve(eye + L, jnp.broadcast_to(eye, L.shape),
                                       left_side=True, lower=True, unit_diagonal=True)
    u = jnp.einsum('nbhij,nbhje->nbhie', Tinv, vb)
    w = jnp.einsum('nbhij,nbhjd->nbhid', Tinv, kb * jnp.exp(gcum)[..., None])
    qk = jnp.where(tril, jnp.einsum('nbhid,nbhjd->nbhij', qc, kc) * decay, 0.0)

    def step(S, xs):
        q_i, k_i, u_i, w_i, g_i, qk_i = xs
        v_new = u_i - jnp.einsum('bhcd,bhde->bhce', w_i, S)
        o = (jnp.einsum('bhcd,bhde->bhce', q_i * jnp.exp(g_i)[..., None], S)
             + jnp.einsum('bhij,bhje->bhie', qk_i, v_new))
        g_last = g_i[..., -1]
        S = (S * jnp.exp(g_last)[..., None, None]
             + jnp.einsum('bhcd,bhce->bhde', k_i * jnp.exp(g_last[..., None] - g_i)[..., None], v_new))
        return S, o

    S, o = lax.scan(step, S0, (qc, kc, u, w, gcum, qk))
    b = q.shape[0]
    o = jnp.moveaxis(o, (0, 2), (1, 3)).reshape(b, -1, q.shape[2], v.shape[-1])[:, :T]
    return o, S


def mixer_layer(x, st_delta, st_qkv, st_sconv, st_cconv, norm_g, w_in, conv_qkv_w, a_log,
                dt_bias, delta_norm_g, sconv_w, cconv_w, cconv_b, cln_g, cln_b, w_out):
    b, t, _ = x.shape
    h = rmsnorm(x, norm_g)
    p = h @ w_in
    (qkv, b_log, a_in, gate_d, s_B, s_C, s_x, gate_s, glu_in, gate_c) = jnp.split(p, SPLIT_POINTS, axis=-1)

    qkv_c, new_qkv = causal_dwconv(qkv, st_qkv, conv_qkv_w)
    qkv_c = jax.nn.silu(qkv_c).astype(jnp.float32)
    q, k, v = jnp.split(qkv_c.reshape(b, t, 3, N_DHEADS, HEAD_DIM), 3, axis=2)
    q, k, v = l2norm(q[:, :, 0]), l2norm(k[:, :, 0]), v[:, :, 0]
    beta = jax.nn.sigmoid(b_log.astype(jnp.float32))
    g = -jnp.exp(a_log.astype(jnp.float32)) * jax.nn.softplus(
        a_in.astype(jnp.float32) + dt_bias.astype(jnp.float32))
    o_d, new_delta = gated_delta_rule(q, k, v, g, beta, st_delta.astype(jnp.float32))
    o_d = rmsnorm(o_d, delta_norm_g).reshape(b, t, D_DELTA).astype(x.dtype)
    o_d = o_d * jax.nn.silu(gate_d)

    hs = s_C * s_x
    ys, new_sconv = causal_dwconv(hs, st_sconv, sconv_w)
    o_s = s_B * ys * jax.nn.silu(gate_s)

    ga, gb = jnp.split(glu_in, 2, axis=-1)
    u = ga * jax.nn.sigmoid(gb)
    yc, new_cconv = causal_dwconv(u, st_cconv, cconv_w)
    yc = (yc + cconv_b).astype(jnp.float32)
    mu = jnp.mean(yc, axis=-1, keepdims=True)
    var = jnp.mean(jnp.square(yc - mu), axis=-1, keepdims=True)
    yc = ((yc - mu) * lax.rsqrt(var + EPS) * cln_g.astype(jnp.float32) + cln_b.astype(jnp.float32)).astype(x.dtype)
    o_c = jax.nn.silu(yc) * jax.nn.silu(gate_c)

    o = jnp.concatenate([o_d, o_s, o_c], axis=-1)
    y = x + o @ w_out
    return (y, new_delta.astype(st_delta.dtype), new_qkv.astype(st_qkv.dtype),
            new_sconv.astype(st_sconv.dtype), new_cconv.astype(st_cconv.dtype))


def setup_inputs(seed: int = 0) -> dict:
    key = jax.random.key(seed)
    ks = jax.random.split(key, 20)
    f32 = jnp.float32
    dt = jnp.exp(jax.random.uniform(ks[9], (DEPTH, N_DHEADS), f32, np.log(1e-3), np.log(1e-1)))
    return {
        "x_prompt": jax.random.normal(ks[0], (BATCH, SEQ, D_MODEL), f32),
        "x_sample": jax.random.normal(ks[1], (DEC_BATCH, DEC_SEQ, D_MODEL), f32),
        "state_delta": 0.3 * jax.random.normal(ks[2], (DEPTH, DEC_BATCH, N_DHEADS, HEAD_DIM, HEAD_DIM), f32),
        "state_qkv_conv": jax.random.normal(ks[3], (DEPTH, DEC_BATCH, QK_CONV - 1, 3 * D_DELTA), f32),
        "state_sconv": jax.random.normal(ks[4], (DEPTH, DEC_BATCH, SCONV_W - 1, D_SCONV), f32),
        "state_cconv": 0.5 * jax.random.normal(ks[5], (DEPTH, DEC_BATCH, CONF_W - 1, D_CONF), f32),
        "norm_g": 1.0 + 0.02 * jax.random.normal(ks[6], (DEPTH, D_MODEL), f32),
        "w_in": jax.random.normal(ks[7], (DEPTH, D_MODEL, D_IN), f32) * D_MODEL ** -0.5,
        "conv_qkv_w": jax.random.normal(ks[8], (DEPTH, QK_CONV, 3 * D_DELTA), f32) * QK_CONV ** -0.5,
        "a_log": jnp.log(jax.random.uniform(ks[10], (DEPTH, N_DHEADS), f32, 1.0, 16.0)),
        "dt_bias": dt + jnp.log(-jnp.expm1(-dt)),
        "delta_norm_g": 1.0 + 0.02 * jax.random.normal(ks[11], (DEPTH, HEAD_DIM), f32),
        "sconv_w": jax.random.normal(ks[12], (DEPTH, SCONV_W, D_SCONV), f32) * SCONV_W ** -0.5,
        "cconv_w": jax.random.normal(ks[13], (DEPTH, CONF_W, D_CONF), f32) * CONF_W ** -0.5,
        "cconv_b": 0.02 * jax.random.normal(ks[14], (DEPTH, D_CONF), f32),
        "cln_g": 1.0 + 0.02 * jax.random.normal(ks[15], (DEPTH, D_CONF), f32),
        "cln_b": 0.02 * jax.random.normal(ks[16], (DEPTH, D_CONF), f32),
        "w_out": jax.random.normal(ks[17], (DEPTH, D_MIX, D_MODEL), f32) * D_MIX ** -0.5,
        "final_norm_g": 1.0 + 0.02 * jax.random.normal(ks[18], (D_MODEL,), f32),
    }


def reference(x_prompt, x_sample, state_delta, state_qkv_conv, state_sconv, state_cconv,
              norm_g, w_in, conv_qkv_w, a_log, dt_bias, delta_norm_g, sconv_w, cconv_w,
              cconv_b, cln_g, cln_b, w_out, final_norm_g):
    bp = x_prompt.shape[0]
    dtp = x_prompt.dtype
    xp, xs = x_prompt, x_sample
    p_delta, p_qkv, p_sconv, p_cconv = [], [], [], []
    s_delta, s_qkv, s_sconv, s_cconv = [], [], [], []
    for l in range(DEPTH):
        w_l = (norm_g[l], w_in[l], conv_qkv_w[l], a_log[l], dt_bias[l], delta_norm_g[l],
               sconv_w[l], cconv_w[l], cconv_b[l], cln_g[l], cln_b[l], w_out[l])
        xp, d0, q0, sc0, cc0 = mixer_layer(
            xp,
            jnp.zeros((bp, N_DHEADS, HEAD_DIM, HEAD_DIM), dtp),
            jnp.zeros((bp, QK_CONV - 1, 3 * D_DELTA), dtp),
            jnp.zeros((bp, SCONV_W - 1, D_SCONV), dtp),
            jnp.zeros((bp, CONF_W - 1, D_CONF), dtp),
            *w_l)
        p_delta.append(d0); p_qkv.append(q0); p_sconv.append(sc0); p_cconv.append(cc0)
        xs, d1, q1, sc1, cc1 = mixer_layer(
            xs, state_delta[l], state_qkv_conv[l], state_sconv[l], state_cconv[l], *w_l)
        s_delta.append(d1); s_qkv.append(q1); s_sconv.append(sc1); s_cconv.append(cc1)
    y_prompt = rmsnorm(xp, final_norm_g)
    y_sample = rmsnorm(xs, final_norm_g)
    return (y_prompt, y_sample,
            jnp.stack(p_delta), jnp.stack(p_qkv), jnp.stack(p_sconv), jnp.stack(p_cconv),
            jnp.stack(s_delta), jnp.stack(s_qkv), jnp.stack(s_sconv), jnp.stack(s_cconv))
```

```python
import functools

import jax
import jax.numpy as jnp
from jax import lax
from jax.experimental import pallas as pl
from jax.experimental.pallas import tpu as pltpu

D_MODEL = 1024
N_DHEADS = 4
HEAD_DIM = 128
D_DELTA = N_DHEADS * HEAD_DIM
D_SCONV = 256
D_CONF = 256
QK_CONV = 4
SCONV_W = 3
CONF_W = 31
CHUNK = 64
EPS = 1e-6

R_GATE_D = 0
R_SB = 512
R_SC = 768
R_SX = 1024
R_GATE_S = 1280
R_GA = 1536
R_GB = 1792
R_GATE_C = 2048
R_BA = 2304
R_WIDTH = 2432

SUBLANES = 8
LANES = 128
TIME_TILE = 256
GROUP = 256
SAMPLE_PAD = 8
SAMPLE_BATCH_BLOCK = 16
CTAIL = 32
VMEM_LIMIT = 56 * 1024 * 1024

F32 = jnp.float32
BF16 = jnp.bfloat16


def _mm(a, b):
    return jnp.dot(a.astype(BF16), b.astype(BF16), preferred_element_type=F32)


def _mm_nt(a, b):
    return lax.dot_general(a.astype(BF16), b.astype(BF16), (((1,), (1,)), ((), ())),
                           preferred_element_type=F32)


def _mm_tn(a, b):
    return lax.dot_general(a.astype(BF16), b.astype(BF16), (((0,), (0,)), ((), ())),
                           preferred_element_type=F32)


def _sigmoid(x):
    return 1.0 / (1.0 + jnp.exp(-x))


def _silu(x):
    return x * _sigmoid(x)


def _softplus(x):
    return jnp.maximum(x, 0.0) + jnp.log1p(jnp.exp(-jnp.abs(x)))


def _rms_rows(x, g):
    return x * lax.rsqrt(jnp.mean(x * x, axis=-1, keepdims=True) + EPS) * g


def _chunk_cumsum(g, chunk, row):
    pos = row % chunk
    s = 1
    while s < chunk:
        g = g + jnp.where(pos >= s, pltpu.roll(g, s, axis=0), 0.0)
        s *= 2
    return g


def _unit_lower_inverse(L, ii, jj, chunk, nil):
    base = min(chunk, 16)
    same = (ii // base) == (jj // base)
    eye = jnp.where(ii == jj, 1.0, 0.0).astype(F32)
    D = jnp.where(same, L, 0.0)
    X = eye - D
    p = 2
    while p < min(base, nil):
        D = _mm(D, D)
        X = X + _mm(X, D)
        p *= 2
    size = base
    while size < chunk:
        big = (ii // (2 * size)) == (jj // (2 * size))
        E = jnp.where(jnp.logical_and(big, jnp.logical_not(same)), L, 0.0)
        X = X - _mm(_mm(X, E), X)
        same = big
        size *= 2
    return X


def _delta_prep(q, k, v, beta, gc_col, gc_row, ii, jj, chunk, nil):
    same = (ii // chunk) == (jj // chunk)
    tril = jnp.logical_and(same, ii >= jj)
    strict = jnp.logical_and(same, ii > jj)
    diff = gc_col - gc_row
    decay = jnp.where(tril, jnp.exp(jnp.where(tril, diff, 0.0)), 0.0)
    kb = k * beta
    vb = v * beta
    L = jnp.where(strict, _mm_nt(kb, k) * decay, 0.0)
    tinv = _unit_lower_inverse(L, ii, jj, chunk, nil)
    eg = jnp.exp(gc_col)
    uw = _mm(tinv, jnp.concatenate([vb, kb * eg], axis=1))
    u = uw[:, :HEAD_DIM]
    w = uw[:, HEAD_DIM:]
    qk = jnp.where(tril, _mm_nt(q, k) * decay, 0.0)
    return u, w, qk, q * eg


def _branch_s(p_ref, buf_ref, w_ref, tail, rows, lead=()):
    n = len(lead)
    acc = None
    for j in range(SCONV_W):
        idx = lead + (pl.ds(tail - (SCONV_W - 1) + j, rows), slice(None))
        term = buf_ref[idx] * w_ref[j:j + 1, :].reshape((1,) * n + (1, D_SCONV))
        acc = term if acc is None else acc + term
    return acc


def _conformer(ubuf_ref, w_ref, b_ref, g_ref, beta_ref, tail, rows, lead=()):
    n = len(lead)
    shp = (1,) * n + (1, D_CONF)
    acc = None
    for j in range(CONF_W):
        idx = lead + (pl.ds(tail - (CONF_W - 1) + j, rows), slice(None))
        term = ubuf_ref[idx] * w_ref[j:j + 1, :].reshape(shp)
        acc = term if acc is None else acc + term
    yc = acc + b_ref[...].reshape(shp)
    mu = jnp.mean(yc, axis=-1, keepdims=True)
    var = jnp.mean(jnp.square(yc - mu), axis=-1, keepdims=True)
    yc = (yc - mu) * lax.rsqrt(var + EPS) * g_ref[...].reshape(shp) + beta_ref[...].reshape(shp)
    return _silu(yc)


def _prompt_kernel(final_norm,
                   x_ref, ng_ref, wqkv_ref, wrest_ref, cw_ref, alog_ref, dt_ref, dng_ref,
                   sw_ref, ccw_ref, ccb_ref, clg_ref, clb_ref, wout_ref, fg_ref,
                   y_ref, s_out_ref, qt_ref, st_ref, ct_ref,
                   qbuf, sbuf, cbuf, s_scr, obuf, vnew, rest):
    i = pl.program_id(1)
    nt = pl.num_programs(1)
    tt = TIME_TILE

    @pl.when(i == 0)
    def _():
        qbuf[0:SUBLANES, :] = jnp.zeros((SUBLANES, 3 * D_DELTA), F32)
        sbuf[0:SUBLANES, :] = jnp.zeros((SUBLANES, D_SCONV), F32)
        cbuf[0:CTAIL, :] = jnp.zeros((CTAIL, D_CONF), F32)
        s_scr[...] = jnp.zeros(s_scr.shape, F32)

    x = x_ref[0]
    h = _rms_rows(x, ng_ref[...]).astype(BF16)
    qbuf[SUBLANES:SUBLANES + tt, :] = jnp.dot(h, wqkv_ref[...], preferred_element_type=F32)
    rest[...] = jnp.dot(h, wrest_ref[...], preferred_element_type=F32)

    sbuf[SUBLANES:SUBLANES + tt, :] = rest[:, R_SC:R_SC + D_SCONV] * rest[:, R_SX:R_SX + D_SCONV]
    ys = _branch_s(rest, sbuf, sw_ref, SUBLANES, tt)
    obuf[:, D_DELTA:D_DELTA + D_SCONV] = (rest[:, R_SB:R_SB + D_SCONV] * ys
                                          * _silu(rest[:, R_GATE_S:R_GATE_S + D_SCONV]))

    cbuf[CTAIL:CTAIL + tt, :] = rest[:, R_GA:R_GA + D_CONF] * _sigmoid(rest[:, R_GB:R_GB + D_CONF])
    yc = _conformer(cbuf, ccw_ref, ccb_ref, clg_ref, clb_ref, CTAIL, tt)
    obuf[:, D_DELTA + D_SCONV:] = yc * _silu(rest[:, R_GATE_C:R_GATE_C + D_CONF])

    row = lax.broadcasted_iota(jnp.int32, (tt, LANES), 0)
    ba = rest[:, R_BA:R_BA + LANES]
    beta_all = _sigmoid(ba)
    g_all = -jnp.exp(alog_ref[...]) * _softplus(ba + dt_ref[...])
    gc = _chunk_cumsum(g_all, CHUNK, row)
    gct = gc.T
    ii = lax.broadcasted_iota(jnp.int32, (GROUP, GROUP), 0)
    jj = lax.broadcasted_iota(jnp.int32, (GROUP, GROUP), 1)

    def conv_act(c0):
        acc = None
        for j in range(QK_CONV):
            term = (qbuf[pl.ds(SUBLANES - (QK_CONV - 1) + j, tt), c0:c0 + HEAD_DIM]
                    * cw_ref[j:j + 1, c0:c0 + HEAD_DIM])
            acc = term if acc is None else acc + term
        return _silu(acc)

    for hd in range(N_DHEADS):
        q = conv_act(hd * HEAD_DIM)
        k = conv_act(D_DELTA + hd * HEAD_DIM)
        v = conv_act(2 * D_DELTA + hd * HEAD_DIM)
        q = q * lax.rsqrt(jnp.sum(q * q, axis=-1, keepdims=True) + EPS) * (HEAD_DIM ** -0.5)
        k = k * lax.rsqrt(jnp.sum(k * k, axis=-1, keepdims=True) + EPS)
        beta = beta_all[:, hd:hd + 1]
        gcol = gc[:, N_DHEADS + hd:N_DHEADS + hd + 1]
        grow = gct[N_DHEADS + hd:N_DHEADS + hd + 1, :]
        for g0 in range(0, tt, GROUP):
            sl = slice(g0, g0 + GROUP)
            u, w, qk, qg = _delta_prep(q[sl], k[sl], v[sl], beta[sl], gcol[sl], grow[:, sl],
                                       ii, jj, CHUNK, CHUNK)
            kg = k[sl]
            gcg = gcol[sl]
            o_inter = []
            for c in range(GROUP // CHUNK):
                cs = slice(c * CHUNK, (c + 1) * CHUNK)
                s_cur = s_scr[hd]
                r = _mm(jnp.concatenate([w[cs], qg[cs]], axis=0), s_cur)
                vn = u[cs] - r[:CHUNK]
                o_inter.append(r[CHUNK:])
                gl = gcg[(c + 1) * CHUNK - 1:(c + 1) * CHUNK, :]
                kdec = kg[cs] * jnp.exp(gl - gcg[cs])
                s_scr[hd] = s_cur * jnp.exp(gl) + _mm(kdec.T, vn)
                vnew[cs, :] = vn
            o = jnp.concatenate(o_inter, axis=0) + _mm(qk, vnew[...])
            o = _rms_rows(o, dng_ref[...])
            obuf[g0:g0 + GROUP, hd * HEAD_DIM:(hd + 1) * HEAD_DIM] = (
                o * _silu(rest[g0:g0 + GROUP, R_GATE_D + hd * HEAD_DIM:R_GATE_D + (hd + 1) * HEAD_DIM]))

    y = x + jnp.dot(obuf[...].astype(BF16), wout_ref[...], preferred_element_type=F32)
    if final_norm:
        y = _rms_rows(y, fg_ref[...])
    y_ref[0] = y

    @pl.when(i == nt - 1)
    def _():
        s_out_ref[0] = s_scr[...]
        qt_ref[0] = qbuf[tt:tt + SUBLANES, :]
        st_ref[0] = sbuf[tt:tt + SUBLANES, :]
        ct_ref[0] = cbuf[tt:tt + CTAIL, :]

    qbuf[0:SUBLANES, :] = qbuf[tt:tt + SUBLANES, :]
    sbuf[0:SUBLANES, :] = sbuf[tt:tt + SUBLANES, :]
    cbuf[0:CTAIL, :] = cbuf[tt:tt + CTAIL, :]


def _const_spec(shape):
    return pl.BlockSpec(shape, lambda *_: (0,) * len(shape))


def _prompt_layer(x, wts, final_norm):
    b, t, d = x.shape
    nt = t // TIME_TILE
    weight_specs = [_const_spec(w.shape) for w in wts]
    out_shape = (
        jax.ShapeDtypeStruct((b, t, d), F32),
        jax.ShapeDtypeStruct((b, N_DHEADS, HEAD_DIM, HEAD_DIM), F32),
        jax.ShapeDtypeStruct((b, SUBLANES, 3 * D_DELTA), F32),
        jax.ShapeDtypeStruct((b, SUBLANES, D_SCONV), F32),
        jax.ShapeDtypeStruct((b, CTAIL, D_CONF), F32),
    )
    out_specs = (
        pl.BlockSpec((1, TIME_TILE, d), lambda bi, ti: (bi, ti, 0)),
        pl.BlockSpec((1, N_DHEADS, HEAD_DIM, HEAD_DIM), lambda bi, ti: (bi, 0, 0, 0)),
        pl.BlockSpec((1, SUBLANES, 3 * D_DELTA), lambda bi, ti: (bi, 0, 0)),
        pl.BlockSpec((1, SUBLANES, D_SCONV), lambda bi, ti: (bi, 0, 0)),
        pl.BlockSpec((1, CTAIL, D_CONF), lambda bi, ti: (bi, 0, 0)),
    )
    scratch = [
        pltpu.VMEM((TIME_TILE + SUBLANES, 3 * D_DELTA), F32),
        pltpu.VMEM((TIME_TILE + SUBLANES, D_SCONV), F32),
        pltpu.VMEM((TIME_TILE + CTAIL, D_CONF), F32),
        pltpu.VMEM((N_DHEADS, HEAD_DIM, HEAD_DIM), F32),
        pltpu.VMEM((TIME_TILE, D_MODEL), F32),
        pltpu.VMEM((GROUP, HEAD_DIM), F32),
        pltpu.VMEM((TIME_TILE, R_WIDTH), F32),
    ]
    return pl.pallas_call(
        functools.partial(_prompt_kernel, final_norm),
        grid=(b, nt),
        in_specs=[pl.BlockSpec((1, TIME_TILE, d), lambda bi, ti: (bi, ti, 0))] + weight_specs,
        out_specs=out_specs,
        out_shape=out_shape,
        scratch_shapes=scratch,
        compiler_params=pltpu.CompilerParams(
            dimension_semantics=("arbitrary", "arbitrary"),
            vmem_limit_bytes=VMEM_LIMIT),
        name="prompt_layer",
    )(x, *wts)


def _sample_kernel(final_norm,
                   x_ref, sd_ref, sq_ref, ss_ref, sc_ref,
                   ng_ref, wqkv_ref, wrest_ref, cw_ref, alog_ref, dt_ref, dng_ref,
                   sw_ref, ccw_ref, ccb_ref, clg_ref, clb_ref, wout_ref, fg_ref,
                   y_ref, sd_out_ref, qt_ref, st_ref, ct_ref,
                   qbuf, sbuf, obuf, ubuf, wqbuf, kdbuf, glbuf, oibuf, rest):
    nb = SAMPLE_BATCH_BLOCK
    rows = nb * SAMPLE_PAD
    x = x_ref[...].reshape(rows, D_MODEL)
    h = _rms_rows(x, ng_ref[...]).astype(BF16)
    qkv = jnp.dot(h, wqkv_ref[...], preferred_element_type=F32)
    rest[...] = jnp.dot(h, wrest_ref[...], preferred_element_type=F32)
    row = lax.broadcasted_iota(jnp.int32, (rows, LANES), 0)
    valid = (row % SAMPLE_PAD) < (SAMPLE_PAD // 2)
    valid1 = valid[:, 0:1]

    qt_ref[...] = qkv.reshape(nb, SAMPLE_PAD, 3 * D_DELTA)
    qbuf[:, 0:SUBLANES, :] = sq_ref[...]
    qbuf[:, SUBLANES:, :] = qkv.reshape(nb, SAMPLE_PAD, 3 * D_DELTA)

    hs = rest[:, R_SC:R_SC + D_SCONV] * rest[:, R_SX:R_SX + D_SCONV]
    st_ref[...] = hs.reshape(nb, SAMPLE_PAD, D_SCONV)
    sbuf[:, 0:SUBLANES, :] = ss_ref[...]
    sbuf[:, SUBLANES:, :] = hs.reshape(nb, SAMPLE_PAD, D_SCONV)
    ys = _branch_s(rest, sbuf, sw_ref, SUBLANES, SAMPLE_PAD, lead=(slice(None),))
    obuf[:, D_DELTA:D_DELTA + D_SCONV] = (rest[:, R_SB:R_SB + D_SCONV] * ys.reshape(rows, D_SCONV)
                                          * _silu(rest[:, R_GATE_S:R_GATE_S + D_SCONV]))

    uc = rest[:, R_GA:R_GA + D_CONF] * _sigmoid(rest[:, R_GB:R_GB + D_CONF])
    ct_ref[:, 0:CTAIL, :] = sc_ref[...]
    ct_ref[:, CTAIL:, :] = uc.reshape(nb, SAMPLE_PAD, D_CONF)
    yc = _conformer(ct_ref, ccw_ref, ccb_ref, clg_ref, clb_ref, CTAIL, SAMPLE_PAD,
                    lead=(slice(None),))
    obuf[:, D_DELTA + D_SCONV:] = yc.reshape(rows, D_CONF) * _silu(rest[:, R_GATE_C:R_GATE_C + D_CONF])

    ba = rest[:, R_BA:R_BA + LANES]
    beta_all = jnp.where(valid, _sigmoid(ba), 0.0)
    g_all = jnp.where(valid, -jnp.exp(alog_ref[...]) * _softplus(ba + dt_ref[...]), 0.0)
    gc = _chunk_cumsum(g_all, SAMPLE_PAD, row)
    gct = gc.T
    gl_all = jnp.broadcast_to(
        gc.reshape(nb, SAMPLE_PAD, LANES)[:, SAMPLE_PAD - 1:SAMPLE_PAD, :],
        (nb, SAMPLE_PAD, LANES)).reshape(rows, LANES)
    ii = lax.broadcasted_iota(jnp.int32, (rows, rows), 0)
    jj = lax.broadcasted_iota(jnp.int32, (rows, rows), 1)

    def conv_act(c0):
        acc = None
        for j in range(QK_CONV):
            term = (qbuf[:, pl.ds(SUBLANES - (QK_CONV - 1) + j, SAMPLE_PAD), c0:c0 + HEAD_DIM]
                    * cw_ref[j:j + 1, c0:c0 + HEAD_DIM].reshape(1, 1, HEAD_DIM))
            acc = term if acc is None else acc + term
        return jnp.where(valid1, _silu(acc.reshape(rows, HEAD_DIM)), 0.0)

    for hd in range(N_DHEADS):
        q = conv_act(hd * HEAD_DIM)
        k = conv_act(D_DELTA + hd * HEAD_DIM)
        v = conv_act(2 * D_DELTA + hd * HEAD_DIM)
        q = q * lax.rsqrt(jnp.sum(q * q, axis=-1, keepdims=True) + EPS) * (HEAD_DIM ** -0.5)
        k = k * lax.rsqrt(jnp.sum(k * k, axis=-1, keepdims=True) + EPS)
        beta = beta_all[:, hd:hd + 1]
        gcol = gc[:, N_DHEADS + hd:N_DHEADS + hd + 1]
        grow = gct[N_DHEADS + hd:N_DHEADS + hd + 1, :]
        u, w, qk, qg = _delta_prep(q, k, v, beta, gcol, grow, ii, jj, SAMPLE_PAD, SAMPLE_PAD // 2)
        gl = gl_all[:, N_DHEADS + hd:N_DHEADS + hd + 1]
        ubuf[...] = u
        wqbuf[:, 0:SAMPLE_PAD, :] = w.reshape(nb, SAMPLE_PAD, HEAD_DIM)
        wqbuf[:, SAMPLE_PAD:, :] = qg.reshape(nb, SAMPLE_PAD, HEAD_DIM)
        kdbuf[...] = (k * jnp.exp(gl - gcol)).reshape(nb, SAMPLE_PAD, HEAD_DIM)
        glbuf[...] = jnp.broadcast_to(jnp.exp(gl), (rows, HEAD_DIM)).reshape(nb, SAMPLE_PAD, HEAD_DIM)

        def body(bi, carry):
            s_cur = sd_ref[bi, hd]
            r = _mm(wqbuf[bi], s_cur)
            r0 = pl.multiple_of(bi * SAMPLE_PAD, SAMPLE_PAD)
            vn = ubuf[pl.ds(r0, SAMPLE_PAD), :] - r[:SAMPLE_PAD]
            ubuf[pl.ds(r0, SAMPLE_PAD), :] = vn
            oibuf[pl.ds(r0, SAMPLE_PAD), :] = r[SAMPLE_PAD:]
            sd_out_ref[bi, hd] = s_cur * glbuf[bi][0:1, :] + _mm_tn(kdbuf[bi], vn)
            return carry

        lax.fori_loop(0, nb, body, 0)
        o = oibuf[...] + _mm(qk, ubuf[...])
        o = _rms_rows(o, dng_ref[...])
        obuf[:, hd * HEAD_DIM:(hd + 1) * HEAD_DIM] = (
            o * _silu(rest[:, R_GATE_D + hd * HEAD_DIM:R_GATE_D + (hd + 1) * HEAD_DIM]))

    y = x + jnp.dot(obuf[...].astype(BF16), wout_ref[...], preferred_element_type=F32)
    if final_norm:
        y = _rms_rows(y, fg_ref[...])
    y_ref[...] = y.reshape(nb, SAMPLE_PAD, D_MODEL)


def _sample_layer(x8, sd, sq8, ss8, sc32, wts, final_norm):
    b = x8.shape[0]
    nb = SAMPLE_BATCH_BLOCK
    weight_specs = [_const_spec(w.shape) for w in wts]

    def bspec(shape):
        return pl.BlockSpec((nb,) + shape, lambda bi: (bi,) + (0,) * len(shape))

    out_shape = (
        jax.ShapeDtypeStruct((b, SAMPLE_PAD, D_MODEL), F32),
        jax.ShapeDtypeStruct((b, N_DHEADS, HEAD_DIM, HEAD_DIM), F32),
        jax.ShapeDtypeStruct((b, SAMPLE_PAD, 3 * D_DELTA), F32),
        jax.ShapeDtypeStruct((b, SAMPLE_PAD, D_SCONV), F32),
        jax.ShapeDtypeStruct((b, CTAIL + SAMPLE_PAD, D_CONF), F32),
    )
    out_specs = (
        bspec((SAMPLE_PAD, D_MODEL)),
        bspec((N_DHEADS, HEAD_DIM, HEAD_DIM)),
        bspec((SAMPLE_PAD, 3 * D_DELTA)),
        bspec((SAMPLE_PAD, D_SCONV)),
        bspec((CTAIL + SAMPLE_PAD, D_CONF)),
    )
    rows = nb * SAMPLE_PAD
    scratch = [
        pltpu.VMEM((nb, 2 * SUBLANES, 3 * D_DELTA), F32),
        pltpu.VMEM((nb, 2 * SUBLANES, D_SCONV), F32),
        pltpu.VMEM((rows, D_MODEL), F32),
        pltpu.VMEM((rows, HEAD_DIM), F32),
        pltpu.VMEM((nb, 2 * SAMPLE_PAD, HEAD_DIM), F32),
        pltpu.VMEM((nb, SAMPLE_PAD, HEAD_DIM), F32),
        pltpu.VMEM((nb, SAMPLE_PAD, HEAD_DIM), F32),
        pltpu.VMEM((rows, HEAD_DIM), F32),
        pltpu.VMEM((rows, R_WIDTH), F32),
    ]
    in_specs = [
        bspec((SAMPLE_PAD, D_MODEL)),
        bspec((N_DHEADS, HEAD_DIM, HEAD_DIM)),
        bspec((SUBLANES, 3 * D_DELTA)),
        bspec((SUBLANES, D_SCONV)),
        bspec((CTAIL, D_CONF)),
    ] + weight_specs
    return pl.pallas_call(
        functools.partial(_sample_kernel, final_norm),
        grid=(b // nb,),
        in_specs=in_specs,
        out_specs=out_specs,
        out_shape=out_shape,
        scratch_shapes=scratch,
        compiler_params=pltpu.CompilerParams(
            dimension_semantics=("arbitrary",),
            vmem_limit_bytes=VMEM_LIMIT),
        name="sample_layer",
    )(x8, sd, sq8, ss8, sc32, *wts)


def _layer_weights(l, norm_g, w_in, conv_qkv_w, a_log, dt_bias, delta_norm_g, sconv_w,
                   cconv_w, cconv_b, cln_g, cln_b, w_out, final_norm_g):
    w = w_in[l]
    n_qkv = 3 * D_DELTA
    ba = jnp.pad(w[:, n_qkv:n_qkv + 2 * N_DHEADS], ((0, 0), (0, LANES - 2 * N_DHEADS)))
    wrest = jnp.concatenate([w[:, n_qkv + 2 * N_DHEADS:], ba], axis=1).astype(BF16)
    lane_pad = (N_DHEADS, LANES - 2 * N_DHEADS)
    return (
        norm_g[l].reshape(1, D_MODEL),
        w[:, :n_qkv].astype(BF16),
        wrest,
        conv_qkv_w[l],
        jnp.pad(a_log[l], lane_pad).reshape(1, LANES),
        jnp.pad(dt_bias[l], lane_pad).reshape(1, LANES),
        delta_norm_g[l].reshape(1, HEAD_DIM),
        sconv_w[l],
        cconv_w[l],
        cconv_b[l].reshape(1, D_CONF),
        cln_g[l].reshape(1, D_CONF),
        cln_b[l].reshape(1, D_CONF),
        w_out[l].astype(BF16),
        final_norm_g.reshape(1, D_MODEL),
    )


def kernel(x_prompt, x_sample, state_delta, state_qkv_conv, state_sconv, state_cconv, norm_g, w_in, conv_qkv_w, a_log, dt_bias, delta_norm_g, sconv_w, cconv_w, cconv_b, cln_g, cln_b, w_out, final_norm_g):
    depth = w_in.shape[0]
    dec_seq = x_sample.shape[1]
    assert x_prompt.shape[1] % TIME_TILE == 0 and TIME_TILE % GROUP == 0
    assert dec_seq == SAMPLE_PAD // 2 and x_sample.shape[0] % SAMPLE_BATCH_BLOCK == 0

    xp = x_prompt
    xs = jnp.pad(x_sample, ((0, 0), (0, SAMPLE_PAD - dec_seq), (0, 0)))
    outs = [[] for _ in range(8)]
    for l in range(depth):
        wts = _layer_weights(l, norm_g, w_in, conv_qkv_w, a_log, dt_bias, delta_norm_g, sconv_w,
                             cconv_w, cconv_b, cln_g, cln_b, w_out, final_norm_g)
        last = l == depth - 1
        xp, pd, pq, ps, pc = _prompt_layer(xp, wts, last)
        outs[0].append(pd)
        outs[1].append(pq[:, SUBLANES - (QK_CONV - 1):, :])
        outs[2].append(ps[:, SUBLANES - (SCONV_W - 1):, :])
        outs[3].append(pc[:, CTAIL - (CONF_W - 1):, :])
        sq8 = jnp.pad(state_qkv_conv[l], ((0, 0), (SUBLANES - (QK_CONV - 1), 0), (0, 0)))
        ss8 = jnp.pad(state_sconv[l], ((0, 0), (SUBLANES - (SCONV_W - 1), 0), (0, 0)))
        sc32 = jnp.pad(state_cconv[l], ((0, 0), (CTAIL - (CONF_W - 1), 0), (0, 0)))
        xs, sd, sq, ss, sc = _sample_layer(xs, state_delta[l], sq8, ss8, sc32, wts, last)
        outs[4].append(sd)
        outs[5].append(sq[:, dec_seq - (QK_CONV - 1):dec_seq, :])
        outs[6].append(ss[:, dec_seq - (SCONV_W - 1):dec_seq, :])
        first = CTAIL + dec_seq - (CONF_W - 1)
        outs[7].append(sc[:, first:first + CONF_W - 1, :])
    return (xp, xs[:, :dec_seq, :]) + tuple(jnp.stack(o) for o in outs)
```

```python
import functools

import jax
import jax.numpy as jnp
from jax import lax
from jax.experimental import pallas as pl
from jax.experimental.pallas import tpu as pltpu

D_MODEL = 1024
N_DHEADS = 4
HEAD_DIM = 128
D_DELTA = N_DHEADS * HEAD_DIM
D_SCONV = 256
D_CONF = 256
QK_CONV = 4
SCONV_W = 3
CONF_W = 31
CHUNK = 64
EPS = 1e-6

R_GATE_D = 0
R_SB = 512
R_SC = 768
R_SX = 1024
R_GATE_S = 1280
R_GA = 1536
R_GB = 1792
R_GATE_C = 2048
R_BA = 2304
R_WIDTH = 2432

SUBLANES = 8
LANES = 128
TIME_TILE = 256
SAMPLE_PAD = 8
SAMPLE_BATCH_BLOCK = 16
CTAIL = 32
VMEM_LIMIT = 56 * 1024 * 1024

F32 = jnp.float32
BF16 = jnp.bfloat16


def _mm(a, b):
    return jnp.dot(a.astype(BF16), b.astype(BF16), preferred_element_type=F32)


def _mm_nt(a, b):
    return lax.dot_general(a.astype(BF16), b.astype(BF16), (((1,), (1,)), ((), ())),
                           preferred_element_type=F32)


def _mm_tn(a, b):
    return lax.dot_general(a.astype(BF16), b.astype(BF16), (((0,), (0,)), ((), ())),
                           preferred_element_type=F32)


def _sigmoid(x):
    return 1.0 / (1.0 + jnp.exp(-x))


def _silu(x):
    return x * _sigmoid(x)


def _softplus(x):
    return jnp.maximum(x, 0.0) + jnp.log1p(jnp.exp(-jnp.abs(x)))


def _rms_rows(x, g):
    return x * lax.rsqrt(jnp.mean(x * x, axis=-1, keepdims=True) + EPS) * g


def _chunk_cumsum(g, chunk, row):
    pos = row % chunk
    s = 1
    while s < chunk:
        g = g + jnp.where(pos >= s, pltpu.roll(g, s, axis=0), 0.0)
        s *= 2
    return g


def _pair_diag(a, b):
    z = jnp.zeros_like(a)
    return jnp.concatenate([jnp.concatenate([a, z], axis=1), jnp.concatenate([z, b], axis=1)], axis=0)


class _Packed:
    def __init__(self, chunk, rows, bf16_tile):
        self.chunk, self.rows, self.n = chunk, rows, rows // chunk
        self.rr = lax.broadcasted_iota(jnp.int32, (chunk, rows), 0)
        lane = lax.broadcasted_iota(jnp.int32, (chunk, rows), 1)
        self.jl = lane % chunk
        self.lane_blk = lane // chunk
        ii = lax.broadcasted_iota(jnp.int32, (rows, rows), 0)
        jj = lax.broadcasted_iota(jnp.int32, (rows, rows), 1)
        mask = jnp.where((ii // chunk) == (jj // chunk), 1.0, 0.0).astype(F32)
        self.bd_mask = mask.astype(BF16) if bf16_tile else mask

    def pack(self, g):
        out = g[0:self.chunk]
        for c in range(1, self.n):
            out = jnp.where(self.lane_blk == c, g[c * self.chunk:(c + 1) * self.chunk], out)
        return out

    def col(self, v):
        shape = (self.chunk, self.rows)
        out = jnp.broadcast_to(v[0:self.chunk], shape)
        for c in range(1, self.n):
            out = jnp.where(self.lane_blk == c,
                            jnp.broadcast_to(v[c * self.chunk:(c + 1) * self.chunk], shape), out)
        return out

    def block_diag(self, xp):
        if self.bd_mask.dtype == BF16:
            return jnp.concatenate([xp.astype(BF16)] * self.n, axis=0) * self.bd_mask
        return (jnp.concatenate([xp] * self.n, axis=0) * self.bd_mask).astype(BF16)

    def unit_lower_inverse(self, lp, nil):
        base = min(self.chunk, 16)
        same = (self.rr // base) == (self.jl // base)
        eye = jnp.where(self.rr == self.jl, 1.0, 0.0).astype(F32)
        d = jnp.where(same, lp, 0.0)
        x = eye - d
        d_bd = self.block_diag(d)
        p = 2
        while p < min(base, nil):
            d = _mm(d, d_bd)
            d_bd = self.block_diag(d)
            x = x + _mm(x, d_bd)
            p *= 2
        size = base
        while size < self.chunk:
            big = (self.rr // (2 * size)) == (self.jl // (2 * size))
            e = jnp.where(jnp.logical_and(big, jnp.logical_not(same)), lp, 0.0)
            x = x - _mm(_mm(x, self.block_diag(e)), self.block_diag(x))
            same = big
            size *= 2
        return x

    def delta_prep(self, q, k, v, beta, gc_col, gc_row, nil):
        tril = self.rr >= self.jl
        diff = self.col(gc_col) - gc_row
        decay = jnp.where(tril, jnp.exp(jnp.where(tril, diff, 0.0)), 0.0)
        kb = k * beta
        vb = v * beta
        g = _mm_nt(jnp.concatenate([kb, q], axis=0), k)
        lp = jnp.where(self.rr > self.jl, self.pack(g[:self.rows]) * decay, 0.0)
        qkp = self.pack(g[self.rows:]) * decay
        tinv = self.unit_lower_inverse(lp, nil)
        eg = jnp.exp(gc_col)
        uw = _mm(self.block_diag(tinv), jnp.concatenate([vb, kb * eg], axis=1))
        return uw[:, :HEAD_DIM], uw[:, HEAD_DIM:], q * eg, qkp


def _branch_s(buf_ref, w_ref, tail, rows, lead=()):
    n = len(lead)
    acc = None
    for j in range(SCONV_W):
        idx = lead + (pl.ds(tail - (SCONV_W - 1) + j, rows), slice(None))
        term = buf_ref[idx] * w_ref[j:j + 1, :].reshape((1,) * n + (1, D_SCONV))
        acc = term if acc is None else acc + term
    return acc


def _conformer(ubuf_ref, w_ref, b_ref, g_ref, beta_ref, tail, rows, lead=()):
    n = len(lead)
    shp = (1,) * n + (1, D_CONF)
    acc = None
    for j in range(CONF_W):
        idx = lead + (pl.ds(tail - (CONF_W - 1) + j, rows), slice(None))
        term = ubuf_ref[idx] * w_ref[j:j + 1, :].reshape(shp)
        acc = term if acc is None else acc + term
    yc = acc + b_ref[...].reshape(shp)
    mu = jnp.mean(yc, axis=-1, keepdims=True)
    var = jnp.mean(jnp.square(yc - mu), axis=-1, keepdims=True)
    yc = (yc - mu) * lax.rsqrt(var + EPS) * g_ref[...].reshape(shp) + beta_ref[...].reshape(shp)
    return _silu(yc)


def _prompt_kernel(final_norm,
                   x_ref, ng_ref, wqkv_ref, wrest_ref, cw_ref, alog_ref, dt_ref, dng_ref,
                   sw_ref, ccw_ref, ccb_ref, clg_ref, clb_ref, wout_ref, fg_ref,
                   y_ref, s_out_ref, qt_ref, st_ref, ct_ref,
                   qbuf, sbuf, cbuf, s_scr, obuf, rest):
    i = pl.program_id(1)
    nt = pl.num_programs(1)
    tt = TIME_TILE
    nchunk = tt // CHUNK

    @pl.when(i == 0)
    def _():
        qbuf[0:SUBLANES, :] = jnp.zeros((SUBLANES, 3 * D_DELTA), F32)
        sbuf[0:SUBLANES, :] = jnp.zeros((SUBLANES, D_SCONV), F32)
        cbuf[0:CTAIL, :] = jnp.zeros((CTAIL, D_CONF), F32)
        s_scr[...] = jnp.zeros(s_scr.shape, F32)

    x = x_ref[0]
    h = _rms_rows(x, ng_ref[...]).astype(BF16)
    qbuf[SUBLANES:SUBLANES + tt, :] = jnp.dot(h, wqkv_ref[...], preferred_element_type=F32)
    rest[...] = jnp.dot(h, wrest_ref[...], preferred_element_type=F32)

    sbuf[SUBLANES:SUBLANES + tt, :] = rest[:, R_SC:R_SC + D_SCONV] * rest[:, R_SX:R_SX + D_SCONV]
    ys = _branch_s(sbuf, sw_ref, SUBLANES, tt)
    obuf[:, D_DELTA:D_DELTA + D_SCONV] = (rest[:, R_SB:R_SB + D_SCONV] * ys
                                          * _silu(rest[:, R_GATE_S:R_GATE_S + D_SCONV]))

    cbuf[CTAIL:CTAIL + tt, :] = rest[:, R_GA:R_GA + D_CONF] * _sigmoid(rest[:, R_GB:R_GB + D_CONF])
    yc = _conformer(cbuf, ccw_ref, ccb_ref, clg_ref, clb_ref, CTAIL, tt)
    obuf[:, D_DELTA + D_SCONV:] = yc * _silu(rest[:, R_GATE_C:R_GATE_C + D_CONF])

    row = lax.broadcasted_iota(jnp.int32, (tt, LANES), 0)
    ba = rest[:, R_BA:R_BA + LANES]
    beta_all = _sigmoid(ba)
    g_all = -jnp.exp(alog_ref[...]) * _softplus(ba + dt_ref[...])
    gc = _chunk_cumsum(g_all, CHUNK, row)
    gct = gc.T
    gl_rows = jnp.concatenate(
        [jnp.broadcast_to(gc[(c + 1) * CHUNK - 1:(c + 1) * CHUNK, :], (CHUNK, LANES))
         for c in range(nchunk)], axis=0)
    pk = _Packed(CHUNK, tt, True)

    def conv_act(c0):
        acc = None
        for j in range(QK_CONV):
            term = (qbuf[pl.ds(SUBLANES - (QK_CONV - 1) + j, tt), c0:c0 + HEAD_DIM]
                    * cw_ref[j:j + 1, c0:c0 + HEAD_DIM])
            acc = term if acc is None else acc + term
        return _silu(acc)

    def head_prep(hd):
        q = conv_act(hd * HEAD_DIM)
        k = conv_act(D_DELTA + hd * HEAD_DIM)
        v = conv_act(2 * D_DELTA + hd * HEAD_DIM)
        q = q * lax.rsqrt(jnp.sum(q * q, axis=-1, keepdims=True) + EPS) * (HEAD_DIM ** -0.5)
        k = k * lax.rsqrt(jnp.sum(k * k, axis=-1, keepdims=True) + EPS)
        lane = N_DHEADS + hd
        gcol = gc[:, lane:lane + 1]
        u, w, qg, qkp = pk.delta_prep(q, k, v, beta_all[:, hd:hd + 1], gcol,
                                      gct[lane:lane + 1, :], CHUNK)
        kdec = k * jnp.exp(gl_rows[:, lane:lane + 1] - gcol)
        return u, w, qg, qkp, kdec

    for pr in range(N_DHEADS // 2):
        h0, h1 = 2 * pr, 2 * pr + 1
        u0, w0, qg0, qkp0, kd0 = head_prep(h0)
        u1, w1, qg1, qkp1, kd1 = head_prep(h1)
        u = jnp.concatenate([u0, u1], axis=1)
        w = jnp.concatenate([w0, w1], axis=1)
        qg = jnp.concatenate([qg0, qg1], axis=1)
        kst = jnp.concatenate(
            [kd[c * CHUNK:(c + 1) * CHUNK] for c in range(nchunk) for kd in (kd0, kd1)], axis=0)
        kdt = kst.T
        s = s_scr[pr]
        vns, o_inter = [], []
        for c in range(nchunk):
            cs = slice(c * CHUNK, (c + 1) * CHUNK)
            s_bd = _pair_diag(s[:, :HEAD_DIM], s[:, HEAD_DIM:])
            r = _mm(jnp.concatenate([w[cs], qg[cs]], axis=0), s_bd)
            vn = u[cs] - r[:CHUNK]
            o_inter.append(r[CHUNK:])
            ge = (c + 1) * CHUNK - 1
            egl = jnp.concatenate(
                [jnp.broadcast_to(jnp.exp(gc[ge:ge + 1, N_DHEADS + hh:N_DHEADS + hh + 1]), (1, HEAD_DIM))
                 for hh in (h0, h1)], axis=1)
            upd = _mm(kdt[:, 2 * c * CHUNK:2 * (c + 1) * CHUNK],
                      _pair_diag(vn[:, :HEAD_DIM], vn[:, HEAD_DIM:]))
            s = s * egl + upd
            vns.append(vn)
        s_scr[pr] = s
        vn_all = jnp.concatenate(vns, axis=0)
        o_int = jnp.concatenate(o_inter, axis=0)
        for hh, qkp in ((h0, qkp0), (h1, qkp1)):
            ls = slice((hh - h0) * HEAD_DIM, (hh - h0 + 1) * HEAD_DIM)
            o = o_int[:, ls] + _mm(pk.block_diag(qkp), vn_all[:, ls])
            o = _rms_rows(o, dng_ref[...])
            obuf[:, hh * HEAD_DIM:(hh + 1) * HEAD_DIM] = (
                o * _silu(rest[:, R_GATE_D + hh * HEAD_DIM:R_GATE_D + (hh + 1) * HEAD_DIM]))

    y = x + jnp.dot(obuf[...].astype(BF16), wout_ref[...], preferred_element_type=F32)
    if final_norm:
        y = _rms_rows(y, fg_ref[...])
    y_ref[0] = y

    @pl.when(i == nt - 1)
    def _():
        for pr in range(N_DHEADS // 2):
            s_out_ref[0, 2 * pr] = s_scr[pr, :, :HEAD_DIM]
            s_out_ref[0, 2 * pr + 1] = s_scr[pr, :, HEAD_DIM:]
        qt_ref[0] = qbuf[tt:tt + SUBLANES, :]
        st_ref[0] = sbuf[tt:tt + SUBLANES, :]
        ct_ref[0] = cbuf[tt:tt + CTAIL, :]

    qbuf[0:SUBLANES, :] = qbuf[tt:tt + SUBLANES, :]
    sbuf[0:SUBLANES, :] = sbuf[tt:tt + SUBLANES, :]
    cbuf[0:CTAIL, :] = cbuf[tt:tt + CTAIL, :]


def _const_spec(shape):
    return pl.BlockSpec(shape, lambda *_: (0,) * len(shape))


def _prompt_layer(x, wts, final_norm):
    b, t, d = x.shape
    nt = t // TIME_TILE
    weight_specs = [_const_spec(w.shape) for w in wts]
    out_shape = (
        jax.ShapeDtypeStruct((b, t, d), F32),
        jax.ShapeDtypeStruct((b, N_DHEADS, HEAD_DIM, HEAD_DIM), F32),
        jax.ShapeDtypeStruct((b, SUBLANES, 3 * D_DELTA), F32),
        jax.ShapeDtypeStruct((b, SUBLANES, D_SCONV), F32),
        jax.ShapeDtypeStruct((b, CTAIL, D_CONF), F32),
    )
    out_specs = (
        pl.BlockSpec((1, TIME_TILE, d), lambda bi, ti: (bi, ti, 0)),
        pl.BlockSpec((1, N_DHEADS, HEAD_DIM, HEAD_DIM), lambda bi, ti: (bi, 0, 0, 0)),
        pl.BlockSpec((1, SUBLANES, 3 * D_DELTA), lambda bi, ti: (bi, 0, 0)),
        pl.BlockSpec((1, SUBLANES, D_SCONV), lambda bi, ti: (bi, 0, 0)),
        pl.BlockSpec((1, CTAIL, D_CONF), lambda bi, ti: (bi, 0, 0)),
    )
    scratch = [
        pltpu.VMEM((TIME_TILE + SUBLANES, 3 * D_DELTA), F32),
        pltpu.VMEM((TIME_TILE + SUBLANES, D_SCONV), F32),
        pltpu.VMEM((TIME_TILE + CTAIL, D_CONF), F32),
        pltpu.VMEM((N_DHEADS // 2, HEAD_DIM, 2 * HEAD_DIM), F32),
        pltpu.VMEM((TIME_TILE, D_MODEL), F32),
        pltpu.VMEM((TIME_TILE, R_WIDTH), F32),
    ]
    return pl.pallas_call(
        functools.partial(_prompt_kernel, final_norm),
        grid=(b, nt),
        in_specs=[pl.BlockSpec((1, TIME_TILE, d), lambda bi, ti: (bi, ti, 0))] + weight_specs,
        out_specs=out_specs,
        out_shape=out_shape,
        scratch_shapes=scratch,
        compiler_params=pltpu.CompilerParams(
            dimension_semantics=("arbitrary", "arbitrary"),
            vmem_limit_bytes=VMEM_LIMIT),
        name="prompt_layer",
    )(x, *wts)


def _sample_kernel(final_norm,
                   x_ref, sd_ref, sq_ref, ss_ref, sc_ref,
                   ng_ref, wqkv_ref, wrest_ref, cw_ref, alog_ref, dt_ref, dng_ref,
                   sw_ref, ccw_ref, ccb_ref, clg_ref, clb_ref, wout_ref, fg_ref,
                   y_ref, sd_out_ref, qt_ref, st_ref, ct_ref,
                   qbuf, sbuf, obuf, ubuf, wqbuf, kdbuf, glbuf, oibuf, rest):
    nb = SAMPLE_BATCH_BLOCK
    pad = SAMPLE_PAD
    rows = nb * pad
    x = x_ref[...].reshape(rows, D_MODEL)
    h = _rms_rows(x, ng_ref[...]).astype(BF16)
    qkv = jnp.dot(h, wqkv_ref[...], preferred_element_type=F32)
    rest[...] = jnp.dot(h, wrest_ref[...], preferred_element_type=F32)
    row = lax.broadcasted_iota(jnp.int32, (rows, LANES), 0)
    valid = (row % pad) < (pad // 2)
    valid1 = valid[:, 0:1]

    qt_ref[...] = qkv.reshape(nb, pad, 3 * D_DELTA)
    qbuf[:, 0:SUBLANES, :] = sq_ref[...]
    qbuf[:, SUBLANES:, :] = qkv.reshape(nb, pad, 3 * D_DELTA)

    hs = rest[:, R_SC:R_SC + D_SCONV] * rest[:, R_SX:R_SX + D_SCONV]
    st_ref[...] = hs.reshape(nb, pad, D_SCONV)
    sbuf[:, 0:SUBLANES, :] = ss_ref[...]
    sbuf[:, SUBLANES:, :] = hs.reshape(nb, pad, D_SCONV)
    ys = _branch_s(sbuf, sw_ref, SUBLANES, pad, lead=(slice(None),))
    obuf[:, D_DELTA:D_DELTA + D_SCONV] = (rest[:, R_SB:R_SB + D_SCONV] * ys.reshape(rows, D_SCONV)
                                          * _silu(rest[:, R_GATE_S:R_GATE_S + D_SCONV]))

    uc = rest[:, R_GA:R_GA + D_CONF] * _sigmoid(rest[:, R_GB:R_GB + D_CONF])
    ct_ref[:, 0:CTAIL, :] = sc_ref[...]
    ct_ref[:, CTAIL:, :] = uc.reshape(nb, pad, D_CONF)
    yc = _conformer(ct_ref, ccw_ref, ccb_ref, clg_ref, clb_ref, CTAIL, pad, lead=(slice(None),))
    obuf[:, D_DELTA + D_SCONV:] = yc.reshape(rows, D_CONF) * _silu(rest[:, R_GATE_C:R_GATE_C + D_CONF])

    ba = rest[:, R_BA:R_BA + LANES]
    beta_all = jnp.where(valid, _sigmoid(ba), 0.0)
    g_all = jnp.where(valid, -jnp.exp(alog_ref[...]) * _softplus(ba + dt_ref[...]), 0.0)
    gc = _chunk_cumsum(g_all, pad, row)
    gct = gc.T
    gl_all = jnp.broadcast_to(
        gc.reshape(nb, pad, LANES)[:, pad - 1:pad, :], (nb, pad, LANES)).reshape(rows, LANES)
    pk = _Packed(pad, rows, False)

    def conv_act(c0):
        acc = None
        for j in range(QK_CONV):
            term = (qbuf[:, pl.ds(SUBLANES - (QK_CONV - 1) + j, pad), c0:c0 + HEAD_DIM]
                    * cw_ref[j:j + 1, c0:c0 + HEAD_DIM].reshape(1, 1, HEAD_DIM))
            acc = term if acc is None else acc + term
        return jnp.where(valid1, _silu(acc.reshape(rows, HEAD_DIM)), 0.0)

    def head_prep(hd):
        q = conv_act(hd * HEAD_DIM)
        k = conv_act(D_DELTA + hd * HEAD_DIM)
        v = conv_act(2 * D_DELTA + hd * HEAD_DIM)
        q = q * lax.rsqrt(jnp.sum(q * q, axis=-1, keepdims=True) + EPS) * (HEAD_DIM ** -0.5)
        k = k * lax.rsqrt(jnp.sum(k * k, axis=-1, keepdims=True) + EPS)
        lane = N_DHEADS + hd
        gcol = gc[:, lane:lane + 1]
        u, w, qg, qkp = pk.delta_prep(q, k, v, beta_all[:, hd:hd + 1], gcol,
                                      gct[lane:lane + 1, :], pad // 2)
        gl = gl_all[:, lane:lane + 1]
        kdec = k * jnp.exp(gl - gcol)
        egl = jnp.broadcast_to(jnp.exp(gl), (rows, HEAD_DIM))
        return u, w, qg, qkp, kdec, egl

    for pr in range(N_DHEADS // 2):
        h0, h1 = 2 * pr, 2 * pr + 1
        u0, w0, qg0, qkp0, kd0, egl0 = head_prep(h0)
        u1, w1, qg1, qkp1, kd1, egl1 = head_prep(h1)
        ubuf[...] = jnp.concatenate([u0, u1], axis=1)
        wqbuf[:, 0:pad, :] = jnp.concatenate([w0, w1], axis=1).reshape(nb, pad, 2 * HEAD_DIM)
        wqbuf[:, pad:, :] = jnp.concatenate([qg0, qg1], axis=1).reshape(nb, pad, 2 * HEAD_DIM)
        kdbuf[:, 0:pad, :] = kd0.reshape(nb, pad, HEAD_DIM)
        kdbuf[:, pad:, :] = kd1.reshape(nb, pad, HEAD_DIM)
        glbuf[...] = jnp.concatenate([egl0, egl1], axis=1).reshape(nb, pad, 2 * HEAD_DIM)

        def body(bi, carry):
            s0 = sd_ref[bi, h0]
            s1 = sd_ref[bi, h1]
            r = _mm(wqbuf[bi], _pair_diag(s0, s1))
            r0 = pl.multiple_of(bi * pad, pad)
            vn = ubuf[pl.ds(r0, pad), :] - r[:pad]
            ubuf[pl.ds(r0, pad), :] = vn
            oibuf[pl.ds(r0, pad), :] = r[pad:]
            upd = _mm_tn(kdbuf[bi], _pair_diag(vn[:, :HEAD_DIM], vn[:, HEAD_DIM:]))
            egl = glbuf[bi][0:1, :]
            sd_out_ref[bi, h0] = s0 * egl[:, :HEAD_DIM] + upd[:, :HEAD_DIM]
            sd_out_ref[bi, h1] = s1 * egl[:, HEAD_DIM:] + upd[:, HEAD_DIM:]
            return carry

        lax.fori_loop(0, nb, body, 0, unroll=2)
        for hh, qkp in ((h0, qkp0), (h1, qkp1)):
            ls = slice((hh - h0) * HEAD_DIM, (hh - h0 + 1) * HEAD_DIM)
            o = oibuf[:, ls] + _mm(pk.block_diag(qkp), ubuf[:, ls])
            o = _rms_rows(o, dng_ref[...])
            obuf[:, hh * HEAD_DIM:(hh + 1) * HEAD_DIM] = (
                o * _silu(rest[:, R_GATE_D + hh * HEAD_DIM:R_GATE_D + (hh + 1) * HEAD_DIM]))

    y = x + jnp.dot(obuf[...].astype(BF16), wout_ref[...], preferred_element_type=F32)
    if final_norm:
        y = _rms_rows(y, fg_ref[...])
    y_ref[...] = y.reshape(nb, pad, D_MODEL)


def _sample_layer(x8, sd, sq8, ss8, sc32, wts, final_norm):
    b = x8.shape[0]
    nb = SAMPLE_BATCH_BLOCK
    weight_specs = [_const_spec(w.shape) for w in wts]

    def bspec(shape):
        return pl.BlockSpec((nb,) + shape, lambda bi: (bi,) + (0,) * len(shape))

    out_shape = (
        jax.ShapeDtypeStruct((b, SAMPLE_PAD, D_MODEL), F32),
        jax.ShapeDtypeStruct((b, N_DHEADS, HEAD_DIM, HEAD_DIM), F32),
        jax.ShapeDtypeStruct((b, SAMPLE_PAD, 3 * D_DELTA), F32),
        jax.ShapeDtypeStruct((b, SAMPLE_PAD, D_SCONV), F32),
        jax.ShapeDtypeStruct((b, CTAIL + SAMPLE_PAD, D_CONF), F32),
    )
    out_specs = (
        bspec((SAMPLE_PAD, D_MODEL)),
        bspec((N_DHEADS, HEAD_DIM, HEAD_DIM)),
        bspec((SAMPLE_PAD, 3 * D_DELTA)),
        bspec((SAMPLE_PAD, D_SCONV)),
        bspec((CTAIL + SAMPLE_PAD, D_CONF)),
    )
    rows = nb * SAMPLE_PAD
    scratch = [
        pltpu.VMEM((nb, 2 * SUBLANES, 3 * D_DELTA), F32),
        pltpu.VMEM((nb, 2 * SUBLANES, D_SCONV), F32),
        pltpu.VMEM((rows, D_MODEL), F32),
        pltpu.VMEM((rows, 2 * HEAD_DIM), F32),
        pltpu.VMEM((nb, 2 * SAMPLE_PAD, 2 * HEAD_DIM), F32),
        pltpu.VMEM((nb, 2 * SAMPLE_PAD, HEAD_DIM), F32),
        pltpu.VMEM((nb, SAMPLE_PAD, 2 * HEAD_DIM), F32),
        pltpu.VMEM((rows, 2 * HEAD_DIM), F32),
        pltpu.VMEM((rows, R_WIDTH), F32),
    ]
    in_specs = [
        bspec((SAMPLE_PAD, D_MODEL)),
        bspec((N_DHEADS, HEAD_DIM, HEAD_DIM)),
        bspec((SUBLANES, 3 * D_DELTA)),
        bspec((SUBLANES, D_SCONV)),
        bspec((CTAIL, D_CONF)),
    ] + weight_specs
    return pl.pallas_call(
        functools.partial(_sample_kernel, final_norm),
        grid=(b // nb,),
        in_specs=in_specs,
        out_specs=out_specs,
        out_shape=out_shape,
        scratch_shapes=scratch,
        compiler_params=pltpu.CompilerParams(
            dimension_semantics=("arbitrary",),
            vmem_limit_bytes=VMEM_LIMIT),
        name="sample_layer",
    )(x8, sd, sq8, ss8, sc32, *wts)


def _layer_weights(l, norm_g, w_in, conv_qkv_w, a_log, dt_bias, delta_norm_g, sconv_w,
                   cconv_w, cconv_b, cln_g, cln_b, w_out, final_norm_g):
    w = w_in[l]
    n_qkv = 3 * D_DELTA
    ba = jnp.pad(w[:, n_qkv:n_qkv + 2 * N_DHEADS], ((0, 0), (0, LANES - 2 * N_DHEADS)))
    wrest = jnp.concatenate([w[:, n_qkv + 2 * N_DHEADS:], ba], axis=1).astype(BF16)
    lane_pad = (N_DHEADS, LANES - 2 * N_DHEADS)
    return (
        norm_g[l].reshape(1, D_MODEL),
        w[:, :n_qkv].astype(BF16),
        wrest,
        conv_qkv_w[l],
        jnp.pad(a_log[l], lane_pad).reshape(1, LANES),
        jnp.pad(dt_bias[l], lane_pad).reshape(1, LANES),
        delta_norm_g[l].reshape(1, HEAD_DIM),
        sconv_w[l],
        cconv_w[l],
        cconv_b[l].reshape(1, D_CONF),
        cln_g[l].reshape(1, D_CONF),
        cln_b[l].reshape(1, D_CONF),
        w_out[l].astype(BF16),
        final_norm_g.reshape(1, D_MODEL),
    )


def kernel(x_prompt, x_sample, state_delta, state_qkv_conv, state_sconv, state_cconv, norm_g, w_in, conv_qkv_w, a_log, dt_bias, delta_norm_g, sconv_w, cconv_w, cconv_b, cln_g, cln_b, w_out, final_norm_g):
    depth = w_in.shape[0]
    dec_seq = x_sample.shape[1]
    assert x_prompt.shape[1] % TIME_TILE == 0
    assert dec_seq == SAMPLE_PAD // 2 and x_sample.shape[0] % SAMPLE_BATCH_BLOCK == 0

    xp = x_prompt
    xs = jnp.pad(x_sample, ((0, 0), (0, SAMPLE_PAD - dec_seq), (0, 0)))
    outs = [[] for _ in range(8)]
    for l in range(depth):
        wts = _layer_weights(l, norm_g, w_in, conv_qkv_w, a_log, dt_bias, delta_norm_g, sconv_w,
                             cconv_w, cconv_b, cln_g, cln_b, w_out, final_norm_g)
        last = l == depth - 1
        xp, pd, pq, ps, pc = _prompt_layer(xp, wts, last)
        outs[0].append(pd)
        outs[1].append(pq[:, SUBLANES - (QK_CONV - 1):, :])
        outs[2].append(ps[:, SUBLANES - (SCONV_W - 1):, :])
        outs[3].append(pc[:, CTAIL - (CONF_W - 1):, :])
        sq8 = jnp.pad(state_qkv_conv[l], ((0, 0), (SUBLANES - (QK_CONV - 1), 0), (0, 0)))
        ss8 = jnp.pad(state_sconv[l], ((0, 0), (SUBLANES - (SCONV_W - 1), 0), (0, 0)))
        sc32 = jnp.pad(state_cconv[l], ((0, 0), (CTAIL - (CONF_W - 1), 0), (0, 0)))
        xs, sd, sq, ss, sc = _sample_layer(xs, state_delta[l], sq8, ss8, sc32, wts, last)
        outs[4].append(sd)
        outs[5].append(sq[:, dec_seq - (QK_CONV - 1):dec_seq, :])
        outs[6].append(ss[:, dec_seq - (SCONV_W - 1):dec_seq, :])
        first = CTAIL + dec_seq - (CONF_W - 1)
        outs[7].append(sc[:, first:first + CONF_W - 1, :])
    return (xp, xs[:, :dec_seq, :]) + tuple(jnp.stack(o) for o in outs)
```

```python
import functools

import jax
import jax.numpy as jnp
from jax import lax
from jax.experimental import pallas as pl
from jax.experimental.pallas import tpu as pltpu

D_MODEL = 1024
N_DHEADS = 4
HEAD_DIM = 128
D_DELTA = N_DHEADS * HEAD_DIM
D_SCONV = 256
D_CONF = 256
QK_CONV = 4
SCONV_W = 3
CONF_W = 31
CHUNK = 64
EPS = 1e-6

R_GATE_D = 0
R_SB = 512
R_SC = 768
R_SX = 1024
R_GATE_S = 1280
R_GA = 1536
R_GB = 1792
R_GATE_C = 2048
R_BA = 2304
R_WIDTH = 2432

SUBLANES = 8
LANES = 128
TIME_TILE = 256
SAMPLE_PAD = 8
SAMPLE_BATCH_BLOCK = 16
CTAIL = 32
VMEM_LIMIT = 56 * 1024 * 1024

F32 = jnp.float32
BF16 = jnp.bfloat16


def _mm(a, b):
    return jnp.dot(a.astype(BF16), b.astype(BF16), preferred_element_type=F32)


def _mm_nt(a, b):
    return lax.dot_general(a.astype(BF16), b.astype(BF16), (((1,), (1,)), ((), ())),
                           preferred_element_type=F32)


def _mm_tn(a, b):
    return lax.dot_general(a.astype(BF16), b.astype(BF16), (((0,), (0,)), ((), ())),
                           preferred_element_type=F32)


def _sigmoid(x):
    return 1.0 / (1.0 + jnp.exp(-x))


def _silu(x):
    return x * _sigmoid(x)


def _softplus(x):
    return jnp.maximum(x, 0.0) + jnp.log1p(jnp.exp(-jnp.abs(x)))


def _rms_rows(x, g):
    return x * lax.rsqrt(jnp.mean(x * x, axis=-1, keepdims=True) + EPS) * g


def _chunk_cumsum(g, chunk, row):
    pos = row % chunk
    s = 1
    while s < chunk:
        g = g + jnp.where(pos >= s, pltpu.roll(g, s, axis=0), 0.0)
        s *= 2
    return g


def _pair_diag(a, b):
    z = jnp.zeros_like(a)
    return jnp.concatenate([jnp.concatenate([a, z], axis=1), jnp.concatenate([z, b], axis=1)], axis=0)


class _Packed:
    def __init__(self, chunk, rows, bf16_tile):
        self.chunk, self.rows, self.n = chunk, rows, rows // chunk
        self.rr = lax.broadcasted_iota(jnp.int32, (chunk, rows), 0)
        lane = lax.broadcasted_iota(jnp.int32, (chunk, rows), 1)
        self.jl = lane % chunk
        self.lane_blk = lane // chunk
        ii = lax.broadcasted_iota(jnp.int32, (rows, rows), 0)
        jj = lax.broadcasted_iota(jnp.int32, (rows, rows), 1)
        mask = jnp.where((ii // chunk) == (jj // chunk), 1.0, 0.0).astype(F32)
        self.bd_mask = mask.astype(BF16) if bf16_tile else mask

    def pack(self, g):
        out = g[0:self.chunk]
        for c in range(1, self.n):
            out = jnp.where(self.lane_blk == c, g[c * self.chunk:(c + 1) * self.chunk], out)
        return out

    def col(self, v):
        shape = (self.chunk, self.rows)
        out = jnp.broadcast_to(v[0:self.chunk], shape)
        for c in range(1, self.n):
            out = jnp.where(self.lane_blk == c,
                            jnp.broadcast_to(v[c * self.chunk:(c + 1) * self.chunk], shape), out)
        return out

    def block_diag(self, xp):
        if self.bd_mask.dtype == BF16:
            return jnp.concatenate([xp.astype(BF16)] * self.n, axis=0) * self.bd_mask
        return (jnp.concatenate([xp] * self.n, axis=0) * self.bd_mask).astype(BF16)

    def unit_lower_inverse(self, lps, nil):
        base = min(self.chunk, 16)
        same = (self.rr // base) == (self.jl // base)
        eye = jnp.where(self.rr == self.jl, 1.0, 0.0).astype(F32)
        ds = [jnp.where(same, lp, 0.0) for lp in lps]
        xs = [eye - d for d in ds]
        d_bds = [self.block_diag(d) for d in ds]
        p = 2
        while p < min(base, nil):
            ds = [_mm(d, d_bd) for d, d_bd in zip(ds, d_bds)]
            d_bds = [self.block_diag(d) for d in ds]
            xs = [x + _mm(x, d_bd) for x, d_bd in zip(xs, d_bds)]
            p *= 2
        size = base
        while size < self.chunk:
            big = (self.rr // (2 * size)) == (self.jl // (2 * size))
            off = jnp.logical_and(big, jnp.logical_not(same))
            xes = [_mm(x, self.block_diag(jnp.where(off, lp, 0.0))) for x, lp in zip(xs, lps)]
            xs = [x - _mm(xe, self.block_diag(x)) for x, xe in zip(xs, xes)]
            same = big
            size *= 2
        return xs

    def delta_prep(self, heads, nil):
        tril = self.rr >= self.jl
        strict = self.rr > self.jl
        decays, kbs, gs = [], [], []
        for q, k, v, beta, gc_col, gc_row in heads:
            diff = self.col(gc_col) - gc_row
            decays.append(jnp.where(tril, jnp.exp(jnp.where(tril, diff, 0.0)), 0.0))
            kbs.append(k * beta)
        for (q, k, *_), kb in zip(heads, kbs):
            gs.append(_mm_nt(jnp.concatenate([kb, q], axis=0), k))
        lps = [jnp.where(strict, self.pack(g[:self.rows]) * dec, 0.0) for g, dec in zip(gs, decays)]
        qkps = [self.pack(g[self.rows:]) * dec for g, dec in zip(gs, decays)]
        tinvs = self.unit_lower_inverse(lps, nil)
        out = []
        for (q, k, v, beta, gc_col, _), kb, tinv, qkp in zip(heads, kbs, tinvs, qkps):
            eg = jnp.exp(gc_col)
            uw = _mm(self.block_diag(tinv), jnp.concatenate([v * beta, kb * eg], axis=1))
            out.append((uw[:, :HEAD_DIM], uw[:, HEAD_DIM:], q * eg, qkp))
        return out


def _branch_s(buf_ref, w_ref, tail, rows, lead=()):
    n = len(lead)
    acc = None
    for j in range(SCONV_W):
        idx = lead + (pl.ds(tail - (SCONV_W - 1) + j, rows), slice(None))
        term = buf_ref[idx] * w_ref[j:j + 1, :].reshape((1,) * n + (1, D_SCONV))
        acc = term if acc is None else acc + term
    return acc


def _conformer(ubuf_ref, w_ref, b_ref, g_ref, beta_ref, tail, rows, lead=()):
    n = len(lead)
    shp = (1,) * n + (1, D_CONF)
    acc = None
    for j in range(CONF_W):
        idx = lead + (pl.ds(tail - (CONF_W - 1) + j, rows), slice(None))
        term = ubuf_ref[idx] * w_ref[j:j + 1, :].reshape(shp)
        acc = term if acc is None else acc + term
    yc = acc + b_ref[...].reshape(shp)
    mu = jnp.mean(yc, axis=-1, keepdims=True)
    var = jnp.mean(jnp.square(yc - mu), axis=-1, keepdims=True)
    yc = (yc - mu) * lax.rsqrt(var + EPS) * g_ref[...].reshape(shp) + beta_ref[...].reshape(shp)
    return _silu(yc)


def _prompt_kernel(final_norm,
                   x_ref, ng_ref, wqkv_ref, wrest_ref, cw_ref, alog_ref, dt_ref, dng_ref,
                   sw_ref, ccw_ref, ccb_ref, clg_ref, clb_ref, wout_ref, fg_ref,
                   y_ref, s_out_ref, qt_ref, st_ref, ct_ref,
                   qbuf, sbuf, cbuf, cshift, s_scr, obuf, rest,
                   u_s, lhs_s, kdt_s, egl_s, oi_s, vn_s, qk_s):
    i = pl.program_id(1)
    nt = pl.num_programs(1)
    tt = TIME_TILE
    nchunk = tt // CHUNK

    @pl.when(i == 0)
    def _():
        qbuf[0:SUBLANES, :] = jnp.zeros((SUBLANES, 3 * D_DELTA), F32)
        sbuf[0:SUBLANES, :] = jnp.zeros((SUBLANES, D_SCONV), F32)
        cbuf[0:CTAIL, :] = jnp.zeros((CTAIL, D_CONF), F32)
        s_scr[...] = jnp.zeros(s_scr.shape, F32)

    x = x_ref[0]
    h = _rms_rows(x, ng_ref[...]).astype(BF16)
    qbuf[SUBLANES:SUBLANES + tt, :] = jnp.dot(h, wqkv_ref[...], preferred_element_type=F32)
    rest[...] = jnp.dot(h, wrest_ref[...], preferred_element_type=F32)

    sbuf[SUBLANES:SUBLANES + tt, :] = rest[:, R_SC:R_SC + D_SCONV] * rest[:, R_SX:R_SX + D_SCONV]
    ys = _branch_s(sbuf, sw_ref, SUBLANES, tt)
    obuf[:, D_DELTA:D_DELTA + D_SCONV] = (rest[:, R_SB:R_SB + D_SCONV] * ys
                                          * _silu(rest[:, R_GATE_S:R_GATE_S + D_SCONV]))

    cbuf[CTAIL:CTAIL + tt, :] = rest[:, R_GA:R_GA + D_CONF] * _sigmoid(rest[:, R_GB:R_GB + D_CONF])
    for sh in range(1, SUBLANES):
        cshift[sh - 1] = cbuf[pl.ds(sh, tt + CTAIL - SUBLANES), :]

    row = lax.broadcasted_iota(jnp.int32, (tt, LANES), 0)
    ba = rest[:, R_BA:R_BA + LANES]
    beta_all = _sigmoid(ba)
    g_all = -jnp.exp(alog_ref[...]) * _softplus(ba + dt_ref[...])
    gc = _chunk_cumsum(g_all, CHUNK, row)
    gct = gc.T
    gl_rows = jnp.concatenate(
        [jnp.broadcast_to(gc[(c + 1) * CHUNK - 1:(c + 1) * CHUNK, :], (CHUNK, LANES))
         for c in range(nchunk)], axis=0)
    pk = _Packed(CHUNK, tt, True)

    def conv_act(c0):
        acc = None
        for j in range(QK_CONV):
            term = (qbuf[pl.ds(SUBLANES - (QK_CONV - 1) + j, tt), c0:c0 + HEAD_DIM]
                    * cw_ref[j:j + 1, c0:c0 + HEAD_DIM])
            acc = term if acc is None else acc + term
        return _silu(acc)

    heads, kdecs = [], []
    for hd in range(N_DHEADS):
        q = conv_act(hd * HEAD_DIM)
        k = conv_act(D_DELTA + hd * HEAD_DIM)
        v = conv_act(2 * D_DELTA + hd * HEAD_DIM)
        q = q * lax.rsqrt(jnp.sum(q * q, axis=-1, keepdims=True) + EPS) * (HEAD_DIM ** -0.5)
        k = k * lax.rsqrt(jnp.sum(k * k, axis=-1, keepdims=True) + EPS)
        lane = N_DHEADS + hd
        gcol = gc[:, lane:lane + 1]
        heads.append((q, k, v, beta_all[:, hd:hd + 1], gcol, gct[lane:lane + 1, :]))
        kdecs.append(k * jnp.exp(gl_rows[:, lane:lane + 1] - gcol))
    prepped = pk.delta_prep(heads, CHUNK)

    for pr in range(N_DHEADS // 2):
        h0, h1 = 2 * pr, 2 * pr + 1
        (u0, w0, qg0, qkp0), (u1, w1, qg1, qkp1) = prepped[h0], prepped[h1]
        kd0, kd1 = kdecs[h0], kdecs[h1]
        u_s[pr] = jnp.concatenate([u0, u1], axis=1)
        w = jnp.concatenate([w0, w1], axis=1).astype(BF16)
        qg = jnp.concatenate([qg0, qg1], axis=1).astype(BF16)
        kst = jnp.concatenate(
            [kd[c * CHUNK:(c + 1) * CHUNK] for c in range(nchunk) for kd in (kd0, kd1)], axis=0)
        kdt = kst.T.astype(BF16)
        qk_s[h0] = qkp0
        qk_s[h1] = qkp1
        for c in range(nchunk):
            cs = slice(c * CHUNK, (c + 1) * CHUNK)
            lhs_s[pr, c, 0:CHUNK, :] = w[cs]
            lhs_s[pr, c, CHUNK:, :] = qg[cs]
            kdt_s[pr, c] = kdt[:, 2 * c * CHUNK:2 * (c + 1) * CHUNK]
            ge = (c + 1) * CHUNK - 1
            egl = jnp.concatenate(
                [jnp.broadcast_to(jnp.exp(gc[ge:ge + 1, N_DHEADS + hh:N_DHEADS + hh + 1]),
                                  (SUBLANES, HEAD_DIM)) for hh in (h0, h1)], axis=1)
            egl_s[c, :, 2 * pr * HEAD_DIM:2 * (pr + 1) * HEAD_DIM] = egl

    def chunk_body(c, carry):
        r0 = pl.multiple_of(c * CHUNK, CHUNK)
        rows_c = pl.ds(r0, CHUNK)
        pairs = range(N_DHEADS // 2)
        ss = [s_scr[pr] for pr in pairs]
        rs = [jnp.dot(lhs_s[pr, c], _pair_diag(s[:, :HEAD_DIM], s[:, HEAD_DIM:]).astype(BF16),
                      preferred_element_type=F32) for pr, s in zip(pairs, ss)]
        vns = [u_s[pr, rows_c, :] - r[:CHUNK] for pr, r in zip(pairs, rs)]
        upds = [jnp.dot(kdt_s[pr, c], _pair_diag(vn[:, :HEAD_DIM], vn[:, HEAD_DIM:]).astype(BF16),
                        preferred_element_type=F32) for pr, vn in zip(pairs, vns)]
        for pr in pairs:
            oi_s[pr, rows_c, :] = rs[pr][CHUNK:]
            vn_s[pr, rows_c, :] = vns[pr]
            egl = egl_s[c][0:1, 2 * pr * HEAD_DIM:2 * (pr + 1) * HEAD_DIM]
            s_scr[pr] = ss[pr] * egl + upds[pr]
        acc = None
        for j in range(CONF_W):
            off = CTAIL - (CONF_W - 1) + j
            start = pl.multiple_of(r0 + (off // SUBLANES) * SUBLANES, SUBLANES)
            sh = off % SUBLANES
            src = cbuf[pl.ds(start, CHUNK), :] if sh == 0 else cshift[sh - 1, pl.ds(start, CHUNK), :]
            term = src * ccw_ref[j:j + 1, :]
            acc = term if acc is None else acc + term
        yc = acc + ccb_ref[...]
        mu = jnp.mean(yc, axis=-1, keepdims=True)
        var = jnp.mean(jnp.square(yc - mu), axis=-1, keepdims=True)
        yc = (yc - mu) * lax.rsqrt(var + EPS) * clg_ref[...] + clb_ref[...]
        obuf[rows_c, D_DELTA + D_SCONV:] = _silu(yc) * _silu(rest[rows_c, R_GATE_C:R_GATE_C + D_CONF])
        return carry

    lax.fori_loop(0, nchunk, chunk_body, 0)

    pk = _Packed(CHUNK, tt, True)
    for hd in range(N_DHEADS):
        pr = hd // 2
        ls = slice((hd % 2) * HEAD_DIM, (hd % 2 + 1) * HEAD_DIM)
        o = oi_s[pr, :, ls] + _mm(pk.block_diag(qk_s[hd]), vn_s[pr, :, ls])
        o = _rms_rows(o, dng_ref[...])
        obuf[:, hd * HEAD_DIM:(hd + 1) * HEAD_DIM] = (
            o * _silu(rest[:, R_GATE_D + hd * HEAD_DIM:R_GATE_D + (hd + 1) * HEAD_DIM]))

    y = x_ref[0] + jnp.dot(obuf[...].astype(BF16), wout_ref[...], preferred_element_type=F32)
    if final_norm:
        y = _rms_rows(y, fg_ref[...])
    y_ref[0] = y

    @pl.when(i == nt - 1)
    def _():
        for pr in range(N_DHEADS // 2):
            s_out_ref[0, 2 * pr] = s_scr[pr, :, :HEAD_DIM]
            s_out_ref[0, 2 * pr + 1] = s_scr[pr, :, HEAD_DIM:]
        qt_ref[0] = qbuf[tt:tt + SUBLANES, :]
        st_ref[0] = sbuf[tt:tt + SUBLANES, :]
        ct_ref[0] = cbuf[tt:tt + CTAIL, :]

    qbuf[0:SUBLANES, :] = qbuf[tt:tt + SUBLANES, :]
    sbuf[0:SUBLANES, :] = sbuf[tt:tt + SUBLANES, :]
    cbuf[0:CTAIL, :] = cbuf[tt:tt + CTAIL, :]


def _const_spec(shape):
    return pl.BlockSpec(shape, lambda *_: (0,) * len(shape))


def _prompt_layer(x, wts, final_norm):
    b, t, d = x.shape
    nt = t // TIME_TILE
    npair = N_DHEADS // 2
    nchunk = TIME_TILE // CHUNK
    weight_specs = [_const_spec(w.shape) for w in wts]
    out_shape = (
        jax.ShapeDtypeStruct((b, t, d), F32),
        jax.ShapeDtypeStruct((b, N_DHEADS, HEAD_DIM, HEAD_DIM), F32),
        jax.ShapeDtypeStruct((b, SUBLANES, 3 * D_DELTA), F32),
        jax.ShapeDtypeStruct((b, SUBLANES, D_SCONV), F32),
        jax.ShapeDtypeStruct((b, CTAIL, D_CONF), F32),
    )
    out_specs = (
        pl.BlockSpec((1, TIME_TILE, d), lambda bi, ti: (bi, ti, 0)),
        pl.BlockSpec((1, N_DHEADS, HEAD_DIM, HEAD_DIM), lambda bi, ti: (bi, 0, 0, 0)),
        pl.BlockSpec((1, SUBLANES, 3 * D_DELTA), lambda bi, ti: (bi, 0, 0)),
        pl.BlockSpec((1, SUBLANES, D_SCONV), lambda bi, ti: (bi, 0, 0)),
        pl.BlockSpec((1, CTAIL, D_CONF), lambda bi, ti: (bi, 0, 0)),
    )
    scratch = [
        pltpu.VMEM((TIME_TILE + SUBLANES, 3 * D_DELTA), F32),
        pltpu.VMEM((TIME_TILE + SUBLANES, D_SCONV), F32),
        pltpu.VMEM((TIME_TILE + CTAIL, D_CONF), F32),
        pltpu.VMEM((SUBLANES - 1, TIME_TILE + CTAIL - SUBLANES, D_CONF), F32),
        pltpu.VMEM((npair, HEAD_DIM, 2 * HEAD_DIM), F32),
        pltpu.VMEM((TIME_TILE, D_MODEL), F32),
        pltpu.VMEM((TIME_TILE, R_WIDTH), F32),
        pltpu.VMEM((npair, TIME_TILE, 2 * HEAD_DIM), F32),
        pltpu.VMEM((npair, nchunk, 2 * CHUNK, 2 * HEAD_DIM), BF16),
        pltpu.VMEM((npair, nchunk, HEAD_DIM, 2 * CHUNK), BF16),
        pltpu.VMEM((nchunk, SUBLANES, N_DHEADS * HEAD_DIM), F32),
        pltpu.VMEM((npair, TIME_TILE, 2 * HEAD_DIM), F32),
        pltpu.VMEM((npair, TIME_TILE, 2 * HEAD_DIM), F32),
        pltpu.VMEM((N_DHEADS, CHUNK, TIME_TILE), F32),
    ]
    return pl.pallas_call(
        functools.partial(_prompt_kernel, final_norm),
        grid=(b, nt),
        in_specs=[pl.BlockSpec((1, TIME_TILE, d), lambda bi, ti: (bi, ti, 0))] + weight_specs,
        out_specs=out_specs,
        out_shape=out_shape,
        scratch_shapes=scratch,
        compiler_params=pltpu.CompilerParams(
            dimension_semantics=("arbitrary", "arbitrary"),
            vmem_limit_bytes=VMEM_LIMIT),
        name="prompt_layer",
    )(x, *wts)


def _sample_kernel(final_norm,
                   x_ref, sd_ref, sq_ref, ss_ref, sc_ref,
                   ng_ref, wqkv_ref, wrest_ref, cw_ref, alog_ref, dt_ref, dng_ref,
                   sw_ref, ccw_ref, ccb_ref, clg_ref, clb_ref, wout_ref, fg_ref,
                   y_ref, sd_out_ref, qt_ref, st_ref, ct_ref,
                   qbuf, sbuf, obuf, ubuf, wqbuf, kdbuf, glbuf, oibuf, rest):
    nb = SAMPLE_BATCH_BLOCK
    pad = SAMPLE_PAD
    rows = nb * pad
    x = x_ref[...].reshape(rows, D_MODEL)
    h = _rms_rows(x, ng_ref[...]).astype(BF16)
    qkv = jnp.dot(h, wqkv_ref[...], preferred_element_type=F32)
    rest[...] = jnp.dot(h, wrest_ref[...], preferred_element_type=F32)
    row = lax.broadcasted_iota(jnp.int32, (rows, LANES), 0)
    valid = (row % pad) < (pad // 2)
    valid1 = valid[:, 0:1]

    qt_ref[...] = qkv.reshape(nb, pad, 3 * D_DELTA)
    qbuf[:, 0:SUBLANES, :] = sq_ref[...]
    qbuf[:, SUBLANES:, :] = qkv.reshape(nb, pad, 3 * D_DELTA)

    hs = rest[:, R_SC:R_SC + D_SCONV] * rest[:, R_SX:R_SX + D_SCONV]
    st_ref[...] = hs.reshape(nb, pad, D_SCONV)
    sbuf[:, 0:SUBLANES, :] = ss_ref[...]
    sbuf[:, SUBLANES:, :] = hs.reshape(nb, pad, D_SCONV)
    ys = _branch_s(sbuf, sw_ref, SUBLANES, pad, lead=(slice(None),))
    obuf[:, D_DELTA:D_DELTA + D_SCONV] = (rest[:, R_SB:R_SB + D_SCONV] * ys.reshape(rows, D_SCONV)
                                          * _silu(rest[:, R_GATE_S:R_GATE_S + D_SCONV]))

    uc = rest[:, R_GA:R_GA + D_CONF] * _sigmoid(rest[:, R_GB:R_GB + D_CONF])
    ct_ref[:, 0:CTAIL, :] = sc_ref[...]
    ct_ref[:, CTAIL:, :] = uc.reshape(nb, pad, D_CONF)
    yc = _conformer(ct_ref, ccw_ref, ccb_ref, clg_ref, clb_ref, CTAIL, pad, lead=(slice(None),))
    obuf[:, D_DELTA + D_SCONV:] = yc.reshape(rows, D_CONF) * _silu(rest[:, R_GATE_C:R_GATE_C + D_CONF])

    ba = rest[:, R_BA:R_BA + LANES]
    beta_all = jnp.where(valid, _sigmoid(ba), 0.0)
    g_all = jnp.where(valid, -jnp.exp(alog_ref[...]) * _softplus(ba + dt_ref[...]), 0.0)
    gc = _chunk_cumsum(g_all, pad, row)
    gct = gc.T
    gl_all = jnp.broadcast_to(
        gc.reshape(nb, pad, LANES)[:, pad - 1:pad, :], (nb, pad, LANES)).reshape(rows, LANES)
    pk = _Packed(pad, rows, False)

    def conv_act(c0):
        acc = None
        for j in range(QK_CONV):
            term = (qbuf[:, pl.ds(SUBLANES - (QK_CONV - 1) + j, pad), c0:c0 + HEAD_DIM]
                    * cw_ref[j:j + 1, c0:c0 + HEAD_DIM].reshape(1, 1, HEAD_DIM))
            acc = term if acc is None else acc + term
        return jnp.where(valid1, _silu(acc.reshape(rows, HEAD_DIM)), 0.0)

    for pr in range(N_DHEADS // 2):
        h0, h1 = 2 * pr, 2 * pr + 1
        heads, kdecs, egls = [], [], []
        for hd in (h0, h1):
            q = conv_act(hd * HEAD_DIM)
            k = conv_act(D_DELTA + hd * HEAD_DIM)
            v = conv_act(2 * D_DELTA + hd * HEAD_DIM)
            q = q * lax.rsqrt(jnp.sum(q * q, axis=-1, keepdims=True) + EPS) * (HEAD_DIM ** -0.5)
            k = k * lax.rsqrt(jnp.sum(k * k, axis=-1, keepdims=True) + EPS)
            lane = N_DHEADS + hd
            gcol = gc[:, lane:lane + 1]
            heads.append((q, k, v, beta_all[:, hd:hd + 1], gcol, gct[lane:lane + 1, :]))
            gl = gl_all[:, lane:lane + 1]
            kdecs.append(k * jnp.exp(gl - gcol))
            egls.append(jnp.broadcast_to(jnp.exp(gl), (rows, HEAD_DIM)))
        (u0, w0, qg0, qkp0), (u1, w1, qg1, qkp1) = pk.delta_prep(heads, pad // 2)
        (kd0, kd1), (egl0, egl1) = kdecs, egls
        ubuf[...] = jnp.concatenate([u0, u1], axis=1)
        wqbuf[:, 0:pad, :] = jnp.concatenate([w0, w1], axis=1).reshape(nb, pad, 2 * HEAD_DIM)
        wqbuf[:, pad:, :] = jnp.concatenate([qg0, qg1], axis=1).reshape(nb, pad, 2 * HEAD_DIM)
        kdbuf[:, 0:pad, :] = kd0.reshape(nb, pad, HEAD_DIM)
        kdbuf[:, pad:, :] = kd1.reshape(nb, pad, HEAD_DIM)
        glbuf[...] = jnp.concatenate([egl0, egl1], axis=1).reshape(nb, pad, 2 * HEAD_DIM)

        def body(bi, carry):
            s0 = sd_ref[bi, h0]
            s1 = sd_ref[bi, h1]
            r = _mm(wqbuf[bi], _pair_diag(s0, s1))
            r0 = pl.multiple_of(bi * pad, pad)
            vn = ubuf[pl.ds(r0, pad), :] - r[:pad]
            ubuf[pl.ds(r0, pad), :] = vn
            oibuf[pl.ds(r0, pad), :] = r[pad:]
            upd = _mm_tn(kdbuf[bi], _pair_diag(vn[:, :HEAD_DIM], vn[:, HEAD_DIM:]))
            egl = glbuf[bi][0:1, :]
            sd_out_ref[bi, h0] = s0 * egl[:, :HEAD_DIM] + upd[:, :HEAD_DIM]
            sd_out_ref[bi, h1] = s1 * egl[:, HEAD_DIM:] + upd[:, HEAD_DIM:]
            return carry

        lax.fori_loop(0, nb, body, 0, unroll=2)
        for hh, qkp in ((h0, qkp0), (h1, qkp1)):
            ls = slice((hh - h0) * HEAD_DIM, (hh - h0 + 1) * HEAD_DIM)
            o = oibuf[:, ls] + _mm(pk.block_diag(qkp), ubuf[:, ls])
            o = _rms_rows(o, dng_ref[...])
            obuf[:, hh * HEAD_DIM:(hh + 1) * HEAD_DIM] = (
                o * _silu(rest[:, R_GATE_D + hh * HEAD_DIM:R_GATE_D + (hh + 1) * HEAD_DIM]))

    y = x + jnp.dot(obuf[...].astype(BF16), wout_ref[...], preferred_element_type=F32)
    if final_norm:
        y = _rms_rows(y, fg_ref[...])
    y_ref[...] = y.reshape(nb, pad, D_MODEL)


def _sample_layer(x8, sd, sq8, ss8, sc32, wts, final_norm):
    b = x8.shape[0]
    nb = SAMPLE_BATCH_BLOCK
    weight_specs = [_const_spec(w.shape) for w in wts]

    def bspec(shape):
        return pl.BlockSpec((nb,) + shape, lambda bi: (bi,) + (0,) * len(shape))

    out_shape = (
        jax.ShapeDtypeStruct((b, SAMPLE_PAD, D_MODEL), F32),
        jax.ShapeDtypeStruct((b, N_DHEADS, HEAD_DIM, HEAD_DIM), F32),
        jax.ShapeDtypeStruct((b, SAMPLE_PAD, 3 * D_DELTA), F32),
        jax.ShapeDtypeStruct((b, SAMPLE_PAD, D_SCONV), F32),
        jax.ShapeDtypeStruct((b, CTAIL + SAMPLE_PAD, D_CONF), F32),
    )
    out_specs = (
        bspec((SAMPLE_PAD, D_MODEL)),
        bspec((N_DHEADS, HEAD_DIM, HEAD_DIM)),
        bspec((SAMPLE_PAD, 3 * D_DELTA)),
        bspec((SAMPLE_PAD, D_SCONV)),
        bspec((CTAIL + SAMPLE_PAD, D_CONF)),
    )
    rows = nb * SAMPLE_PAD
    scratch = [
        pltpu.VMEM((nb, 2 * SUBLANES, 3 * D_DELTA), F32),
        pltpu.VMEM((nb, 2 * SUBLANES, D_SCONV), F32),
        pltpu.VMEM((rows, D_MODEL), F32),
        pltpu.VMEM((rows, 2 * HEAD_DIM), F32),
        pltpu.VMEM((nb, 2 * SAMPLE_PAD, 2 * HEAD_DIM), F32),
        pltpu.VMEM((nb, 2 * SAMPLE_PAD, HEAD_DIM), F32),
        pltpu.VMEM((nb, SAMPLE_PAD, 2 * HEAD_DIM), F32),
        pltpu.VMEM((rows, 2 * HEAD_DIM), F32),
        pltpu.VMEM((rows, R_WIDTH), F32),
    ]
    in_specs = [
        bspec((SAMPLE_PAD, D_MODEL)),
        bspec((N_DHEADS, HEAD_DIM, HEAD_DIM)),
        bspec((SUBLANES, 3 * D_DELTA)),
        bspec((SUBLANES, D_SCONV)),
        bspec((CTAIL, D_CONF)),
    ] + weight_specs
    return pl.pallas_call(
        functools.partial(_sample_kernel, final_norm),
        grid=(b // nb,),
        in_specs=in_specs,
        out_specs=out_specs,
        out_shape=out_shape,
        scratch_shapes=scratch,
        compiler_params=pltpu.CompilerParams(
            dimension_semantics=("arbitrary",),
            vmem_limit_bytes=VMEM_LIMIT),
        name="sample_layer",
    )(x8, sd, sq8, ss8, sc32, *wts)


def _layer_weights(l, norm_g, w_in, conv_qkv_w, a_log, dt_bias, delta_norm_g, sconv_w,
                   cconv_w, cconv_b, cln_g, cln_b, w_out, final_norm_g):
    w = w_in[l]
    n_qkv = 3 * D_DELTA
    ba = jnp.pad(w[:, n_qkv:n_qkv + 2 * N_DHEADS], ((0, 0), (0, LANES - 2 * N_DHEADS)))
    wrest = jnp.concatenate([w[:, n_qkv + 2 * N_DHEADS:], ba], axis=1).astype(BF16)
    lane_pad = (N_DHEADS, LANES - 2 * N_DHEADS)
    return (
        norm_g[l].reshape(1, D_MODEL),
        w[:, :n_qkv].astype(BF16),
        wrest,
        conv_qkv_w[l],
        jnp.pad(a_log[l], lane_pad).reshape(1, LANES),
        jnp.pad(dt_bias[l], lane_pad).reshape(1, LANES),
        delta_norm_g[l].reshape(1, HEAD_DIM),
        sconv_w[l],
        cconv_w[l],
        cconv_b[l].reshape(1, D_CONF),
        cln_g[l].reshape(1, D_CONF),
        cln_b[l].reshape(1, D_CONF),
        w_out[l].astype(BF16),
        final_norm_g.reshape(1, D_MODEL),
    )


def kernel(x_prompt, x_sample, state_delta, state_qkv_conv, state_sconv, state_cconv, norm_g, w_in, conv_qkv_w, a_log, dt_bias, delta_norm_g, sconv_w, cconv_w, cconv_b, cln_g, cln_b, w_out, final_norm_g):
    depth = w_in.shape[0]
    dec_seq = x_sample.shape[1]
    assert x_prompt.shape[1] % TIME_TILE == 0
    assert dec_seq == SAMPLE_PAD // 2 and x_sample.shape[0] % SAMPLE_BATCH_BLOCK == 0

    xp = x_prompt
    xs = jnp.pad(x_sample, ((0, 0), (0, SAMPLE_PAD - dec_seq), (0, 0)))
    outs = [[] for _ in range(8)]
    for l in range(depth):
        wts = _layer_weights(l, norm_g, w_in, conv_qkv_w, a_log, dt_bias, delta_norm_g, sconv_w,
                             cconv_w, cconv_b, cln_g, cln_b, w_out, final_norm_g)
        last = l == depth - 1
        xp, pd, pq, ps, pc = _prompt_layer(xp, wts, last)
        outs[0].append(pd)
        outs[1].append(pq[:, SUBLANES - (QK_CONV - 1):, :])
        outs[2].append(ps[:, SUBLANES - (SCONV_W - 1):, :])
        outs[3].append(pc[:, CTAIL - (CONF_W - 1):, :])
        sq8 = jnp.pad(state_qkv_conv[l], ((0, 0), (SUBLANES - (QK_CONV - 1), 0), (0, 0)))
        ss8 = jnp.pad(state_sconv[l], ((0, 0), (SUBLANES - (SCONV_W - 1), 0), (0, 0)))
        sc32 = jnp.pad(state_cconv[l], ((0, 0), (CTAIL - (CONF_W - 1), 0), (0, 0)))
        xs, sd, sq, ss, sc = _sample_layer(xs, state_delta[l], sq8, ss8, sc32, wts, last)
        outs[4].append(sd)
        outs[5].append(sq[:, dec_seq - (QK_CONV - 1):dec_seq, :])
        outs[6].append(ss[:, dec_seq - (SCONV_W - 1):dec_seq, :])
        first = CTAIL + dec_seq - (CONF_W - 1)
        outs[7].append(sc[:, first:first + CONF_W - 1, :])
    return (xp, xs[:, :dec_seq, :]) + tuple(jnp.stack(o) for o in outs)
```

```python
import functools

import jax
import jax.numpy as jnp
from jax import lax
from jax.experimental import pallas as pl
from jax.experimental.pallas import tpu as pltpu

D_MODEL = 1024
N_DHEADS = 4
HEAD_DIM = 128
D_DELTA = N_DHEADS * HEAD_DIM
D_SCONV = 256
D_CONF = 256
QK_CONV = 4
SCONV_W = 3
CONF_W = 31
CHUNK = 64
EPS = 1e-6

R_GATE_D = 0
R_SB = 512
R_SC = 768
R_SX = 1024
R_GATE_S = 1280
R_GA = 1536
R_GB = 1792
R_GATE_C = 2048
R_BA = 2304
R_WIDTH = 2432

SUBLANES = 8
LANES = 128
TIME_TILE = 256
SAMPLE_PAD = 8
SAMPLE_BATCH_BLOCK = 16
SAMPLE_SEQ_LOCKSTEP = 4
CTAIL = 32
VMEM_LIMIT = 56 * 1024 * 1024

F32 = jnp.float32
BF16 = jnp.bfloat16


def _mm(a, b):
    return jnp.dot(a.astype(BF16), b.astype(BF16), preferred_element_type=F32)


def _mm_nt(a, b):
    return lax.dot_general(a.astype(BF16), b.astype(BF16), (((1,), (1,)), ((), ())),
                           preferred_element_type=F32)


def _mm_tn(a, b):
    return lax.dot_general(a.astype(BF16), b.astype(BF16), (((0,), (0,)), ((), ())),
                           preferred_element_type=F32)


def _sigmoid(x):
    return 1.0 / (1.0 + jnp.exp(-x))


def _silu(x):
    return x * _sigmoid(x)


def _softplus(x):
    return jnp.maximum(x, 0.0) + jnp.log1p(jnp.exp(-jnp.abs(x)))


def _rms_rows(x, g):
    return x * lax.rsqrt(jnp.mean(x * x, axis=-1, keepdims=True) + EPS) * g


def _chunk_cumsum(g, chunk, row):
    pos = row % chunk
    s = 1
    while s < chunk:
        g = g + jnp.where(pos >= s, pltpu.roll(g, s, axis=0), 0.0)
        s *= 2
    return g


def _pair_diag(a, b):
    z = jnp.zeros_like(a)
    return jnp.concatenate([jnp.concatenate([a, z], axis=1), jnp.concatenate([z, b], axis=1)], axis=0)


class _Packed:
    def __init__(self, chunk, rows, bf16_tile):
        self.chunk, self.rows, self.n = chunk, rows, rows // chunk
        self.rr = lax.broadcasted_iota(jnp.int32, (chunk, rows), 0)
        lane = lax.broadcasted_iota(jnp.int32, (chunk, rows), 1)
        self.jl = lane % chunk
        self.lane_blk = lane // chunk
        ii = lax.broadcasted_iota(jnp.int32, (rows, rows), 0)
        jj = lax.broadcasted_iota(jnp.int32, (rows, rows), 1)
        mask = jnp.where((ii // chunk) == (jj // chunk), 1.0, 0.0).astype(F32)
        self.bd_mask = mask.astype(BF16) if bf16_tile else mask

    def pack(self, g):
        out = g[0:self.chunk]
        for c in range(1, self.n):
            out = jnp.where(self.lane_blk == c, g[c * self.chunk:(c + 1) * self.chunk], out)
        return out

    def col(self, v):
        shape = (self.chunk, self.rows)
        out = jnp.broadcast_to(v[0:self.chunk], shape)
        for c in range(1, self.n):
            out = jnp.where(self.lane_blk == c,
                            jnp.broadcast_to(v[c * self.chunk:(c + 1) * self.chunk], shape), out)
        return out

    def block_diag(self, xp):
        if self.bd_mask.dtype == BF16:
            return jnp.concatenate([xp.astype(BF16)] * self.n, axis=0) * self.bd_mask
        return (jnp.concatenate([xp] * self.n, axis=0) * self.bd_mask).astype(BF16)

    def unit_lower_inverse(self, lps, nil):
        base = min(self.chunk, 16)
        same = (self.rr // base) == (self.jl // base)
        eye = jnp.where(self.rr == self.jl, 1.0, 0.0).astype(F32)
        ds = [jnp.where(same, lp, 0.0) for lp in lps]
        xs = [eye - d for d in ds]
        d_bds = [self.block_diag(d) for d in ds]
        p = 2
        while p < min(base, nil):
            ds = [_mm(d, d_bd) for d, d_bd in zip(ds, d_bds)]
            d_bds = [self.block_diag(d) for d in ds]
            xs = [x + _mm(x, d_bd) for x, d_bd in zip(xs, d_bds)]
            p *= 2
        size = base
        while size < self.chunk:
            big = (self.rr // (2 * size)) == (self.jl // (2 * size))
            off = jnp.logical_and(big, jnp.logical_not(same))
            xes = [_mm(x, self.block_diag(jnp.where(off, lp, 0.0))) for x, lp in zip(xs, lps)]
            xs = [x - _mm(xe, self.block_diag(x)) for x, xe in zip(xs, xes)]
            same = big
            size *= 2
        return xs

    def delta_prep(self, heads, nil):
        tril = self.rr >= self.jl
        strict = self.rr > self.jl
        decays, kbs, gs = [], [], []
        for q, k, v, beta, gc_col, gc_row in heads:
            diff = self.col(gc_col) - gc_row
            decays.append(jnp.where(tril, jnp.exp(jnp.where(tril, diff, 0.0)), 0.0))
            kbs.append(k * beta)
        for (q, k, *_), kb in zip(heads, kbs):
            gs.append(_mm_nt(jnp.concatenate([kb, q], axis=0), k))
        lps = [jnp.where(strict, self.pack(g[:self.rows]) * dec, 0.0) for g, dec in zip(gs, decays)]
        qkps = [self.pack(g[self.rows:]) * dec for g, dec in zip(gs, decays)]
        tinvs = self.unit_lower_inverse(lps, nil)
        out = []
        for (q, k, v, beta, gc_col, _), kb, tinv, qkp in zip(heads, kbs, tinvs, qkps):
            eg = jnp.exp(gc_col)
            uw = _mm(self.block_diag(tinv), jnp.concatenate([v * beta, kb * eg], axis=1))
            out.append((uw[:, :HEAD_DIM], uw[:, HEAD_DIM:], q * eg, qkp))
        return out


def _branch_s(buf_ref, w_ref, tail, rows, lead=()):
    n = len(lead)
    acc = None
    for j in range(SCONV_W):
        idx = lead + (pl.ds(tail - (SCONV_W - 1) + j, rows), slice(None))
        term = buf_ref[idx] * w_ref[j:j + 1, :].reshape((1,) * n + (1, D_SCONV))
        acc = term if acc is None else acc + term
    return acc


def _conformer(ubuf_ref, w_ref, b_ref, g_ref, beta_ref, tail, rows, lead=()):
    n = len(lead)
    shp = (1,) * n + (1, D_CONF)
    acc = None
    for j in range(CONF_W):
        idx = lead + (pl.ds(tail - (CONF_W - 1) + j, rows), slice(None))
        term = ubuf_ref[idx] * w_ref[j:j + 1, :].reshape(shp)
        acc = term if acc is None else acc + term
    yc = acc + b_ref[...].reshape(shp)
    mu = jnp.mean(yc, axis=-1, keepdims=True)
    var = jnp.mean(jnp.square(yc - mu), axis=-1, keepdims=True)
    yc = (yc - mu) * lax.rsqrt(var + EPS) * g_ref[...].reshape(shp) + beta_ref[...].reshape(shp)
    return _silu(yc)


def _prompt_kernel(final_norm,
                   x_ref, ng_ref, wqkv_ref, wrest_ref, cw_ref, alog_ref, dt_ref, dng_ref,
                   sw_ref, ccw_ref, ccb_ref, clg_ref, clb_ref, wout_ref, fg_ref,
                   y_ref, s_out_ref, qt_ref, st_ref, ct_ref,
                   qbuf, sbuf, cbuf, cshift, s_scr, obuf, rest,
                   u_s, lhs_s, kdt_s, egl_s, oi_s, vn_s, qk_s):
    i = pl.program_id(1)
    nt = pl.num_programs(1)
    tt = TIME_TILE
    nchunk = tt // CHUNK

    @pl.when(i == 0)
    def _():
        qbuf[0:SUBLANES, :] = jnp.zeros((SUBLANES, 3 * D_DELTA), F32)
        sbuf[0:SUBLANES, :] = jnp.zeros((SUBLANES, D_SCONV), F32)
        cbuf[0:CTAIL, :] = jnp.zeros((CTAIL, D_CONF), F32)
        s_scr[...] = jnp.zeros(s_scr.shape, F32)

    x = x_ref[0]
    h = _rms_rows(x, ng_ref[...]).astype(BF16)
    qbuf[SUBLANES:SUBLANES + tt, :] = jnp.dot(h, wqkv_ref[...], preferred_element_type=F32)
    rest[...] = jnp.dot(h, wrest_ref[...], preferred_element_type=F32)

    sbuf[SUBLANES:SUBLANES + tt, :] = rest[:, R_SC:R_SC + D_SCONV] * rest[:, R_SX:R_SX + D_SCONV]
    ys = _branch_s(sbuf, sw_ref, SUBLANES, tt)
    obuf[:, D_DELTA:D_DELTA + D_SCONV] = (rest[:, R_SB:R_SB + D_SCONV] * ys
                                          * _silu(rest[:, R_GATE_S:R_GATE_S + D_SCONV]))

    cbuf[CTAIL:CTAIL + tt, :] = rest[:, R_GA:R_GA + D_CONF] * _sigmoid(rest[:, R_GB:R_GB + D_CONF])
    for sh in range(1, SUBLANES):
        cshift[sh - 1] = cbuf[pl.ds(sh, tt + CTAIL - SUBLANES), :]

    row = lax.broadcasted_iota(jnp.int32, (tt, LANES), 0)
    ba = rest[:, R_BA:R_BA + LANES]
    beta_all = _sigmoid(ba)
    g_all = -jnp.exp(alog_ref[...]) * _softplus(ba + dt_ref[...])
    gc = _chunk_cumsum(g_all, CHUNK, row)
    gct = gc.T
    gl_rows = jnp.concatenate(
        [jnp.broadcast_to(gc[(c + 1) * CHUNK - 1:(c + 1) * CHUNK, :], (CHUNK, LANES))
         for c in range(nchunk)], axis=0)
    pk = _Packed(CHUNK, tt, True)

    def conv_act(c0):
        acc = None
        for j in range(QK_CONV):
            term = (qbuf[pl.ds(SUBLANES - (QK_CONV - 1) + j, tt), c0:c0 + HEAD_DIM]
                    * cw_ref[j:j + 1, c0:c0 + HEAD_DIM])
            acc = term if acc is None else acc + term
        return _silu(acc)

    heads, kdecs = [], []
    for hd in range(N_DHEADS):
        q = conv_act(hd * HEAD_DIM)
        k = conv_act(D_DELTA + hd * HEAD_DIM)
        v = conv_act(2 * D_DELTA + hd * HEAD_DIM)
        q = q * lax.rsqrt(jnp.sum(q * q, axis=-1, keepdims=True) + EPS) * (HEAD_DIM ** -0.5)
        k = k * lax.rsqrt(jnp.sum(k * k, axis=-1, keepdims=True) + EPS)
        lane = N_DHEADS + hd
        gcol = gc[:, lane:lane + 1]
        heads.append((q, k, v, beta_all[:, hd:hd + 1], gcol, gct[lane:lane + 1, :]))
        kdecs.append(k * jnp.exp(gl_rows[:, lane:lane + 1] - gcol))
    prepped = pk.delta_prep(heads, CHUNK)

    for pr in range(N_DHEADS // 2):
        h0, h1 = 2 * pr, 2 * pr + 1
        (u0, w0, qg0, qkp0), (u1, w1, qg1, qkp1) = prepped[h0], prepped[h1]
        kd0, kd1 = kdecs[h0], kdecs[h1]
        u_s[pr] = jnp.concatenate([u0, u1], axis=1)
        w = jnp.concatenate([w0, w1], axis=1).astype(BF16)
        qg = jnp.concatenate([qg0, qg1], axis=1).astype(BF16)
        kst = jnp.concatenate(
            [kd[c * CHUNK:(c + 1) * CHUNK] for c in range(nchunk) for kd in (kd0, kd1)], axis=0)
        kdt = kst.T.astype(BF16)
        qk_s[h0] = qkp0
        qk_s[h1] = qkp1
        for c in range(nchunk):
            cs = slice(c * CHUNK, (c + 1) * CHUNK)
            lhs_s[pr, c, 0:CHUNK, :] = w[cs]
            lhs_s[pr, c, CHUNK:, :] = qg[cs]
            kdt_s[pr, c] = kdt[:, 2 * c * CHUNK:2 * (c + 1) * CHUNK]
            ge = (c + 1) * CHUNK - 1
            egl = jnp.concatenate(
                [jnp.broadcast_to(jnp.exp(gc[ge:ge + 1, N_DHEADS + hh:N_DHEADS + hh + 1]),
                                  (SUBLANES, HEAD_DIM)) for hh in (h0, h1)], axis=1)
            egl_s[c, :, 2 * pr * HEAD_DIM:2 * (pr + 1) * HEAD_DIM] = egl

    def chunk_body(c, carry):
        r0 = pl.multiple_of(c * CHUNK, CHUNK)
        rows_c = pl.ds(r0, CHUNK)
        pairs = range(N_DHEADS // 2)
        ss = [s_scr[pr] for pr in pairs]
        rs = [jnp.dot(lhs_s[pr, c], _pair_diag(s[:, :HEAD_DIM], s[:, HEAD_DIM:]).astype(BF16),
                      preferred_element_type=F32) for pr, s in zip(pairs, ss)]
        vns = [u_s[pr, rows_c, :] - r[:CHUNK] for pr, r in zip(pairs, rs)]
        upds = [jnp.dot(kdt_s[pr, c], _pair_diag(vn[:, :HEAD_DIM], vn[:, HEAD_DIM:]).astype(BF16),
                        preferred_element_type=F32) for pr, vn in zip(pairs, vns)]
        for pr in pairs:
            oi_s[pr, rows_c, :] = rs[pr][CHUNK:]
            vn_s[pr, rows_c, :] = vns[pr]
            egl = egl_s[c][0:1, 2 * pr * HEAD_DIM:2 * (pr + 1) * HEAD_DIM]
            s_scr[pr] = ss[pr] * egl + upds[pr]
        acc = None
        for j in range(CONF_W):
            off = CTAIL - (CONF_W - 1) + j
            start = pl.multiple_of(r0 + (off // SUBLANES) * SUBLANES, SUBLANES)
            sh = off % SUBLANES
            src = cbuf[pl.ds(start, CHUNK), :] if sh == 0 else cshift[sh - 1, pl.ds(start, CHUNK), :]
            term = src * ccw_ref[j:j + 1, :]
            acc = term if acc is None else acc + term
        yc = acc + ccb_ref[...]
        mu = jnp.mean(yc, axis=-1, keepdims=True)
        var = jnp.mean(jnp.square(yc - mu), axis=-1, keepdims=True)
        yc = (yc - mu) * lax.rsqrt(var + EPS) * clg_ref[...] + clb_ref[...]
        obuf[rows_c, D_DELTA + D_SCONV:] = _silu(yc) * _silu(rest[rows_c, R_GATE_C:R_GATE_C + D_CONF])
        return carry

    lax.fori_loop(0, nchunk, chunk_body, 0)

    pk = _Packed(CHUNK, tt, True)
    for hd in range(N_DHEADS):
        pr = hd // 2
        ls = slice((hd % 2) * HEAD_DIM, (hd % 2 + 1) * HEAD_DIM)
        o = oi_s[pr, :, ls] + _mm(pk.block_diag(qk_s[hd]), vn_s[pr, :, ls])
        o = _rms_rows(o, dng_ref[...])
        obuf[:, hd * HEAD_DIM:(hd + 1) * HEAD_DIM] = (
            o * _silu(rest[:, R_GATE_D + hd * HEAD_DIM:R_GATE_D + (hd + 1) * HEAD_DIM]))

    y = x_ref[0] + jnp.dot(obuf[...].astype(BF16), wout_ref[...], preferred_element_type=F32)
    if final_norm:
        y = _rms_rows(y, fg_ref[...])
    y_ref[0] = y

    @pl.when(i == nt - 1)
    def _():
        for pr in range(N_DHEADS // 2):
            s_out_ref[0, 2 * pr] = s_scr[pr, :, :HEAD_DIM]
            s_out_ref[0, 2 * pr + 1] = s_scr[pr, :, HEAD_DIM:]
        qt_ref[0] = qbuf[tt:tt + SUBLANES, :]
        st_ref[0] = sbuf[tt:tt + SUBLANES, :]
        ct_ref[0] = cbuf[tt:tt + CTAIL, :]

    qbuf[0:SUBLANES, :] = qbuf[tt:tt + SUBLANES, :]
    sbuf[0:SUBLANES, :] = sbuf[tt:tt + SUBLANES, :]
    cbuf[0:CTAIL, :] = cbuf[tt:tt + CTAIL, :]


def _weight_specs(wts, layer_of):
    specs = []
    for w in wts[:-1]:
        tail = (0,) * (w.ndim - 1)
        specs.append(pl.BlockSpec((None,) + w.shape[1:], lambda *ids, tail=tail: (layer_of(*ids),) + tail))
    specs.append(pl.BlockSpec(wts[-1].shape, lambda *ids: (0,) * wts[-1].ndim))
    return specs


def _prompt_layer(x, wts, layer, final_norm):
    b, t, d = x.shape
    nt = t // TIME_TILE
    npair = N_DHEADS // 2
    nchunk = TIME_TILE // CHUNK
    weight_specs = _weight_specs(wts, lambda *ids: layer)
    out_shape = (
        jax.ShapeDtypeStruct((b, t, d), F32),
        jax.ShapeDtypeStruct((b, N_DHEADS, HEAD_DIM, HEAD_DIM), F32),
        jax.ShapeDtypeStruct((b, SUBLANES, 3 * D_DELTA), F32),
        jax.ShapeDtypeStruct((b, SUBLANES, D_SCONV), F32),
        jax.ShapeDtypeStruct((b, CTAIL, D_CONF), F32),
    )
    out_specs = (
        pl.BlockSpec((1, TIME_TILE, d), lambda bi, ti: (bi, ti, 0)),
        pl.BlockSpec((1, N_DHEADS, HEAD_DIM, HEAD_DIM), lambda bi, ti: (bi, 0, 0, 0)),
        pl.BlockSpec((1, SUBLANES, 3 * D_DELTA), lambda bi, ti: (bi, 0, 0)),
        pl.BlockSpec((1, SUBLANES, D_SCONV), lambda bi, ti: (bi, 0, 0)),
        pl.BlockSpec((1, CTAIL, D_CONF), lambda bi, ti: (bi, 0, 0)),
    )
    scratch = [
        pltpu.VMEM((TIME_TILE + SUBLANES, 3 * D_DELTA), F32),
        pltpu.VMEM((TIME_TILE + SUBLANES, D_SCONV), F32),
        pltpu.VMEM((TIME_TILE + CTAIL, D_CONF), F32),
        pltpu.VMEM((SUBLANES - 1, TIME_TILE + CTAIL - SUBLANES, D_CONF), F32),
        pltpu.VMEM((npair, HEAD_DIM, 2 * HEAD_DIM), F32),
        pltpu.VMEM((TIME_TILE, D_MODEL), F32),
        pltpu.VMEM((TIME_TILE, R_WIDTH), F32),
        pltpu.VMEM((npair, TIME_TILE, 2 * HEAD_DIM), F32),
        pltpu.VMEM((npair, nchunk, 2 * CHUNK, 2 * HEAD_DIM), BF16),
        pltpu.VMEM((npair, nchunk, HEAD_DIM, 2 * CHUNK), BF16),
        pltpu.VMEM((nchunk, SUBLANES, N_DHEADS * HEAD_DIM), F32),
        pltpu.VMEM((npair, TIME_TILE, 2 * HEAD_DIM), F32),
        pltpu.VMEM((npair, TIME_TILE, 2 * HEAD_DIM), F32),
        pltpu.VMEM((N_DHEADS, CHUNK, TIME_TILE), F32),
    ]
    return pl.pallas_call(
        functools.partial(_prompt_kernel, final_norm),
        grid=(b, nt),
        in_specs=[pl.BlockSpec((1, TIME_TILE, d), lambda bi, ti: (bi, ti, 0))] + weight_specs,
        out_specs=out_specs,
        out_shape=out_shape,
        scratch_shapes=scratch,
        compiler_params=pltpu.CompilerParams(
            dimension_semantics=("arbitrary", "arbitrary"),
            vmem_limit_bytes=VMEM_LIMIT),
        name="prompt_layer",
    )(x, *wts)


def _sample_kernel(x_ref, sd_ref, sq_ref, ss_ref, sc_ref,
                   ng_ref, wqkv_ref, wrest_ref, cw_ref, alog_ref, dt_ref, dng_ref,
                   sw_ref, ccw_ref, ccb_ref, clg_ref, clb_ref, wout_ref, fg_ref,
                   y_ref, sd_out_ref, qt_ref, st_ref, ct_ref,
                   qbuf, sbuf, obuf, ubuf, wqbuf, kdbuf, glbuf, oibuf, rest, xcarry):
    layer = pl.program_id(0)
    blk = pl.program_id(1)
    nb = SAMPLE_BATCH_BLOCK
    pad = SAMPLE_PAD
    rows = nb * pad

    @pl.when(layer == 0)
    def _():
        xcarry[blk] = x_ref[...].reshape(rows, D_MODEL)

    x = xcarry[blk]
    h = _rms_rows(x, ng_ref[...]).astype(BF16)
    qkv = jnp.dot(h, wqkv_ref[...], preferred_element_type=F32)
    rest[...] = jnp.dot(h, wrest_ref[...], preferred_element_type=F32)
    row = lax.broadcasted_iota(jnp.int32, (rows, LANES), 0)
    valid = (row % pad) < (pad // 2)
    valid1 = valid[:, 0:1]

    qt_ref[...] = qkv.reshape(nb, pad, 3 * D_DELTA)
    qbuf[:, 0:SUBLANES, :] = sq_ref[...]
    qbuf[:, SUBLANES:, :] = qkv.reshape(nb, pad, 3 * D_DELTA)

    hs = rest[:, R_SC:R_SC + D_SCONV] * rest[:, R_SX:R_SX + D_SCONV]
    st_ref[...] = hs.reshape(nb, pad, D_SCONV)
    sbuf[:, 0:SUBLANES, :] = ss_ref[...]
    sbuf[:, SUBLANES:, :] = hs.reshape(nb, pad, D_SCONV)
    ys = _branch_s(sbuf, sw_ref, SUBLANES, pad, lead=(slice(None),))
    obuf[:, D_DELTA:D_DELTA + D_SCONV] = (rest[:, R_SB:R_SB + D_SCONV] * ys.reshape(rows, D_SCONV)
                                          * _silu(rest[:, R_GATE_S:R_GATE_S + D_SCONV]))

    uc = rest[:, R_GA:R_GA + D_CONF] * _sigmoid(rest[:, R_GB:R_GB + D_CONF])
    ct_ref[:, 0:CTAIL, :] = sc_ref[...]
    ct_ref[:, CTAIL:, :] = uc.reshape(nb, pad, D_CONF)
    yc = _conformer(ct_ref, ccw_ref, ccb_ref, clg_ref, clb_ref, CTAIL, pad, lead=(slice(None),))
    obuf[:, D_DELTA + D_SCONV:] = yc.reshape(rows, D_CONF) * _silu(rest[:, R_GATE_C:R_GATE_C + D_CONF])

    ba = rest[:, R_BA:R_BA + LANES]
    beta_all = jnp.where(valid, _sigmoid(ba), 0.0)
    g_all = jnp.where(valid, -jnp.exp(alog_ref[...]) * _softplus(ba + dt_ref[...]), 0.0)
    gc = _chunk_cumsum(g_all, pad, row)
    gct = gc.T
    gl_all = jnp.broadcast_to(
        gc.reshape(nb, pad, LANES)[:, pad - 1:pad, :], (nb, pad, LANES)).reshape(rows, LANES)
    pk = _Packed(pad, rows, False)

    def conv_act(c0):
        acc = None
        for j in range(QK_CONV):
            term = (qbuf[:, pl.ds(SUBLANES - (QK_CONV - 1) + j, pad), c0:c0 + HEAD_DIM]
                    * cw_ref[j:j + 1, c0:c0 + HEAD_DIM].reshape(1, 1, HEAD_DIM))
            acc = term if acc is None else acc + term
        return jnp.where(valid1, _silu(acc.reshape(rows, HEAD_DIM)), 0.0)

    for pr in range(N_DHEADS // 2):
        h0, h1 = 2 * pr, 2 * pr + 1
        heads, kdecs, egls = [], [], []
        for hd in (h0, h1):
            q = conv_act(hd * HEAD_DIM)
            k = conv_act(D_DELTA + hd * HEAD_DIM)
            v = conv_act(2 * D_DELTA + hd * HEAD_DIM)
            q = q * lax.rsqrt(jnp.sum(q * q, axis=-1, keepdims=True) + EPS) * (HEAD_DIM ** -0.5)
            k = k * lax.rsqrt(jnp.sum(k * k, axis=-1, keepdims=True) + EPS)
            lane = N_DHEADS + hd
            gcol = gc[:, lane:lane + 1]
            heads.append((q, k, v, beta_all[:, hd:hd + 1], gcol, gct[lane:lane + 1, :]))
            gl = gl_all[:, lane:lane + 1]
            kdecs.append(k * jnp.exp(gl - gcol))
            egls.append(jnp.broadcast_to(jnp.exp(gl), (rows, HEAD_DIM)))
        (u0, w0, qg0, qkp0), (u1, w1, qg1, qkp1) = pk.delta_prep(heads, pad // 2)
        (kd0, kd1), (egl0, egl1) = kdecs, egls
        ubuf[...] = jnp.concatenate([u0, u1], axis=1)
        wqbuf[:, 0:pad, :] = jnp.concatenate([w0, w1], axis=1).reshape(nb, pad, 2 * HEAD_DIM)
        wqbuf[:, pad:, :] = jnp.concatenate([qg0, qg1], axis=1).reshape(nb, pad, 2 * HEAD_DIM)
        kdbuf[:, 0:pad, :] = kd0.reshape(nb, pad, HEAD_DIM)
        kdbuf[:, pad:, :] = kd1.reshape(nb, pad, HEAD_DIM)
        glbuf[...] = jnp.concatenate([egl0, egl1], axis=1).reshape(nb, pad, 2 * HEAD_DIM)

        def body(it, carry):
            bis = [it * SAMPLE_SEQ_LOCKSTEP + t for t in range(SAMPLE_SEQ_LOCKSTEP)]
            r0s = [pl.multiple_of(bi * pad, pad) for bi in bis]
            s0s = [sd_ref[bi, h0] for bi in bis]
            s1s = [sd_ref[bi, h1] for bi in bis]
            rs = [_mm(wqbuf[bi], _pair_diag(s0, s1)) for bi, s0, s1 in zip(bis, s0s, s1s)]
            vns = [ubuf[pl.ds(r0, pad), :] - r[:pad] for r0, r in zip(r0s, rs)]
            upds = [_mm_tn(kdbuf[bi], _pair_diag(vn[:, :HEAD_DIM], vn[:, HEAD_DIM:]))
                    for bi, vn in zip(bis, vns)]
            for bi, r0, s0, s1, r, vn, upd in zip(bis, r0s, s0s, s1s, rs, vns, upds):
                ubuf[pl.ds(r0, pad), :] = vn
                oibuf[pl.ds(r0, pad), :] = r[pad:]
                egl = glbuf[bi][0:1, :]
                sd_out_ref[bi, h0] = s0 * egl[:, :HEAD_DIM] + upd[:, :HEAD_DIM]
                sd_out_ref[bi, h1] = s1 * egl[:, HEAD_DIM:] + upd[:, HEAD_DIM:]
            return carry

        lax.fori_loop(0, nb // SAMPLE_SEQ_LOCKSTEP, body, 0)
        for hh, qkp in ((h0, qkp0), (h1, qkp1)):
            ls = slice((hh - h0) * HEAD_DIM, (hh - h0 + 1) * HEAD_DIM)
            o = oibuf[:, ls] + _mm(pk.block_diag(qkp), ubuf[:, ls])
            o = _rms_rows(o, dng_ref[...])
            obuf[:, hh * HEAD_DIM:(hh + 1) * HEAD_DIM] = (
                o * _silu(rest[:, R_GATE_D + hh * HEAD_DIM:R_GATE_D + (hh + 1) * HEAD_DIM]))

    y = x + jnp.dot(obuf[...].astype(BF16), wout_ref[...], preferred_element_type=F32)
    xcarry[blk] = y
    is_last = layer == pl.num_programs(0) - 1
    y_ref[...] = jnp.where(is_last, _rms_rows(y, fg_ref[...]), y).reshape(nb, pad, D_MODEL)


def _sample_layers(x8, sd, sq8, ss8, sc32, wts):
    depth = sd.shape[0]
    b = x8.shape[0]
    nb = SAMPLE_BATCH_BLOCK

    def bspec(shape):
        return pl.BlockSpec((nb,) + shape, lambda l, bi: (bi,) + (0,) * len(shape))

    def lspec(shape):
        return pl.BlockSpec((None, nb) + shape, lambda l, bi: (l, bi) + (0,) * len(shape))

    def lshape(shape):
        return jax.ShapeDtypeStruct((depth, b) + shape, F32)

    out_shape = (
        lshape((SAMPLE_PAD, D_MODEL)),
        lshape((N_DHEADS, HEAD_DIM, HEAD_DIM)),
        lshape((SAMPLE_PAD, 3 * D_DELTA)),
        lshape((SAMPLE_PAD, D_SCONV)),
        lshape((CTAIL + SAMPLE_PAD, D_CONF)),
    )
    out_specs = (
        lspec((SAMPLE_PAD, D_MODEL)),
        lspec((N_DHEADS, HEAD_DIM, HEAD_DIM)),
        lspec((SAMPLE_PAD, 3 * D_DELTA)),
        lspec((SAMPLE_PAD, D_SCONV)),
        lspec((CTAIL + SAMPLE_PAD, D_CONF)),
    )
    rows = nb * SAMPLE_PAD
    scratch = [
        pltpu.VMEM((nb, 2 * SUBLANES, 3 * D_DELTA), F32),
        pltpu.VMEM((nb, 2 * SUBLANES, D_SCONV), F32),
        pltpu.VMEM((rows, D_MODEL), F32),
        pltpu.VMEM((rows, 2 * HEAD_DIM), F32),
        pltpu.VMEM((nb, 2 * SAMPLE_PAD, 2 * HEAD_DIM), F32),
        pltpu.VMEM((nb, 2 * SAMPLE_PAD, HEAD_DIM), F32),
        pltpu.VMEM((nb, SAMPLE_PAD, 2 * HEAD_DIM), F32),
        pltpu.VMEM((rows, 2 * HEAD_DIM), F32),
        pltpu.VMEM((rows, R_WIDTH), F32),
        pltpu.VMEM((b // nb, rows, D_MODEL), F32),
    ]
    in_specs = [
        bspec((SAMPLE_PAD, D_MODEL)),
        lspec((N_DHEADS, HEAD_DIM, HEAD_DIM)),
        lspec((SUBLANES, 3 * D_DELTA)),
        lspec((SUBLANES, D_SCONV)),
        lspec((CTAIL, D_CONF)),
    ] + _weight_specs(wts, lambda l, bi: l)
    return pl.pallas_call(
        _sample_kernel,
        grid=(depth, b // nb),
        in_specs=in_specs,
        out_specs=out_specs,
        out_shape=out_shape,
        scratch_shapes=scratch,
        compiler_params=pltpu.CompilerParams(
            dimension_semantics=("arbitrary", "arbitrary"),
            vmem_limit_bytes=VMEM_LIMIT),
        name="sample_layers",
    )(x8, sd, sq8, ss8, sc32, *wts)


def _stacked_weights(norm_g, w_in, conv_qkv_w, a_log, dt_bias, delta_norm_g, sconv_w,
                     cconv_w, cconv_b, cln_g, cln_b, w_out, final_norm_g):
    depth = w_in.shape[0]
    n_qkv = 3 * D_DELTA
    ba = jnp.pad(w_in[:, :, n_qkv:n_qkv + 2 * N_DHEADS], ((0, 0), (0, 0), (0, LANES - 2 * N_DHEADS)))
    wrest = jnp.concatenate([w_in[:, :, n_qkv + 2 * N_DHEADS:], ba], axis=2).astype(BF16)
    lane_pad = ((0, 0), (N_DHEADS, LANES - 2 * N_DHEADS))
    return (
        norm_g.reshape(depth, 1, D_MODEL),
        w_in[:, :, :n_qkv].astype(BF16),
        wrest,
        conv_qkv_w,
        jnp.pad(a_log, lane_pad).reshape(depth, 1, LANES),
        jnp.pad(dt_bias, lane_pad).reshape(depth, 1, LANES),
        delta_norm_g.reshape(depth, 1, HEAD_DIM),
        sconv_w,
        cconv_w,
        cconv_b.reshape(depth, 1, D_CONF),
        cln_g.reshape(depth, 1, D_CONF),
        cln_b.reshape(depth, 1, D_CONF),
        w_out.astype(BF16),
        final_norm_g.reshape(1, D_MODEL),
    )


def kernel(x_prompt, x_sample, state_delta, state_qkv_conv, state_sconv, state_cconv, norm_g, w_in, conv_qkv_w, a_log, dt_bias, delta_norm_g, sconv_w, cconv_w, cconv_b, cln_g, cln_b, w_out, final_norm_g):
    depth = w_in.shape[0]
    dec_seq = x_sample.shape[1]
    assert x_prompt.shape[1] % TIME_TILE == 0
    assert dec_seq == SAMPLE_PAD // 2 and x_sample.shape[0] % SAMPLE_BATCH_BLOCK == 0

    wts = _stacked_weights(norm_g, w_in, conv_qkv_w, a_log, dt_bias, delta_norm_g, sconv_w,
                           cconv_w, cconv_b, cln_g, cln_b, w_out, final_norm_g)

    xp = x_prompt
    p_outs = [[] for _ in range(4)]
    for l in range(depth):
        xp, pd, pq, ps, pc = _prompt_layer(xp, wts, l, l == depth - 1)
        for acc, o in zip(p_outs, (pd, pq, ps, pc)):
            acc.append(o)
    pd, pq, ps, pc = (jnp.stack(o) for o in p_outs)

    def front_pad(a, rows):
        return jnp.pad(a, ((0, 0), (0, 0), (rows - a.shape[2], 0), (0, 0)))

    xs, sd, sq, ss, sc = _sample_layers(
        jnp.pad(x_sample, ((0, 0), (0, SAMPLE_PAD - dec_seq), (0, 0))), state_delta,
        front_pad(state_qkv_conv, SUBLANES), front_pad(state_sconv, SUBLANES),
        front_pad(state_cconv, CTAIL), wts)
    first = CTAIL + dec_seq - (CONF_W - 1)
    return (xp, xs[depth - 1, :, :dec_seq, :],
            pd, pq[:, :, SUBLANES - (QK_CONV - 1):, :], ps[:, :, SUBLANES - (SCONV_W - 1):, :],
            pc[:, :, CTAIL - (CONF_W - 1):, :],
            sd, sq[:, :, dec_seq - (QK_CONV - 1):dec_seq, :], ss[:, :, dec_seq - (SCONV_W - 1):dec_seq, :],
            sc[:, :, first:first + CONF_W - 1, :])
```

```python
import functools

import jax
import jax.numpy as jnp
from jax import lax
from jax.experimental import pallas as pl
from jax.experimental.pallas import tpu as pltpu

D_MODEL = 1024
N_DHEADS = 4
HEAD_DIM = 128
D_DELTA = N_DHEADS * HEAD_DIM
D_SCONV = 256
D_CONF = 256
QK_CONV = 4
SCONV_W = 3
CONF_W = 31
CHUNK = 64
EPS = 1e-6

R_GATE_D = 0
R_SB = 512
R_SC = 768
R_SX = 1024
R_GATE_S = 1280
R_GA = 1536
R_GB = 1792
R_GATE_C = 2048
R_BA = 2304
R_WIDTH = 2432

SUBLANES = 8
LANES = 128
TIME_TILE = 256
SAMPLE_PAD = 8
SAMPLE_BATCH_BLOCK = 16
SAMPLE_SEQ_LOCKSTEP = 4
CTAIL = 32
VMEM_LIMIT = 56 * 1024 * 1024

F32 = jnp.float32
BF16 = jnp.bfloat16


def _mm(a, b):
    return jnp.dot(a.astype(BF16), b.astype(BF16), preferred_element_type=F32)


def _mm_nt(a, b):
    return lax.dot_general(a.astype(BF16), b.astype(BF16), (((1,), (1,)), ((), ())),
                           preferred_element_type=F32)


def _mm_tn(a, b):
    return lax.dot_general(a.astype(BF16), b.astype(BF16), (((0,), (0,)), ((), ())),
                           preferred_element_type=F32)


def _sigmoid(x):
    return 1.0 / (1.0 + jnp.exp(-x))


def _silu(x):
    return x * _sigmoid(x)


def _softplus(x):
    return jnp.maximum(x, 0.0) + jnp.log1p(jnp.exp(-jnp.abs(x)))


def _rms_rows(x, g):
    return x * lax.rsqrt(jnp.mean(x * x, axis=-1, keepdims=True) + EPS) * g


def _chunk_cumsum(g, chunk, row):
    pos = row % chunk
    s = 1
    while s < chunk:
        g = g + jnp.where(pos >= s, pltpu.roll(g, s, axis=0), 0.0)
        s *= 2
    return g


def _run(staged):
    try:
        while True:
            next(staged)
    except StopIteration as stop:
        return stop.value


def _interleave(*staged):
    live = list(staged)
    while live:
        for s in list(live):
            try:
                next(s)
            except StopIteration:
                live.remove(s)


def _pair_diag(a, b):
    z = jnp.zeros_like(a)
    return jnp.concatenate([jnp.concatenate([a, z], axis=1), jnp.concatenate([z, b], axis=1)], axis=0)


class _Packed:
    def __init__(self, chunk, rows, bf16_tile):
        self.chunk, self.rows, self.n = chunk, rows, rows // chunk
        self.rr = lax.broadcasted_iota(jnp.int32, (chunk, rows), 0)
        lane = lax.broadcasted_iota(jnp.int32, (chunk, rows), 1)
        self.jl = lane % chunk
        self.lane_blk = lane // chunk
        ii = lax.broadcasted_iota(jnp.int32, (rows, rows), 0)
        jj = lax.broadcasted_iota(jnp.int32, (rows, rows), 1)
        mask = jnp.where((ii // chunk) == (jj // chunk), 1.0, 0.0).astype(F32)
        self.bd_mask = mask.astype(BF16) if bf16_tile else mask

    def pack(self, g):
        out = g[0:self.chunk]
        for c in range(1, self.n):
            out = jnp.where(self.lane_blk == c, g[c * self.chunk:(c + 1) * self.chunk], out)
        return out

    def col(self, v):
        shape = (self.chunk, self.rows)
        out = jnp.broadcast_to(v[0:self.chunk], shape)
        for c in range(1, self.n):
            out = jnp.where(self.lane_blk == c,
                            jnp.broadcast_to(v[c * self.chunk:(c + 1) * self.chunk], shape), out)
        return out

    def block_diag(self, xp):
        if self.bd_mask.dtype == BF16:
            return jnp.concatenate([xp.astype(BF16)] * self.n, axis=0) * self.bd_mask
        return (jnp.concatenate([xp] * self.n, axis=0) * self.bd_mask).astype(BF16)

    def unit_lower_inverse(self, lps, nil):
        base = min(self.chunk, 16)
        same = (self.rr // base) == (self.jl // base)
        eye = jnp.where(self.rr == self.jl, 1.0, 0.0).astype(F32)
        ds = [jnp.where(same, lp, 0.0) for lp in lps]
        xs = [eye - d for d in ds]
        d_bds = [self.block_diag(d) for d in ds]
        p = 2
        while p < min(base, nil):
            ds = [_mm(d, d_bd) for d, d_bd in zip(ds, d_bds)]
            yield
            d_bds = [self.block_diag(d) for d in ds]
            xs = [x + _mm(x, d_bd) for x, d_bd in zip(xs, d_bds)]
            yield
            p *= 2
        size = base
        while size < self.chunk:
            big = (self.rr // (2 * size)) == (self.jl // (2 * size))
            off = jnp.logical_and(big, jnp.logical_not(same))
            xes = [_mm(x, self.block_diag(jnp.where(off, lp, 0.0))) for x, lp in zip(xs, lps)]
            yield
            xs = [x - _mm(xe, self.block_diag(x)) for x, xe in zip(xs, xes)]
            yield
            same = big
            size *= 2
        return xs

    def delta_prep(self, heads, nil):
        tril = self.rr >= self.jl
        strict = self.rr > self.jl
        decays, kbs, gs = [], [], []
        for q, k, v, beta, gc_col, gc_row in heads:
            diff = self.col(gc_col) - gc_row
            decays.append(jnp.where(tril, jnp.exp(jnp.where(tril, diff, 0.0)), 0.0))
            kbs.append(k * beta)
        for (q, k, *_), kb in zip(heads, kbs):
            gs.append(_mm_nt(jnp.concatenate([kb, q], axis=0), k))
        yield
        lps = [jnp.where(strict, self.pack(g[:self.rows]) * dec, 0.0) for g, dec in zip(gs, decays)]
        qkps = [self.pack(g[self.rows:]) * dec for g, dec in zip(gs, decays)]
        tinvs = yield from self.unit_lower_inverse(lps, nil)
        out = []
        for (q, k, v, beta, gc_col, _), kb, tinv, qkp in zip(heads, kbs, tinvs, qkps):
            eg = jnp.exp(gc_col)
            uw = _mm(self.block_diag(tinv), jnp.concatenate([v * beta, kb * eg], axis=1))
            out.append((uw[:, :HEAD_DIM], uw[:, HEAD_DIM:], q * eg, qkp))
        yield
        return out


def _branch_s(buf_ref, w_ref, tail, rows, lead=()):
    n = len(lead)
    acc = None
    for j in range(SCONV_W):
        idx = lead + (pl.ds(tail - (SCONV_W - 1) + j, rows), slice(None))
        term = buf_ref[idx] * w_ref[j:j + 1, :].reshape((1,) * n + (1, D_SCONV))
        acc = term if acc is None else acc + term
    return acc


def _conformer(ubuf_ref, w_ref, b_ref, g_ref, beta_ref, tail, rows, lead=()):
    n = len(lead)
    shp = (1,) * n + (1, D_CONF)
    acc = None
    for j in range(CONF_W):
        idx = lead + (pl.ds(tail - (CONF_W - 1) + j, rows), slice(None))
        term = ubuf_ref[idx] * w_ref[j:j + 1, :].reshape(shp)
        acc = term if acc is None else acc + term
    yc = acc + b_ref[...].reshape(shp)
    mu = jnp.mean(yc, axis=-1, keepdims=True)
    var = jnp.mean(jnp.square(yc - mu), axis=-1, keepdims=True)
    yc = (yc - mu) * lax.rsqrt(var + EPS) * g_ref[...].reshape(shp) + beta_ref[...].reshape(shp)
    return _silu(yc)


def _prompt_kernel(final_norm,
                   x_ref, ng_ref, wqkv_ref, wrest_ref, cw_ref, alog_ref, dt_ref, dng_ref,
                   sw_ref, ccw_ref, ccb_ref, clg_ref, clb_ref, wout_ref, fg_ref,
                   y_ref, s_out_ref, qt_ref, st_ref, ct_ref,
                   qbuf, sbuf, cbuf, cshift, s_scr, obuf, rest,
                   u_s, lhs_s, kdt_s, egl_s, qk_s):
    i = pl.program_id(1)
    nt = pl.num_programs(1)
    tt = TIME_TILE
    nchunk = tt // CHUNK

    @pl.when(i == 0)
    def _():
        qbuf[0:SUBLANES, :] = jnp.zeros((SUBLANES, 3 * D_DELTA), F32)
        sbuf[0:SUBLANES, :] = jnp.zeros((SUBLANES, D_SCONV), F32)
        cbuf[0:CTAIL, :] = jnp.zeros((CTAIL, D_CONF), F32)
        s_scr[...] = jnp.zeros(s_scr.shape, F32)

    x = x_ref[0]
    h = _rms_rows(x, ng_ref[...]).astype(BF16)
    qbuf[SUBLANES:SUBLANES + tt, :] = jnp.dot(h, wqkv_ref[...], preferred_element_type=F32)
    rest[...] = jnp.dot(h, wrest_ref[...], preferred_element_type=F32)

    sbuf[SUBLANES:SUBLANES + tt, :] = rest[:, R_SC:R_SC + D_SCONV] * rest[:, R_SX:R_SX + D_SCONV]
    ys = _branch_s(sbuf, sw_ref, SUBLANES, tt)
    obuf[:, D_DELTA:D_DELTA + D_SCONV] = (rest[:, R_SB:R_SB + D_SCONV] * ys
                                          * _silu(rest[:, R_GATE_S:R_GATE_S + D_SCONV]))

    cbuf[CTAIL:CTAIL + tt, :] = rest[:, R_GA:R_GA + D_CONF] * _sigmoid(rest[:, R_GB:R_GB + D_CONF])
    for sh in range(1, SUBLANES):
        cshift[sh - 1] = cbuf[pl.ds(sh, tt + CTAIL - SUBLANES), :]

    row = lax.broadcasted_iota(jnp.int32, (tt, LANES), 0)
    ba = rest[:, R_BA:R_BA + LANES]
    beta_all = _sigmoid(ba)
    g_all = -jnp.exp(alog_ref[...]) * _softplus(ba + dt_ref[...])
    gc = _chunk_cumsum(g_all, CHUNK, row)
    gct = gc.T
    gl_rows = jnp.concatenate(
        [jnp.broadcast_to(gc[(c + 1) * CHUNK - 1:(c + 1) * CHUNK, :], (CHUNK, LANES))
         for c in range(nchunk)], axis=0)
    pk = _Packed(CHUNK, tt, True)

    def conv_act(c0):
        acc = None
        for j in range(QK_CONV):
            term = (qbuf[pl.ds(SUBLANES - (QK_CONV - 1) + j, tt), c0:c0 + HEAD_DIM]
                    * cw_ref[j:j + 1, c0:c0 + HEAD_DIM])
            acc = term if acc is None else acc + term
        return _silu(acc)

    heads, kdecs = [], []
    for hd in range(N_DHEADS):
        q = conv_act(hd * HEAD_DIM)
        k = conv_act(D_DELTA + hd * HEAD_DIM)
        v = conv_act(2 * D_DELTA + hd * HEAD_DIM)
        q = q * lax.rsqrt(jnp.sum(q * q, axis=-1, keepdims=True) + EPS) * (HEAD_DIM ** -0.5)
        k = k * lax.rsqrt(jnp.sum(k * k, axis=-1, keepdims=True) + EPS)
        lane = N_DHEADS + hd
        gcol = gc[:, lane:lane + 1]
        heads.append((q, k, v, beta_all[:, hd:hd + 1], gcol, gct[lane:lane + 1, :]))
        kdecs.append(k * jnp.exp(gl_rows[:, lane:lane + 1] - gcol))
    prepped = pk.delta_prep(heads, CHUNK)

    for pr in range(N_DHEADS // 2):
        h0, h1 = 2 * pr, 2 * pr + 1
        (u0, w0, qg0, qkp0), (u1, w1, qg1, qkp1) = prepped[h0], prepped[h1]
        kd0, kd1 = kdecs[h0], kdecs[h1]
        u_s[pr] = jnp.concatenate([u0, u1], axis=1)
        w = jnp.concatenate([w0, w1], axis=1).astype(BF16)
        qg = jnp.concatenate([qg0, qg1], axis=1).astype(BF16)
        kst = jnp.concatenate(
            [kd[c * CHUNK:(c + 1) * CHUNK] for c in range(nchunk) for kd in (kd0, kd1)], axis=0)
        kdt = kst.T.astype(BF16)
        low = (lax.broadcasted_iota(jnp.int32, (CHUNK, tt), 1) % LANES) < CHUNK
        qk_even = jnp.where(low, qkp0, pltpu.roll(qkp1, CHUNK, axis=1)).astype(BF16)
        qk_odd = jnp.where(low, pltpu.roll(qkp0, tt - CHUNK, axis=1), qkp1).astype(BF16)
        for c in range(nchunk):
            cs = slice(c * CHUNK, (c + 1) * CHUNK)
            lhs_s[pr, c, 0:CHUNK, :] = w[cs]
            lhs_s[pr, c, CHUNK:, :] = qg[cs]
            kdt_s[pr, c] = kdt[:, 2 * c * CHUNK:2 * (c + 1) * CHUNK]
            tile = slice((c // 2) * LANES, (c // 2 + 1) * LANES)
            qk_s[pr, c] = (qk_even if c % 2 == 0 else qk_odd)[:, tile]
            ge = (c + 1) * CHUNK - 1
            egl = jnp.concatenate(
                [jnp.broadcast_to(jnp.exp(gc[ge:ge + 1, N_DHEADS + hh:N_DHEADS + hh + 1]),
                                  (SUBLANES, HEAD_DIM)) for hh in (h0, h1)], axis=1)
            egl_s[c, :, 2 * pr * HEAD_DIM:2 * (pr + 1) * HEAD_DIM] = egl

    def chunk_body(c, carry):
        r0 = pl.multiple_of(c * CHUNK, CHUNK)
        rows_c = pl.ds(r0, CHUNK)
        pairs = range(N_DHEADS // 2)
        ss = [s_scr[pr] for pr in pairs]
        rs = [jnp.dot(lhs_s[pr, c], _pair_diag(s[:, :HEAD_DIM], s[:, HEAD_DIM:]).astype(BF16),
                      preferred_element_type=F32) for pr, s in zip(pairs, ss)]
        vns = [u_s[pr, rows_c, :] - r[:CHUNK] for pr, r in zip(pairs, rs)]
        v_bds = [_pair_diag(vn[:, :HEAD_DIM], vn[:, HEAD_DIM:]).astype(BF16) for vn in vns]
        upds = [jnp.dot(kdt_s[pr, c], v_bd, preferred_element_type=F32) for pr, v_bd in zip(pairs, v_bds)]
        o_intra = [jnp.dot(qk_s[pr, c], v_bd, preferred_element_type=F32) for pr, v_bd in zip(pairs, v_bds)]
        for pr in pairs:
            egl = egl_s[c][0:1, 2 * pr * HEAD_DIM:2 * (pr + 1) * HEAD_DIM]
            s_scr[pr] = ss[pr] * egl + upds[pr]
            o_pair = rs[pr][CHUNK:] + o_intra[pr]
            for hl in range(2):
                hd = 2 * pr + hl
                o = _rms_rows(o_pair[:, hl * HEAD_DIM:(hl + 1) * HEAD_DIM], dng_ref[...])
                gate = rest[rows_c, R_GATE_D + hd * HEAD_DIM:R_GATE_D + (hd + 1) * HEAD_DIM]
                obuf[rows_c, hd * HEAD_DIM:(hd + 1) * HEAD_DIM] = o * _silu(gate)
        acc = None
        for j in range(CONF_W):
            off = CTAIL - (CONF_W - 1) + j
            start = pl.multiple_of(r0 + (off // SUBLANES) * SUBLANES, SUBLANES)
            sh = off % SUBLANES
            src = cbuf[pl.ds(start, CHUNK), :] if sh == 0 else cshift[sh - 1, pl.ds(start, CHUNK), :]
            term = src * ccw_ref[j:j + 1, :]
            acc = term if acc is None else acc + term
        yc = acc + ccb_ref[...]
        mu = jnp.mean(yc, axis=-1, keepdims=True)
        var = jnp.mean(jnp.square(yc - mu), axis=-1, keepdims=True)
        yc = (yc - mu) * lax.rsqrt(var + EPS) * clg_ref[...] + clb_ref[...]
        obuf[rows_c, D_DELTA + D_SCONV:] = _silu(yc) * _silu(rest[rows_c, R_GATE_C:R_GATE_C + D_CONF])
        return carry

    lax.fori_loop(0, nchunk, chunk_body, 0)

    y = x_ref[0] + jnp.dot(obuf[...].astype(BF16), wout_ref[...], preferred_element_type=F32)
    if final_norm:
        y = _rms_rows(y, fg_ref[...])
    y_ref[0] = y

    @pl.when(i == nt - 1)
    def _():
        for pr in range(N_DHEADS // 2):
            s_out_ref[0, 2 * pr] = s_scr[pr, :, :HEAD_DIM]
            s_out_ref[0, 2 * pr + 1] = s_scr[pr, :, HEAD_DIM:]
        qt_ref[0] = qbuf[tt:tt + SUBLANES, :]
        st_ref[0] = sbuf[tt:tt + SUBLANES, :]
        ct_ref[0] = cbuf[tt:tt + CTAIL, :]

    qbuf[0:SUBLANES, :] = qbuf[tt:tt + SUBLANES, :]
    sbuf[0:SUBLANES, :] = sbuf[tt:tt + SUBLANES, :]
    cbuf[0:CTAIL, :] = cbuf[tt:tt + CTAIL, :]


def _weight_specs(wts, layer_of):
    specs = []
    for w in wts[:-1]:
        tail = (0,) * (w.ndim - 1)
        specs.append(pl.BlockSpec((None,) + w.shape[1:], lambda *ids, tail=tail: (layer_of(*ids),) + tail))
    specs.append(pl.BlockSpec(wts[-1].shape, lambda *ids: (0,) * wts[-1].ndim))
    return specs


def _prompt_layer(x, wts, layer, final_norm):
    b, t, d = x.shape
    nt = t // TIME_TILE
    npair = N_DHEADS // 2
    nchunk = TIME_TILE // CHUNK
    weight_specs = _weight_specs(wts, lambda *ids: layer)
    out_shape = (
        jax.ShapeDtypeStruct((b, t, d), F32),
        jax.ShapeDtypeStruct((b, N_DHEADS, HEAD_DIM, HEAD_DIM), F32),
        jax.ShapeDtypeStruct((b, SUBLANES, 3 * D_DELTA), F32),
        jax.ShapeDtypeStruct((b, SUBLANES, D_SCONV), F32),
        jax.ShapeDtypeStruct((b, CTAIL, D_CONF), F32),
    )
    out_specs = (
        pl.BlockSpec((1, TIME_TILE, d), lambda bi, ti: (bi, ti, 0)),
        pl.BlockSpec((1, N_DHEADS, HEAD_DIM, HEAD_DIM), lambda bi, ti: (bi, 0, 0, 0)),
        pl.BlockSpec((1, SUBLANES, 3 * D_DELTA), lambda bi, ti: (bi, 0, 0)),
        pl.BlockSpec((1, SUBLANES, D_SCONV), lambda bi, ti: (bi, 0, 0)),
        pl.BlockSpec((1, CTAIL, D_CONF), lambda bi, ti: (bi, 0, 0)),
    )
    scratch = [
        pltpu.VMEM((TIME_TILE + SUBLANES, 3 * D_DELTA), F32),
        pltpu.VMEM((TIME_TILE + SUBLANES, D_SCONV), F32),
        pltpu.VMEM((TIME_TILE + CTAIL, D_CONF), F32),
        pltpu.VMEM((SUBLANES - 1, TIME_TILE + CTAIL - SUBLANES, D_CONF), F32),
        pltpu.VMEM((npair, HEAD_DIM, 2 * HEAD_DIM), F32),
        pltpu.VMEM((TIME_TILE, D_MODEL), F32),
        pltpu.VMEM((TIME_TILE, R_WIDTH), F32),
        pltpu.VMEM((npair, TIME_TILE, 2 * HEAD_DIM), F32),
        pltpu.VMEM((npair, nchunk, 2 * CHUNK, 2 * HEAD_DIM), BF16),
        pltpu.VMEM((npair, nchunk, HEAD_DIM, 2 * CHUNK), BF16),
        pltpu.VMEM((nchunk, SUBLANES, N_DHEADS * HEAD_DIM), F32),
        pltpu.VMEM((npair, nchunk, CHUNK, 2 * CHUNK), BF16),
    ]
    return pl.pallas_call(
        functools.partial(_prompt_kernel, final_norm),
        grid=(b, nt),
        in_specs=[pl.BlockSpec((1, TIME_TILE, d), lambda bi, ti: (bi, ti, 0))] + weight_specs,
        out_specs=out_specs,
        out_shape=out_shape,
        scratch_shapes=scratch,
        compiler_params=pltpu.CompilerParams(
            dimension_semantics=("arbitrary", "arbitrary"),
            vmem_limit_bytes=VMEM_LIMIT),
        name="prompt_layer",
    )(x, *wts)


def _pipe_kernel(final_norm, nt,
                 x_ref, ng_ref, wqkv_ref, wrest_ref, cw_ref, alog_ref, dt_ref, dng_ref,
                 sw_ref, ccw_ref, ccb_ref, clg_ref, clb_ref, wout_ref, fg_ref,
                 y_ref, s_out_ref, qt_ref, st_ref, ct_ref,
                 qbuf, sbuf, s_scr, xs, qa, bg, rest, cbuf, cshift, obuf):
    g = pl.program_id(0)
    n_tiles = pl.num_programs(0) - 1
    tt = TIME_TILE
    nchunk = tt // CHUNK
    npair = N_DHEADS // 2
    t1 = jnp.minimum(g, n_tiles - 1) % nt
    t2 = jnp.maximum(g - 1, 0) % nt
    p = g % 2
    q = 1 - p

    @pl.when(g == 0)
    def _():
        xs[1] = jnp.zeros(xs.shape[1:], F32)
        qa[1] = jnp.zeros(qa.shape[1:], F32)
        bg[1] = jnp.zeros(bg.shape[1:], F32)
        rest[1] = jnp.zeros(rest.shape[1:], F32)
        cbuf[1] = jnp.zeros(cbuf.shape[1:], F32)
        cshift[1] = jnp.zeros(cshift.shape[1:], F32)
        obuf[1] = jnp.zeros(obuf.shape[1:], F32)

    @pl.when(t1 == 0)
    def _():
        qbuf[0:SUBLANES, :] = jnp.zeros((SUBLANES, 3 * D_DELTA), F32)
        sbuf[0:SUBLANES, :] = jnp.zeros((SUBLANES, D_SCONV), F32)

    @pl.when(t2 == 0)
    def _():
        s_scr[...] = jnp.zeros(s_scr.shape, F32)

    def stage1():
        x = x_ref[0]
        xs[p] = x
        h = _rms_rows(x, ng_ref[...]).astype(BF16)
        yield
        for c0 in range(0, 3 * D_DELTA, 2 * LANES):
            qbuf[SUBLANES:SUBLANES + tt, c0:c0 + 2 * LANES] = jnp.dot(
                h, wqkv_ref[:, c0:c0 + 2 * LANES], preferred_element_type=F32)
            yield
        for c0 in range(0, R_WIDTH, 2 * LANES):
            c1 = min(c0 + 2 * LANES, R_WIDTH)
            rest[p, :, c0:c1] = jnp.dot(h, wrest_ref[:, c0:c1], preferred_element_type=F32)
            yield

        def conv_act(c0):
            acc = None
            for j in range(QK_CONV):
                term = (qbuf[pl.ds(SUBLANES - (QK_CONV - 1) + j, tt), c0:c0 + HEAD_DIM]
                        * cw_ref[j:j + 1, c0:c0 + HEAD_DIM])
                acc = term if acc is None else acc + term
            return _silu(acc)

        for hd in range(N_DHEADS):
            qh = conv_act(hd * HEAD_DIM)
            qa[p, :, hd * HEAD_DIM:(hd + 1) * HEAD_DIM] = (
                qh * lax.rsqrt(jnp.sum(qh * qh, axis=-1, keepdims=True) + EPS) * (HEAD_DIM ** -0.5))
            kh = conv_act(D_DELTA + hd * HEAD_DIM)
            qa[p, :, D_DELTA + hd * HEAD_DIM:D_DELTA + (hd + 1) * HEAD_DIM] = (
                kh * lax.rsqrt(jnp.sum(kh * kh, axis=-1, keepdims=True) + EPS))
            qa[p, :, 2 * D_DELTA + hd * HEAD_DIM:2 * D_DELTA + (hd + 1) * HEAD_DIM] = conv_act(
                2 * D_DELTA + hd * HEAD_DIM)
            yield

        row = lax.broadcasted_iota(jnp.int32, (tt, LANES), 0)
        ba = rest[p, :, R_BA:R_BA + LANES]
        bg[p, 0] = _sigmoid(ba)
        bg[p, 1] = _chunk_cumsum(-jnp.exp(alog_ref[...]) * _softplus(ba + dt_ref[...]), CHUNK, row)
        yield

        sbuf[SUBLANES:SUBLANES + tt, :] = (rest[p, :, R_SC:R_SC + D_SCONV]
                                           * rest[p, :, R_SX:R_SX + D_SCONV])
        ys = _branch_s(sbuf, sw_ref, SUBLANES, tt)
        obuf[p, :, D_DELTA:D_DELTA + D_SCONV] = (rest[p, :, R_SB:R_SB + D_SCONV] * ys
                                                 * _silu(rest[p, :, R_GATE_S:R_GATE_S + D_SCONV]))
        yield

        prev_tail = jnp.where(t1 == 0, 0.0, cbuf[q, tt:tt + CTAIL, :])
        cbuf[p, 0:CTAIL, :] = prev_tail
        cbuf[p, CTAIL:CTAIL + tt, :] = (rest[p, :, R_GA:R_GA + D_CONF]
                                        * _sigmoid(rest[p, :, R_GB:R_GB + D_CONF]))
        yield
        for sh in range(1, SUBLANES):
            cshift[p, sh - 1] = cbuf[p, pl.ds(sh, tt + CTAIL - SUBLANES), :]
        yield

        qbuf[0:SUBLANES, :] = qbuf[tt:tt + SUBLANES, :]
        sbuf[0:SUBLANES, :] = sbuf[tt:tt + SUBLANES, :]

    def stage2():
        beta_all = bg[q, 0]
        gc = bg[q, 1]
        gct = gc.T
        gl_rows = jnp.concatenate(
            [jnp.broadcast_to(gc[(c + 1) * CHUNK - 1:(c + 1) * CHUNK, :], (CHUNK, LANES))
             for c in range(nchunk)], axis=0)
        pk = _Packed(CHUNK, tt, True)
        heads, kdecs = [], []
        for hd in range(N_DHEADS):
            lane = N_DHEADS + hd
            gcol = gc[:, lane:lane + 1]
            kh = qa[q, :, D_DELTA + hd * HEAD_DIM:D_DELTA + (hd + 1) * HEAD_DIM]
            heads.append((qa[q, :, hd * HEAD_DIM:(hd + 1) * HEAD_DIM], kh,
                          qa[q, :, 2 * D_DELTA + hd * HEAD_DIM:2 * D_DELTA + (hd + 1) * HEAD_DIM],
                          beta_all[:, hd:hd + 1], gcol, gct[lane:lane + 1, :]))
            kdecs.append(kh * jnp.exp(gl_rows[:, lane:lane + 1] - gcol))
        yield
        prepped = yield from pk.delta_prep(heads, CHUNK)

        us, lhss, kdts, qks = [], [], [], []
        low = (lax.broadcasted_iota(jnp.int32, (CHUNK, tt), 1) % LANES) < CHUNK
        for pr in range(npair):
            (u0, w0, qg0, qkp0), (u1, w1, qg1, qkp1) = prepped[2 * pr], prepped[2 * pr + 1]
            us.append(jnp.concatenate([u0, u1], axis=1))
            w = jnp.concatenate([w0, w1], axis=1).astype(BF16)
            qg = jnp.concatenate([qg0, qg1], axis=1).astype(BF16)
            lhss.append([jnp.concatenate([w[c * CHUNK:(c + 1) * CHUNK], qg[c * CHUNK:(c + 1) * CHUNK]],
                                         axis=0) for c in range(nchunk)])
            kst = jnp.concatenate([kd[c * CHUNK:(c + 1) * CHUNK] for c in range(nchunk)
                                   for kd in (kdecs[2 * pr], kdecs[2 * pr + 1])], axis=0)
            kdts.append(kst.T.astype(BF16))
            qks.append((jnp.where(low, qkp0, pltpu.roll(qkp1, CHUNK, axis=1)).astype(BF16),
                        jnp.where(low, pltpu.roll(qkp0, tt - CHUNK, axis=1), qkp1).astype(BF16)))
        yield

        ss = [s_scr[pr] for pr in range(npair)]
        for c in range(nchunk):
            cs = slice(c * CHUNK, (c + 1) * CHUNK)
            rs = [jnp.dot(lhss[pr][c], _pair_diag(s[:, :HEAD_DIM], s[:, HEAD_DIM:]).astype(BF16),
                          preferred_element_type=F32) for pr, s in enumerate(ss)]
            yield
            vns = [us[pr][cs] - r[:CHUNK] for pr, r in enumerate(rs)]
            v_bds = [_pair_diag(vn[:, :HEAD_DIM], vn[:, HEAD_DIM:]).astype(BF16) for vn in vns]
            upds = [jnp.dot(kdts[pr][:, 2 * c * CHUNK:2 * (c + 1) * CHUNK], v_bd,
                            preferred_element_type=F32) for pr, v_bd in enumerate(v_bds)]
            tile = slice((c // 2) * LANES, (c // 2 + 1) * LANES)
            o_intra = [jnp.dot(qks[pr][c % 2][:, tile], v_bd, preferred_element_type=F32)
                       for pr, v_bd in enumerate(v_bds)]
            ge = (c + 1) * CHUNK - 1
            for pr in range(npair):
                egl = jnp.concatenate(
                    [jnp.broadcast_to(jnp.exp(gc[ge:ge + 1, N_DHEADS + hh:N_DHEADS + hh + 1]),
                                      (1, HEAD_DIM)) for hh in (2 * pr, 2 * pr + 1)], axis=1)
                ss[pr] = ss[pr] * egl + upds[pr]
                o_pair = rs[pr][CHUNK:] + o_intra[pr]
                for hl in range(2):
                    hd = 2 * pr + hl
                    o = _rms_rows(o_pair[:, hl * HEAD_DIM:(hl + 1) * HEAD_DIM], dng_ref[...])
                    gate = rest[q, cs, R_GATE_D + hd * HEAD_DIM:R_GATE_D + (hd + 1) * HEAD_DIM]
                    obuf[q, cs, hd * HEAD_DIM:(hd + 1) * HEAD_DIM] = o * _silu(gate)
            yield
            acc = None
            for j in range(CONF_W):
                off = CTAIL - (CONF_W - 1) + j
                start = c * CHUNK + (off // SUBLANES) * SUBLANES
                sh = off % SUBLANES
                src = (cbuf[q, start:start + CHUNK, :] if sh == 0
                       else cshift[q, sh - 1, start:start + CHUNK, :])
                term = src * ccw_ref[j:j + 1, :]
                acc = term if acc is None else acc + term
            yc = acc + ccb_ref[...]
            mu = jnp.mean(yc, axis=-1, keepdims=True)
            var = jnp.mean(jnp.square(yc - mu), axis=-1, keepdims=True)
            yc = (yc - mu) * lax.rsqrt(var + EPS) * clg_ref[...] + clb_ref[...]
            obuf[q, cs, D_DELTA + D_SCONV:] = _silu(yc) * _silu(rest[q, cs, R_GATE_C:R_GATE_C + D_CONF])
            yield
        for pr in range(npair):
            s_scr[pr] = ss[pr]

        y = xs[q] + jnp.dot(obuf[q].astype(BF16), wout_ref[...], preferred_element_type=F32)
        if final_norm:
            y = _rms_rows(y, fg_ref[...])
        y_ref[0] = y

    _interleave(stage1(), stage2())

    @pl.when(jnp.logical_and(g >= 1, t2 == nt - 1))
    def _():
        for pr in range(npair):
            s_out_ref[0, 2 * pr] = s_scr[pr, :, :HEAD_DIM]
            s_out_ref[0, 2 * pr + 1] = s_scr[pr, :, HEAD_DIM:]

    @pl.when(jnp.logical_and(g < n_tiles, t1 == nt - 1))
    def _():
        qt_ref[0] = qbuf[0:SUBLANES, :]
        st_ref[0] = sbuf[0:SUBLANES, :]
        ct_ref[0] = cbuf[p, tt:tt + CTAIL, :]


def _prompt_layer_pipelined(x, wts, layer, final_norm):
    b, t, d = x.shape
    nt = t // TIME_TILE
    n_tiles = b * nt
    npair = N_DHEADS // 2

    def tile1(g):
        g1 = jnp.minimum(g, n_tiles - 1)
        return g1 // nt, g1 % nt

    def tile2(g):
        g2 = jnp.maximum(g - 1, 0)
        return g2 // nt, g2 % nt

    out_shape = (
        jax.ShapeDtypeStruct((b, t, d), F32),
        jax.ShapeDtypeStruct((b, N_DHEADS, HEAD_DIM, HEAD_DIM), F32),
        jax.ShapeDtypeStruct((b, SUBLANES, 3 * D_DELTA), F32),
        jax.ShapeDtypeStruct((b, SUBLANES, D_SCONV), F32),
        jax.ShapeDtypeStruct((b, CTAIL, D_CONF), F32),
    )
    out_specs = (
        pl.BlockSpec((1, TIME_TILE, d), lambda g: tile2(g) + (0,)),
        pl.BlockSpec((1, N_DHEADS, HEAD_DIM, HEAD_DIM), lambda g: (tile2(g)[0], 0, 0, 0)),
        pl.BlockSpec((1, SUBLANES, 3 * D_DELTA), lambda g: (tile1(g)[0], 0, 0)),
        pl.BlockSpec((1, SUBLANES, D_SCONV), lambda g: (tile1(g)[0], 0, 0)),
        pl.BlockSpec((1, CTAIL, D_CONF), lambda g: (tile1(g)[0], 0, 0)),
    )
    scratch = [
        pltpu.VMEM((TIME_TILE + SUBLANES, 3 * D_DELTA), F32),
        pltpu.VMEM((TIME_TILE + SUBLANES, D_SCONV), F32),
        pltpu.VMEM((npair, HEAD_DIM, 2 * HEAD_DIM), F32),
        pltpu.VMEM((2, TIME_TILE, D_MODEL), F32),
        pltpu.VMEM((2, TIME_TILE, 3 * D_DELTA), F32),
        pltpu.VMEM((2, 2, TIME_TILE, LANES), F32),
        pltpu.VMEM((2, TIME_TILE, R_WIDTH), F32),
        pltpu.VMEM((2, TIME_TILE + CTAIL, D_CONF), F32),
        pltpu.VMEM((2, SUBLANES - 1, TIME_TILE + CTAIL - SUBLANES, D_CONF), F32),
        pltpu.VMEM((2, TIME_TILE, D_MODEL), F32),
    ]
    return pl.pallas_call(
        functools.partial(_pipe_kernel, final_norm, nt),
        grid=(n_tiles + 1,),
        in_specs=[pl.BlockSpec((1, TIME_TILE, d), lambda g: tile1(g) + (0,))]
        + _weight_specs(wts, lambda g: layer),
        out_specs=out_specs,
        out_shape=out_shape,
        scratch_shapes=scratch,
        compiler_params=pltpu.CompilerParams(
            dimension_semantics=("arbitrary",),
            vmem_limit_bytes=VMEM_LIMIT),
        name="prompt_layer",
    )(x, *wts)


def _sample_kernel(x_ref, sd_ref, sq_ref, ss_ref, sc_ref,
                   ng_ref, wqkv_ref, wrest_ref, cw_ref, alog_ref, dt_ref, dng_ref,
                   sw_ref, ccw_ref, ccb_ref, clg_ref, clb_ref, wout_ref, fg_ref,
                   y_ref, sd_out_ref, qt_ref, st_ref, ct_ref,
                   qbuf, sbuf, obuf, ubuf, wqbuf, kdbuf, glbuf, oibuf, rest, xcarry):
    layer = pl.program_id(0)
    blk = pl.program_id(1)
    nb = SAMPLE_BATCH_BLOCK
    pad = SAMPLE_PAD
    rows = nb * pad

    @pl.when(layer == 0)
    def _():
        xcarry[blk] = x_ref[...].reshape(rows, D_MODEL)

    x = xcarry[blk]
    h = _rms_rows(x, ng_ref[...]).astype(BF16)
    qkv = jnp.dot(h, wqkv_ref[...], preferred_element_type=F32)
    rest[...] = jnp.dot(h, wrest_ref[...], preferred_element_type=F32)
    row = lax.broadcasted_iota(jnp.int32, (rows, LANES), 0)
    valid = (row % pad) < (pad // 2)
    valid1 = valid[:, 0:1]

    qt_ref[...] = qkv.reshape(nb, pad, 3 * D_DELTA)
    qbuf[:, 0:SUBLANES, :] = sq_ref[...]
    qbuf[:, SUBLANES:, :] = qkv.reshape(nb, pad, 3 * D_DELTA)

    hs = rest[:, R_SC:R_SC + D_SCONV] * rest[:, R_SX:R_SX + D_SCONV]
    st_ref[...] = hs.reshape(nb, pad, D_SCONV)
    sbuf[:, 0:SUBLANES, :] = ss_ref[...]
    sbuf[:, SUBLANES:, :] = hs.reshape(nb, pad, D_SCONV)
    ys = _branch_s(sbuf, sw_ref, SUBLANES, pad, lead=(slice(None),))
    obuf[:, D_DELTA:D_DELTA + D_SCONV] = (rest[:, R_SB:R_SB + D_SCONV] * ys.reshape(rows, D_SCONV)
                                          * _silu(rest[:, R_GATE_S:R_GATE_S + D_SCONV]))

    uc = rest[:, R_GA:R_GA + D_CONF] * _sigmoid(rest[:, R_GB:R_GB + D_CONF])
    ct_ref[:, 0:CTAIL, :] = sc_ref[...]
    ct_ref[:, CTAIL:, :] = uc.reshape(nb, pad, D_CONF)
    yc = _conformer(ct_ref, ccw_ref, ccb_ref, clg_ref, clb_ref, CTAIL, pad, lead=(slice(None),))
    obuf[:, D_DELTA + D_SCONV:] = yc.reshape(rows, D_CONF) * _silu(rest[:, R_GATE_C:R_GATE_C + D_CONF])

    ba = rest[:, R_BA:R_BA + LANES]
    beta_all = jnp.where(valid, _sigmoid(ba), 0.0)
    g_all = jnp.where(valid, -jnp.exp(alog_ref[...]) * _softplus(ba + dt_ref[...]), 0.0)
    gc = _chunk_cumsum(g_all, pad, row)
    gct = gc.T
    gl_all = jnp.broadcast_to(
        gc.reshape(nb, pad, LANES)[:, pad - 1:pad, :], (nb, pad, LANES)).reshape(rows, LANES)
    pk = _Packed(pad, rows, False)

    def conv_act(c0):
        acc = None
        for j in range(QK_CONV):
            term = (qbuf[:, pl.ds(SUBLANES - (QK_CONV - 1) + j, pad), c0:c0 + HEAD_DIM]
                    * cw_ref[j:j + 1, c0:c0 + HEAD_DIM].reshape(1, 1, HEAD_DIM))
            acc = term if acc is None else acc + term
        return jnp.where(valid1, _silu(acc.reshape(rows, HEAD_DIM)), 0.0)

    for pr in range(N_DHEADS // 2):
        h0, h1 = 2 * pr, 2 * pr + 1
        heads, kdecs, egls = [], [], []
        for hd in (h0, h1):
            q = conv_act(hd * HEAD_DIM)
            k = conv_act(D_DELTA + hd * HEAD_DIM)
            v = conv_act(2 * D_DELTA + hd * HEAD_DIM)
            q = q * lax.rsqrt(jnp.sum(q * q, axis=-1, keepdims=True) + EPS) * (HEAD_DIM ** -0.5)
            k = k * lax.rsqrt(jnp.sum(k * k, axis=-1, keepdims=True) + EPS)
            lane = N_DHEADS + hd
            gcol = gc[:, lane:lane + 1]
            heads.append((q, k, v, beta_all[:, hd:hd + 1], gcol, gct[lane:lane + 1, :]))
            gl = gl_all[:, lane:lane + 1]
            kdecs.append(k * jnp.exp(gl - gcol))
            egls.append(jnp.broadcast_to(jnp.exp(gl), (rows, HEAD_DIM)))
        (u0, w0, qg0, qkp0), (u1, w1, qg1, qkp1) = _run(pk.delta_prep(heads, pad // 2))
        (kd0, kd1), (egl0, egl1) = kdecs, egls
        ubuf[...] = jnp.concatenate([u0, u1], axis=1)
        wqbuf[:, 0:pad, :] = jnp.concatenate([w0, w1], axis=1).reshape(nb, pad, 2 * HEAD_DIM)
        wqbuf[:, pad:, :] = jnp.concatenate([qg0, qg1], axis=1).reshape(nb, pad, 2 * HEAD_DIM)
        kdbuf[:, 0:pad, :] = kd0.reshape(nb, pad, HEAD_DIM)
        kdbuf[:, pad:, :] = kd1.reshape(nb, pad, HEAD_DIM)
        glbuf[...] = jnp.concatenate([egl0, egl1], axis=1).reshape(nb, pad, 2 * HEAD_DIM)

        def body(it, carry):
            bis = [it * SAMPLE_SEQ_LOCKSTEP + t for t in range(SAMPLE_SEQ_LOCKSTEP)]
            r0s = [pl.multiple_of(bi * pad, pad) for bi in bis]
            s0s = [sd_ref[bi, h0] for bi in bis]
            s1s = [sd_ref[bi, h1] for bi in bis]
            rs = [_mm(wqbuf[bi], _pair_diag(s0, s1)) for bi, s0, s1 in zip(bis, s0s, s1s)]
            vns = [ubuf[pl.ds(r0, pad), :] - r[:pad] for r0, r in zip(r0s, rs)]
            upds = [_mm_tn(kdbuf[bi], _pair_diag(vn[:, :HEAD_DIM], vn[:, HEAD_DIM:]))
                    for bi, vn in zip(bis, vns)]
            for bi, r0, s0, s1, r, vn, upd in zip(bis, r0s, s0s, s1s, rs, vns, upds):
                ubuf[pl.ds(r0, pad), :] = vn
                oibuf[pl.ds(r0, pad), :] = r[pad:]
                egl = glbuf[bi][0:1, :]
                sd_out_ref[bi, h0] = s0 * egl[:, :HEAD_DIM] + upd[:, :HEAD_DIM]
                sd_out_ref[bi, h1] = s1 * egl[:, HEAD_DIM:] + upd[:, HEAD_DIM:]
            return carry

        lax.fori_loop(0, nb // SAMPLE_SEQ_LOCKSTEP, body, 0)
        for hh, qkp in ((h0, qkp0), (h1, qkp1)):
            ls = slice((hh - h0) * HEAD_DIM, (hh - h0 + 1) * HEAD_DIM)
            o = oibuf[:, ls] + _mm(pk.block_diag(qkp), ubuf[:, ls])
            o = _rms_rows(o, dng_ref[...])
            obuf[:, hh * HEAD_DIM:(hh + 1) * HEAD_DIM] = (
                o * _silu(rest[:, R_GATE_D + hh * HEAD_DIM:R_GATE_D + (hh + 1) * HEAD_DIM]))

    y = x + jnp.dot(obuf[...].astype(BF16), wout_ref[...], preferred_element_type=F32)
    xcarry[blk] = y
    is_last = layer == pl.num_programs(0) - 1
    y_ref[...] = jnp.where(is_last, _rms_rows(y, fg_ref[...]), y).reshape(nb, pad, D_MODEL)


def _sample_layers(x8, sd, sq8, ss8, sc32, wts):
    depth = sd.shape[0]
    b = x8.shape[0]
    nb = SAMPLE_BATCH_BLOCK

    def bspec(shape):
        return pl.BlockSpec((nb,) + shape, lambda l, bi: (bi,) + (0,) * len(shape))

    def lspec(shape):
        return pl.BlockSpec((None, nb) + shape, lambda l, bi: (l, bi) + (0,) * len(shape))

    def lshape(shape):
        return jax.ShapeDtypeStruct((depth, b) + shape, F32)

    out_shape = (
        lshape((SAMPLE_PAD, D_MODEL)),
        lshape((N_DHEADS, HEAD_DIM, HEAD_DIM)),
        lshape((SAMPLE_PAD, 3 * D_DELTA)),
        lshape((SAMPLE_PAD, D_SCONV)),
        lshape((CTAIL + SAMPLE_PAD, D_CONF)),
    )
    out_specs = (
        lspec((SAMPLE_PAD, D_MODEL)),
        lspec((N_DHEADS, HEAD_DIM, HEAD_DIM)),
        lspec((SAMPLE_PAD, 3 * D_DELTA)),
        lspec((SAMPLE_PAD, D_SCONV)),
        lspec((CTAIL + SAMPLE_PAD, D_CONF)),
    )
    rows = nb * SAMPLE_PAD
    scratch = [
        pltpu.VMEM((nb, 2 * SUBLANES, 3 * D_DELTA), F32),
        pltpu.VMEM((nb, 2 * SUBLANES, D_SCONV), F32),
        pltpu.VMEM((rows, D_MODEL), F32),
        pltpu.VMEM((rows, 2 * HEAD_DIM), F32),
        pltpu.VMEM((nb, 2 * SAMPLE_PAD, 2 * HEAD_DIM), F32),
        pltpu.VMEM((nb, 2 * SAMPLE_PAD, HEAD_DIM), F32),
        pltpu.VMEM((nb, SAMPLE_PAD, 2 * HEAD_DIM), F32),
        pltpu.VMEM((rows, 2 * HEAD_DIM), F32),
        pltpu.VMEM((rows, R_WIDTH), F32),
        pltpu.VMEM((b // nb, rows, D_MODEL), F32),
    ]
    in_specs = [
        bspec((SAMPLE_PAD, D_MODEL)),
        lspec((N_DHEADS, HEAD_DIM, HEAD_DIM)),
        lspec((SUBLANES, 3 * D_DELTA)),
        lspec((SUBLANES, D_SCONV)),
        lspec((CTAIL, D_CONF)),
    ] + _weight_specs(wts, lambda l, bi: l)
    return pl.pallas_call(
        _sample_kernel,
        grid=(depth, b // nb),
        in_specs=in_specs,
        out_specs=out_specs,
        out_shape=out_shape,
        scratch_shapes=scratch,
        compiler_params=pltpu.CompilerParams(
            dimension_semantics=("arbitrary", "arbitrary"),
            vmem_limit_bytes=VMEM_LIMIT),
        name="sample_layers",
    )(x8, sd, sq8, ss8, sc32, *wts)


def _stacked_weights(norm_g, w_in, conv_qkv_w, a_log, dt_bias, delta_norm_g, sconv_w,
                     cconv_w, cconv_b, cln_g, cln_b, w_out, final_norm_g):
    depth = w_in.shape[0]
    n_qkv = 3 * D_DELTA
    ba = jnp.pad(w_in[:, :, n_qkv:n_qkv + 2 * N_DHEADS], ((0, 0), (0, 0), (0, LANES - 2 * N_DHEADS)))
    wrest = jnp.concatenate([w_in[:, :, n_qkv + 2 * N_DHEADS:], ba], axis=2).astype(BF16)
    lane_pad = ((0, 0), (N_DHEADS, LANES - 2 * N_DHEADS))
    return (
        norm_g.reshape(depth, 1, D_MODEL),
        w_in[:, :, :n_qkv].astype(BF16),
        wrest,
        conv_qkv_w,
        jnp.pad(a_log, lane_pad).reshape(depth, 1, LANES),
        jnp.pad(dt_bias, lane_pad).reshape(depth, 1, LANES),
        delta_norm_g.reshape(depth, 1, HEAD_DIM),
        sconv_w,
        cconv_w,
        cconv_b.reshape(depth, 1, D_CONF),
        cln_g.reshape(depth, 1, D_CONF),
        cln_b.reshape(depth, 1, D_CONF),
        w_out.astype(BF16),
        final_norm_g.reshape(1, D_MODEL),
    )


def kernel(x_prompt, x_sample, state_delta, state_qkv_conv, state_sconv, state_cconv, norm_g, w_in, conv_qkv_w, a_log, dt_bias, delta_norm_g, sconv_w, cconv_w, cconv_b, cln_g, cln_b, w_out, final_norm_g):
    depth = w_in.shape[0]
    dec_seq = x_sample.shape[1]
    assert x_prompt.shape[1] % TIME_TILE == 0
    assert dec_seq == SAMPLE_PAD // 2 and x_sample.shape[0] % SAMPLE_BATCH_BLOCK == 0

    wts = _stacked_weights(norm_g, w_in, conv_qkv_w, a_log, dt_bias, delta_norm_g, sconv_w,
                           cconv_w, cconv_b, cln_g, cln_b, w_out, final_norm_g)

    xp = x_prompt
    p_outs = [[] for _ in range(4)]
    for l in range(depth):
        xp, pd, pq, ps, pc = _prompt_layer_pipelined(xp, wts, l, l == depth - 1)
        for acc, o in zip(p_outs, (pd, pq, ps, pc)):
            acc.append(o)
    pd, pq, ps, pc = (jnp.stack(o) for o in p_outs)

    def front_pad(a, rows):
        return jnp.pad(a, ((0, 0), (0, 0), (rows - a.shape[2], 0), (0, 0)))

    xs, sd, sq, ss, sc = _sample_layers(
        jnp.pad(x_sample, ((0, 0), (0, SAMPLE_PAD - dec_seq), (0, 0))), state_delta,
        front_pad(state_qkv_conv, SUBLANES), front_pad(state_sconv, SUBLANES),
        front_pad(state_cconv, CTAIL), wts)
    first = CTAIL + dec_seq - (CONF_W - 1)
    return (xp, xs[depth - 1, :, :dec_seq, :],
            pd, pq[:, :, SUBLANES - (QK_CONV - 1):, :], ps[:, :, SUBLANES - (SCONV_W - 1):, :],
            pc[:, :, CTAIL - (CONF_W - 1):, :],
            sd, sq[:, :, dec_seq - (QK_CONV - 1):dec_seq, :], ss[:, :, dec_seq - (SCONV_W - 1):dec_seq, :],
            sc[:, :, first:first + CONF_W - 1, :])
```

```python
import functools

import jax
import jax.numpy as jnp
from jax import lax
from jax.experimental import pallas as pl
from jax.experimental.pallas import tpu as pltpu

D_MODEL = 1024
N_DHEADS = 4
HEAD_DIM = 128
D_DELTA = N_DHEADS * HEAD_DIM
D_SCONV = 256
D_CONF = 256
QK_CONV = 4
SCONV_W = 3
CONF_W = 31
CHUNK = 64
EPS = 1e-6

R_GATE_D = 0
R_SB = 512
R_SC = 768
R_SX = 1024
R_GATE_S = 1280
R_GA = 1536
R_GB = 1792
R_GATE_C = 2048
R_BA = 2304
R_WIDTH = 2432
W_REST = 3 * D_DELTA

SUBLANES = 8
LANES = 128
TIME_TILE = 256
SAMPLE_PAD = 8
SAMPLE_BATCH_BLOCK = 16
SAMPLE_SEQ_LOCKSTEP = 4
CTAIL = 32
VMEM_LIMIT = 56 * 1024 * 1024

F32 = jnp.float32
BF16 = jnp.bfloat16


def _mm(a, b):
    return jnp.dot(a.astype(BF16), b.astype(BF16), preferred_element_type=F32)


def _mm_nt(a, b):
    return lax.dot_general(a.astype(BF16), b.astype(BF16), (((1,), (1,)), ((), ())),
                           preferred_element_type=F32)


def _mm_tn(a, b):
    return lax.dot_general(a.astype(BF16), b.astype(BF16), (((0,), (0,)), ((), ())),
                           preferred_element_type=F32)


def _sigmoid(x):
    return 1.0 / (1.0 + jnp.exp(-x))


def _silu(x):
    return x * _sigmoid(x)


def _softplus(x):
    return jnp.maximum(x, 0.0) + jnp.log1p(jnp.exp(-jnp.abs(x)))


def _rms_rows(x, g):
    return x * lax.rsqrt(jnp.mean(x * x, axis=-1, keepdims=True) + EPS) * g


def _chunk_cumsum(g, chunk, row):
    pos = row % chunk
    s = 1
    while s < chunk:
        g = g + jnp.where(pos >= s, pltpu.roll(g, s, axis=0), 0.0)
        s *= 2
    return g


def _run(staged):
    try:
        while True:
            next(staged)
    except StopIteration as stop:
        return stop.value


def _interleave(*staged):
    live = list(staged)
    while live:
        for s in list(live):
            try:
                next(s)
            except StopIteration:
                live.remove(s)


def _pair_diag(a, b):
    z = jnp.zeros_like(a)
    return jnp.concatenate([jnp.concatenate([a, z], axis=1), jnp.concatenate([z, b], axis=1)], axis=0)


class _Packed:
    def __init__(self, chunk, rows, bf16_tile):
        self.chunk, self.rows, self.n = chunk, rows, rows // chunk
        self.rr = lax.broadcasted_iota(jnp.int32, (chunk, rows), 0)
        lane = lax.broadcasted_iota(jnp.int32, (chunk, rows), 1)
        self.jl = lane % chunk
        self.lane_blk = lane // chunk
        ii = lax.broadcasted_iota(jnp.int32, (rows, rows), 0)
        jj = lax.broadcasted_iota(jnp.int32, (rows, rows), 1)
        mask = jnp.where((ii // chunk) == (jj // chunk), 1.0, 0.0).astype(F32)
        self.bd_mask = mask.astype(BF16) if bf16_tile else mask

    def pack(self, g):
        out = g[0:self.chunk]
        for c in range(1, self.n):
            out = jnp.where(self.lane_blk == c, g[c * self.chunk:(c + 1) * self.chunk], out)
        return out

    def col(self, v):
        shape = (self.chunk, self.rows)
        out = jnp.broadcast_to(v[0:self.chunk], shape)
        for c in range(1, self.n):
            out = jnp.where(self.lane_blk == c,
                            jnp.broadcast_to(v[c * self.chunk:(c + 1) * self.chunk], shape), out)
        return out

    def block_diag(self, xp):
        if self.bd_mask.dtype == BF16:
            return jnp.concatenate([xp.astype(BF16)] * self.n, axis=0) * self.bd_mask
        return (jnp.concatenate([xp] * self.n, axis=0) * self.bd_mask).astype(BF16)

    def unit_lower_inverse(self, lps, nil):
        base = min(self.chunk, 16)
        same = (self.rr // base) == (self.jl // base)
        eye = jnp.where(self.rr == self.jl, 1.0, 0.0).astype(F32)
        ds = [jnp.where(same, lp, 0.0) for lp in lps]
        xs = [eye - d for d in ds]
        d_bds = [self.block_diag(d) for d in ds]
        p = 2
        while p < min(base, nil):
            ds = [_mm(d, d_bd) for d, d_bd in zip(ds, d_bds)]
            yield
            d_bds = [self.block_diag(d) for d in ds]
            xs = [x + _mm(x, d_bd) for x, d_bd in zip(xs, d_bds)]
            yield
            p *= 2
        size = base
        while size < self.chunk:
            big = (self.rr // (2 * size)) == (self.jl // (2 * size))
            off = jnp.logical_and(big, jnp.logical_not(same))
            xes = [_mm(x, self.block_diag(jnp.where(off, lp, 0.0))) for x, lp in zip(xs, lps)]
            yield
            xs = [x - _mm(xe, self.block_diag(x)) for x, xe in zip(xs, xes)]
            yield
            same = big
            size *= 2
        return xs

    def delta_prep(self, heads, nil):
        tril = self.rr >= self.jl
        strict = self.rr > self.jl
        decays, kbs, gs = [], [], []
        for q, k, v, beta, gc_col, gc_row in heads:
            diff = self.col(gc_col) - gc_row
            decays.append(jnp.where(tril, jnp.exp(jnp.where(tril, diff, 0.0)), 0.0))
            kbs.append(k * beta)
        for (q, k, *_), kb in zip(heads, kbs):
            gs.append(_mm_nt(jnp.concatenate([kb, q], axis=0), k))
        yield
        lps = [jnp.where(strict, self.pack(g[:self.rows]) * dec, 0.0) for g, dec in zip(gs, decays)]
        qkps = [self.pack(g[self.rows:]) * dec for g, dec in zip(gs, decays)]
        tinvs = yield from self.unit_lower_inverse(lps, nil)
        out = []
        for (q, k, v, beta, gc_col, _), kb, tinv, qkp in zip(heads, kbs, tinvs, qkps):
            eg = jnp.exp(gc_col)
            uw = _mm(self.block_diag(tinv), jnp.concatenate([v * beta, kb * eg], axis=1))
            out.append((uw[:, :HEAD_DIM], uw[:, HEAD_DIM:], q * eg, qkp))
        yield
        return out


def _branch_s(buf_ref, w_ref, tail, rows, lead=()):
    n = len(lead)
    acc = None
    for j in range(SCONV_W):
        idx = lead + (pl.ds(tail - (SCONV_W - 1) + j, rows), slice(None))
        term = buf_ref[idx] * w_ref[j:j + 1, :].reshape((1,) * n + (1, D_SCONV))
        acc = term if acc is None else acc + term
    return acc


def _conformer(ubuf_ref, w_ref, b_ref, g_ref, beta_ref, tail, rows, lead=()):
    n = len(lead)
    shp = (1,) * n + (1, D_CONF)
    acc = None
    for j in range(CONF_W):
        idx = lead + (pl.ds(tail - (CONF_W - 1) + j, rows), slice(None))
        term = ubuf_ref[idx] * w_ref[j:j + 1, :].reshape(shp)
        acc = term if acc is None else acc + term
    yc = acc + b_ref[...].reshape(shp)
    mu = jnp.mean(yc, axis=-1, keepdims=True)
    var = jnp.mean(jnp.square(yc - mu), axis=-1, keepdims=True)
    yc = (yc - mu) * lax.rsqrt(var + EPS) * g_ref[...].reshape(shp) + beta_ref[...].reshape(shp)
    return _silu(yc)


def _prompt_kernel(final_norm,
                   x_ref, ng_ref, wqkv_ref, wrest_ref, cw_ref, alog_ref, dt_ref, dng_ref,
                   sw_ref, ccw_ref, ccb_ref, clg_ref, clb_ref, wout_ref, fg_ref,
                   y_ref, s_out_ref, qt_ref, st_ref, ct_ref,
                   qbuf, sbuf, cbuf, cshift, s_scr, obuf, rest,
                   u_s, lhs_s, kdt_s, egl_s, qk_s):
    i = pl.program_id(1)
    nt = pl.num_programs(1)
    tt = TIME_TILE
    nchunk = tt // CHUNK

    @pl.when(i == 0)
    def _():
        qbuf[0:SUBLANES, :] = jnp.zeros((SUBLANES, 3 * D_DELTA), F32)
        sbuf[0:SUBLANES, :] = jnp.zeros((SUBLANES, D_SCONV), F32)
        cbuf[0:CTAIL, :] = jnp.zeros((CTAIL, D_CONF), F32)
        s_scr[...] = jnp.zeros(s_scr.shape, F32)

    x = x_ref[0]
    h = _rms_rows(x, ng_ref[...]).astype(BF16)
    qbuf[SUBLANES:SUBLANES + tt, :] = jnp.dot(h, wqkv_ref[...], preferred_element_type=F32)
    rest[...] = jnp.dot(h, wrest_ref[...], preferred_element_type=F32)

    sbuf[SUBLANES:SUBLANES + tt, :] = rest[:, R_SC:R_SC + D_SCONV] * rest[:, R_SX:R_SX + D_SCONV]
    ys = _branch_s(sbuf, sw_ref, SUBLANES, tt)
    obuf[:, D_DELTA:D_DELTA + D_SCONV] = (rest[:, R_SB:R_SB + D_SCONV] * ys
                                          * _silu(rest[:, R_GATE_S:R_GATE_S + D_SCONV]))

    cbuf[CTAIL:CTAIL + tt, :] = rest[:, R_GA:R_GA + D_CONF] * _sigmoid(rest[:, R_GB:R_GB + D_CONF])
    for sh in range(1, SUBLANES):
        cshift[sh - 1] = cbuf[pl.ds(sh, tt + CTAIL - SUBLANES), :]

    row = lax.broadcasted_iota(jnp.int32, (tt, LANES), 0)
    ba = rest[:, R_BA:R_BA + LANES]
    beta_all = _sigmoid(ba)
    g_all = -jnp.exp(alog_ref[...]) * _softplus(ba + dt_ref[...])
    gc = _chunk_cumsum(g_all, CHUNK, row)
    gct = gc.T
    gl_rows = jnp.concatenate(
        [jnp.broadcast_to(gc[(c + 1) * CHUNK - 1:(c + 1) * CHUNK, :], (CHUNK, LANES))
         for c in range(nchunk)], axis=0)
    pk = _Packed(CHUNK, tt, True)

    def conv_act(c0):
        acc = None
        for j in range(QK_CONV):
            term = (qbuf[pl.ds(SUBLANES - (QK_CONV - 1) + j, tt), c0:c0 + HEAD_DIM]
                    * cw_ref[j:j + 1, c0:c0 + HEAD_DIM])
            acc = term if acc is None else acc + term
        return _silu(acc)

    heads, kdecs = [], []
    for hd in range(N_DHEADS):
        q = conv_act(hd * HEAD_DIM)
        k = conv_act(D_DELTA + hd * HEAD_DIM)
        v = conv_act(2 * D_DELTA + hd * HEAD_DIM)
        q = q * lax.rsqrt(jnp.sum(q * q, axis=-1, keepdims=True) + EPS) * (HEAD_DIM ** -0.5)
        k = k * lax.rsqrt(jnp.sum(k * k, axis=-1, keepdims=True) + EPS)
        lane = N_DHEADS + hd
        gcol = gc[:, lane:lane + 1]
        heads.append((q, k, v, beta_all[:, hd:hd + 1], gcol, gct[lane:lane + 1, :]))
        kdecs.append(k * jnp.exp(gl_rows[:, lane:lane + 1] - gcol))
    prepped = pk.delta_prep(heads, CHUNK)

    for pr in range(N_DHEADS // 2):
        h0, h1 = 2 * pr, 2 * pr + 1
        (u0, w0, qg0, qkp0), (u1, w1, qg1, qkp1) = prepped[h0], prepped[h1]
        kd0, kd1 = kdecs[h0], kdecs[h1]
        u_s[pr] = jnp.concatenate([u0, u1], axis=1)
        w = jnp.concatenate([w0, w1], axis=1).astype(BF16)
        qg = jnp.concatenate([qg0, qg1], axis=1).astype(BF16)
        kst = jnp.concatenate(
            [kd[c * CHUNK:(c + 1) * CHUNK] for c in range(nchunk) for kd in (kd0, kd1)], axis=0)
        kdt = kst.T.astype(BF16)
        low = (lax.broadcasted_iota(jnp.int32, (CHUNK, tt), 1) % LANES) < CHUNK
        qk_even = jnp.where(low, qkp0, pltpu.roll(qkp1, CHUNK, axis=1)).astype(BF16)
        qk_odd = jnp.where(low, pltpu.roll(qkp0, tt - CHUNK, axis=1), qkp1).astype(BF16)
        for c in range(nchunk):
            cs = slice(c * CHUNK, (c + 1) * CHUNK)
            lhs_s[pr, c, 0:CHUNK, :] = w[cs]
            lhs_s[pr, c, CHUNK:, :] = qg[cs]
            kdt_s[pr, c] = kdt[:, 2 * c * CHUNK:2 * (c + 1) * CHUNK]
            tile = slice((c // 2) * LANES, (c // 2 + 1) * LANES)
            qk_s[pr, c] = (qk_even if c % 2 == 0 else qk_odd)[:, tile]
            ge = (c + 1) * CHUNK - 1
            egl = jnp.concatenate(
                [jnp.broadcast_to(jnp.exp(gc[ge:ge + 1, N_DHEADS + hh:N_DHEADS + hh + 1]),
                                  (SUBLANES, HEAD_DIM)) for hh in (h0, h1)], axis=1)
            egl_s[c, :, 2 * pr * HEAD_DIM:2 * (pr + 1) * HEAD_DIM] = egl

    def chunk_body(c, carry):
        r0 = pl.multiple_of(c * CHUNK, CHUNK)
        rows_c = pl.ds(r0, CHUNK)
        pairs = range(N_DHEADS // 2)
        ss = [s_scr[pr] for pr in pairs]
        rs = [jnp.dot(lhs_s[pr, c], _pair_diag(s[:, :HEAD_DIM], s[:, HEAD_DIM:]).astype(BF16),
                      preferred_element_type=F32) for pr, s in zip(pairs, ss)]
        vns = [u_s[pr, rows_c, :] - r[:CHUNK] for pr, r in zip(pairs, rs)]
        v_bds = [_pair_diag(vn[:, :HEAD_DIM], vn[:, HEAD_DIM:]).astype(BF16) for vn in vns]
        upds = [jnp.dot(kdt_s[pr, c], v_bd, preferred_element_type=F32) for pr, v_bd in zip(pairs, v_bds)]
        o_intra = [jnp.dot(qk_s[pr, c], v_bd, preferred_element_type=F32) for pr, v_bd in zip(pairs, v_bds)]
        for pr in pairs:
            egl = egl_s[c][0:1, 2 * pr * HEAD_DIM:2 * (pr + 1) * HEAD_DIM]
            s_scr[pr] = ss[pr] * egl + upds[pr]
            o_pair = rs[pr][CHUNK:] + o_intra[pr]
            for hl in range(2):
                hd = 2 * pr + hl
                o = _rms_rows(o_pair[:, hl * HEAD_DIM:(hl + 1) * HEAD_DIM], dng_ref[...])
                gate = rest[rows_c, R_GATE_D + hd * HEAD_DIM:R_GATE_D + (hd + 1) * HEAD_DIM]
                obuf[rows_c, hd * HEAD_DIM:(hd + 1) * HEAD_DIM] = o * _silu(gate)
        acc = None
        for j in range(CONF_W):
            off = CTAIL - (CONF_W - 1) + j
            start = pl.multiple_of(r0 + (off // SUBLANES) * SUBLANES, SUBLANES)
            sh = off % SUBLANES
            src = cbuf[pl.ds(start, CHUNK), :] if sh == 0 else cshift[sh - 1, pl.ds(start, CHUNK), :]
            term = src * ccw_ref[j:j + 1, :]
            acc = term if acc is None else acc + term
        yc = acc + ccb_ref[...]
        mu = jnp.mean(yc, axis=-1, keepdims=True)
        var = jnp.mean(jnp.square(yc - mu), axis=-1, keepdims=True)
        yc = (yc - mu) * lax.rsqrt(var + EPS) * clg_ref[...] + clb_ref[...]
        obuf[rows_c, D_DELTA + D_SCONV:] = _silu(yc) * _silu(rest[rows_c, R_GATE_C:R_GATE_C + D_CONF])
        return carry

    lax.fori_loop(0, nchunk, chunk_body, 0)

    y = x_ref[0] + jnp.dot(obuf[...].astype(BF16), wout_ref[...], preferred_element_type=F32)
    if final_norm:
        y = _rms_rows(y, fg_ref[...])
    y_ref[0] = y

    @pl.when(i == nt - 1)
    def _():
        for pr in range(N_DHEADS // 2):
            s_out_ref[0, 2 * pr] = s_scr[pr, :, :HEAD_DIM]
            s_out_ref[0, 2 * pr + 1] = s_scr[pr, :, HEAD_DIM:]
        qt_ref[0] = qbuf[tt:tt + SUBLANES, :]
        st_ref[0] = sbuf[tt:tt + SUBLANES, :]
        ct_ref[0] = cbuf[tt:tt + CTAIL, :]

    qbuf[0:SUBLANES, :] = qbuf[tt:tt + SUBLANES, :]
    sbuf[0:SUBLANES, :] = sbuf[tt:tt + SUBLANES, :]
    cbuf[0:CTAIL, :] = cbuf[tt:tt + CTAIL, :]


def _weight_specs(wts, layer_of):
    specs = []
    for w in wts[:-1]:
        tail = (0,) * (w.ndim - 1)
        specs.append(pl.BlockSpec((None,) + w.shape[1:], lambda *ids, tail=tail: (layer_of(*ids),) + tail))
    specs.append(pl.BlockSpec(wts[-1].shape, lambda *ids: (0,) * wts[-1].ndim))
    return specs


def _prompt_layer(x, wts, layer, final_norm):
    b, t, d = x.shape
    nt = t // TIME_TILE
    npair = N_DHEADS // 2
    nchunk = TIME_TILE // CHUNK
    weight_specs = _weight_specs(wts, lambda *ids: layer)
    out_shape = (
        jax.ShapeDtypeStruct((b, t, d), F32),
        jax.ShapeDtypeStruct((b, N_DHEADS, HEAD_DIM, HEAD_DIM), F32),
        jax.ShapeDtypeStruct((b, SUBLANES, 3 * D_DELTA), F32),
        jax.ShapeDtypeStruct((b, SUBLANES, D_SCONV), F32),
        jax.ShapeDtypeStruct((b, CTAIL, D_CONF), F32),
    )
    out_specs = (
        pl.BlockSpec((1, TIME_TILE, d), lambda bi, ti: (bi, ti, 0)),
        pl.BlockSpec((1, N_DHEADS, HEAD_DIM, HEAD_DIM), lambda bi, ti: (bi, 0, 0, 0)),
        pl.BlockSpec((1, SUBLANES, 3 * D_DELTA), lambda bi, ti: (bi, 0, 0)),
        pl.BlockSpec((1, SUBLANES, D_SCONV), lambda bi, ti: (bi, 0, 0)),
        pl.BlockSpec((1, CTAIL, D_CONF), lambda bi, ti: (bi, 0, 0)),
    )
    scratch = [
        pltpu.VMEM((TIME_TILE + SUBLANES, 3 * D_DELTA), F32),
        pltpu.VMEM((TIME_TILE + SUBLANES, D_SCONV), F32),
        pltpu.VMEM((TIME_TILE + CTAIL, D_CONF), F32),
        pltpu.VMEM((SUBLANES - 1, TIME_TILE + CTAIL - SUBLANES, D_CONF), F32),
        pltpu.VMEM((npair, HEAD_DIM, 2 * HEAD_DIM), F32),
        pltpu.VMEM((TIME_TILE, D_MODEL), F32),
        pltpu.VMEM((TIME_TILE, R_WIDTH), F32),
        pltpu.VMEM((npair, TIME_TILE, 2 * HEAD_DIM), F32),
        pltpu.VMEM((npair, nchunk, 2 * CHUNK, 2 * HEAD_DIM), BF16),
        pltpu.VMEM((npair, nchunk, HEAD_DIM, 2 * CHUNK), BF16),
        pltpu.VMEM((nchunk, SUBLANES, N_DHEADS * HEAD_DIM), F32),
        pltpu.VMEM((npair, nchunk, CHUNK, 2 * CHUNK), BF16),
    ]
    return pl.pallas_call(
        functools.partial(_prompt_kernel, final_norm),
        grid=(b, nt),
        in_specs=[pl.BlockSpec((1, TIME_TILE, d), lambda bi, ti: (bi, ti, 0))] + weight_specs,
        out_specs=out_specs,
        out_shape=out_shape,
        scratch_shapes=scratch,
        compiler_params=pltpu.CompilerParams(
            dimension_semantics=("arbitrary", "arbitrary"),
            vmem_limit_bytes=VMEM_LIMIT),
        name="prompt_layer",
    )(x, *wts)


def _pipe_kernel(final_norm, nt,
                 x_ref, ng_ref, win_ref, cw_ref, alog_ref, dt_ref, dng_ref,
                 sw_ref, ccw_ref, ccb_ref, clg_ref, clb_ref, wout_ref, fg_ref,
                 y_ref, s_out_ref, qt_ref, st_ref, ct_ref,
                 qbuf, sbuf, s_scr, xs, qa, bg, rest, cbuf, cshift, obuf):
    g = pl.program_id(0)
    n_tiles = pl.num_programs(0) - 1
    tt = TIME_TILE
    nchunk = tt // CHUNK
    npair = N_DHEADS // 2
    t1 = jnp.minimum(g, n_tiles - 1) % nt
    t2 = jnp.maximum(g - 1, 0) % nt
    p = g % 2
    q = 1 - p

    @pl.when(g == 0)
    def _():
        xs[1] = jnp.zeros(xs.shape[1:], F32)
        qa[1] = jnp.zeros(qa.shape[1:], F32)
        bg[1] = jnp.zeros(bg.shape[1:], F32)
        rest[1] = jnp.zeros(rest.shape[1:], F32)
        cbuf[1] = jnp.zeros(cbuf.shape[1:], F32)
        cshift[1] = jnp.zeros(cshift.shape[1:], F32)
        obuf[1] = jnp.zeros(obuf.shape[1:], F32)

    @pl.when(t1 == 0)
    def _():
        qbuf[0:SUBLANES, :] = jnp.zeros((SUBLANES, 3 * D_DELTA), F32)
        sbuf[0:SUBLANES, :] = jnp.zeros((SUBLANES, D_SCONV), F32)

    @pl.when(t2 == 0)
    def _():
        s_scr[...] = jnp.zeros(s_scr.shape, F32)

    def stage1():
        x = x_ref[0]
        xs[p] = x
        h = _rms_rows(x, ng_ref[...]).astype(BF16)
        yield
        for c0 in range(0, 3 * D_DELTA, 2 * LANES):
            qbuf[SUBLANES:SUBLANES + tt, c0:c0 + 2 * LANES] = jnp.dot(
                h, win_ref[:, c0:c0 + 2 * LANES], preferred_element_type=F32)
            yield
        for c0 in range(0, R_WIDTH, 2 * LANES):
            c1 = min(c0 + 2 * LANES, R_WIDTH)
            rest[p, :, c0:c1] = jnp.dot(h, win_ref[:, W_REST + c0:W_REST + c1],
                                        preferred_element_type=F32)
            yield

        def conv_act(c0):
            acc = None
            for j in range(QK_CONV):
                term = (qbuf[pl.ds(SUBLANES - (QK_CONV - 1) + j, tt), c0:c0 + HEAD_DIM]
                        * cw_ref[j:j + 1, c0:c0 + HEAD_DIM])
                acc = term if acc is None else acc + term
            return _silu(acc)

        for hd in range(N_DHEADS):
            qh = conv_act(hd * HEAD_DIM)
            qa[p, :, hd * HEAD_DIM:(hd + 1) * HEAD_DIM] = (
                qh * lax.rsqrt(jnp.sum(qh * qh, axis=-1, keepdims=True) + EPS) * (HEAD_DIM ** -0.5))
            kh = conv_act(D_DELTA + hd * HEAD_DIM)
            qa[p, :, D_DELTA + hd * HEAD_DIM:D_DELTA + (hd + 1) * HEAD_DIM] = (
                kh * lax.rsqrt(jnp.sum(kh * kh, axis=-1, keepdims=True) + EPS))
            qa[p, :, 2 * D_DELTA + hd * HEAD_DIM:2 * D_DELTA + (hd + 1) * HEAD_DIM] = conv_act(
                2 * D_DELTA + hd * HEAD_DIM)
            yield

        row = lax.broadcasted_iota(jnp.int32, (tt, LANES), 0)
        ba = rest[p, :, R_BA:R_BA + LANES]
        bg[p, 0] = _sigmoid(ba)
        bg[p, 1] = _chunk_cumsum(-jnp.exp(alog_ref[...]) * _softplus(ba + dt_ref[...]), CHUNK, row)
        yield

        sbuf[SUBLANES:SUBLANES + tt, :] = (rest[p, :, R_SC:R_SC + D_SCONV]
                                           * rest[p, :, R_SX:R_SX + D_SCONV])
        ys = _branch_s(sbuf, sw_ref, SUBLANES, tt)
        obuf[p, :, D_DELTA:D_DELTA + D_SCONV] = (rest[p, :, R_SB:R_SB + D_SCONV] * ys
                                                 * _silu(rest[p, :, R_GATE_S:R_GATE_S + D_SCONV]))
        yield

        prev_tail = jnp.where(t1 == 0, 0.0, cbuf[q, tt:tt + CTAIL, :])
        cbuf[p, 0:CTAIL, :] = prev_tail
        cbuf[p, CTAIL:CTAIL + tt, :] = (rest[p, :, R_GA:R_GA + D_CONF]
                                        * _sigmoid(rest[p, :, R_GB:R_GB + D_CONF]))
        yield
        for sh in range(1, SUBLANES):
            cshift[p, sh - 1] = cbuf[p, pl.ds(sh, tt + CTAIL - SUBLANES), :]
        yield

        qbuf[0:SUBLANES, :] = qbuf[tt:tt + SUBLANES, :]
        sbuf[0:SUBLANES, :] = sbuf[tt:tt + SUBLANES, :]

    def stage2():
        beta_all = bg[q, 0]
        gc = bg[q, 1]
        gct = gc.T
        gl_rows = jnp.concatenate(
            [jnp.broadcast_to(gc[(c + 1) * CHUNK - 1:(c + 1) * CHUNK, :], (CHUNK, LANES))
             for c in range(nchunk)], axis=0)
        pk = _Packed(CHUNK, tt, True)
        heads, kdecs = [], []
        for hd in range(N_DHEADS):
            lane = N_DHEADS + hd
            gcol = gc[:, lane:lane + 1]
            kh = qa[q, :, D_DELTA + hd * HEAD_DIM:D_DELTA + (hd + 1) * HEAD_DIM]
            heads.append((qa[q, :, hd * HEAD_DIM:(hd + 1) * HEAD_DIM], kh,
                          qa[q, :, 2 * D_DELTA + hd * HEAD_DIM:2 * D_DELTA + (hd + 1) * HEAD_DIM],
                          beta_all[:, hd:hd + 1], gcol, gct[lane:lane + 1, :]))
            kdecs.append(kh * jnp.exp(gl_rows[:, lane:lane + 1] - gcol))
        yield
        prepped = yield from pk.delta_prep(heads, CHUNK)

        us, lhss, kdts, qks = [], [], [], []
        low = (lax.broadcasted_iota(jnp.int32, (CHUNK, tt), 1) % LANES) < CHUNK
        for pr in range(npair):
            (u0, w0, qg0, qkp0), (u1, w1, qg1, qkp1) = prepped[2 * pr], prepped[2 * pr + 1]
            us.append(jnp.concatenate([u0, u1], axis=1))
            w = jnp.concatenate([w0, w1], axis=1).astype(BF16)
            qg = jnp.concatenate([qg0, qg1], axis=1).astype(BF16)
            lhss.append([jnp.concatenate([w[c * CHUNK:(c + 1) * CHUNK], qg[c * CHUNK:(c + 1) * CHUNK]],
                                         axis=0) for c in range(nchunk)])
            kst = jnp.concatenate([kd[c * CHUNK:(c + 1) * CHUNK] for c in range(nchunk)
                                   for kd in (kdecs[2 * pr], kdecs[2 * pr + 1])], axis=0)
            kdts.append(kst.T.astype(BF16))
            qks.append((jnp.where(low, qkp0, pltpu.roll(qkp1, CHUNK, axis=1)).astype(BF16),
                        jnp.where(low, pltpu.roll(qkp0, tt - CHUNK, axis=1), qkp1).astype(BF16)))
        yield

        ss = [s_scr[pr] for pr in range(npair)]
        for c in range(nchunk):
            cs = slice(c * CHUNK, (c + 1) * CHUNK)
            rs = [jnp.dot(lhss[pr][c], _pair_diag(s[:, :HEAD_DIM], s[:, HEAD_DIM:]).astype(BF16),
                          preferred_element_type=F32) for pr, s in enumerate(ss)]
            yield
            vns = [us[pr][cs] - r[:CHUNK] for pr, r in enumerate(rs)]
            v_bds = [_pair_diag(vn[:, :HEAD_DIM], vn[:, HEAD_DIM:]).astype(BF16) for vn in vns]
            upds = [jnp.dot(kdts[pr][:, 2 * c * CHUNK:2 * (c + 1) * CHUNK], v_bd,
                            preferred_element_type=F32) for pr, v_bd in enumerate(v_bds)]
            tile = slice((c // 2) * LANES, (c // 2 + 1) * LANES)
            o_intra = [jnp.dot(qks[pr][c % 2][:, tile], v_bd, preferred_element_type=F32)
                       for pr, v_bd in enumerate(v_bds)]
            ge = (c + 1) * CHUNK - 1
            for pr in range(npair):
                egl = jnp.concatenate(
                    [jnp.broadcast_to(jnp.exp(gc[ge:ge + 1, N_DHEADS + hh:N_DHEADS + hh + 1]),
                                      (1, HEAD_DIM)) for hh in (2 * pr, 2 * pr + 1)], axis=1)
                ss[pr] = ss[pr] * egl + upds[pr]
                o_pair = rs[pr][CHUNK:] + o_intra[pr]
                for hl in range(2):
                    hd = 2 * pr + hl
                    o = _rms_rows(o_pair[:, hl * HEAD_DIM:(hl + 1) * HEAD_DIM], dng_ref[...])
                    gate = rest[q, cs, R_GATE_D + hd * HEAD_DIM:R_GATE_D + (hd + 1) * HEAD_DIM]
                    obuf[q, cs, hd * HEAD_DIM:(hd + 1) * HEAD_DIM] = o * _silu(gate)
            yield
            acc = None
            for j in range(CONF_W):
                off = CTAIL - (CONF_W - 1) + j
                start = c * CHUNK + (off // SUBLANES) * SUBLANES
                sh = off % SUBLANES
                src = (cbuf[q, start:start + CHUNK, :] if sh == 0
                       else cshift[q, sh - 1, start:start + CHUNK, :])
                term = src * ccw_ref[j:j + 1, :]
                acc = term if acc is None else acc + term
            yc = acc + ccb_ref[...]
            mu = jnp.mean(yc, axis=-1, keepdims=True)
            var = jnp.mean(jnp.square(yc - mu), axis=-1, keepdims=True)
            yc = (yc - mu) * lax.rsqrt(var + EPS) * clg_ref[...] + clb_ref[...]
            obuf[q, cs, D_DELTA + D_SCONV:] = _silu(yc) * _silu(rest[q, cs, R_GATE_C:R_GATE_C + D_CONF])
            yield
        for pr in range(npair):
            s_scr[pr] = ss[pr]

        y = xs[q] + jnp.dot(obuf[q].astype(BF16), wout_ref[...], preferred_element_type=F32)
        if final_norm:
            y = _rms_rows(y, fg_ref[...])
        y_ref[0] = y

    _interleave(stage1(), stage2())

    @pl.when(jnp.logical_and(g >= 1, t2 == nt - 1))
    def _():
        for pr in range(npair):
            s_out_ref[0, 2 * pr] = s_scr[pr, :, :HEAD_DIM]
            s_out_ref[0, 2 * pr + 1] = s_scr[pr, :, HEAD_DIM:]

    @pl.when(jnp.logical_and(g < n_tiles, t1 == nt - 1))
    def _():
        qt_ref[0] = qbuf[0:SUBLANES, :]
        st_ref[0] = sbuf[0:SUBLANES, :]
        ct_ref[0] = cbuf[p, tt:tt + CTAIL, :]


def _prompt_layer_pipelined(x, wts, layer, final_norm):
    b, t, d = x.shape
    nt = t // TIME_TILE
    n_tiles = b * nt
    npair = N_DHEADS // 2

    def tile1(g):
        g1 = jnp.minimum(g, n_tiles - 1)
        return g1 // nt, g1 % nt

    def tile2(g):
        g2 = jnp.maximum(g - 1, 0)
        return g2 // nt, g2 % nt

    out_shape = (
        jax.ShapeDtypeStruct((b, t, d), F32),
        jax.ShapeDtypeStruct((b, N_DHEADS, HEAD_DIM, HEAD_DIM), F32),
        jax.ShapeDtypeStruct((b, SUBLANES, 3 * D_DELTA), F32),
        jax.ShapeDtypeStruct((b, SUBLANES, D_SCONV), F32),
        jax.ShapeDtypeStruct((b, CTAIL, D_CONF), F32),
    )
    out_specs = (
        pl.BlockSpec((1, TIME_TILE, d), lambda g: tile2(g) + (0,)),
        pl.BlockSpec((1, N_DHEADS, HEAD_DIM, HEAD_DIM), lambda g: (tile2(g)[0], 0, 0, 0)),
        pl.BlockSpec((1, SUBLANES, 3 * D_DELTA), lambda g: (tile1(g)[0], 0, 0)),
        pl.BlockSpec((1, SUBLANES, D_SCONV), lambda g: (tile1(g)[0], 0, 0)),
        pl.BlockSpec((1, CTAIL, D_CONF), lambda g: (tile1(g)[0], 0, 0)),
    )
    scratch = [
        pltpu.VMEM((TIME_TILE + SUBLANES, 3 * D_DELTA), F32),
        pltpu.VMEM((TIME_TILE + SUBLANES, D_SCONV), F32),
        pltpu.VMEM((npair, HEAD_DIM, 2 * HEAD_DIM), F32),
        pltpu.VMEM((2, TIME_TILE, D_MODEL), F32),
        pltpu.VMEM((2, TIME_TILE, 3 * D_DELTA), F32),
        pltpu.VMEM((2, 2, TIME_TILE, LANES), F32),
        pltpu.VMEM((2, TIME_TILE, R_WIDTH), F32),
        pltpu.VMEM((2, TIME_TILE + CTAIL, D_CONF), F32),
        pltpu.VMEM((2, SUBLANES - 1, TIME_TILE + CTAIL - SUBLANES, D_CONF), F32),
        pltpu.VMEM((2, TIME_TILE, D_MODEL), F32),
    ]
    return pl.pallas_call(
        functools.partial(_pipe_kernel, final_norm, nt),
        grid=(n_tiles + 1,),
        in_specs=[pl.BlockSpec((1, TIME_TILE, d), lambda g: tile1(g) + (0,))]
        + _weight_specs(wts, lambda g: layer),
        out_specs=out_specs,
        out_shape=out_shape,
        scratch_shapes=scratch,
        compiler_params=pltpu.CompilerParams(
            dimension_semantics=("arbitrary",),
            vmem_limit_bytes=VMEM_LIMIT),
        name="prompt_layer",
    )(x, *wts)


def _sample_kernel(x_ref, sd_ref, sq_ref, ss_ref, sc_ref,
                   ng_ref, win_ref, cw_ref, alog_ref, dt_ref, dng_ref,
                   sw_ref, ccw_ref, ccb_ref, clg_ref, clb_ref, wout_ref, fg_ref,
                   y_ref, sd_out_ref, qt_ref, st_ref, ct_ref,
                   qbuf, sbuf, cbuf, obuf, ubuf, wqbuf, kdbuf, glbuf, oibuf, rest, xcarry):
    layer = pl.program_id(0)
    blk = pl.program_id(1)
    nb = SAMPLE_BATCH_BLOCK
    pad = SAMPLE_PAD
    ntok = pad // 2
    rows = nb * pad

    @pl.when(layer == 0)
    def _():
        xcarry[blk, :, 0:ntok, :] = x_ref[...]
        xcarry[blk, :, ntok:, :] = jnp.zeros((nb, pad - ntok, D_MODEL), F32)

    x = xcarry[blk].reshape(rows, D_MODEL)
    h = _rms_rows(x, ng_ref[...]).astype(BF16)
    qkv = jnp.dot(h, win_ref[:, :W_REST], preferred_element_type=F32)
    rest[...] = jnp.dot(h, win_ref[:, W_REST:], preferred_element_type=F32)
    row = lax.broadcasted_iota(jnp.int32, (rows, LANES), 0)
    valid = (row % pad) < (pad // 2)
    valid1 = valid[:, 0:1]

    qkv3 = qkv.reshape(nb, pad, 3 * D_DELTA)
    qt_ref[...] = qkv3[:, ntok - (QK_CONV - 1):ntok, :]
    qbuf[:, SUBLANES - (QK_CONV - 1):SUBLANES, :] = sq_ref[...]
    qbuf[:, SUBLANES:, :] = qkv3

    hs = (rest[:, R_SC:R_SC + D_SCONV] * rest[:, R_SX:R_SX + D_SCONV]).reshape(nb, pad, D_SCONV)
    st_ref[...] = hs[:, ntok - (SCONV_W - 1):ntok, :]
    sbuf[:, SUBLANES - (SCONV_W - 1):SUBLANES, :] = ss_ref[...]
    sbuf[:, SUBLANES:, :] = hs
    ys = _branch_s(sbuf, sw_ref, SUBLANES, pad, lead=(slice(None),))
    obuf[:, D_DELTA:D_DELTA + D_SCONV] = (rest[:, R_SB:R_SB + D_SCONV] * ys.reshape(rows, D_SCONV)
                                          * _silu(rest[:, R_GATE_S:R_GATE_S + D_SCONV]))

    uc = rest[:, R_GA:R_GA + D_CONF] * _sigmoid(rest[:, R_GB:R_GB + D_CONF])
    cbuf[:, CTAIL - (CONF_W - 1):CTAIL, :] = sc_ref[...]
    cbuf[:, CTAIL:, :] = uc.reshape(nb, pad, D_CONF)
    first = CTAIL + ntok - (CONF_W - 1)
    ct_ref[...] = cbuf[:, first:first + CONF_W - 1, :]
    yc = _conformer(cbuf, ccw_ref, ccb_ref, clg_ref, clb_ref, CTAIL, pad, lead=(slice(None),))
    obuf[:, D_DELTA + D_SCONV:] = yc.reshape(rows, D_CONF) * _silu(rest[:, R_GATE_C:R_GATE_C + D_CONF])

    ba = rest[:, R_BA:R_BA + LANES]
    beta_all = jnp.where(valid, _sigmoid(ba), 0.0)
    g_all = jnp.where(valid, -jnp.exp(alog_ref[...]) * _softplus(ba + dt_ref[...]), 0.0)
    gc = _chunk_cumsum(g_all, pad, row)
    gct = gc.T
    gl_all = jnp.broadcast_to(
        gc.reshape(nb, pad, LANES)[:, pad - 1:pad, :], (nb, pad, LANES)).reshape(rows, LANES)
    pk = _Packed(pad, rows, False)

    def conv_act(c0):
        acc = None
        for j in range(QK_CONV):
            term = (qbuf[:, pl.ds(SUBLANES - (QK_CONV - 1) + j, pad), c0:c0 + HEAD_DIM]
                    * cw_ref[j:j + 1, c0:c0 + HEAD_DIM].reshape(1, 1, HEAD_DIM))
            acc = term if acc is None else acc + term
        return jnp.where(valid1, _silu(acc.reshape(rows, HEAD_DIM)), 0.0)

    for pr in range(N_DHEADS // 2):
        h0, h1 = 2 * pr, 2 * pr + 1
        heads, kdecs, egls = [], [], []
        for hd in (h0, h1):
            q = conv_act(hd * HEAD_DIM)
            k = conv_act(D_DELTA + hd * HEAD_DIM)
            v = conv_act(2 * D_DELTA + hd * HEAD_DIM)
            q = q * lax.rsqrt(jnp.sum(q * q, axis=-1, keepdims=True) + EPS) * (HEAD_DIM ** -0.5)
            k = k * lax.rsqrt(jnp.sum(k * k, axis=-1, keepdims=True) + EPS)
            lane = N_DHEADS + hd
            gcol = gc[:, lane:lane + 1]
            heads.append((q, k, v, beta_all[:, hd:hd + 1], gcol, gct[lane:lane + 1, :]))
            gl = gl_all[:, lane:lane + 1]
            kdecs.append(k * jnp.exp(gl - gcol))
            egls.append(jnp.broadcast_to(jnp.exp(gl), (rows, HEAD_DIM)))
        (u0, w0, qg0, qkp0), (u1, w1, qg1, qkp1) = _run(pk.delta_prep(heads, pad // 2))
        (kd0, kd1), (egl0, egl1) = kdecs, egls
        ubuf[...] = jnp.concatenate([u0, u1], axis=1)
        wqbuf[:, 0:pad, :] = jnp.concatenate([w0, w1], axis=1).reshape(nb, pad, 2 * HEAD_DIM)
        wqbuf[:, pad:, :] = jnp.concatenate([qg0, qg1], axis=1).reshape(nb, pad, 2 * HEAD_DIM)
        kdbuf[:, 0:pad, :] = kd0.reshape(nb, pad, HEAD_DIM)
        kdbuf[:, pad:, :] = kd1.reshape(nb, pad, HEAD_DIM)
        glbuf[...] = jnp.concatenate([egl0, egl1], axis=1).reshape(nb, pad, 2 * HEAD_DIM)

        def body(it, carry):
            bis = [it * SAMPLE_SEQ_LOCKSTEP + t for t in range(SAMPLE_SEQ_LOCKSTEP)]
            r0s = [pl.multiple_of(bi * pad, pad) for bi in bis]
            s0s = [sd_ref[bi, h0] for bi in bis]
            s1s = [sd_ref[bi, h1] for bi in bis]
            rs = [_mm(wqbuf[bi], _pair_diag(s0, s1)) for bi, s0, s1 in zip(bis, s0s, s1s)]
            vns = [ubuf[pl.ds(r0, pad), :] - r[:pad] for r0, r in zip(r0s, rs)]
            upds = [_mm_tn(kdbuf[bi], _pair_diag(vn[:, :HEAD_DIM], vn[:, HEAD_DIM:]))
                    for bi, vn in zip(bis, vns)]
            for bi, r0, s0, s1, r, vn, upd in zip(bis, r0s, s0s, s1s, rs, vns, upds):
                ubuf[pl.ds(r0, pad), :] = vn
                oibuf[pl.ds(r0, pad), :] = r[pad:]
                egl = glbuf[bi][0:1, :]
                sd_out_ref[bi, h0] = s0 * egl[:, :HEAD_DIM] + upd[:, :HEAD_DIM]
                sd_out_ref[bi, h1] = s1 * egl[:, HEAD_DIM:] + upd[:, HEAD_DIM:]
            return carry

        lax.fori_loop(0, nb // SAMPLE_SEQ_LOCKSTEP, body, 0)
        for hh, qkp in ((h0, qkp0), (h1, qkp1)):
            ls = slice((hh - h0) * HEAD_DIM, (hh - h0 + 1) * HEAD_DIM)
            o = oibuf[:, ls] + _mm(pk.block_diag(qkp), ubuf[:, ls])
            o = _rms_rows(o, dng_ref[...])
            obuf[:, hh * HEAD_DIM:(hh + 1) * HEAD_DIM] = (
                o * _silu(rest[:, R_GATE_D + hh * HEAD_DIM:R_GATE_D + (hh + 1) * HEAD_DIM]))

    y = x + jnp.dot(obuf[...].astype(BF16), wout_ref[...], preferred_element_type=F32)
    xcarry[blk] = y.reshape(nb, pad, D_MODEL)
    is_last = layer == pl.num_programs(0) - 1
    y_ref[...] = jnp.where(is_last, _rms_rows(y, fg_ref[...]), y).reshape(nb, pad, D_MODEL)[:, 0:ntok, :]


def _sample_layers(x, sd, sq, ss, sc, wts):
    depth = sd.shape[0]
    b, ntok, _ = x.shape
    nb = SAMPLE_BATCH_BLOCK

    def bspec(shape):
        return pl.BlockSpec((nb,) + shape, lambda l, bi: (bi,) + (0,) * len(shape))

    def lspec(shape):
        return pl.BlockSpec((None, nb) + shape, lambda l, bi: (l, bi) + (0,) * len(shape))

    def lshape(shape):
        return jax.ShapeDtypeStruct((depth, b) + shape, F32)

    state_shapes = ((N_DHEADS, HEAD_DIM, HEAD_DIM), (QK_CONV - 1, 3 * D_DELTA),
                    (SCONV_W - 1, D_SCONV), (CONF_W - 1, D_CONF))
    out_shape = (lshape((ntok, D_MODEL)),) + tuple(lshape(s) for s in state_shapes)
    out_specs = (lspec((ntok, D_MODEL)),) + tuple(lspec(s) for s in state_shapes)
    rows = nb * SAMPLE_PAD
    scratch = [
        pltpu.VMEM((nb, 2 * SUBLANES, 3 * D_DELTA), F32),
        pltpu.VMEM((nb, 2 * SUBLANES, D_SCONV), F32),
        pltpu.VMEM((nb, CTAIL + SAMPLE_PAD, D_CONF), F32),
        pltpu.VMEM((rows, D_MODEL), F32),
        pltpu.VMEM((rows, 2 * HEAD_DIM), F32),
        pltpu.VMEM((nb, 2 * SAMPLE_PAD, 2 * HEAD_DIM), F32),
        pltpu.VMEM((nb, 2 * SAMPLE_PAD, HEAD_DIM), F32),
        pltpu.VMEM((nb, SAMPLE_PAD, 2 * HEAD_DIM), F32),
        pltpu.VMEM((rows, 2 * HEAD_DIM), F32),
        pltpu.VMEM((rows, R_WIDTH), F32),
        pltpu.VMEM((b // nb, nb, SAMPLE_PAD, D_MODEL), F32),
    ]
    in_specs = ([bspec((ntok, D_MODEL))] + [lspec(s) for s in state_shapes]
                + _weight_specs(wts, lambda l, bi: l))
    return pl.pallas_call(
        _sample_kernel,
        grid=(depth, b // nb),
        in_specs=in_specs,
        out_specs=out_specs,
        out_shape=out_shape,
        scratch_shapes=scratch,
        compiler_params=pltpu.CompilerParams(
            dimension_semantics=("arbitrary", "arbitrary"),
            vmem_limit_bytes=VMEM_LIMIT),
        name="sample_layers",
    )(x, sd, sq, ss, sc, *wts)


def _stacked_weights(norm_g, w_in, conv_qkv_w, a_log, dt_bias, delta_norm_g, sconv_w,
                     cconv_w, cconv_b, cln_g, cln_b, w_out, final_norm_g):
    depth = w_in.shape[0]
    n_qkv = 3 * D_DELTA
    zeros = jnp.zeros(w_in.shape[:2] + (LANES - 2 * N_DHEADS,), w_in.dtype)
    w_all = jnp.concatenate([w_in[:, :, :n_qkv], w_in[:, :, n_qkv + 2 * N_DHEADS:],
                             w_in[:, :, n_qkv:n_qkv + 2 * N_DHEADS], zeros], axis=2).astype(BF16)
    lane_pad = ((0, 0), (N_DHEADS, LANES - 2 * N_DHEADS))
    return (
        norm_g.reshape(depth, 1, D_MODEL),
        w_all,
        conv_qkv_w,
        jnp.pad(a_log, lane_pad).reshape(depth, 1, LANES),
        jnp.pad(dt_bias, lane_pad).reshape(depth, 1, LANES),
        delta_norm_g.reshape(depth, 1, HEAD_DIM),
        sconv_w,
        cconv_w,
        cconv_b.reshape(depth, 1, D_CONF),
        cln_g.reshape(depth, 1, D_CONF),
        cln_b.reshape(depth, 1, D_CONF),
        w_out.astype(BF16),
        final_norm_g.reshape(1, D_MODEL),
    )


def kernel(x_prompt, x_sample, state_delta, state_qkv_conv, state_sconv, state_cconv, norm_g, w_in, conv_qkv_w, a_log, dt_bias, delta_norm_g, sconv_w, cconv_w, cconv_b, cln_g, cln_b, w_out, final_norm_g):
    depth = w_in.shape[0]
    dec_seq = x_sample.shape[1]
    assert x_prompt.shape[1] % TIME_TILE == 0
    assert dec_seq == SAMPLE_PAD // 2 and x_sample.shape[0] % SAMPLE_BATCH_BLOCK == 0

    wts = _stacked_weights(norm_g, w_in, conv_qkv_w, a_log, dt_bias, delta_norm_g, sconv_w,
                           cconv_w, cconv_b, cln_g, cln_b, w_out, final_norm_g)

    xp = x_prompt
    p_outs = [[] for _ in range(4)]
    for l in range(depth):
        xp, pd, pq, ps, pc = _prompt_layer_pipelined(xp, wts, l, l == depth - 1)
        for acc, o in zip(p_outs, (pd, pq, ps, pc)):
            acc.append(o)
    pd, pq, ps, pc = (jnp.stack(o) for o in p_outs)

    xs, sd, sq, ss, sc = _sample_layers(x_sample, state_delta, state_qkv_conv, state_sconv,
                                        state_cconv, wts)
    return (xp, xs[depth - 1],
            pd, pq[:, :, SUBLANES - (QK_CONV - 1):, :], ps[:, :, SUBLANES - (SCONV_W - 1):, :],
            pc[:, :, CTAIL - (CONF_W - 1):, :],
            sd, sq, ss, sc)
```

```python
import functools

import jax
import jax.numpy as jnp
from jax import lax
from jax.experimental import pallas as pl
from jax.experimental.pallas import tpu as pltpu

D_MODEL = 1024
N_DHEADS = 4
HEAD_DIM = 128
D_DELTA = N_DHEADS * HEAD_DIM
D_SCONV = 256
D_CONF = 256
QK_CONV = 4
SCONV_W = 3
CONF_W = 31
CHUNK = 64
EPS = 1e-6

R_GATE_D = 0
R_SB = 512
R_SC = 768
R_SX = 1024
R_GATE_S = 1280
R_GA = 1536
R_GB = 1792
R_GATE_C = 2048
R_BA = 2304
R_WIDTH = 2432
W_REST = 3 * D_DELTA

SUBLANES = 8
LANES = 128
TIME_TILE = 256
SAMPLE_PAD = 8
SAMPLE_BATCH_BLOCK = 16
SAMPLE_SEQ_LOCKSTEP = 4
CTAIL = 32
VMEM_LIMIT = 56 * 1024 * 1024

F32 = jnp.float32
BF16 = jnp.bfloat16


def _mm(a, b):
    return jnp.dot(a.astype(BF16), b.astype(BF16), preferred_element_type=F32)


def _mm_nt(a, b):
    return lax.dot_general(a.astype(BF16), b.astype(BF16), (((1,), (1,)), ((), ())),
                           preferred_element_type=F32)


def _mm_tn(a, b):
    return lax.dot_general(a.astype(BF16), b.astype(BF16), (((0,), (0,)), ((), ())),
                           preferred_element_type=F32)


def _sigmoid(x):
    return 1.0 / (1.0 + jnp.exp(-x))


def _silu(x):
    return x * _sigmoid(x)


def _softplus(x):
    return jnp.maximum(x, 0.0) + jnp.log1p(jnp.exp(-jnp.abs(x)))


def _rms_rows(x, g):
    return x * lax.rsqrt(jnp.mean(x * x, axis=-1, keepdims=True) + EPS) * g


def _chunk_cumsum(g, chunk, row):
    pos = row % chunk
    s = 1
    while s < chunk:
        g = g + jnp.where(pos >= s, pltpu.roll(g, s, axis=0), 0.0)
        s *= 2
    return g


def _run(staged):
    try:
        while True:
            next(staged)
    except StopIteration as stop:
        return stop.value


def _interleave(*staged):
    live = list(staged)
    while live:
        for s in list(live):
            try:
                next(s)
            except StopIteration:
                live.remove(s)


def _pair_diag(a, b):
    z = jnp.zeros_like(a)
    return jnp.concatenate([jnp.concatenate([a, z], axis=1), jnp.concatenate([z, b], axis=1)], axis=0)


class _Packed:
    def __init__(self, chunk, rows, bf16_tile):
        self.chunk, self.rows, self.n = chunk, rows, rows // chunk
        self.rr = lax.broadcasted_iota(jnp.int32, (chunk, rows), 0)
        lane = lax.broadcasted_iota(jnp.int32, (chunk, rows), 1)
        self.jl = lane % chunk
        self.lane_blk = lane // chunk
        ii = lax.broadcasted_iota(jnp.int32, (rows, rows), 0)
        jj = lax.broadcasted_iota(jnp.int32, (rows, rows), 1)
        mask = jnp.where((ii // chunk) == (jj // chunk), 1.0, 0.0).astype(F32)
        self.bd_mask = mask.astype(BF16) if bf16_tile else mask

    def pack(self, g):
        out = g[0:self.chunk]
        for c in range(1, self.n):
            out = jnp.where(self.lane_blk == c, g[c * self.chunk:(c + 1) * self.chunk], out)
        return out

    def col(self, v):
        shape = (self.chunk, self.rows)
        out = jnp.broadcast_to(v[0:self.chunk], shape)
        for c in range(1, self.n):
            out = jnp.where(self.lane_blk == c,
                            jnp.broadcast_to(v[c * self.chunk:(c + 1) * self.chunk], shape), out)
        return out

    def block_diag(self, xp):
        if self.bd_mask.dtype == BF16:
            return jnp.concatenate([xp.astype(BF16)] * self.n, axis=0) * self.bd_mask
        return (jnp.concatenate([xp] * self.n, axis=0) * self.bd_mask).astype(BF16)

    def unit_lower_inverse(self, lps, nil):
        base = min(self.chunk, 16)
        same = (self.rr // base) == (self.jl // base)
        eye = jnp.where(self.rr == self.jl, 1.0, 0.0).astype(F32)
        ds = [jnp.where(same, lp, 0.0) for lp in lps]
        xs = [eye - d for d in ds]
        d_bds = [self.block_diag(d) for d in ds]
        p = 2
        while p < min(base, nil):
            ds = [_mm(d, d_bd) for d, d_bd in zip(ds, d_bds)]
            yield
            d_bds = [self.block_diag(d) for d in ds]
            xs = [x + _mm(x, d_bd) for x, d_bd in zip(xs, d_bds)]
            yield
            p *= 2
        size = base
        while size < self.chunk:
            big = (self.rr // (2 * size)) == (self.jl // (2 * size))
            off = jnp.logical_and(big, jnp.logical_not(same))
            xes = [_mm(x, self.block_diag(jnp.where(off, lp, 0.0))) for x, lp in zip(xs, lps)]
            yield
            xs = [x - _mm(xe, self.block_diag(x)) for x, xe in zip(xs, xes)]
            yield
            same = big
            size *= 2
        return xs

    def delta_prep(self, heads, nil):
        tril = self.rr >= self.jl
        strict = self.rr > self.jl
        decays, kbs, gs = [], [], []
        for q, k, v, beta, gc_col, gc_row in heads:
            diff = self.col(gc_col) - gc_row
            decays.append(jnp.where(tril, jnp.exp(jnp.where(tril, diff, 0.0)), 0.0))
            kbs.append(k * beta)
        for (q, k, *_), kb in zip(heads, kbs):
            gs.append(_mm_nt(jnp.concatenate([kb, q], axis=0), k))
        yield
        lps = [jnp.where(strict, self.pack(g[:self.rows]) * dec, 0.0) for g, dec in zip(gs, decays)]
        qkps = [self.pack(g[self.rows:]) * dec for g, dec in zip(gs, decays)]
        tinvs = yield from self.unit_lower_inverse(lps, nil)
        out = []
        for (q, k, v, beta, gc_col, _), kb, tinv, qkp in zip(heads, kbs, tinvs, qkps):
            eg = jnp.exp(gc_col)
            uw = _mm(self.block_diag(tinv), jnp.concatenate([v * beta, kb * eg], axis=1))
            out.append((uw[:, :HEAD_DIM], uw[:, HEAD_DIM:], q * eg, qkp))
        yield
        return out


def _branch_s(buf_ref, w_ref, tail, rows, lead=()):
    n = len(lead)
    acc = None
    for j in range(SCONV_W):
        idx = lead + (pl.ds(tail - (SCONV_W - 1) + j, rows), slice(None))
        term = buf_ref[idx] * w_ref[j:j + 1, :].reshape((1,) * n + (1, D_SCONV))
        acc = term if acc is None else acc + term
    return acc


def _conformer(ubuf_ref, w_ref, b_ref, g_ref, beta_ref, tail, rows, lead=()):
    n = len(lead)
    shp = (1,) * n + (1, D_CONF)
    acc = None
    for j in range(CONF_W):
        idx = lead + (pl.ds(tail - (CONF_W - 1) + j, rows), slice(None))
        term = ubuf_ref[idx] * w_ref[j:j + 1, :].reshape(shp)
        acc = term if acc is None else acc + term
    yc = acc + b_ref[...].reshape(shp)
    mu = jnp.mean(yc, axis=-1, keepdims=True)
    var = jnp.mean(jnp.square(yc - mu), axis=-1, keepdims=True)
    yc = (yc - mu) * lax.rsqrt(var + EPS) * g_ref[...].reshape(shp) + beta_ref[...].reshape(shp)
    return _silu(yc)


def _prompt_kernel(final_norm,
                   x_ref, ng_ref, wqkv_ref, wrest_ref, cw_ref, alog_ref, dt_ref, dng_ref,
                   sw_ref, ccw_ref, ccb_ref, clg_ref, clb_ref, wout_ref, fg_ref,
                   y_ref, s_out_ref, qt_ref, st_ref, ct_ref,
                   qbuf, sbuf, cbuf, cshift, s_scr, obuf, rest,
                   u_s, lhs_s, kdt_s, egl_s, qk_s):
    i = pl.program_id(1)
    nt = pl.num_programs(1)
    tt = TIME_TILE
    nchunk = tt // CHUNK

    @pl.when(i == 0)
    def _():
        qbuf[0:SUBLANES, :] = jnp.zeros((SUBLANES, 3 * D_DELTA), F32)
        sbuf[0:SUBLANES, :] = jnp.zeros((SUBLANES, D_SCONV), F32)
        cbuf[0:CTAIL, :] = jnp.zeros((CTAIL, D_CONF), F32)
        s_scr[...] = jnp.zeros(s_scr.shape, F32)

    x = x_ref[0]
    h = _rms_rows(x, ng_ref[...]).astype(BF16)
    qbuf[SUBLANES:SUBLANES + tt, :] = jnp.dot(h, wqkv_ref[...], preferred_element_type=F32)
    rest[...] = jnp.dot(h, wrest_ref[...], preferred_element_type=F32)

    sbuf[SUBLANES:SUBLANES + tt, :] = rest[:, R_SC:R_SC + D_SCONV] * rest[:, R_SX:R_SX + D_SCONV]
    ys = _branch_s(sbuf, sw_ref, SUBLANES, tt)
    obuf[:, D_DELTA:D_DELTA + D_SCONV] = (rest[:, R_SB:R_SB + D_SCONV] * ys
                                          * _silu(rest[:, R_GATE_S:R_GATE_S + D_SCONV]))

    cbuf[CTAIL:CTAIL + tt, :] = rest[:, R_GA:R_GA + D_CONF] * _sigmoid(rest[:, R_GB:R_GB + D_CONF])
    for sh in range(1, SUBLANES):
        cshift[sh - 1] = cbuf[pl.ds(sh, tt + CTAIL - SUBLANES), :]

    row = lax.broadcasted_iota(jnp.int32, (tt, LANES), 0)
    ba = rest[:, R_BA:R_BA + LANES]
    beta_all = _sigmoid(ba)
    g_all = -jnp.exp(alog_ref[...]) * _softplus(ba + dt_ref[...])
    gc = _chunk_cumsum(g_all, CHUNK, row)
    gct = gc.T
    gl_rows = jnp.concatenate(
        [jnp.broadcast_to(gc[(c + 1) * CHUNK - 1:(c + 1) * CHUNK, :], (CHUNK, LANES))
         for c in range(nchunk)], axis=0)
    pk = _Packed(CHUNK, tt, True)

    def conv_act(c0):
        acc = None
        for j in range(QK_CONV):
            term = (qbuf[pl.ds(SUBLANES - (QK_CONV - 1) + j, tt), c0:c0 + HEAD_DIM]
                    * cw_ref[j:j + 1, c0:c0 + HEAD_DIM])
            acc = term if acc is None else acc + term
        return _silu(acc)

    heads, kdecs = [], []
    for hd in range(N_DHEADS):
        q = conv_act(hd * HEAD_DIM)
        k = conv_act(D_DELTA + hd * HEAD_DIM)
        v = conv_act(2 * D_DELTA + hd * HEAD_DIM)
        q = q * lax.rsqrt(jnp.sum(q * q, axis=-1, keepdims=True) + EPS) * (HEAD_DIM ** -0.5)
        k = k * lax.rsqrt(jnp.sum(k * k, axis=-1, keepdims=True) + EPS)
        lane = N_DHEADS + hd
        gcol = gc[:, lane:lane + 1]
        heads.append((q, k, v, beta_all[:, hd:hd + 1], gcol, gct[lane:lane + 1, :]))
        kdecs.append(k * jnp.exp(gl_rows[:, lane:lane + 1] - gcol))
    prepped = pk.delta_prep(heads, CHUNK)

    for pr in range(N_DHEADS // 2):
        h0, h1 = 2 * pr, 2 * pr + 1
        (u0, w0, qg0, qkp0), (u1, w1, qg1, qkp1) = prepped[h0], prepped[h1]
        kd0, kd1 = kdecs[h0], kdecs[h1]
        u_s[pr] = jnp.concatenate([u0, u1], axis=1)
        w = jnp.concatenate([w0, w1], axis=1).astype(BF16)
        qg = jnp.concatenate([qg0, qg1], axis=1).astype(BF16)
        kst = jnp.concatenate(
            [kd[c * CHUNK:(c + 1) * CHUNK] for c in range(nchunk) for kd in (kd0, kd1)], axis=0)
        kdt = kst.T.astype(BF16)
        low = (lax.broadcasted_iota(jnp.int32, (CHUNK, tt), 1) % LANES) < CHUNK
        qk_even = jnp.where(low, qkp0, pltpu.roll(qkp1, CHUNK, axis=1)).astype(BF16)
        qk_odd = jnp.where(low, pltpu.roll(qkp0, tt - CHUNK, axis=1), qkp1).astype(BF16)
        for c in range(nchunk):
            cs = slice(c * CHUNK, (c + 1) * CHUNK)
            lhs_s[pr, c, 0:CHUNK, :] = w[cs]
            lhs_s[pr, c, CHUNK:, :] = qg[cs]
            kdt_s[pr, c] = kdt[:, 2 * c * CHUNK:2 * (c + 1) * CHUNK]
            tile = slice((c // 2) * LANES, (c // 2 + 1) * LANES)
            qk_s[pr, c] = (qk_even if c % 2 == 0 else qk_odd)[:, tile]
            ge = (c + 1) * CHUNK - 1
            egl = jnp.concatenate(
                [jnp.broadcast_to(jnp.exp(gc[ge:ge + 1, N_DHEADS + hh:N_DHEADS + hh + 1]),
                                  (SUBLANES, HEAD_DIM)) for hh in (h0, h1)], axis=1)
            egl_s[c, :, 2 * pr * HEAD_DIM:2 * (pr + 1) * HEAD_DIM] = egl

    def chunk_body(c, carry):
        r0 = pl.multiple_of(c * CHUNK, CHUNK)
        rows_c = pl.ds(r0, CHUNK)
        pairs = range(N_DHEADS // 2)
        ss = [s_scr[pr] for pr in pairs]
        rs = [jnp.dot(lhs_s[pr, c], _pair_diag(s[:, :HEAD_DIM], s[:, HEAD_DIM:]).astype(BF16),
                      preferred_element_type=F32) for pr, s in zip(pairs, ss)]
        vns = [u_s[pr, rows_c, :] - r[:CHUNK] for pr, r in zip(pairs, rs)]
        v_bds = [_pair_diag(vn[:, :HEAD_DIM], vn[:, HEAD_DIM:]).astype(BF16) for vn in vns]
        upds = [jnp.dot(kdt_s[pr, c], v_bd, preferred_element_type=F32) for pr, v_bd in zip(pairs, v_bds)]
        o_intra = [jnp.dot(qk_s[pr, c], v_bd, preferred_element_type=F32) for pr, v_bd in zip(pairs, v_bds)]
        for pr in pairs:
            egl = egl_s[c][0:1, 2 * pr * HEAD_DIM:2 * (pr + 1) * HEAD_DIM]
            s_scr[pr] = ss[pr] * egl + upds[pr]
            o_pair = rs[pr][CHUNK:] + o_intra[pr]
            for hl in range(2):
                hd = 2 * pr + hl
                o = _rms_rows(o_pair[:, hl * HEAD_DIM:(hl + 1) * HEAD_DIM], dng_ref[...])
                gate = rest[rows_c, R_GATE_D + hd * HEAD_DIM:R_GATE_D + (hd + 1) * HEAD_DIM]
                obuf[rows_c, hd * HEAD_DIM:(hd + 1) * HEAD_DIM] = o * _silu(gate)
        acc = None
        for j in range(CONF_W):
            off = CTAIL - (CONF_W - 1) + j
            start = pl.multiple_of(r0 + (off // SUBLANES) * SUBLANES, SUBLANES)
            sh = off % SUBLANES
            src = cbuf[pl.ds(start, CHUNK), :] if sh == 0 else cshift[sh - 1, pl.ds(start, CHUNK), :]
            term = src * ccw_ref[j:j + 1, :]
            acc = term if acc is None else acc + term
        yc = acc + ccb_ref[...]
        mu = jnp.mean(yc, axis=-1, keepdims=True)
        var = jnp.mean(jnp.square(yc - mu), axis=-1, keepdims=True)
        yc = (yc - mu) * lax.rsqrt(var + EPS) * clg_ref[...] + clb_ref[...]
        obuf[rows_c, D_DELTA + D_SCONV:] = _silu(yc) * _silu(rest[rows_c, R_GATE_C:R_GATE_C + D_CONF])
        return carry

    lax.fori_loop(0, nchunk, chunk_body, 0)

    y = x_ref[0] + jnp.dot(obuf[...].astype(BF16), wout_ref[...], preferred_element_type=F32)
    if final_norm:
        y = _rms_rows(y, fg_ref[...])
    y_ref[0] = y

    @pl.when(i == nt - 1)
    def _():
        for pr in range(N_DHEADS // 2):
            s_out_ref[0, 2 * pr] = s_scr[pr, :, :HEAD_DIM]
            s_out_ref[0, 2 * pr + 1] = s_scr[pr, :, HEAD_DIM:]
        qt_ref[0] = qbuf[tt:tt + SUBLANES, :]
        st_ref[0] = sbuf[tt:tt + SUBLANES, :]
        ct_ref[0] = cbuf[tt:tt + CTAIL, :]

    qbuf[0:SUBLANES, :] = qbuf[tt:tt + SUBLANES, :]
    sbuf[0:SUBLANES, :] = sbuf[tt:tt + SUBLANES, :]
    cbuf[0:CTAIL, :] = cbuf[tt:tt + CTAIL, :]


def _weight_specs(wts, layer_of, buffers=2):
    specs = []
    for w in wts[:-1]:
        tail = (0,) * (w.ndim - 1)
        specs.append(pl.BlockSpec((None,) + w.shape[1:], lambda *ids, tail=tail: (layer_of(*ids),) + tail,
                                  pipeline_mode=pl.Buffered(buffers)))
    specs.append(pl.BlockSpec(wts[-1].shape, lambda *ids: (0,) * wts[-1].ndim))
    return specs


def _prompt_layer(x, wts, layer, final_norm):
    b, t, d = x.shape
    nt = t // TIME_TILE
    npair = N_DHEADS // 2
    nchunk = TIME_TILE // CHUNK
    weight_specs = _weight_specs(wts, lambda *ids: layer)
    out_shape = (
        jax.ShapeDtypeStruct((b, t, d), F32),
        jax.ShapeDtypeStruct((b, N_DHEADS, HEAD_DIM, HEAD_DIM), F32),
        jax.ShapeDtypeStruct((b, SUBLANES, 3 * D_DELTA), F32),
        jax.ShapeDtypeStruct((b, SUBLANES, D_SCONV), F32),
        jax.ShapeDtypeStruct((b, CTAIL, D_CONF), F32),
    )
    out_specs = (
        pl.BlockSpec((1, TIME_TILE, d), lambda bi, ti: (bi, ti, 0)),
        pl.BlockSpec((1, N_DHEADS, HEAD_DIM, HEAD_DIM), lambda bi, ti: (bi, 0, 0, 0)),
        pl.BlockSpec((1, SUBLANES, 3 * D_DELTA), lambda bi, ti: (bi, 0, 0)),
        pl.BlockSpec((1, SUBLANES, D_SCONV), lambda bi, ti: (bi, 0, 0)),
        pl.BlockSpec((1, CTAIL, D_CONF), lambda bi, ti: (bi, 0, 0)),
    )
    scratch = [
        pltpu.VMEM((TIME_TILE + SUBLANES, 3 * D_DELTA), F32),
        pltpu.VMEM((TIME_TILE + SUBLANES, D_SCONV), F32),
        pltpu.VMEM((TIME_TILE + CTAIL, D_CONF), F32),
        pltpu.VMEM((SUBLANES - 1, TIME_TILE + CTAIL - SUBLANES, D_CONF), F32),
        pltpu.VMEM((npair, HEAD_DIM, 2 * HEAD_DIM), F32),
        pltpu.VMEM((TIME_TILE, D_MODEL), F32),
        pltpu.VMEM((TIME_TILE, R_WIDTH), F32),
        pltpu.VMEM((npair, TIME_TILE, 2 * HEAD_DIM), F32),
        pltpu.VMEM((npair, nchunk, 2 * CHUNK, 2 * HEAD_DIM), BF16),
        pltpu.VMEM((npair, nchunk, HEAD_DIM, 2 * CHUNK), BF16),
        pltpu.VMEM((nchunk, SUBLANES, N_DHEADS * HEAD_DIM), F32),
        pltpu.VMEM((npair, nchunk, CHUNK, 2 * CHUNK), BF16),
    ]
    return pl.pallas_call(
        functools.partial(_prompt_kernel, final_norm),
        grid=(b, nt),
        in_specs=[pl.BlockSpec((1, TIME_TILE, d), lambda bi, ti: (bi, ti, 0))] + weight_specs,
        out_specs=out_specs,
        out_shape=out_shape,
        scratch_shapes=scratch,
        compiler_params=pltpu.CompilerParams(
            dimension_semantics=("arbitrary", "arbitrary"),
            vmem_limit_bytes=VMEM_LIMIT),
        name="prompt_layer",
    )(x, *wts)


def _pipe_kernel(final_norm, nt,
                 x_ref, ng_ref, win_ref, cw_ref, alog_ref, dt_ref, dng_ref,
                 sw_ref, ccw_ref, ccb_ref, clg_ref, clb_ref, wout_ref, fg_ref,
                 y_ref, s_out_ref, qt_ref, st_ref, ct_ref,
                 qbuf, sbuf, s_scr, xs, qa, bg, rest, cbuf, cshift, obuf):
    g = pl.program_id(0)
    n_tiles = pl.num_programs(0) - 1
    tt = TIME_TILE
    nchunk = tt // CHUNK
    npair = N_DHEADS // 2
    t1 = jnp.minimum(g, n_tiles - 1) % nt
    t2 = jnp.maximum(g - 1, 0) % nt
    p = g % 2
    q = 1 - p

    @pl.when(g == 0)
    def _():
        xs[1] = jnp.zeros(xs.shape[1:], F32)
        qa[1] = jnp.zeros(qa.shape[1:], F32)
        bg[1] = jnp.zeros(bg.shape[1:], F32)
        rest[1] = jnp.zeros(rest.shape[1:], F32)
        cbuf[1] = jnp.zeros(cbuf.shape[1:], F32)
        cshift[1] = jnp.zeros(cshift.shape[1:], F32)
        obuf[1] = jnp.zeros(obuf.shape[1:], F32)

    @pl.when(t1 == 0)
    def _():
        qbuf[0:SUBLANES, :] = jnp.zeros((SUBLANES, 3 * D_DELTA), F32)
        sbuf[0:SUBLANES, :] = jnp.zeros((SUBLANES, D_SCONV), F32)

    @pl.when(t2 == 0)
    def _():
        s_scr[...] = jnp.zeros(s_scr.shape, F32)

    def stage1():
        x = x_ref[0]
        xs[p] = x
        h = _rms_rows(x, ng_ref[...]).astype(BF16)
        yield
        for c0 in range(0, 3 * D_DELTA, 2 * LANES):
            qbuf[SUBLANES:SUBLANES + tt, c0:c0 + 2 * LANES] = jnp.dot(
                h, win_ref[:, c0:c0 + 2 * LANES], preferred_element_type=F32)
            yield
        rest_pieces = list(range(0, R_WIDTH, 2 * LANES))

        def rest_piece():
            c0 = rest_pieces.pop(0)
            c1 = min(c0 + 2 * LANES, R_WIDTH)
            rest[p, :, c0:c1] = jnp.dot(h, win_ref[:, W_REST + c0:W_REST + c1],
                                        preferred_element_type=F32)

        def conv_act(c0):
            acc = None
            for j in range(QK_CONV):
                term = (qbuf[pl.ds(SUBLANES - (QK_CONV - 1) + j, tt), c0:c0 + HEAD_DIM]
                        * cw_ref[j:j + 1, c0:c0 + HEAD_DIM])
                acc = term if acc is None else acc + term
            return _silu(acc)

        for hd in range(N_DHEADS):
            qh = conv_act(hd * HEAD_DIM)
            qh = qh * lax.rsqrt(jnp.sum(qh * qh, axis=-1, keepdims=True) + EPS) * (HEAD_DIM ** -0.5)
            qa[p, :, hd * HEAD_DIM:(hd + 1) * HEAD_DIM] = qh
            rest_piece()
            yield
            kh = conv_act(D_DELTA + hd * HEAD_DIM)
            kh = kh * lax.rsqrt(jnp.sum(kh * kh, axis=-1, keepdims=True) + EPS)
            qa[p, :, D_DELTA + hd * HEAD_DIM:D_DELTA + (hd + 1) * HEAD_DIM] = kh
            rest_piece()
            yield
            qa[p, :, 2 * D_DELTA + hd * HEAD_DIM:2 * D_DELTA + (hd + 1) * HEAD_DIM] = conv_act(
                2 * D_DELTA + hd * HEAD_DIM)
            yield
        while rest_pieces:
            rest_piece()
            yield

        row = lax.broadcasted_iota(jnp.int32, (tt, LANES), 0)
        ba = rest[p, :, R_BA:R_BA + LANES]
        bg[p, 0] = _sigmoid(ba)
        bg[p, 1] = _chunk_cumsum(-jnp.exp(alog_ref[...]) * _softplus(ba + dt_ref[...]), CHUNK, row)
        yield

        sbuf[SUBLANES:SUBLANES + tt, :] = (rest[p, :, R_SC:R_SC + D_SCONV]
                                           * rest[p, :, R_SX:R_SX + D_SCONV])
        obuf[p, :, D_DELTA:D_DELTA + D_SCONV] = (
            rest[p, :, R_SB:R_SB + D_SCONV] * _branch_s(sbuf, sw_ref, SUBLANES, tt)
            * _silu(rest[p, :, R_GATE_S:R_GATE_S + D_SCONV]))
        yield

        prev_tail = jnp.where(t1 == 0, 0.0, cbuf[q, tt:tt + CTAIL, :])
        cbuf[p, 0:CTAIL, :] = prev_tail
        cbuf[p, CTAIL:CTAIL + tt, :] = (rest[p, :, R_GA:R_GA + D_CONF]
                                        * _sigmoid(rest[p, :, R_GB:R_GB + D_CONF]))
        yield
        for sh in range(1, SUBLANES):
            cshift[p, sh - 1] = cbuf[p, pl.ds(sh, tt + CTAIL - SUBLANES), :]
        yield

        qbuf[0:SUBLANES, :] = qbuf[tt:tt + SUBLANES, :]
        sbuf[0:SUBLANES, :] = sbuf[tt:tt + SUBLANES, :]

    def stage2():
        beta_all = bg[q, 0]
        gc = bg[q, 1]
        gct = gc.T
        gl_rows = jnp.concatenate(
            [jnp.broadcast_to(gc[(c + 1) * CHUNK - 1:(c + 1) * CHUNK, :], (CHUNK, LANES))
             for c in range(nchunk)], axis=0)
        pk = _Packed(CHUNK, tt, True)
        heads, kdecs = [], []
        for hd in range(N_DHEADS):
            lane = N_DHEADS + hd
            gcol = gc[:, lane:lane + 1]
            kh = qa[q, :, D_DELTA + hd * HEAD_DIM:D_DELTA + (hd + 1) * HEAD_DIM]
            heads.append((qa[q, :, hd * HEAD_DIM:(hd + 1) * HEAD_DIM], kh,
                          qa[q, :, 2 * D_DELTA + hd * HEAD_DIM:2 * D_DELTA + (hd + 1) * HEAD_DIM],
                          beta_all[:, hd:hd + 1], gcol, gct[lane:lane + 1, :]))
            kdecs.append(kh * jnp.exp(gl_rows[:, lane:lane + 1] - gcol))
        yield
        prepped = yield from pk.delta_prep(heads, CHUNK)

        us, lhss, kdts, qks = [], [], [], []
        low = (lax.broadcasted_iota(jnp.int32, (CHUNK, tt), 1) % LANES) < CHUNK
        for pr in range(npair):
            (u0, w0, qg0, qkp0), (u1, w1, qg1, qkp1) = prepped[2 * pr], prepped[2 * pr + 1]
            us.append(jnp.concatenate([u0, u1], axis=1))
            w = jnp.concatenate([w0, w1], axis=1).astype(BF16)
            qg = jnp.concatenate([qg0, qg1], axis=1).astype(BF16)
            lhss.append([jnp.concatenate([w[c * CHUNK:(c + 1) * CHUNK], qg[c * CHUNK:(c + 1) * CHUNK]],
                                         axis=0) for c in range(nchunk)])
            kst = jnp.concatenate([kd[c * CHUNK:(c + 1) * CHUNK] for c in range(nchunk)
                                   for kd in (kdecs[2 * pr], kdecs[2 * pr + 1])], axis=0)
            kdts.append(kst.T.astype(BF16))
            qks.append((jnp.where(low, qkp0, pltpu.roll(qkp1, CHUNK, axis=1)).astype(BF16),
                        jnp.where(low, pltpu.roll(qkp0, tt - CHUNK, axis=1), qkp1).astype(BF16)))
        yield

        ss = [s_scr[pr] for pr in range(npair)]
        for c in range(nchunk):
            cs = slice(c * CHUNK, (c + 1) * CHUNK)
            rs = [jnp.dot(lhss[pr][c], _pair_diag(s[:, :HEAD_DIM], s[:, HEAD_DIM:]).astype(BF16),
                          preferred_element_type=F32) for pr, s in enumerate(ss)]
            yield
            vns = [us[pr][cs] - r[:CHUNK] for pr, r in enumerate(rs)]
            v_bds = [_pair_diag(vn[:, :HEAD_DIM], vn[:, HEAD_DIM:]).astype(BF16) for vn in vns]
            upds = [jnp.dot(kdts[pr][:, 2 * c * CHUNK:2 * (c + 1) * CHUNK], v_bd,
                            preferred_element_type=F32) for pr, v_bd in enumerate(v_bds)]
            tile = slice((c // 2) * LANES, (c // 2 + 1) * LANES)
            o_intra = [jnp.dot(qks[pr][c % 2][:, tile], v_bd, preferred_element_type=F32)
                       for pr, v_bd in enumerate(v_bds)]
            ge = (c + 1) * CHUNK - 1
            for pr in range(npair):
                egl = jnp.concatenate(
                    [jnp.broadcast_to(jnp.exp(gc[ge:ge + 1, N_DHEADS + hh:N_DHEADS + hh + 1]),
                                      (1, HEAD_DIM)) for hh in (2 * pr, 2 * pr + 1)], axis=1)
                ss[pr] = ss[pr] * egl + upds[pr]
                o_pair = rs[pr][CHUNK:] + o_intra[pr]
                for hl in range(2):
                    hd = 2 * pr + hl
                    o = _rms_rows(o_pair[:, hl * HEAD_DIM:(hl + 1) * HEAD_DIM], dng_ref[...])
                    gate = rest[q, cs, R_GATE_D + hd * HEAD_DIM:R_GATE_D + (hd + 1) * HEAD_DIM]
                    obuf[q, cs, hd * HEAD_DIM:(hd + 1) * HEAD_DIM] = o * _silu(gate)
            yield
            acc = None
            for j in range(CONF_W):
                off = CTAIL - (CONF_W - 1) + j
                start = c * CHUNK + (off // SUBLANES) * SUBLANES
                sh = off % SUBLANES
                src = (cbuf[q, start:start + CHUNK, :] if sh == 0
                       else cshift[q, sh - 1, start:start + CHUNK, :])
                term = src * ccw_ref[j:j + 1, :]
                acc = term if acc is None else acc + term
            yc = acc + ccb_ref[...]
            mu = jnp.mean(yc, axis=-1, keepdims=True)
            var = jnp.mean(jnp.square(yc - mu), axis=-1, keepdims=True)
            yc = (yc - mu) * lax.rsqrt(var + EPS) * clg_ref[...] + clb_ref[...]
            obuf[q, cs, D_DELTA + D_SCONV:] = _silu(yc) * _silu(rest[q, cs, R_GATE_C:R_GATE_C + D_CONF])
            yield
        for pr in range(npair):
            s_scr[pr] = ss[pr]

        y = xs[q] + jnp.dot(obuf[q].astype(BF16), wout_ref[...], preferred_element_type=F32)
        if final_norm:
            y = _rms_rows(y, fg_ref[...])
        y_ref[0] = y

    _interleave(stage1(), stage2())

    @pl.when(jnp.logical_and(g >= 1, t2 == nt - 1))
    def _():
        for pr in range(npair):
            s_out_ref[0, 2 * pr] = s_scr[pr, :, :HEAD_DIM]
            s_out_ref[0, 2 * pr + 1] = s_scr[pr, :, HEAD_DIM:]

    @pl.when(jnp.logical_and(g < n_tiles, t1 == nt - 1))
    def _():
        qt_ref[0] = qbuf[0:SUBLANES, :]
        st_ref[0] = sbuf[0:SUBLANES, :]
        ct_ref[0] = cbuf[p, tt:tt + CTAIL, :]


def _prompt_layer_pipelined(x, wts, layer, final_norm):
    b, t, d = x.shape
    nt = t // TIME_TILE
    n_tiles = b * nt
    npair = N_DHEADS // 2

    def tile1(g):
        g1 = jnp.minimum(g, n_tiles - 1)
        return g1 // nt, g1 % nt

    def tile2(g):
        g2 = jnp.maximum(g - 1, 0)
        return g2 // nt, g2 % nt

    out_shape = (
        jax.ShapeDtypeStruct((b, t, d), F32),
        jax.ShapeDtypeStruct((b, N_DHEADS, HEAD_DIM, HEAD_DIM), F32),
        jax.ShapeDtypeStruct((b, SUBLANES, 3 * D_DELTA), F32),
        jax.ShapeDtypeStruct((b, SUBLANES, D_SCONV), F32),
        jax.ShapeDtypeStruct((b, CTAIL, D_CONF), F32),
    )
    out_specs = (
        pl.BlockSpec((1, TIME_TILE, d), lambda g: tile2(g) + (0,)),
        pl.BlockSpec((1, N_DHEADS, HEAD_DIM, HEAD_DIM), lambda g: (tile2(g)[0], 0, 0, 0)),
        pl.BlockSpec((1, SUBLANES, 3 * D_DELTA), lambda g: (tile1(g)[0], 0, 0)),
        pl.BlockSpec((1, SUBLANES, D_SCONV), lambda g: (tile1(g)[0], 0, 0)),
        pl.BlockSpec((1, CTAIL, D_CONF), lambda g: (tile1(g)[0], 0, 0)),
    )
    scratch = [
        pltpu.VMEM((TIME_TILE + SUBLANES, 3 * D_DELTA), F32),
        pltpu.VMEM((TIME_TILE + SUBLANES, D_SCONV), F32),
        pltpu.VMEM((npair, HEAD_DIM, 2 * HEAD_DIM), F32),
        pltpu.VMEM((2, TIME_TILE, D_MODEL), F32),
        pltpu.VMEM((2, TIME_TILE, 3 * D_DELTA), F32),
        pltpu.VMEM((2, 2, TIME_TILE, LANES), F32),
        pltpu.VMEM((2, TIME_TILE, R_WIDTH), F32),
        pltpu.VMEM((2, TIME_TILE + CTAIL, D_CONF), F32),
        pltpu.VMEM((2, SUBLANES - 1, TIME_TILE + CTAIL - SUBLANES, D_CONF), F32),
        pltpu.VMEM((2, TIME_TILE, D_MODEL), F32),
    ]
    return pl.pallas_call(
        functools.partial(_pipe_kernel, final_norm, nt),
        grid=(n_tiles + 1,),
        in_specs=[pl.BlockSpec((1, TIME_TILE, d), lambda g: tile1(g) + (0,))]
        + _weight_specs(wts, lambda g: layer, buffers=1),
        out_specs=out_specs,
        out_shape=out_shape,
        scratch_shapes=scratch,
        compiler_params=pltpu.CompilerParams(
            dimension_semantics=("arbitrary",),
            vmem_limit_bytes=VMEM_LIMIT),
        name="prompt_layer",
    )(x, *wts)


def _sample_kernel(x_ref, sd_ref, sq_ref, ss_ref, sc_ref,
                   ng_ref, win_ref, cw_ref, alog_ref, dt_ref, dng_ref,
                   sw_ref, ccw_ref, ccb_ref, clg_ref, clb_ref, wout_ref, fg_ref,
                   y_ref, sd_out_ref, qt_ref, st_ref, ct_ref,
                   qbuf, sbuf, cbuf, obuf, ubuf, wqbuf, kdbuf, glbuf, oibuf, rest, xcarry):
    layer = pl.program_id(0)
    blk = pl.program_id(1)
    nb = SAMPLE_BATCH_BLOCK
    pad = SAMPLE_PAD
    ntok = pad // 2
    rows = nb * pad

    @pl.when(layer == 0)
    def _():
        xcarry[blk, :, 0:ntok, :] = x_ref[...]
        xcarry[blk, :, ntok:, :] = jnp.zeros((nb, pad - ntok, D_MODEL), F32)

    x = xcarry[blk].reshape(rows, D_MODEL)
    h = _rms_rows(x, ng_ref[...]).astype(BF16)
    qkv = jnp.dot(h, win_ref[:, :W_REST], preferred_element_type=F32)
    rest[...] = jnp.dot(h, win_ref[:, W_REST:], preferred_element_type=F32)
    row = lax.broadcasted_iota(jnp.int32, (rows, LANES), 0)
    valid = (row % pad) < (pad // 2)
    valid1 = valid[:, 0:1]

    qkv3 = qkv.reshape(nb, pad, 3 * D_DELTA)
    qt_ref[...] = qkv3[:, ntok - (QK_CONV - 1):ntok, :]
    qbuf[:, SUBLANES - (QK_CONV - 1):SUBLANES, :] = sq_ref[...]
    qbuf[:, SUBLANES:, :] = qkv3

    hs = (rest[:, R_SC:R_SC + D_SCONV] * rest[:, R_SX:R_SX + D_SCONV]).reshape(nb, pad, D_SCONV)
    st_ref[...] = hs[:, ntok - (SCONV_W - 1):ntok, :]
    sbuf[:, SUBLANES - (SCONV_W - 1):SUBLANES, :] = ss_ref[...]
    sbuf[:, SUBLANES:, :] = hs
    ys = _branch_s(sbuf, sw_ref, SUBLANES, pad, lead=(slice(None),))
    obuf[:, D_DELTA:D_DELTA + D_SCONV] = (rest[:, R_SB:R_SB + D_SCONV] * ys.reshape(rows, D_SCONV)
                                          * _silu(rest[:, R_GATE_S:R_GATE_S + D_SCONV]))

    uc = rest[:, R_GA:R_GA + D_CONF] * _sigmoid(rest[:, R_GB:R_GB + D_CONF])
    cbuf[:, CTAIL - (CONF_W - 1):CTAIL, :] = sc_ref[...]
    cbuf[:, CTAIL:, :] = uc.reshape(nb, pad, D_CONF)
    first = CTAIL + ntok - (CONF_W - 1)
    ct_ref[...] = cbuf[:, first:first + CONF_W - 1, :]
    yc = _conformer(cbuf, ccw_ref, ccb_ref, clg_ref, clb_ref, CTAIL, pad, lead=(slice(None),))
    obuf[:, D_DELTA + D_SCONV:] = yc.reshape(rows, D_CONF) * _silu(rest[:, R_GATE_C:R_GATE_C + D_CONF])

    ba = rest[:, R_BA:R_BA + LANES]
    beta_all = jnp.where(valid, _sigmoid(ba), 0.0)
    g_all = jnp.where(valid, -jnp.exp(alog_ref[...]) * _softplus(ba + dt_ref[...]), 0.0)
    gc = _chunk_cumsum(g_all, pad, row)
    gct = gc.T
    gl_all = jnp.broadcast_to(
        gc.reshape(nb, pad, LANES)[:, pad - 1:pad, :], (nb, pad, LANES)).reshape(rows, LANES)
    pk = _Packed(pad, rows, False)

    def conv_act(c0):
        acc = None
        for j in range(QK_CONV):
            term = (qbuf[:, pl.ds(SUBLANES - (QK_CONV - 1) + j, pad), c0:c0 + HEAD_DIM]
                    * cw_ref[j:j + 1, c0:c0 + HEAD_DIM].reshape(1, 1, HEAD_DIM))
            acc = term if acc is None else acc + term
        return jnp.where(valid1, _silu(acc.reshape(rows, HEAD_DIM)), 0.0)

    for pr in range(N_DHEADS // 2):
        h0, h1 = 2 * pr, 2 * pr + 1
        heads, kdecs, egls = [], [], []
        for hd in (h0, h1):
            q = conv_act(hd * HEAD_DIM)
            k = conv_act(D_DELTA + hd * HEAD_DIM)
            v = conv_act(2 * D_DELTA + hd * HEAD_DIM)
            q = q * lax.rsqrt(jnp.sum(q * q, axis=-1, keepdims=True) + EPS) * (HEAD_DIM ** -0.5)
            k = k * lax.rsqrt(jnp.sum(k * k, axis=-1, keepdims=True) + EPS)
            lane = N_DHEADS + hd
            gcol = gc[:, lane:lane + 1]
            heads.append((q, k, v, beta_all[:, hd:hd + 1], gcol, gct[lane:lane + 1, :]))
            gl = gl_all[:, lane:lane + 1]
            kdecs.append(k * jnp.exp(gl - gcol))
            egls.append(jnp.broadcast_to(jnp.exp(gl), (rows, HEAD_DIM)))
        (u0, w0, qg0, qkp0), (u1, w1, qg1, qkp1) = _run(pk.delta_prep(heads, pad // 2))
        (kd0, kd1), (egl0, egl1) = kdecs, egls
        ubuf[...] = jnp.concatenate([u0, u1], axis=1)
        wqbuf[:, 0:pad, :] = jnp.concatenate([w0, w1], axis=1).reshape(nb, pad, 2 * HEAD_DIM)
        wqbuf[:, pad:, :] = jnp.concatenate([qg0, qg1], axis=1).reshape(nb, pad, 2 * HEAD_DIM)
        kdbuf[:, 0:pad, :] = kd0.reshape(nb, pad, HEAD_DIM)
        kdbuf[:, pad:, :] = kd1.reshape(nb, pad, HEAD_DIM)
        glbuf[...] = jnp.concatenate([egl0, egl1], axis=1).reshape(nb, pad, 2 * HEAD_DIM)

        def body(it, carry):
            bis = [it * SAMPLE_SEQ_LOCKSTEP + t for t in range(SAMPLE_SEQ_LOCKSTEP)]
            r0s = [pl.multiple_of(bi * pad, pad) for bi in bis]
            s0s = [sd_ref[bi, h0] for bi in bis]
            s1s = [sd_ref[bi, h1] for bi in bis]
            rs = [_mm(wqbuf[bi], _pair_diag(s0, s1)) for bi, s0, s1 in zip(bis, s0s, s1s)]
            vns = [ubuf[pl.ds(r0, pad), :] - r[:pad] for r0, r in zip(r0s, rs)]
            upds = [_mm_tn(kdbuf[bi], _pair_diag(vn[:, :HEAD_DIM], vn[:, HEAD_DIM:]))
                    for bi, vn in zip(bis, vns)]
            for bi, r0, s0, s1, r, vn, upd in zip(bis, r0s, s0s, s1s, rs, vns, upds):
                ubuf[pl.ds(r0, pad), :] = vn
                oibuf[pl.ds(r0, pad), :] = r[pad:]
                egl = glbuf[bi][0:1, :]
                sd_out_ref[bi, h0] = s0 * egl[:, :HEAD_DIM] + upd[:, :HEAD_DIM]
                sd_out_ref[bi, h1] = s1 * egl[:, HEAD_DIM:] + upd[:, HEAD_DIM:]
            return carry

        lax.fori_loop(0, nb // SAMPLE_SEQ_LOCKSTEP, body, 0)
        for hh, qkp in ((h0, qkp0), (h1, qkp1)):
            ls = slice((hh - h0) * HEAD_DIM, (hh - h0 + 1) * HEAD_DIM)
            o = oibuf[:, ls] + _mm(pk.block_diag(qkp), ubuf[:, ls])
            o = _rms_rows(o, dng_ref[...])
            obuf[:, hh * HEAD_DIM:(hh + 1) * HEAD_DIM] = (
                o * _silu(rest[:, R_GATE_D + hh * HEAD_DIM:R_GATE_D + (hh + 1) * HEAD_DIM]))

    y = x + jnp.dot(obuf[...].astype(BF16), wout_ref[...], preferred_element_type=F32)
    xcarry[blk] = y.reshape(nb, pad, D_MODEL)
    is_last = layer == pl.num_programs(0) - 1
    y_ref[...] = jnp.where(is_last, _rms_rows(y, fg_ref[...]), y).reshape(nb, pad, D_MODEL)[:, 0:ntok, :]


def _sample_layers(x, sd, sq, ss, sc, wts):
    depth = sd.shape[0]
    b, ntok, _ = x.shape
    nb = SAMPLE_BATCH_BLOCK

    def bspec(shape):
        return pl.BlockSpec((nb,) + shape, lambda l, bi: (bi,) + (0,) * len(shape))

    def lspec(shape):
        return pl.BlockSpec((None, nb) + shape, lambda l, bi: (l, bi) + (0,) * len(shape))

    def lshape(shape):
        return jax.ShapeDtypeStruct((depth, b) + shape, F32)

    state_shapes = ((N_DHEADS, HEAD_DIM, HEAD_DIM), (QK_CONV - 1, 3 * D_DELTA),
                    (SCONV_W - 1, D_SCONV), (CONF_W - 1, D_CONF))
    out_shape = (lshape((ntok, D_MODEL)),) + tuple(lshape(s) for s in state_shapes)
    out_specs = (lspec((ntok, D_MODEL)),) + tuple(lspec(s) for s in state_shapes)
    rows = nb * SAMPLE_PAD
    scratch = [
        pltpu.VMEM((nb, 2 * SUBLANES, 3 * D_DELTA), F32),
        pltpu.VMEM((nb, 2 * SUBLANES, D_SCONV), F32),
        pltpu.VMEM((nb, CTAIL + SAMPLE_PAD, D_CONF), F32),
        pltpu.VMEM((rows, D_MODEL), F32),
        pltpu.VMEM((rows, 2 * HEAD_DIM), F32),
        pltpu.VMEM((nb, 2 * SAMPLE_PAD, 2 * HEAD_DIM), F32),
        pltpu.VMEM((nb, 2 * SAMPLE_PAD, HEAD_DIM), F32),
        pltpu.VMEM((nb, SAMPLE_PAD, 2 * HEAD_DIM), F32),
        pltpu.VMEM((rows, 2 * HEAD_DIM), F32),
        pltpu.VMEM((rows, R_WIDTH), F32),
        pltpu.VMEM((b // nb, nb, SAMPLE_PAD, D_MODEL), F32),
    ]
    in_specs = ([bspec((ntok, D_MODEL))] + [lspec(s) for s in state_shapes]
                + _weight_specs(wts, lambda l, bi: l))
    return pl.pallas_call(
        _sample_kernel,
        grid=(depth, b // nb),
        in_specs=in_specs,
        out_specs=out_specs,
        out_shape=out_shape,
        scratch_shapes=scratch,
        compiler_params=pltpu.CompilerParams(
            dimension_semantics=("arbitrary", "arbitrary"),
            vmem_limit_bytes=VMEM_LIMIT),
        name="sample_layers",
    )(x, sd, sq, ss, sc, *wts)


def _reorder_kernel(w_ref, o_ref):
    n_qkv = 3 * D_DELTA
    n_ba = 2 * N_DHEADS
    rows = w_ref.shape[0]
    o_ref[:, 0:n_qkv] = w_ref[:, 0:n_qkv].astype(BF16)
    o_ref[:, n_qkv:n_qkv + R_BA] = w_ref[:, n_qkv + n_ba:].astype(BF16)
    o_ref[:, n_qkv + R_BA:] = jnp.concatenate(
        [w_ref[:, n_qkv:n_qkv + n_ba], jnp.zeros((rows, LANES - n_ba), F32)], axis=1).astype(BF16)


def _reorder_in_proj(w_in):
    depth, d, d_in = w_in.shape
    rows = 256
    return pl.pallas_call(
        _reorder_kernel,
        grid=(depth, d // rows),
        in_specs=[pl.BlockSpec((None, rows, d_in), lambda l, r: (l, r, 0))],
        out_specs=pl.BlockSpec((None, rows, W_REST + R_WIDTH), lambda l, r: (l, r, 0)),
        out_shape=jax.ShapeDtypeStruct((depth, d, W_REST + R_WIDTH), BF16),
        compiler_params=pltpu.CompilerParams(dimension_semantics=("arbitrary", "arbitrary")),
        name="reorder_in_proj",
    )(w_in)


def _stacked_weights(norm_g, w_in, conv_qkv_w, a_log, dt_bias, delta_norm_g, sconv_w,
                     cconv_w, cconv_b, cln_g, cln_b, w_out, final_norm_g):
    depth = w_in.shape[0]
    w_all = _reorder_in_proj(w_in)
    lane_pad = ((0, 0), (N_DHEADS, LANES - 2 * N_DHEADS))
    return (
        norm_g.reshape(depth, 1, D_MODEL),
        w_all,
        conv_qkv_w,
        jnp.pad(a_log, lane_pad).reshape(depth, 1, LANES),
        jnp.pad(dt_bias, lane_pad).reshape(depth, 1, LANES),
        delta_norm_g.reshape(depth, 1, HEAD_DIM),
        sconv_w,
        cconv_w,
        cconv_b.reshape(depth, 1, D_CONF),
        cln_g.reshape(depth, 1, D_CONF),
        cln_b.reshape(depth, 1, D_CONF),
        w_out.astype(BF16),
        final_norm_g.reshape(1, D_MODEL),
    )


def kernel(x_prompt, x_sample, state_delta, state_qkv_conv, state_sconv, state_cconv, norm_g, w_in, conv_qkv_w, a_log, dt_bias, delta_norm_g, sconv_w, cconv_w, cconv_b, cln_g, cln_b, w_out, final_norm_g):
    depth = w_in.shape[0]
    dec_seq = x_sample.shape[1]
    assert x_prompt.shape[1] % TIME_TILE == 0
    assert dec_seq == SAMPLE_PAD // 2 and x_sample.shape[0] % SAMPLE_BATCH_BLOCK == 0

    wts = _stacked_weights(norm_g, w_in, conv_qkv_w, a_log, dt_bias, delta_norm_g, sconv_w,
                           cconv_w, cconv_b, cln_g, cln_b, w_out, final_norm_g)

    xp = x_prompt
    p_outs = [[] for _ in range(4)]
    for l in range(depth):
        xp, pd, pq, ps, pc = _prompt_layer_pipelined(xp, wts, l, l == depth - 1)
        for acc, o in zip(p_outs, (pd, pq, ps, pc)):
            acc.append(o)
    pd, pq, ps, pc = (jnp.stack(o) for o in p_outs)

    xs, sd, sq, ss, sc = _sample_layers(x_sample, state_delta, state_qkv_conv, state_sconv,
                                        state_cconv, wts)
    return (xp, xs[depth - 1],
            pd, pq[:, :, SUBLANES - (QK_CONV - 1):, :], ps[:, :, SUBLANES - (SCONV_W - 1):, :],
            pc[:, :, CTAIL - (CONF_W - 1):, :],
            sd, sq, ss, sc)
```

```python
import functools

import jax
import jax.numpy as jnp
from jax import lax
from jax.experimental import pallas as pl
from jax.experimental.pallas import tpu as pltpu

D_MODEL = 1024
N_DHEADS = 4
HEAD_DIM = 128
D_DELTA = N_DHEADS * HEAD_DIM
D_SCONV = 256
D_CONF = 256
QK_CONV = 4
SCONV_W = 3
CONF_W = 31
CHUNK = 64
EPS = 1e-6

R_GATE_D = 0
R_SB = 512
R_SC = 768
R_SX = 1024
R_GATE_S = 1280
R_GA = 1536
R_GB = 1792
R_GATE_C = 2048
R_BA = 2304
R_WIDTH = 2432
W_REST = 3 * D_DELTA

SUBLANES = 8
LANES = 128
TIME_TILE = 256
SAMPLE_PAD = 8
SAMPLE_BATCH_BLOCK = 16
SAMPLE_SEQ_LOCKSTEP = 4
CTAIL = 32
VMEM_LIMIT = 56 * 1024 * 1024

F32 = jnp.float32
BF16 = jnp.bfloat16


def _mm(a, b):
    return jnp.dot(a.astype(BF16), b.astype(BF16), preferred_element_type=F32)


def _mm_nt(a, b):
    return lax.dot_general(a.astype(BF16), b.astype(BF16), (((1,), (1,)), ((), ())),
                           preferred_element_type=F32)


def _mm_tn(a, b):
    return lax.dot_general(a.astype(BF16), b.astype(BF16), (((0,), (0,)), ((), ())),
                           preferred_element_type=F32)


def _sigmoid(x):
    return 1.0 / (1.0 + jnp.exp(-x))


def _silu(x):
    return x * _sigmoid(x)


def _softplus(x):
    return jnp.maximum(x, 0.0) + jnp.log1p(jnp.exp(-jnp.abs(x)))


def _rms_rows(x, g):
    return x * lax.rsqrt(jnp.mean(x * x, axis=-1, keepdims=True) + EPS) * g


def _chunk_cumsum(g, chunk, row):
    pos = row % chunk
    s = 1
    while s < chunk:
        g = g + jnp.where(pos >= s, pltpu.roll(g, s, axis=0), 0.0)
        s *= 2
    return g


def _run(staged):
    try:
        while True:
            next(staged)
    except StopIteration as stop:
        return stop.value


def _interleave(*staged):
    live = list(staged)
    while live:
        for s in list(live):
            try:
                next(s)
            except StopIteration:
                live.remove(s)


def _pair_diag(a, b):
    z = jnp.zeros_like(a)
    return jnp.concatenate([jnp.concatenate([a, z], axis=1), jnp.concatenate([z, b], axis=1)], axis=0)


class _Packed:
    def __init__(self, chunk, rows, bf16_tile):
        self.chunk, self.rows, self.n = chunk, rows, rows // chunk
        self.rr = lax.broadcasted_iota(jnp.int32, (chunk, rows), 0)
        lane = lax.broadcasted_iota(jnp.int32, (chunk, rows), 1)
        self.jl = lane % chunk
        self.lane_blk = lane // chunk
        ii = lax.broadcasted_iota(jnp.int32, (rows, rows), 0)
        jj = lax.broadcasted_iota(jnp.int32, (rows, rows), 1)
        mask = jnp.where((ii // chunk) == (jj // chunk), 1.0, 0.0).astype(F32)
        self.bd_mask = mask.astype(BF16) if bf16_tile else mask

    def pack(self, g):
        out = g[0:self.chunk]
        for c in range(1, self.n):
            out = jnp.where(self.lane_blk == c, g[c * self.chunk:(c + 1) * self.chunk], out)
        return out

    def col(self, v):
        shape = (self.chunk, self.rows)
        out = jnp.broadcast_to(v[0:self.chunk], shape)
        for c in range(1, self.n):
            out = jnp.where(self.lane_blk == c,
                            jnp.broadcast_to(v[c * self.chunk:(c + 1) * self.chunk], shape), out)
        return out

    def block_diag(self, xp):
        if self.bd_mask.dtype == BF16:
            return jnp.concatenate([xp.astype(BF16)] * self.n, axis=0) * self.bd_mask
        return (jnp.concatenate([xp] * self.n, axis=0) * self.bd_mask).astype(BF16)

    def unit_lower_inverse(self, lps, nil):
        base = min(self.chunk, 16)
        same = (self.rr // base) == (self.jl // base)
        eye = jnp.where(self.rr == self.jl, 1.0, 0.0).astype(F32)
        ds = [jnp.where(same, lp, 0.0) for lp in lps]
        xs = [eye - d for d in ds]
        d_bds = [self.block_diag(d) for d in ds]
        p = 2
        while p < min(base, nil):
            ds = [_mm(d, d_bd) for d, d_bd in zip(ds, d_bds)]
            yield
            d_bds = [self.block_diag(d) for d in ds]
            xs = [x + _mm(x, d_bd) for x, d_bd in zip(xs, d_bds)]
            yield
            p *= 2
        size = base
        while size < self.chunk:
            big = (self.rr // (2 * size)) == (self.jl // (2 * size))
            off = jnp.logical_and(big, jnp.logical_not(same))
            xes = [_mm(x, self.block_diag(jnp.where(off, lp, 0.0))) for x, lp in zip(xs, lps)]
            yield
            xs = [x - _mm(xe, self.block_diag(x)) for x, xe in zip(xs, xes)]
            yield
            same = big
            size *= 2
        return xs

    def delta_prep(self, heads, nil):
        tril = self.rr >= self.jl
        strict = self.rr > self.jl
        decays, kbs, gs = [], [], []
        for q, k, v, beta, gc_col, gc_row in heads:
            diff = self.col(gc_col) - gc_row
            decays.append(jnp.where(tril, jnp.exp(jnp.where(tril, diff, 0.0)), 0.0))
            kbs.append(k * beta)
        for (q, k, *_), kb in zip(heads, kbs):
            gs.append(_mm_nt(jnp.concatenate([kb, q], axis=0), k))
        yield
        lps = [jnp.where(strict, self.pack(g[:self.rows]) * dec, 0.0) for g, dec in zip(gs, decays)]
        qkps = [self.pack(g[self.rows:]) * dec for g, dec in zip(gs, decays)]
        tinvs = yield from self.unit_lower_inverse(lps, nil)
        out = []
        for (q, k, v, beta, gc_col, _), kb, tinv, qkp in zip(heads, kbs, tinvs, qkps):
            eg = jnp.exp(gc_col)
            uw = _mm(self.block_diag(tinv), jnp.concatenate([v * beta, kb * eg], axis=1))
            out.append((uw[:, :HEAD_DIM], uw[:, HEAD_DIM:], q * eg, qkp))
        yield
        return out


def _branch_s(buf_ref, w_ref, tail, rows, lead=()):
    n = len(lead)
    acc = None
    for j in range(SCONV_W):
        idx = lead + (pl.ds(tail - (SCONV_W - 1) + j, rows), slice(None))
        term = buf_ref[idx] * w_ref[j:j + 1, :].reshape((1,) * n + (1, D_SCONV))
        acc = term if acc is None else acc + term
    return acc


def _conformer(ubuf_ref, w_ref, b_ref, g_ref, beta_ref, tail, rows, lead=()):
    n = len(lead)
    shp = (1,) * n + (1, D_CONF)
    acc = None
    for j in range(CONF_W):
        idx = lead + (pl.ds(tail - (CONF_W - 1) + j, rows), slice(None))
        term = ubuf_ref[idx] * w_ref[j:j + 1, :].reshape(shp)
        acc = term if acc is None else acc + term
    yc = acc + b_ref[...].reshape(shp)
    mu = jnp.mean(yc, axis=-1, keepdims=True)
    var = jnp.mean(jnp.square(yc - mu), axis=-1, keepdims=True)
    yc = (yc - mu) * lax.rsqrt(var + EPS) * g_ref[...].reshape(shp) + beta_ref[...].reshape(shp)
    return _silu(yc)


def _prompt_kernel(final_norm,
                   x_ref, ng_ref, wqkv_ref, wrest_ref, cw_ref, alog_ref, dt_ref, dng_ref,
                   sw_ref, ccw_ref, ccb_ref, clg_ref, clb_ref, wout_ref, fg_ref,
                   y_ref, s_out_ref, qt_ref, st_ref, ct_ref,
                   qbuf, sbuf, cbuf, cshift, s_scr, obuf, rest,
                   u_s, lhs_s, kdt_s, egl_s, qk_s):
    i = pl.program_id(1)
    nt = pl.num_programs(1)
    tt = TIME_TILE
    nchunk = tt // CHUNK

    @pl.when(i == 0)
    def _():
        qbuf[0:SUBLANES, :] = jnp.zeros((SUBLANES, 3 * D_DELTA), F32)
        sbuf[0:SUBLANES, :] = jnp.zeros((SUBLANES, D_SCONV), F32)
        cbuf[0:CTAIL, :] = jnp.zeros((CTAIL, D_CONF), F32)
        s_scr[...] = jnp.zeros(s_scr.shape, F32)

    x = x_ref[0]
    h = _rms_rows(x, ng_ref[...]).astype(BF16)
    qbuf[SUBLANES:SUBLANES + tt, :] = jnp.dot(h, wqkv_ref[...], preferred_element_type=F32)
    rest[...] = jnp.dot(h, wrest_ref[...], preferred_element_type=F32)

    sbuf[SUBLANES:SUBLANES + tt, :] = rest[:, R_SC:R_SC + D_SCONV] * rest[:, R_SX:R_SX + D_SCONV]
    ys = _branch_s(sbuf, sw_ref, SUBLANES, tt)
    obuf[:, D_DELTA:D_DELTA + D_SCONV] = (rest[:, R_SB:R_SB + D_SCONV] * ys
                                          * _silu(rest[:, R_GATE_S:R_GATE_S + D_SCONV]))

    cbuf[CTAIL:CTAIL + tt, :] = rest[:, R_GA:R_GA + D_CONF] * _sigmoid(rest[:, R_GB:R_GB + D_CONF])
    for sh in range(1, SUBLANES):
        cshift[sh - 1] = cbuf[pl.ds(sh, tt + CTAIL - SUBLANES), :]

    row = lax.broadcasted_iota(jnp.int32, (tt, LANES), 0)
    ba = rest[:, R_BA:R_BA + LANES]
    beta_all = _sigmoid(ba)
    g_all = -jnp.exp(alog_ref[...]) * _softplus(ba + dt_ref[...])
    gc = _chunk_cumsum(g_all, CHUNK, row)
    gct = gc.T
    gl_rows = jnp.concatenate(
        [jnp.broadcast_to(gc[(c + 1) * CHUNK - 1:(c + 1) * CHUNK, :], (CHUNK, LANES))
         for c in range(nchunk)], axis=0)
    pk = _Packed(CHUNK, tt, True)

    def conv_act(c0):
        acc = None
        for j in range(QK_CONV):
            term = (qbuf[pl.ds(SUBLANES - (QK_CONV - 1) + j, tt), c0:c0 + HEAD_DIM]
                    * cw_ref[j:j + 1, c0:c0 + HEAD_DIM])
            acc = term if acc is None else acc + term
        return _silu(acc)

    heads, kdecs = [], []
    for hd in range(N_DHEADS):
        q = conv_act(hd * HEAD_DIM)
        k = conv_act(D_DELTA + hd * HEAD_DIM)
        v = conv_act(2 * D_DELTA + hd * HEAD_DIM)
        q = q * lax.rsqrt(jnp.sum(q * q, axis=-1, keepdims=True) + EPS) * (HEAD_DIM ** -0.5)
        k = k * lax.rsqrt(jnp.sum(k * k, axis=-1, keepdims=True) + EPS)
        lane = N_DHEADS + hd
        gcol = gc[:, lane:lane + 1]
        heads.append((q, k, v, beta_all[:, hd:hd + 1], gcol, gct[lane:lane + 1, :]))
        kdecs.append(k * jnp.exp(gl_rows[:, lane:lane + 1] - gcol))
    prepped = pk.delta_prep(heads, CHUNK)

    for pr in range(N_DHEADS // 2):
        h0, h1 = 2 * pr, 2 * pr + 1
        (u0, w0, qg0, qkp0), (u1, w1, qg1, qkp1) = prepped[h0], prepped[h1]
        kd0, kd1 = kdecs[h0], kdecs[h1]
        u_s[pr] = jnp.concatenate([u0, u1], axis=1)
        w = jnp.concatenate([w0, w1], axis=1).astype(BF16)
        qg = jnp.concatenate([qg0, qg1], axis=1).astype(BF16)
        kst = jnp.concatenate(
            [kd[c * CHUNK:(c + 1) * CHUNK] for c in range(nchunk) for kd in (kd0, kd1)], axis=0)
        kdt = kst.T.astype(BF16)
        low = (lax.broadcasted_iota(jnp.int32, (CHUNK, tt), 1) % LANES) < CHUNK
        qk_even = jnp.where(low, qkp0, pltpu.roll(qkp1, CHUNK, axis=1)).astype(BF16)
        qk_odd = jnp.where(low, pltpu.roll(qkp0, tt - CHUNK, axis=1), qkp1).astype(BF16)
        for c in range(nchunk):
            cs = slice(c * CHUNK, (c + 1) * CHUNK)
            lhs_s[pr, c, 0:CHUNK, :] = w[cs]
            lhs_s[pr, c, CHUNK:, :] = qg[cs]
            kdt_s[pr, c] = kdt[:, 2 * c * CHUNK:2 * (c + 1) * CHUNK]
            tile = slice((c // 2) * LANES, (c // 2 + 1) * LANES)
            qk_s[pr, c] = (qk_even if c % 2 == 0 else qk_odd)[:, tile]
            ge = (c + 1) * CHUNK - 1
            egl = jnp.concatenate(
                [jnp.broadcast_to(jnp.exp(gc[ge:ge + 1, N_DHEADS + hh:N_DHEADS + hh + 1]),
                                  (SUBLANES, HEAD_DIM)) for hh in (h0, h1)], axis=1)
            egl_s[c, :, 2 * pr * HEAD_DIM:2 * (pr + 1) * HEAD_DIM] = egl

    def chunk_body(c, carry):
        r0 = pl.multiple_of(c * CHUNK, CHUNK)
        rows_c = pl.ds(r0, CHUNK)
        pairs = range(N_DHEADS // 2)
        ss = [s_scr[pr] for pr in pairs]
        rs = [jnp.dot(lhs_s[pr, c], _pair_diag(s[:, :HEAD_DIM], s[:, HEAD_DIM:]).astype(BF16),
                      preferred_element_type=F32) for pr, s in zip(pairs, ss)]
        vns = [u_s[pr, rows_c, :] - r[:CHUNK] for pr, r in zip(pairs, rs)]
        v_bds = [_pair_diag(vn[:, :HEAD_DIM], vn[:, HEAD_DIM:]).astype(BF16) for vn in vns]
        upds = [jnp.dot(kdt_s[pr, c], v_bd, preferred_element_type=F32) for pr, v_bd in zip(pairs, v_bds)]
        o_intra = [jnp.dot(qk_s[pr, c], v_bd, preferred_element_type=F32) for pr, v_bd in zip(pairs, v_bds)]
        for pr in pairs:
            egl = egl_s[c][0:1, 2 * pr * HEAD_DIM:2 * (pr + 1) * HEAD_DIM]
            s_scr[pr] = ss[pr] * egl + upds[pr]
            o_pair = rs[pr][CHUNK:] + o_intra[pr]
            for hl in range(2):
                hd = 2 * pr + hl
                o = _rms_rows(o_pair[:, hl * HEAD_DIM:(hl + 1) * HEAD_DIM], dng_ref[...])
                gate = rest[rows_c, R_GATE_D + hd * HEAD_DIM:R_GATE_D + (hd + 1) * HEAD_DIM]
                obuf[rows_c, hd * HEAD_DIM:(hd + 1) * HEAD_DIM] = o * _silu(gate)
        acc = None
        for j in range(CONF_W):
            off = CTAIL - (CONF_W - 1) + j
            start = pl.multiple_of(r0 + (off // SUBLANES) * SUBLANES, SUBLANES)
            sh = off % SUBLANES
            src = cbuf[pl.ds(start, CHUNK), :] if sh == 0 else cshift[sh - 1, pl.ds(start, CHUNK), :]
            term = src * ccw_ref[j:j + 1, :]
            acc = term if acc is None else acc + term
        yc = acc + ccb_ref[...]
        mu = jnp.mean(yc, axis=-1, keepdims=True)
        var = jnp.mean(jnp.square(yc - mu), axis=-1, keepdims=True)
        yc = (yc - mu) * lax.rsqrt(var + EPS) * clg_ref[...] + clb_ref[...]
        obuf[rows_c, D_DELTA + D_SCONV:] = _silu(yc) * _silu(rest[rows_c, R_GATE_C:R_GATE_C + D_CONF])
        return carry

    lax.fori_loop(0, nchunk, chunk_body, 0)

    y = x_ref[0] + jnp.dot(obuf[...].astype(BF16), wout_ref[...], preferred_element_type=F32)
    if final_norm:
        y = _rms_rows(y, fg_ref[...])
    y_ref[0] = y

    @pl.when(i == nt - 1)
    def _():
        for pr in range(N_DHEADS // 2):
            s_out_ref[0, 2 * pr] = s_scr[pr, :, :HEAD_DIM]
            s_out_ref[0, 2 * pr + 1] = s_scr[pr, :, HEAD_DIM:]
        qt_ref[0] = qbuf[tt:tt + SUBLANES, :]
        st_ref[0] = sbuf[tt:tt + SUBLANES, :]
        ct_ref[0] = cbuf[tt:tt + CTAIL, :]

    qbuf[0:SUBLANES, :] = qbuf[tt:tt + SUBLANES, :]
    sbuf[0:SUBLANES, :] = sbuf[tt:tt + SUBLANES, :]
    cbuf[0:CTAIL, :] = cbuf[tt:tt + CTAIL, :]


def _weight_specs(wts, layer_of, buffers=2):
    specs = []
    for w in wts[:-1]:
        tail = (0,) * (w.ndim - 1)
        specs.append(pl.BlockSpec((None,) + w.shape[1:], lambda *ids, tail=tail: (layer_of(*ids),) + tail,
                                  pipeline_mode=pl.Buffered(buffers)))
    specs.append(pl.BlockSpec(wts[-1].shape, lambda *ids: (0,) * wts[-1].ndim))
    return specs


def _prompt_layer(x, wts, layer, final_norm):
    b, t, d = x.shape
    nt = t // TIME_TILE
    npair = N_DHEADS // 2
    nchunk = TIME_TILE // CHUNK
    weight_specs = _weight_specs(wts, lambda *ids: layer)
    out_shape = (
        jax.ShapeDtypeStruct((b, t, d), F32),
        jax.ShapeDtypeStruct((b, N_DHEADS, HEAD_DIM, HEAD_DIM), F32),
        jax.ShapeDtypeStruct((b, SUBLANES, 3 * D_DELTA), F32),
        jax.ShapeDtypeStruct((b, SUBLANES, D_SCONV), F32),
        jax.ShapeDtypeStruct((b, CTAIL, D_CONF), F32),
    )
    out_specs = (
        pl.BlockSpec((1, TIME_TILE, d), lambda bi, ti: (bi, ti, 0)),
        pl.BlockSpec((1, N_DHEADS, HEAD_DIM, HEAD_DIM), lambda bi, ti: (bi, 0, 0, 0)),
        pl.BlockSpec((1, SUBLANES, 3 * D_DELTA), lambda bi, ti: (bi, 0, 0)),
        pl.BlockSpec((1, SUBLANES, D_SCONV), lambda bi, ti: (bi, 0, 0)),
        pl.BlockSpec((1, CTAIL, D_CONF), lambda bi, ti: (bi, 0, 0)),
    )
    scratch = [
        pltpu.VMEM((TIME_TILE + SUBLANES, 3 * D_DELTA), F32),
        pltpu.VMEM((TIME_TILE + SUBLANES, D_SCONV), F32),
        pltpu.VMEM((TIME_TILE + CTAIL, D_CONF), F32),
        pltpu.VMEM((SUBLANES - 1, TIME_TILE + CTAIL - SUBLANES, D_CONF), F32),
        pltpu.VMEM((npair, HEAD_DIM, 2 * HEAD_DIM), F32),
        pltpu.VMEM((TIME_TILE, D_MODEL), F32),
        pltpu.VMEM((TIME_TILE, R_WIDTH), F32),
        pltpu.VMEM((npair, TIME_TILE, 2 * HEAD_DIM), F32),
        pltpu.VMEM((npair, nchunk, 2 * CHUNK, 2 * HEAD_DIM), BF16),
        pltpu.VMEM((npair, nchunk, HEAD_DIM, 2 * CHUNK), BF16),
        pltpu.VMEM((nchunk, SUBLANES, N_DHEADS * HEAD_DIM), F32),
        pltpu.VMEM((npair, nchunk, CHUNK, 2 * CHUNK), BF16),
    ]
    return pl.pallas_call(
        functools.partial(_prompt_kernel, final_norm),
        grid=(b, nt),
        in_specs=[pl.BlockSpec((1, TIME_TILE, d), lambda bi, ti: (bi, ti, 0))] + weight_specs,
        out_specs=out_specs,
        out_shape=out_shape,
        scratch_shapes=scratch,
        compiler_params=pltpu.CompilerParams(
            dimension_semantics=("arbitrary", "arbitrary"),
            vmem_limit_bytes=VMEM_LIMIT),
        name="prompt_layer",
    )(x, *wts)


def _pipe_kernel(final_norm, nt,
                 x_ref, ng_ref, win_ref, cw_ref, alog_ref, dt_ref, dng_ref,
                 sw_ref, ccw_ref, ccb_ref, clg_ref, clb_ref, wout_ref, fg_ref,
                 y_ref, s_out_ref, qt_ref, st_ref, ct_ref,
                 qbuf, sbuf, s_scr, xs, qa, bg, rest, cbuf, cshift, obuf):
    g = pl.program_id(0)
    n_tiles = pl.num_programs(0) - 1
    tt = TIME_TILE
    nchunk = tt // CHUNK
    npair = N_DHEADS // 2
    t1 = jnp.minimum(g, n_tiles - 1) % nt
    t2 = jnp.maximum(g - 1, 0) % nt
    p = g % 2
    q = 1 - p

    @pl.when(g == 0)
    def _():
        xs[1] = jnp.zeros(xs.shape[1:], F32)
        qa[1] = jnp.zeros(qa.shape[1:], F32)
        bg[1] = jnp.zeros(bg.shape[1:], F32)
        rest[1] = jnp.zeros(rest.shape[1:], F32)
        cbuf[1] = jnp.zeros(cbuf.shape[1:], F32)
        cshift[1] = jnp.zeros(cshift.shape[1:], F32)
        obuf[1] = jnp.zeros(obuf.shape[1:], F32)

    @pl.when(t1 == 0)
    def _():
        qbuf[0:SUBLANES, :] = jnp.zeros((SUBLANES, 3 * D_DELTA), F32)
        sbuf[0:SUBLANES, :] = jnp.zeros((SUBLANES, D_SCONV), F32)

    @pl.when(t2 == 0)
    def _():
        s_scr[...] = jnp.zeros(s_scr.shape, F32)

    def stage1():
        x = x_ref[0]
        xs[p] = x
        h = _rms_rows(x, ng_ref[...]).astype(BF16)
        yield
        for c0 in range(0, 3 * D_DELTA, 2 * LANES):
            qbuf[SUBLANES:SUBLANES + tt, c0:c0 + 2 * LANES] = jnp.dot(
                h, win_ref[:, c0:c0 + 2 * LANES], preferred_element_type=F32)
            yield
        for c0 in range(0, R_WIDTH, 2 * LANES):
            c1 = min(c0 + 2 * LANES, R_WIDTH)
            rest[p, :, c0:c1] = jnp.dot(h, win_ref[:, W_REST + c0:W_REST + c1],
                                        preferred_element_type=F32)
            yield

        def conv_act(c0):
            acc = None
            for j in range(QK_CONV):
                term = (qbuf[pl.ds(SUBLANES - (QK_CONV - 1) + j, tt), c0:c0 + HEAD_DIM]
                        * cw_ref[j:j + 1, c0:c0 + HEAD_DIM])
                acc = term if acc is None else acc + term
            return _silu(acc)

        for hd in range(N_DHEADS):
            qh = conv_act(hd * HEAD_DIM)
            qa[p, :, hd * HEAD_DIM:(hd + 1) * HEAD_DIM] = (
                qh * lax.rsqrt(jnp.sum(qh * qh, axis=-1, keepdims=True) + EPS) * (HEAD_DIM ** -0.5))
            kh = conv_act(D_DELTA + hd * HEAD_DIM)
            qa[p, :, D_DELTA + hd * HEAD_DIM:D_DELTA + (hd + 1) * HEAD_DIM] = (
                kh * lax.rsqrt(jnp.sum(kh * kh, axis=-1, keepdims=True) + EPS))
            qa[p, :, 2 * D_DELTA + hd * HEAD_DIM:2 * D_DELTA + (hd + 1) * HEAD_DIM] = conv_act(
                2 * D_DELTA + hd * HEAD_DIM)
            yield

        row = lax.broadcasted_iota(jnp.int32, (tt, LANES), 0)
        ba = rest[p, :, R_BA:R_BA + LANES]
        bg[p, 0] = _sigmoid(ba)
        bg[p, 1] = _chunk_cumsum(-jnp.exp(alog_ref[...]) * _softplus(ba + dt_ref[...]), CHUNK, row)
        yield

        sbuf[SUBLANES:SUBLANES + tt, :] = (rest[p, :, R_SC:R_SC + D_SCONV]
                                           * rest[p, :, R_SX:R_SX + D_SCONV])
        obuf[p, :, D_DELTA:D_DELTA + D_SCONV] = (
            rest[p, :, R_SB:R_SB + D_SCONV] * _branch_s(sbuf, sw_ref, SUBLANES, tt)
            * _silu(rest[p, :, R_GATE_S:R_GATE_S + D_SCONV]))
        yield

        prev_tail = jnp.where(t1 == 0, 0.0, cbuf[q, tt:tt + CTAIL, :])
        cbuf[p, 0:CTAIL, :] = prev_tail
        cbuf[p, CTAIL:CTAIL + tt, :] = (rest[p, :, R_GA:R_GA + D_CONF]
                                        * _sigmoid(rest[p, :, R_GB:R_GB + D_CONF]))
        yield
        for sh in range(1, SUBLANES):
            cshift[p, sh - 1] = cbuf[p, pl.ds(sh, tt + CTAIL - SUBLANES), :]
        yield

        qbuf[0:SUBLANES, :] = qbuf[tt:tt + SUBLANES, :]
        sbuf[0:SUBLANES, :] = sbuf[tt:tt + SUBLANES, :]

    def stage2():
        beta_all = bg[q, 0]
        gc = bg[q, 1]
        gct = gc.T
        gl_rows = jnp.concatenate(
            [jnp.broadcast_to(gc[(c + 1) * CHUNK - 1:(c + 1) * CHUNK, :], (CHUNK, LANES))
             for c in range(nchunk)], axis=0)
        pk = _Packed(CHUNK, tt, True)
        heads, kdecs = [], []
        for hd in range(N_DHEADS):
            lane = N_DHEADS + hd
            gcol = gc[:, lane:lane + 1]
            kh = qa[q, :, D_DELTA + hd * HEAD_DIM:D_DELTA + (hd + 1) * HEAD_DIM]
            heads.append((qa[q, :, hd * HEAD_DIM:(hd + 1) * HEAD_DIM], kh,
                          qa[q, :, 2 * D_DELTA + hd * HEAD_DIM:2 * D_DELTA + (hd + 1) * HEAD_DIM],
                          beta_all[:, hd:hd + 1], gcol, gct[lane:lane + 1, :]))
            kdecs.append(kh * jnp.exp(gl_rows[:, lane:lane + 1] - gcol))
        yield
        prepped = yield from pk.delta_prep(heads, CHUNK)

        us, lhss, kdts, qks = [], [], [], []
        low = (lax.broadcasted_iota(jnp.int32, (CHUNK, tt), 1) % LANES) < CHUNK
        for pr in range(npair):
            (u0, w0, qg0, qkp0), (u1, w1, qg1, qkp1) = prepped[2 * pr], prepped[2 * pr + 1]
            us.append(jnp.concatenate([u0, u1], axis=1))
            w = jnp.concatenate([w0, w1], axis=1).astype(BF16)
            qg = jnp.concatenate([qg0, qg1], axis=1).astype(BF16)
            lhss.append([jnp.concatenate([w[c * CHUNK:(c + 1) * CHUNK], qg[c * CHUNK:(c + 1) * CHUNK]],
                                         axis=0) for c in range(nchunk)])
            kst = jnp.concatenate([kd[c * CHUNK:(c + 1) * CHUNK] for c in range(nchunk)
                                   for kd in (kdecs[2 * pr], kdecs[2 * pr + 1])], axis=0)
            kdts.append(kst.T.astype(BF16))
            qks.append((jnp.where(low, qkp0, pltpu.roll(qkp1, CHUNK, axis=1)).astype(BF16),
                        jnp.where(low, pltpu.roll(qkp0, tt - CHUNK, axis=1), qkp1).astype(BF16)))
        yield

        ss = [s_scr[pr] for pr in range(npair)]
        for c in range(nchunk):
            cs = slice(c * CHUNK, (c + 1) * CHUNK)
            rs = [jnp.dot(lhss[pr][c], _pair_diag(s[:, :HEAD_DIM], s[:, HEAD_DIM:]).astype(BF16),
                          preferred_element_type=F32) for pr, s in enumerate(ss)]
            yield
            vns = [us[pr][cs] - r[:CHUNK] for pr, r in enumerate(rs)]
            v_bds = [_pair_diag(vn[:, :HEAD_DIM], vn[:, HEAD_DIM:]).astype(BF16) for vn in vns]
            upds = [jnp.dot(kdts[pr][:, 2 * c * CHUNK:2 * (c + 1) * CHUNK], v_bd,
                            preferred_element_type=F32) for pr, v_bd in enumerate(v_bds)]
            tile = slice((c // 2) * LANES, (c // 2 + 1) * LANES)
            o_intra = [jnp.dot(qks[pr][c % 2][:, tile], v_bd, preferred_element_type=F32)
                       for pr, v_bd in enumerate(v_bds)]
            ge = (c + 1) * CHUNK - 1
            for pr in range(npair):
                egl = jnp.concatenate(
                    [jnp.broadcast_to(jnp.exp(gc[ge:ge + 1, N_DHEADS + hh:N_DHEADS + hh + 1]),
                                      (1, HEAD_DIM)) for hh in (2 * pr, 2 * pr + 1)], axis=1)
                ss[pr] = ss[pr] * egl + upds[pr]
                o_pair = rs[pr][CHUNK:] + o_intra[pr]
                for hl in range(2):
                    hd = 2 * pr + hl
                    o = _rms_rows(o_pair[:, hl * HEAD_DIM:(hl + 1) * HEAD_DIM], dng_ref[...])
                    gate = rest[q, cs, R_GATE_D + hd * HEAD_DIM:R_GATE_D + (hd + 1) * HEAD_DIM]
                    obuf[q, cs, hd * HEAD_DIM:(hd + 1) * HEAD_DIM] = o * _silu(gate)
            yield
            acc = None
            for j in range(CONF_W):
                off = CTAIL - (CONF_W - 1) + j
                start = c * CHUNK + (off // SUBLANES) * SUBLANES
                sh = off % SUBLANES
                src = (cbuf[q, start:start + CHUNK, :] if sh == 0
                       else cshift[q, sh - 1, start:start + CHUNK, :])
                term = src * ccw_ref[j:j + 1, :]
                acc = term if acc is None else acc + term
            yc = acc + ccb_ref[...]
            mu = jnp.mean(yc, axis=-1, keepdims=True)
            var = jnp.mean(jnp.square(yc - mu), axis=-1, keepdims=True)
            yc = (yc - mu) * lax.rsqrt(var + EPS) * clg_ref[...] + clb_ref[...]
            obuf[q, cs, D_DELTA + D_SCONV:] = _silu(yc) * _silu(rest[q, cs, R_GATE_C:R_GATE_C + D_CONF])
            yield
        for pr in range(npair):
            s_scr[pr] = ss[pr]

        y = xs[q] + jnp.dot(obuf[q].astype(BF16), wout_ref[...], preferred_element_type=F32)
        if final_norm:
            y = _rms_rows(y, fg_ref[...])
        y_ref[0] = y

    _interleave(stage1(), stage2())

    @pl.when(jnp.logical_and(g >= 1, t2 == nt - 1))
    def _():
        for pr in range(npair):
            s_out_ref[0, 2 * pr] = s_scr[pr, :, :HEAD_DIM]
            s_out_ref[0, 2 * pr + 1] = s_scr[pr, :, HEAD_DIM:]

    @pl.when(jnp.logical_and(g < n_tiles, t1 == nt - 1))
    def _():
        qt_ref[0] = qbuf[0:SUBLANES, :]
        st_ref[0] = sbuf[0:SUBLANES, :]
        ct_ref[0] = cbuf[p, tt:tt + CTAIL, :]


def _prompt_layer_pipelined(x, wts, layer, final_norm):
    b, t, d = x.shape
    nt = t // TIME_TILE
    n_tiles = b * nt
    npair = N_DHEADS // 2

    def tile1(g):
        g1 = jnp.minimum(g, n_tiles - 1)
        return g1 // nt, g1 % nt

    def tile2(g):
        g2 = jnp.maximum(g - 1, 0)
        return g2 // nt, g2 % nt

    out_shape = (
        jax.ShapeDtypeStruct((b, t, d), F32),
        jax.ShapeDtypeStruct((b, N_DHEADS, HEAD_DIM, HEAD_DIM), F32),
        jax.ShapeDtypeStruct((b, SUBLANES, 3 * D_DELTA), F32),
        jax.ShapeDtypeStruct((b, SUBLANES, D_SCONV), F32),
        jax.ShapeDtypeStruct((b, CTAIL, D_CONF), F32),
    )
    out_specs = (
        pl.BlockSpec((1, TIME_TILE, d), lambda g: tile2(g) + (0,)),
        pl.BlockSpec((1, N_DHEADS, HEAD_DIM, HEAD_DIM), lambda g: (tile2(g)[0], 0, 0, 0)),
        pl.BlockSpec((1, SUBLANES, 3 * D_DELTA), lambda g: (tile1(g)[0], 0, 0)),
        pl.BlockSpec((1, SUBLANES, D_SCONV), lambda g: (tile1(g)[0], 0, 0)),
        pl.BlockSpec((1, CTAIL, D_CONF), lambda g: (tile1(g)[0], 0, 0)),
    )
    scratch = [
        pltpu.VMEM((TIME_TILE + SUBLANES, 3 * D_DELTA), F32),
        pltpu.VMEM((TIME_TILE + SUBLANES, D_SCONV), F32),
        pltpu.VMEM((npair, HEAD_DIM, 2 * HEAD_DIM), F32),
        pltpu.VMEM((2, TIME_TILE, D_MODEL), F32),
        pltpu.VMEM((2, TIME_TILE, 3 * D_DELTA), F32),
        pltpu.VMEM((2, 2, TIME_TILE, LANES), F32),
        pltpu.VMEM((2, TIME_TILE, R_WIDTH), F32),
        pltpu.VMEM((2, TIME_TILE + CTAIL, D_CONF), F32),
        pltpu.VMEM((2, SUBLANES - 1, TIME_TILE + CTAIL - SUBLANES, D_CONF), F32),
        pltpu.VMEM((2, TIME_TILE, D_MODEL), F32),
    ]
    return pl.pallas_call(
        functools.partial(_pipe_kernel, final_norm, nt),
        grid=(n_tiles + 1,),
        in_specs=[pl.BlockSpec((1, TIME_TILE, d), lambda g: tile1(g) + (0,))]
        + _weight_specs(wts, lambda g: layer, buffers=1),
        out_specs=out_specs,
        out_shape=out_shape,
        scratch_shapes=scratch,
        compiler_params=pltpu.CompilerParams(
            dimension_semantics=("arbitrary",),
            vmem_limit_bytes=VMEM_LIMIT),
        name="prompt_layer",
    )(x, *wts)


def _sample_kernel(x_ref, sd_ref, sq_ref, ss_ref, sc_ref,
                   ng_ref, win_ref, cw_ref, alog_ref, dt_ref, dng_ref,
                   sw_ref, ccw_ref, ccb_ref, clg_ref, clb_ref, wout_ref, fg_ref,
                   y_ref, sd_out_ref, qt_ref, st_ref, ct_ref,
                   qbuf, sbuf, cbuf, obuf, ubuf, wqbuf, kdbuf, glbuf, oibuf, rest, xcarry):
    layer = pl.program_id(0)
    blk = pl.program_id(1)
    nb = SAMPLE_BATCH_BLOCK
    pad = SAMPLE_PAD
    ntok = pad // 2
    rows = nb * pad

    @pl.when(layer == 0)
    def _():
        xcarry[blk, :, 0:ntok, :] = x_ref[...]
        xcarry[blk, :, ntok:, :] = jnp.zeros((nb, pad - ntok, D_MODEL), F32)

    x = xcarry[blk].reshape(rows, D_MODEL)
    h = _rms_rows(x, ng_ref[...]).astype(BF16)
    qkv = jnp.dot(h, win_ref[:, :W_REST], preferred_element_type=F32)
    rest[...] = jnp.dot(h, win_ref[:, W_REST:], preferred_element_type=F32)
    row = lax.broadcasted_iota(jnp.int32, (rows, LANES), 0)
    valid = (row % pad) < (pad // 2)
    valid1 = valid[:, 0:1]

    qkv3 = qkv.reshape(nb, pad, 3 * D_DELTA)
    qt_ref[...] = qkv3[:, ntok - (QK_CONV - 1):ntok, :]
    qbuf[:, SUBLANES - (QK_CONV - 1):SUBLANES, :] = sq_ref[...]
    qbuf[:, SUBLANES:, :] = qkv3

    hs = (rest[:, R_SC:R_SC + D_SCONV] * rest[:, R_SX:R_SX + D_SCONV]).reshape(nb, pad, D_SCONV)
    st_ref[...] = hs[:, ntok - (SCONV_W - 1):ntok, :]
    sbuf[:, SUBLANES - (SCONV_W - 1):SUBLANES, :] = ss_ref[...]
    sbuf[:, SUBLANES:, :] = hs
    ys = _branch_s(sbuf, sw_ref, SUBLANES, pad, lead=(slice(None),))
    obuf[:, D_DELTA:D_DELTA + D_SCONV] = (rest[:, R_SB:R_SB + D_SCONV] * ys.reshape(rows, D_SCONV)
                                          * _silu(rest[:, R_GATE_S:R_GATE_S + D_SCONV]))

    uc = rest[:, R_GA:R_GA + D_CONF] * _sigmoid(rest[:, R_GB:R_GB + D_CONF])
    cbuf[:, CTAIL - (CONF_W - 1):CTAIL, :] = sc_ref[...]
    cbuf[:, CTAIL:, :] = uc.reshape(nb, pad, D_CONF)
    first = CTAIL + ntok - (CONF_W - 1)
    ct_ref[...] = cbuf[:, first:first + CONF_W - 1, :]
    yc = _conformer(cbuf, ccw_ref, ccb_ref, clg_ref, clb_ref, CTAIL, pad, lead=(slice(None),))
    obuf[:, D_DELTA + D_SCONV:] = yc.reshape(rows, D_CONF) * _silu(rest[:, R_GATE_C:R_GATE_C + D_CONF])

    ba = rest[:, R_BA:R_BA + LANES]
    beta_all = jnp.where(valid, _sigmoid(ba), 0.0)
    g_all = jnp.where(valid, -jnp.exp(alog_ref[...]) * _softplus(ba + dt_ref[...]), 0.0)
    gc = _chunk_cumsum(g_all, pad, row)
    gct = gc.T
    gl_all = jnp.broadcast_to(
        gc.reshape(nb, pad, LANES)[:, pad - 1:pad, :], (nb, pad, LANES)).reshape(rows, LANES)
    pk = _Packed(pad, rows, False)

    def conv_act(c0):
        acc = None
        for j in range(QK_CONV):
            term = (qbuf[:, pl.ds(SUBLANES - (QK_CONV - 1) + j, pad), c0:c0 + HEAD_DIM]
                    * cw_ref[j:j + 1, c0:c0 + HEAD_DIM].reshape(1, 1, HEAD_DIM))
            acc = term if acc is None else acc + term
        return jnp.where(valid1, _silu(acc.reshape(rows, HEAD_DIM)), 0.0)

    for pr in range(N_DHEADS // 2):
        h0, h1 = 2 * pr, 2 * pr + 1
        heads, kdecs, egls = [], [], []
        for hd in (h0, h1):
            q = conv_act(hd * HEAD_DIM)
            k = conv_act(D_DELTA + hd * HEAD_DIM)
            v = conv_act(2 * D_DELTA + hd * HEAD_DIM)
            q = q * lax.rsqrt(jnp.sum(q * q, axis=-1, keepdims=True) + EPS) * (HEAD_DIM ** -0.5)
            k = k * lax.rsqrt(jnp.sum(k * k, axis=-1, keepdims=True) + EPS)
            lane = N_DHEADS + hd
            gcol = gc[:, lane:lane + 1]
            heads.append((q, k, v, beta_all[:, hd:hd + 1], gcol, gct[lane:lane + 1, :]))
            gl = gl_all[:, lane:lane + 1]
            kdecs.append(k * jnp.exp(gl - gcol))
            egls.append(jnp.broadcast_to(jnp.exp(gl), (rows, HEAD_DIM)))
        (u0, w0, qg0, qkp0), (u1, w1, qg1, qkp1) = _run(pk.delta_prep(heads, pad // 2))
        (kd0, kd1), (egl0, egl1) = kdecs, egls
        ubuf[...] = jnp.concatenate([u0, u1], axis=1)
        wqbuf[:, 0:pad, :] = jnp.concatenate([w0, w1], axis=1).reshape(nb, pad, 2 * HEAD_DIM)
        wqbuf[:, pad:, :] = jnp.concatenate([qg0, qg1], axis=1).reshape(nb, pad, 2 * HEAD_DIM)
        kdbuf[:, 0:pad, :] = kd0.reshape(nb, pad, HEAD_DIM)
        kdbuf[:, pad:, :] = kd1.reshape(nb, pad, HEAD_DIM)
        glbuf[...] = jnp.concatenate([egl0, egl1], axis=1).reshape(nb, pad, 2 * HEAD_DIM)

        def body(it, carry):
            bis = [it * SAMPLE_SEQ_LOCKSTEP + t for t in range(SAMPLE_SEQ_LOCKSTEP)]
            r0s = [pl.multiple_of(bi * pad, pad) for bi in bis]
            s0s = [sd_ref[bi, h0] for bi in bis]
            s1s = [sd_ref[bi, h1] for bi in bis]
            rs = [_mm(wqbuf[bi], _pair_diag(s0, s1)) for bi, s0, s1 in zip(bis, s0s, s1s)]
            vns = [ubuf[pl.ds(r0, pad), :] - r[:pad] for r0, r in zip(r0s, rs)]
            upds = [_mm_tn(kdbuf[bi], _pair_diag(vn[:, :HEAD_DIM], vn[:, HEAD_DIM:]))
                    for bi, vn in zip(bis, vns)]
            for bi, r0, s0, s1, r, vn, upd in zip(bis, r0s, s0s, s1s, rs, vns, upds):
                ubuf[pl.ds(r0, pad), :] = vn
                oibuf[pl.ds(r0, pad), :] = r[pad:]
                egl = glbuf[bi][0:1, :]
                sd_out_ref[bi, h0] = s0 * egl[:, :HEAD_DIM] + upd[:, :HEAD_DIM]
                sd_out_ref[bi, h1] = s1 * egl[:, HEAD_DIM:] + upd[:, HEAD_DIM:]
            return carry

        lax.fori_loop(0, nb // SAMPLE_SEQ_LOCKSTEP, body, 0)
        for hh, qkp in ((h0, qkp0), (h1, qkp1)):
            ls = slice((hh - h0) * HEAD_DIM, (hh - h0 + 1) * HEAD_DIM)
            o = oibuf[:, ls] + _mm(pk.block_diag(qkp), ubuf[:, ls])
            o = _rms_rows(o, dng_ref[...])
            obuf[:, hh * HEAD_DIM:(hh + 1) * HEAD_DIM] = (
                o * _silu(rest[:, R_GATE_D + hh * HEAD_DIM:R_GATE_D + (hh + 1) * HEAD_DIM]))

    y = x + jnp.dot(obuf[...].astype(BF16), wout_ref[...], preferred_element_type=F32)
    xcarry[blk] = y.reshape(nb, pad, D_MODEL)
    is_last = layer == pl.num_programs(0) - 1
    y_ref[...] = jnp.where(is_last, _rms_rows(y, fg_ref[...]), y).reshape(nb, pad, D_MODEL)[:, 0:ntok, :]


def _sample_layers(x, sd, sq, ss, sc, wts):
    depth = sd.shape[0]
    b, ntok, _ = x.shape
    nb = SAMPLE_BATCH_BLOCK

    def bspec(shape):
        return pl.BlockSpec((nb,) + shape, lambda l, bi: (bi,) + (0,) * len(shape))

    def lspec(shape):
        return pl.BlockSpec((None, nb) + shape, lambda l, bi: (l, bi) + (0,) * len(shape))

    def lshape(shape):
        return jax.ShapeDtypeStruct((depth, b) + shape, F32)

    state_shapes = ((N_DHEADS, HEAD_DIM, HEAD_DIM), (QK_CONV - 1, 3 * D_DELTA),
                    (SCONV_W - 1, D_SCONV), (CONF_W - 1, D_CONF))
    out_shape = (lshape((ntok, D_MODEL)),) + tuple(lshape(s) for s in state_shapes)
    out_specs = (lspec((ntok, D_MODEL)),) + tuple(lspec(s) for s in state_shapes)
    rows = nb * SAMPLE_PAD
    scratch = [
        pltpu.VMEM((nb, 2 * SUBLANES, 3 * D_DELTA), F32),
        pltpu.VMEM((nb, 2 * SUBLANES, D_SCONV), F32),
        pltpu.VMEM((nb, CTAIL + SAMPLE_PAD, D_CONF), F32),
        pltpu.VMEM((rows, D_MODEL), F32),
        pltpu.VMEM((rows, 2 * HEAD_DIM), F32),
        pltpu.VMEM((nb, 2 * SAMPLE_PAD, 2 * HEAD_DIM), F32),
        pltpu.VMEM((nb, 2 * SAMPLE_PAD, HEAD_DIM), F32),
        pltpu.VMEM((nb, SAMPLE_PAD, 2 * HEAD_DIM), F32),
        pltpu.VMEM((rows, 2 * HEAD_DIM), F32),
        pltpu.VMEM((rows, R_WIDTH), F32),
        pltpu.VMEM((b // nb, nb, SAMPLE_PAD, D_MODEL), F32),
    ]
    in_specs = ([bspec((ntok, D_MODEL))] + [lspec(s) for s in state_shapes]
                + _weight_specs(wts, lambda l, bi: l))
    return pl.pallas_call(
        _sample_kernel,
        grid=(depth, b // nb),
        in_specs=in_specs,
        out_specs=out_specs,
        out_shape=out_shape,
        scratch_shapes=scratch,
        compiler_params=pltpu.CompilerParams(
            dimension_semantics=("arbitrary", "arbitrary"),
            vmem_limit_bytes=VMEM_LIMIT),
        name="sample_layers",
    )(x, sd, sq, ss, sc, *wts)


def _reorder_kernel(w_ref, o_ref):
    n_qkv = 3 * D_DELTA
    n_ba = 2 * N_DHEADS
    rows = w_ref.shape[0]
    o_ref[:, 0:n_qkv] = w_ref[:, 0:n_qkv].astype(BF16)
    o_ref[:, n_qkv:n_qkv + R_BA] = w_ref[:, n_qkv + n_ba:].astype(BF16)
    o_ref[:, n_qkv + R_BA:] = jnp.concatenate(
        [w_ref[:, n_qkv:n_qkv + n_ba], jnp.zeros((rows, LANES - n_ba), F32)], axis=1).astype(BF16)


def _reorder_in_proj(w_in):
    depth, d, d_in = w_in.shape
    rows = 256
    return pl.pallas_call(
        _reorder_kernel,
        grid=(depth, d // rows),
        in_specs=[pl.BlockSpec((None, rows, d_in), lambda l, r: (l, r, 0))],
        out_specs=pl.BlockSpec((None, rows, W_REST + R_WIDTH), lambda l, r: (l, r, 0)),
        out_shape=jax.ShapeDtypeStruct((depth, d, W_REST + R_WIDTH), BF16),
        compiler_params=pltpu.CompilerParams(dimension_semantics=("arbitrary", "arbitrary")),
        name="reorder_in_proj",
    )(w_in)


def _stacked_weights(norm_g, w_in, conv_qkv_w, a_log, dt_bias, delta_norm_g, sconv_w,
                     cconv_w, cconv_b, cln_g, cln_b, w_out, final_norm_g):
    depth = w_in.shape[0]
    w_all = _reorder_in_proj(w_in)
    lane_pad = ((0, 0), (N_DHEADS, LANES - 2 * N_DHEADS))
    return (
        norm_g.reshape(depth, 1, D_MODEL),
        w_all,
        conv_qkv_w,
        jnp.pad(a_log, lane_pad).reshape(depth, 1, LANES),
        jnp.pad(dt_bias, lane_pad).reshape(depth, 1, LANES),
        delta_norm_g.reshape(depth, 1, HEAD_DIM),
        sconv_w,
        cconv_w,
        cconv_b.reshape(depth, 1, D_CONF),
        cln_g.reshape(depth, 1, D_CONF),
        cln_b.reshape(depth, 1, D_CONF),
        w_out.astype(BF16),
        final_norm_g.reshape(1, D_MODEL),
    )


def kernel(x_prompt, x_sample, state_delta, state_qkv_conv, state_sconv, state_cconv, norm_g, w_in, conv_qkv_w, a_log, dt_bias, delta_norm_g, sconv_w, cconv_w, cconv_b, cln_g, cln_b, w_out, final_norm_g):
    depth = w_in.shape[0]
    dec_seq = x_sample.shape[1]
    assert x_prompt.shape[1] % TIME_TILE == 0
    assert dec_seq == SAMPLE_PAD // 2 and x_sample.shape[0] % SAMPLE_BATCH_BLOCK == 0

    wts = _stacked_weights(norm_g, w_in, conv_qkv_w, a_log, dt_bias, delta_norm_g, sconv_w,
                           cconv_w, cconv_b, cln_g, cln_b, w_out, final_norm_g)

    xp = x_prompt
    p_outs = [[] for _ in range(4)]
    for l in range(depth):
        xp, pd, pq, ps, pc = _prompt_layer_pipelined(xp, wts, l, l == depth - 1)
        for acc, o in zip(p_outs, (pd, pq, ps, pc)):
            acc.append(o)
    pd, pq, ps, pc = (jnp.stack(o) for o in p_outs)

    xs, sd, sq, ss, sc = _sample_layers(x_sample, state_delta, state_qkv_conv, state_sconv,
                                        state_cconv, wts)
    return (xp, xs[depth - 1],
            pd, pq[:, :, SUBLANES - (QK_CONV - 1):, :], ps[:, :, SUBLANES - (SCONV_W - 1):, :],
            pc[:, :, CTAIL - (CONF_W - 1):, :],
            sd, sq, ss, sc)
```

```python
import functools

import jax
import jax.numpy as jnp
from jax import lax
from jax.experimental import pallas as pl
from jax.experimental.pallas import tpu as pltpu

D_MODEL = 1024
N_DHEADS = 4
HEAD_DIM = 128
D_DELTA = N_DHEADS * HEAD_DIM
D_SCONV = 256
D_CONF = 256
QK_CONV = 4
SCONV_W = 3
CONF_W = 31
CHUNK = 64
EPS = 1e-6

R_GATE_D = 0
R_SB = 512
R_SC = 768
R_SX = 1024
R_GATE_S = 1280
R_GA = 1536
R_GB = 1792
R_GATE_C = 2048
R_BA = 2304
R_WIDTH = 2432
W_REST = 3 * D_DELTA

SUBLANES = 8
LANES = 128
TIME_TILE = 256
SAMPLE_PAD = 8
SAMPLE_BATCH_BLOCK = 16
SAMPLE_SEQ_LOCKSTEP = 4
CTAIL = 32
VMEM_LIMIT = 56 * 1024 * 1024

F32 = jnp.float32
BF16 = jnp.bfloat16


def _mm(a, b):
    return jnp.dot(a.astype(BF16), b.astype(BF16), preferred_element_type=F32)


def _mm_nt(a, b):
    return lax.dot_general(a.astype(BF16), b.astype(BF16), (((1,), (1,)), ((), ())),
                           preferred_element_type=F32)


def _mm_tn(a, b):
    return lax.dot_general(a.astype(BF16), b.astype(BF16), (((0,), (0,)), ((), ())),
                           preferred_element_type=F32)


def _sigmoid(x):
    return 1.0 / (1.0 + jnp.exp(-x))


def _silu(x):
    return x * _sigmoid(x)


def _softplus(x):
    return jnp.maximum(x, 0.0) + jnp.log1p(jnp.exp(-jnp.abs(x)))


def _rms_rows(x, g):
    return x * lax.rsqrt(jnp.mean(x * x, axis=-1, keepdims=True) + EPS) * g


def _chunk_cumsum(g, chunk, row):
    pos = row % chunk
    s = 1
    while s < chunk:
        g = g + jnp.where(pos >= s, pltpu.roll(g, s, axis=0), 0.0)
        s *= 2
    return g


def _run(staged):
    try:
        while True:
            next(staged)
    except StopIteration as stop:
        return stop.value


def _interleave(*staged, steps=None):
    live = list(staged)
    steps = dict(zip(live, steps or [1] * len(live)))
    while live:
        for s in list(live):
            try:
                for _ in range(steps[s]):
                    next(s)
            except StopIteration:
                live.remove(s)


def _pair_diag(a, b):
    z = jnp.zeros_like(a)
    return jnp.concatenate([jnp.concatenate([a, z], axis=1), jnp.concatenate([z, b], axis=1)], axis=0)


class _Packed:
    def __init__(self, chunk, rows, bf16_tile):
        self.chunk, self.rows, self.n = chunk, rows, rows // chunk
        self.rr = lax.broadcasted_iota(jnp.int32, (chunk, rows), 0)
        lane = lax.broadcasted_iota(jnp.int32, (chunk, rows), 1)
        self.jl = lane % chunk
        self.lane_blk = lane // chunk
        ii = lax.broadcasted_iota(jnp.int32, (rows, rows), 0)
        jj = lax.broadcasted_iota(jnp.int32, (rows, rows), 1)
        mask = jnp.where((ii // chunk) == (jj // chunk), 1.0, 0.0).astype(F32)
        self.bd_mask = mask.astype(BF16) if bf16_tile else mask

    def pack(self, g):
        out = g[0:self.chunk]
        for c in range(1, self.n):
            out = jnp.where(self.lane_blk == c, g[c * self.chunk:(c + 1) * self.chunk], out)
        return out

    def col(self, v):
        shape = (self.chunk, self.rows)
        out = jnp.broadcast_to(v[0:self.chunk], shape)
        for c in range(1, self.n):
            out = jnp.where(self.lane_blk == c,
                            jnp.broadcast_to(v[c * self.chunk:(c + 1) * self.chunk], shape), out)
        return out

    def block_diag(self, xp):
        if self.bd_mask.dtype == BF16:
            return jnp.concatenate([xp.astype(BF16)] * self.n, axis=0) * self.bd_mask
        return (jnp.concatenate([xp] * self.n, axis=0) * self.bd_mask).astype(BF16)

    def unit_lower_inverse(self, lps, nil):
        base = min(self.chunk, 16)
        same = (self.rr // base) == (self.jl // base)
        eye = jnp.where(self.rr == self.jl, 1.0, 0.0).astype(F32)
        ds = [jnp.where(same, lp, 0.0) for lp in lps]
        xs = [eye - d for d in ds]
        d_bds = [self.block_diag(d) for d in ds]
        p = 2
        while p < min(base, nil):
            ds = [_mm(d, d_bd) for d, d_bd in zip(ds, d_bds)]
            yield
            d_bds = [self.block_diag(d) for d in ds]
            xs = [x + _mm(x, d_bd) for x, d_bd in zip(xs, d_bds)]
            yield
            p *= 2
        size = base
        while size < self.chunk:
            big = (self.rr // (2 * size)) == (self.jl // (2 * size))
            off = jnp.logical_and(big, jnp.logical_not(same))
            xes = [_mm(x, self.block_diag(jnp.where(off, lp, 0.0))) for x, lp in zip(xs, lps)]
            yield
            xs = [x - _mm(xe, self.block_diag(x)) for x, xe in zip(xs, xes)]
            yield
            same = big
            size *= 2
        return xs

    def delta_prep(self, heads, nil):
        tril = self.rr >= self.jl
        strict = self.rr > self.jl
        decays, kbs, gs = [], [], []
        for q, k, v, beta, gc_col, gc_row in heads:
            diff = self.col(gc_col) - gc_row
            decays.append(jnp.where(tril, jnp.exp(jnp.where(tril, diff, 0.0)), 0.0))
            kbs.append(k * beta)
        for (q, k, *_), kb in zip(heads, kbs):
            gs.append(_mm_nt(jnp.concatenate([kb, q], axis=0), k))
        yield
        lps = [jnp.where(strict, self.pack(g[:self.rows]) * dec, 0.0) for g, dec in zip(gs, decays)]
        qkps = [self.pack(g[self.rows:]) * dec for g, dec in zip(gs, decays)]
        tinvs = yield from self.unit_lower_inverse(lps, nil)
        out = []
        for (q, k, v, beta, gc_col, _), kb, tinv, qkp in zip(heads, kbs, tinvs, qkps):
            eg = jnp.exp(gc_col)
            uw = _mm(self.block_diag(tinv), jnp.concatenate([v * beta, kb * eg], axis=1))
            out.append((uw[:, :HEAD_DIM], uw[:, HEAD_DIM:], q * eg, qkp))
        yield
        return out


def _branch_s(buf_ref, w_ref, tail, rows, lead=()):
    n = len(lead)
    acc = None
    for j in range(SCONV_W):
        idx = lead + (pl.ds(tail - (SCONV_W - 1) + j, rows), slice(None))
        term = buf_ref[idx] * w_ref[j:j + 1, :].reshape((1,) * n + (1, D_SCONV))
        acc = term if acc is None else acc + term
    return acc


def _conformer(ubuf_ref, w_ref, b_ref, g_ref, beta_ref, tail, rows, lead=()):
    n = len(lead)
    shp = (1,) * n + (1, D_CONF)
    acc = None
    for j in range(CONF_W):
        idx = lead + (pl.ds(tail - (CONF_W - 1) + j, rows), slice(None))
        term = ubuf_ref[idx] * w_ref[j:j + 1, :].reshape(shp)
        acc = term if acc is None else acc + term
    yc = acc + b_ref[...].reshape(shp)
    mu = jnp.mean(yc, axis=-1, keepdims=True)
    var = jnp.mean(jnp.square(yc - mu), axis=-1, keepdims=True)
    yc = (yc - mu) * lax.rsqrt(var + EPS) * g_ref[...].reshape(shp) + beta_ref[...].reshape(shp)
    return _silu(yc)


def _prompt_kernel(final_norm,
                   x_ref, ng_ref, wqkv_ref, wrest_ref, cw_ref, alog_ref, dt_ref, dng_ref,
                   sw_ref, ccw_ref, ccb_ref, clg_ref, clb_ref, wout_ref, fg_ref,
                   y_ref, s_out_ref, qt_ref, st_ref, ct_ref,
                   qbuf, sbuf, cbuf, cshift, s_scr, obuf, rest,
                   u_s, lhs_s, kdt_s, egl_s, qk_s):
    i = pl.program_id(1)
    nt = pl.num_programs(1)
    tt = TIME_TILE
    nchunk = tt // CHUNK

    @pl.when(i == 0)
    def _():
        qbuf[0:SUBLANES, :] = jnp.zeros((SUBLANES, 3 * D_DELTA), F32)
        sbuf[0:SUBLANES, :] = jnp.zeros((SUBLANES, D_SCONV), F32)
        cbuf[0:CTAIL, :] = jnp.zeros((CTAIL, D_CONF), F32)
        s_scr[...] = jnp.zeros(s_scr.shape, F32)

    x = x_ref[0]
    h = _rms_rows(x, ng_ref[...]).astype(BF16)
    qbuf[SUBLANES:SUBLANES + tt, :] = jnp.dot(h, wqkv_ref[...], preferred_element_type=F32)
    rest[...] = jnp.dot(h, wrest_ref[...], preferred_element_type=F32)

    sbuf[SUBLANES:SUBLANES + tt, :] = rest[:, R_SC:R_SC + D_SCONV] * rest[:, R_SX:R_SX + D_SCONV]
    ys = _branch_s(sbuf, sw_ref, SUBLANES, tt)
    obuf[:, D_DELTA:D_DELTA + D_SCONV] = (rest[:, R_SB:R_SB + D_SCONV] * ys
                                          * _silu(rest[:, R_GATE_S:R_GATE_S + D_SCONV]))

    cbuf[CTAIL:CTAIL + tt, :] = rest[:, R_GA:R_GA + D_CONF] * _sigmoid(rest[:, R_GB:R_GB + D_CONF])
    for sh in range(1, SUBLANES):
        cshift[sh - 1] = cbuf[pl.ds(sh, tt + CTAIL - SUBLANES), :]

    row = lax.broadcasted_iota(jnp.int32, (tt, LANES), 0)
    ba = rest[:, R_BA:R_BA + LANES]
    beta_all = _sigmoid(ba)
    g_all = -jnp.exp(alog_ref[...]) * _softplus(ba + dt_ref[...])
    gc = _chunk_cumsum(g_all, CHUNK, row)
    gct = gc.T
    gl_rows = jnp.concatenate(
        [jnp.broadcast_to(gc[(c + 1) * CHUNK - 1:(c + 1) * CHUNK, :], (CHUNK, LANES))
         for c in range(nchunk)], axis=0)
    pk = _Packed(CHUNK, tt, True)

    def conv_act(c0):
        acc = None
        for j in range(QK_CONV):
            term = (qbuf[pl.ds(SUBLANES - (QK_CONV - 1) + j, tt), c0:c0 + HEAD_DIM]
                    * cw_ref[j:j + 1, c0:c0 + HEAD_DIM])
            acc = term if acc is None else acc + term
        return _silu(acc)

    heads, kdecs = [], []
    for hd in range(N_DHEADS):
        q = conv_act(hd * HEAD_DIM)
        k = conv_act(D_DELTA + hd * HEAD_DIM)
        v = conv_act(2 * D_DELTA + hd * HEAD_DIM)
        q = q * lax.rsqrt(jnp.sum(q * q, axis=-1, keepdims=True) + EPS) * (HEAD_DIM ** -0.5)
        k = k * lax.rsqrt(jnp.sum(k * k, axis=-1, keepdims=True) + EPS)
        lane = N_DHEADS + hd
        gcol = gc[:, lane:lane + 1]
        heads.append((q, k, v, beta_all[:, hd:hd + 1], gcol, gct[lane:lane + 1, :]))
        kdecs.append(k * jnp.exp(gl_rows[:, lane:lane + 1] - gcol))
    prepped = pk.delta_prep(heads, CHUNK)

    for pr in range(N_DHEADS // 2):
        h0, h1 = 2 * pr, 2 * pr + 1
        (u0, w0, qg0, qkp0), (u1, w1, qg1, qkp1) = prepped[h0], prepped[h1]
        kd0, kd1 = kdecs[h0], kdecs[h1]
        u_s[pr] = jnp.concatenate([u0, u1], axis=1)
        w = jnp.concatenate([w0, w1], axis=1).astype(BF16)
        qg = jnp.concatenate([qg0, qg1], axis=1).astype(BF16)
        kst = jnp.concatenate(
            [kd[c * CHUNK:(c + 1) * CHUNK] for c in range(nchunk) for kd in (kd0, kd1)], axis=0)
        kdt = kst.T.astype(BF16)
        low = (lax.broadcasted_iota(jnp.int32, (CHUNK, tt), 1) % LANES) < CHUNK
        qk_even = jnp.where(low, qkp0, pltpu.roll(qkp1, CHUNK, axis=1)).astype(BF16)
        qk_odd = jnp.where(low, pltpu.roll(qkp0, tt - CHUNK, axis=1), qkp1).astype(BF16)
        for c in range(nchunk):
            cs = slice(c * CHUNK, (c + 1) * CHUNK)
            lhs_s[pr, c, 0:CHUNK, :] = w[cs]
            lhs_s[pr, c, CHUNK:, :] = qg[cs]
            kdt_s[pr, c] = kdt[:, 2 * c * CHUNK:2 * (c + 1) * CHUNK]
            tile = slice((c // 2) * LANES, (c // 2 + 1) * LANES)
            qk_s[pr, c] = (qk_even if c % 2 == 0 else qk_odd)[:, tile]
            ge = (c + 1) * CHUNK - 1
            egl = jnp.concatenate(
                [jnp.broadcast_to(jnp.exp(gc[ge:ge + 1, N_DHEADS + hh:N_DHEADS + hh + 1]),
                                  (SUBLANES, HEAD_DIM)) for hh in (h0, h1)], axis=1)
            egl_s[c, :, 2 * pr * HEAD_DIM:2 * (pr + 1) * HEAD_DIM] = egl

    def chunk_body(c, carry):
        r0 = pl.multiple_of(c * CHUNK, CHUNK)
        rows_c = pl.ds(r0, CHUNK)
        pairs = range(N_DHEADS // 2)
        ss = [s_scr[pr] for pr in pairs]
        rs = [jnp.dot(lhs_s[pr, c], _pair_diag(s[:, :HEAD_DIM], s[:, HEAD_DIM:]).astype(BF16),
                      preferred_element_type=F32) for pr, s in zip(pairs, ss)]
        vns = [u_s[pr, rows_c, :] - r[:CHUNK] for pr, r in zip(pairs, rs)]
        v_bds = [_pair_diag(vn[:, :HEAD_DIM], vn[:, HEAD_DIM:]).astype(BF16) for vn in vns]
        upds = [jnp.dot(kdt_s[pr, c], v_bd, preferred_element_type=F32) for pr, v_bd in zip(pairs, v_bds)]
        o_intra = [jnp.dot(qk_s[pr, c], v_bd, preferred_element_type=F32) for pr, v_bd in zip(pairs, v_bds)]
        for pr in pairs:
            egl = egl_s[c][0:1, 2 * pr * HEAD_DIM:2 * (pr + 1) * HEAD_DIM]
            s_scr[pr] = ss[pr] * egl + upds[pr]
            o_pair = rs[pr][CHUNK:] + o_intra[pr]
            for hl in range(2):
                hd = 2 * pr + hl
                o = _rms_rows(o_pair[:, hl * HEAD_DIM:(hl + 1) * HEAD_DIM], dng_ref[...])
                gate = rest[rows_c, R_GATE_D + hd * HEAD_DIM:R_GATE_D + (hd + 1) * HEAD_DIM]
                obuf[rows_c, hd * HEAD_DIM:(hd + 1) * HEAD_DIM] = o * _silu(gate)
        acc = None
        for j in range(CONF_W):
            off = CTAIL - (CONF_W - 1) + j
            start = pl.multiple_of(r0 + (off // SUBLANES) * SUBLANES, SUBLANES)
            sh = off % SUBLANES
            src = cbuf[pl.ds(start, CHUNK), :] if sh == 0 else cshift[sh - 1, pl.ds(start, CHUNK), :]
            term = src * ccw_ref[j:j + 1, :]
            acc = term if acc is None else acc + term
        yc = acc + ccb_ref[...]
        mu = jnp.mean(yc, axis=-1, keepdims=True)
        var = jnp.mean(jnp.square(yc - mu), axis=-1, keepdims=True)
        yc = (yc - mu) * lax.rsqrt(var + EPS) * clg_ref[...] + clb_ref[...]
        obuf[rows_c, D_DELTA + D_SCONV:] = _silu(yc) * _silu(rest[rows_c, R_GATE_C:R_GATE_C + D_CONF])
        return carry

    lax.fori_loop(0, nchunk, chunk_body, 0)

    y = x_ref[0] + jnp.dot(obuf[...].astype(BF16), wout_ref[...], preferred_element_type=F32)
    if final_norm:
        y = _rms_rows(y, fg_ref[...])
    y_ref[0] = y

    @pl.when(i == nt - 1)
    def _():
        for pr in range(N_DHEADS // 2):
            s_out_ref[0, 2 * pr] = s_scr[pr, :, :HEAD_DIM]
            s_out_ref[0, 2 * pr + 1] = s_scr[pr, :, HEAD_DIM:]
        qt_ref[0] = qbuf[tt:tt + SUBLANES, :]
        st_ref[0] = sbuf[tt:tt + SUBLANES, :]
        ct_ref[0] = cbuf[tt:tt + CTAIL, :]

    qbuf[0:SUBLANES, :] = qbuf[tt:tt + SUBLANES, :]
    sbuf[0:SUBLANES, :] = sbuf[tt:tt + SUBLANES, :]
    cbuf[0:CTAIL, :] = cbuf[tt:tt + CTAIL, :]


def _weight_specs(wts, layer_of, buffers=2):
    specs = []
    for w in wts[:-1]:
        tail = (0,) * (w.ndim - 1)
        specs.append(pl.BlockSpec((None,) + w.shape[1:], lambda *ids, tail=tail: (layer_of(*ids),) + tail,
                                  pipeline_mode=pl.Buffered(buffers)))
    specs.append(pl.BlockSpec(wts[-1].shape, lambda *ids: (0,) * wts[-1].ndim))
    return specs


def _prompt_layer(x, wts, layer, final_norm):
    b, t, d = x.shape
    nt = t // TIME_TILE
    npair = N_DHEADS // 2
    nchunk = TIME_TILE // CHUNK
    weight_specs = _weight_specs(wts, lambda *ids: layer)
    out_shape = (
        jax.ShapeDtypeStruct((b, t, d), F32),
        jax.ShapeDtypeStruct((b, N_DHEADS, HEAD_DIM, HEAD_DIM), F32),
        jax.ShapeDtypeStruct((b, SUBLANES, 3 * D_DELTA), F32),
        jax.ShapeDtypeStruct((b, SUBLANES, D_SCONV), F32),
        jax.ShapeDtypeStruct((b, CTAIL, D_CONF), F32),
    )
    out_specs = (
        pl.BlockSpec((1, TIME_TILE, d), lambda bi, ti: (bi, ti, 0)),
        pl.BlockSpec((1, N_DHEADS, HEAD_DIM, HEAD_DIM), lambda bi, ti: (bi, 0, 0, 0)),
        pl.BlockSpec((1, SUBLANES, 3 * D_DELTA), lambda bi, ti: (bi, 0, 0)),
        pl.BlockSpec((1, SUBLANES, D_SCONV), lambda bi, ti: (bi, 0, 0)),
        pl.BlockSpec((1, CTAIL, D_CONF), lambda bi, ti: (bi, 0, 0)),
    )
    scratch = [
        pltpu.VMEM((TIME_TILE + SUBLANES, 3 * D_DELTA), F32),
        pltpu.VMEM((TIME_TILE + SUBLANES, D_SCONV), F32),
        pltpu.VMEM((TIME_TILE + CTAIL, D_CONF), F32),
        pltpu.VMEM((SUBLANES - 1, TIME_TILE + CTAIL - SUBLANES, D_CONF), F32),
        pltpu.VMEM((npair, HEAD_DIM, 2 * HEAD_DIM), F32),
        pltpu.VMEM((TIME_TILE, D_MODEL), F32),
        pltpu.VMEM((TIME_TILE, R_WIDTH), F32),
        pltpu.VMEM((npair, TIME_TILE, 2 * HEAD_DIM), F32),
        pltpu.VMEM((npair, nchunk, 2 * CHUNK, 2 * HEAD_DIM), BF16),
        pltpu.VMEM((npair, nchunk, HEAD_DIM, 2 * CHUNK), BF16),
        pltpu.VMEM((nchunk, SUBLANES, N_DHEADS * HEAD_DIM), F32),
        pltpu.VMEM((npair, nchunk, CHUNK, 2 * CHUNK), BF16),
    ]
    return pl.pallas_call(
        functools.partial(_prompt_kernel, final_norm),
        grid=(b, nt),
        in_specs=[pl.BlockSpec((1, TIME_TILE, d), lambda bi, ti: (bi, ti, 0))] + weight_specs,
        out_specs=out_specs,
        out_shape=out_shape,
        scratch_shapes=scratch,
        compiler_params=pltpu.CompilerParams(
            dimension_semantics=("arbitrary", "arbitrary"),
            vmem_limit_bytes=VMEM_LIMIT),
        name="prompt_layer",
    )(x, *wts)


def _pipe_kernel(final_norm, nt,
                 x_ref, ng_ref, win_ref, cw_ref, alog_ref, dt_ref, dng_ref,
                 sw_ref, ccw_ref, ccb_ref, clg_ref, clb_ref, wout_ref, fg_ref,
                 y_ref, s_out_ref, qt_ref, st_ref, ct_ref,
                 qbuf, sbuf, s_scr, xs, qa, bg, rest, cbuf, cshift, obuf):
    g = pl.program_id(0)
    n_tiles = pl.num_programs(0) - 1
    tt = TIME_TILE
    nchunk = tt // CHUNK
    npair = N_DHEADS // 2
    t1 = jnp.minimum(g, n_tiles - 1) % nt
    t2 = jnp.maximum(g - 1, 0) % nt
    p = g % 2
    q = 1 - p

    @pl.when(g == 0)
    def _():
        xs[1] = jnp.zeros(xs.shape[1:], F32)
        qa[1] = jnp.zeros(qa.shape[1:], F32)
        bg[1] = jnp.zeros(bg.shape[1:], F32)
        rest[1] = jnp.zeros(rest.shape[1:], F32)
        cbuf[1] = jnp.zeros(cbuf.shape[1:], F32)
        cshift[1] = jnp.zeros(cshift.shape[1:], F32)
        obuf[1] = jnp.zeros(obuf.shape[1:], F32)

    @pl.when(t1 == 0)
    def _():
        qbuf[0:SUBLANES, :] = jnp.zeros((SUBLANES, 3 * D_DELTA), F32)
        sbuf[0:SUBLANES, :] = jnp.zeros((SUBLANES, D_SCONV), F32)

    @pl.when(t2 == 0)
    def _():
        s_scr[...] = jnp.zeros(s_scr.shape, F32)

    def stage1():
        x = x_ref[0]
        xs[p] = x
        h = _rms_rows(x, ng_ref[...]).astype(BF16)
        yield
        for c0 in range(0, 3 * D_DELTA, 2 * LANES):
            qbuf[SUBLANES:SUBLANES + tt, c0:c0 + 2 * LANES] = jnp.dot(
                h, win_ref[:, c0:c0 + 2 * LANES], preferred_element_type=F32)
            yield
        for c0 in range(0, R_WIDTH, 2 * LANES):
            c1 = min(c0 + 2 * LANES, R_WIDTH)
            rest[p, :, c0:c1] = jnp.dot(h, win_ref[:, W_REST + c0:W_REST + c1],
                                        preferred_element_type=F32)
            yield

        def conv_act(c0):
            acc = None
            for j in range(QK_CONV):
                term = (qbuf[pl.ds(SUBLANES - (QK_CONV - 1) + j, tt), c0:c0 + HEAD_DIM]
                        * cw_ref[j:j + 1, c0:c0 + HEAD_DIM])
                acc = term if acc is None else acc + term
            return _silu(acc)

        for hd in range(N_DHEADS):
            qh = conv_act(hd * HEAD_DIM)
            qa[p, :, hd * HEAD_DIM:(hd + 1) * HEAD_DIM] = (
                qh * lax.rsqrt(jnp.sum(qh * qh, axis=-1, keepdims=True) + EPS) * (HEAD_DIM ** -0.5))
            kh = conv_act(D_DELTA + hd * HEAD_DIM)
            qa[p, :, D_DELTA + hd * HEAD_DIM:D_DELTA + (hd + 1) * HEAD_DIM] = (
                kh * lax.rsqrt(jnp.sum(kh * kh, axis=-1, keepdims=True) + EPS))
            qa[p, :, 2 * D_DELTA + hd * HEAD_DIM:2 * D_DELTA + (hd + 1) * HEAD_DIM] = conv_act(
                2 * D_DELTA + hd * HEAD_DIM)
            yield

        row = lax.broadcasted_iota(jnp.int32, (tt, LANES), 0)
        ba = rest[p, :, R_BA:R_BA + LANES]
        bg[p, 0] = _sigmoid(ba)
        bg[p, 1] = _chunk_cumsum(-jnp.exp(alog_ref[...]) * _softplus(ba + dt_ref[...]), CHUNK, row)
        yield

        sbuf[SUBLANES:SUBLANES + tt, :] = (rest[p, :, R_SC:R_SC + D_SCONV]
                                           * rest[p, :, R_SX:R_SX + D_SCONV])
        obuf[p, :, D_DELTA:D_DELTA + D_SCONV] = (
            rest[p, :, R_SB:R_SB + D_SCONV] * _branch_s(sbuf, sw_ref, SUBLANES, tt)
            * _silu(rest[p, :, R_GATE_S:R_GATE_S + D_SCONV]))
        yield

        prev_tail = jnp.where(t1 == 0, 0.0, cbuf[q, tt:tt + CTAIL, :])
        cbuf[p, 0:CTAIL, :] = prev_tail
        cbuf[p, CTAIL:CTAIL + tt, :] = (rest[p, :, R_GA:R_GA + D_CONF]
                                        * _sigmoid(rest[p, :, R_GB:R_GB + D_CONF]))
        yield
        for sh in range(1, SUBLANES):
            cshift[p, sh - 1] = cbuf[p, pl.ds(sh, tt + CTAIL - SUBLANES), :]
        yield

        qbuf[0:SUBLANES, :] = qbuf[tt:tt + SUBLANES, :]
        sbuf[0:SUBLANES, :] = sbuf[tt:tt + SUBLANES, :]

    def stage2():
        beta_all = bg[q, 0]
        gc = bg[q, 1]
        gct = gc.T
        gl_rows = jnp.concatenate(
            [jnp.broadcast_to(gc[(c + 1) * CHUNK - 1:(c + 1) * CHUNK, :], (CHUNK, LANES))
             for c in range(nchunk)], axis=0)
        pk = _Packed(CHUNK, tt, True)
        heads, kdecs = [], []
        for hd in range(N_DHEADS):
            lane = N_DHEADS + hd
            gcol = gc[:, lane:lane + 1]
            kh = qa[q, :, D_DELTA + hd * HEAD_DIM:D_DELTA + (hd + 1) * HEAD_DIM]
            heads.append((qa[q, :, hd * HEAD_DIM:(hd + 1) * HEAD_DIM], kh,
                          qa[q, :, 2 * D_DELTA + hd * HEAD_DIM:2 * D_DELTA + (hd + 1) * HEAD_DIM],
                          beta_all[:, hd:hd + 1], gcol, gct[lane:lane + 1, :]))
            kdecs.append(kh * jnp.exp(gl_rows[:, lane:lane + 1] - gcol))
        yield
        prepped = yield from pk.delta_prep(heads, CHUNK)

        us, lhss, kdts, qks = [], [], [], []
        low = (lax.broadcasted_iota(jnp.int32, (CHUNK, tt), 1) % LANES) < CHUNK
        for pr in range(npair):
            (u0, w0, qg0, qkp0), (u1, w1, qg1, qkp1) = prepped[2 * pr], prepped[2 * pr + 1]
            us.append(jnp.concatenate([u0, u1], axis=1))
            w = jnp.concatenate([w0, w1], axis=1).astype(BF16)
            qg = jnp.concatenate([qg0, qg1], axis=1).astype(BF16)
            lhss.append([jnp.concatenate([w[c * CHUNK:(c + 1) * CHUNK], qg[c * CHUNK:(c + 1) * CHUNK]],
                                         axis=0) for c in range(nchunk)])
            kst = jnp.concatenate([kd[c * CHUNK:(c + 1) * CHUNK] for c in range(nchunk)
                                   for kd in (kdecs[2 * pr], kdecs[2 * pr + 1])], axis=0)
            kdts.append(kst.T.astype(BF16))
            qks.append((jnp.where(low, qkp0, pltpu.roll(qkp1, CHUNK, axis=1)).astype(BF16),
                        jnp.where(low, pltpu.roll(qkp0, tt - CHUNK, axis=1), qkp1).astype(BF16)))
        yield

        ss = [s_scr[pr] for pr in range(npair)]
        for c in range(nchunk):
            cs = slice(c * CHUNK, (c + 1) * CHUNK)
            rs = [jnp.dot(lhss[pr][c], _pair_diag(s[:, :HEAD_DIM], s[:, HEAD_DIM:]).astype(BF16),
                          preferred_element_type=F32) for pr, s in enumerate(ss)]
            yield
            vns = [us[pr][cs] - r[:CHUNK] for pr, r in enumerate(rs)]
            v_bds = [_pair_diag(vn[:, :HEAD_DIM], vn[:, HEAD_DIM:]).astype(BF16) for vn in vns]
            upds = [jnp.dot(kdts[pr][:, 2 * c * CHUNK:2 * (c + 1) * CHUNK], v_bd,
                            preferred_element_type=F32) for pr, v_bd in enumerate(v_bds)]
            tile = slice((c // 2) * LANES, (c // 2 + 1) * LANES)
            o_intra = [jnp.dot(qks[pr][c % 2][:, tile], v_bd, preferred_element_type=F32)
                       for pr, v_bd in enumerate(v_bds)]
            ge = (c + 1) * CHUNK - 1
            for pr in range(npair):
                egl = jnp.concatenate(
                    [jnp.broadcast_to(jnp.exp(gc[ge:ge + 1, N_DHEADS + hh:N_DHEADS + hh + 1]),
                                      (1, HEAD_DIM)) for hh in (2 * pr, 2 * pr + 1)], axis=1)
                ss[pr] = ss[pr] * egl + upds[pr]
                o_pair = rs[pr][CHUNK:] + o_intra[pr]
                for hl in range(2):
                    hd = 2 * pr + hl
                    o = _rms_rows(o_pair[:, hl * HEAD_DIM:(hl + 1) * HEAD_DIM], dng_ref[...])
                    gate = rest[q, cs, R_GATE_D + hd * HEAD_DIM:R_GATE_D + (hd + 1) * HEAD_DIM]
                    obuf[q, cs, hd * HEAD_DIM:(hd + 1) * HEAD_DIM] = o * _silu(gate)
            yield
            acc = None
            for j in range(CONF_W):
                off = CTAIL - (CONF_W - 1) + j
                start = c * CHUNK + (off // SUBLANES) * SUBLANES
                sh = off % SUBLANES
                src = (cbuf[q, start:start + CHUNK, :] if sh == 0
                       else cshift[q, sh - 1, start:start + CHUNK, :])
                term = src * ccw_ref[j:j + 1, :]
                acc = term if acc is None else acc + term
            yc = acc + ccb_ref[...]
            mu = jnp.mean(yc, axis=-1, keepdims=True)
            var = jnp.mean(jnp.square(yc - mu), axis=-1, keepdims=True)
            yc = (yc - mu) * lax.rsqrt(var + EPS) * clg_ref[...] + clb_ref[...]
            obuf[q, cs, D_DELTA + D_SCONV:] = _silu(yc) * _silu(rest[q, cs, R_GATE_C:R_GATE_C + D_CONF])
            yield
        for pr in range(npair):
            s_scr[pr] = ss[pr]

        y = xs[q] + jnp.dot(obuf[q].astype(BF16), wout_ref[...], preferred_element_type=F32)
        if final_norm:
            y = _rms_rows(y, fg_ref[...])
        y_ref[0] = y

    _interleave(stage1(), stage2(), steps=(2, 1))

    @pl.when(jnp.logical_and(g >= 1, t2 == nt - 1))
    def _():
        for pr in range(npair):
            s_out_ref[0, 2 * pr] = s_scr[pr, :, :HEAD_DIM]
            s_out_ref[0, 2 * pr + 1] = s_scr[pr, :, HEAD_DIM:]

    @pl.when(jnp.logical_and(g < n_tiles, t1 == nt - 1))
    def _():
        qt_ref[0] = qbuf[0:SUBLANES, :]
        st_ref[0] = sbuf[0:SUBLANES, :]
        ct_ref[0] = cbuf[p, tt:tt + CTAIL, :]


def _prompt_layer_pipelined(x, wts, layer, final_norm):
    b, t, d = x.shape
    nt = t // TIME_TILE
    n_tiles = b * nt
    npair = N_DHEADS // 2

    def tile1(g):
        g1 = jnp.minimum(g, n_tiles - 1)
        return g1 // nt, g1 % nt

    def tile2(g):
        g2 = jnp.maximum(g - 1, 0)
        return g2 // nt, g2 % nt

    out_shape = (
        jax.ShapeDtypeStruct((b, t, d), F32),
        jax.ShapeDtypeStruct((b, N_DHEADS, HEAD_DIM, HEAD_DIM), F32),
        jax.ShapeDtypeStruct((b, SUBLANES, 3 * D_DELTA), F32),
        jax.ShapeDtypeStruct((b, SUBLANES, D_SCONV), F32),
        jax.ShapeDtypeStruct((b, CTAIL, D_CONF), F32),
    )
    out_specs = (
        pl.BlockSpec((1, TIME_TILE, d), lambda g: tile2(g) + (0,)),
        pl.BlockSpec((1, N_DHEADS, HEAD_DIM, HEAD_DIM), lambda g: (tile2(g)[0], 0, 0, 0)),
        pl.BlockSpec((1, SUBLANES, 3 * D_DELTA), lambda g: (tile1(g)[0], 0, 0)),
        pl.BlockSpec((1, SUBLANES, D_SCONV), lambda g: (tile1(g)[0], 0, 0)),
        pl.BlockSpec((1, CTAIL, D_CONF), lambda g: (tile1(g)[0], 0, 0)),
    )
    scratch = [
        pltpu.VMEM((TIME_TILE + SUBLANES, 3 * D_DELTA), F32),
        pltpu.VMEM((TIME_TILE + SUBLANES, D_SCONV), F32),
        pltpu.VMEM((npair, HEAD_DIM, 2 * HEAD_DIM), F32),
        pltpu.VMEM((2, TIME_TILE, D_MODEL), F32),
        pltpu.VMEM((2, TIME_TILE, 3 * D_DELTA), F32),
        pltpu.VMEM((2, 2, TIME_TILE, LANES), F32),
        pltpu.VMEM((2, TIME_TILE, R_WIDTH), F32),
        pltpu.VMEM((2, TIME_TILE + CTAIL, D_CONF), F32),
        pltpu.VMEM((2, SUBLANES - 1, TIME_TILE + CTAIL - SUBLANES, D_CONF), F32),
        pltpu.VMEM((2, TIME_TILE, D_MODEL), F32),
    ]
    return pl.pallas_call(
        functools.partial(_pipe_kernel, final_norm, nt),
        grid=(n_tiles + 1,),
        in_specs=[pl.BlockSpec((1, TIME_TILE, d), lambda g: tile1(g) + (0,))]
        + _weight_specs(wts, lambda g: layer, buffers=1),
        out_specs=out_specs,
        out_shape=out_shape,
        scratch_shapes=scratch,
        compiler_params=pltpu.CompilerParams(
            dimension_semantics=("arbitrary",),
            vmem_limit_bytes=VMEM_LIMIT),
        name="prompt_layer",
    )(x, *wts)


def _sample_kernel(x_ref, sd_ref, sq_ref, ss_ref, sc_ref,
                   ng_ref, win_ref, cw_ref, alog_ref, dt_ref, dng_ref,
                   sw_ref, ccw_ref, ccb_ref, clg_ref, clb_ref, wout_ref, fg_ref,
                   y_ref, sd_out_ref, qt_ref, st_ref, ct_ref,
                   qbuf, sbuf, cbuf, obuf, ubuf, wqbuf, kdbuf, glbuf, oibuf, rest, xcarry):
    layer = pl.program_id(0)
    blk = pl.program_id(1)
    nb = SAMPLE_BATCH_BLOCK
    pad = SAMPLE_PAD
    ntok = pad // 2
    rows = nb * pad

    @pl.when(layer == 0)
    def _():
        xcarry[blk, :, 0:ntok, :] = x_ref[...]
        xcarry[blk, :, ntok:, :] = jnp.zeros((nb, pad - ntok, D_MODEL), F32)

    x = xcarry[blk].reshape(rows, D_MODEL)
    h = _rms_rows(x, ng_ref[...]).astype(BF16)
    qkv = jnp.dot(h, win_ref[:, :W_REST], preferred_element_type=F32)
    rest[...] = jnp.dot(h, win_ref[:, W_REST:], preferred_element_type=F32)
    row = lax.broadcasted_iota(jnp.int32, (rows, LANES), 0)
    valid = (row % pad) < (pad // 2)
    valid1 = valid[:, 0:1]

    qkv3 = qkv.reshape(nb, pad, 3 * D_DELTA)
    qt_ref[...] = qkv3[:, ntok - (QK_CONV - 1):ntok, :]
    qbuf[:, SUBLANES - (QK_CONV - 1):SUBLANES, :] = sq_ref[...]
    qbuf[:, SUBLANES:, :] = qkv3

    hs = (rest[:, R_SC:R_SC + D_SCONV] * rest[:, R_SX:R_SX + D_SCONV]).reshape(nb, pad, D_SCONV)
    st_ref[...] = hs[:, ntok - (SCONV_W - 1):ntok, :]
    sbuf[:, SUBLANES - (SCONV_W - 1):SUBLANES, :] = ss_ref[...]
    sbuf[:, SUBLANES:, :] = hs
    ys = _branch_s(sbuf, sw_ref, SUBLANES, pad, lead=(slice(None),))
    obuf[:, D_DELTA:D_DELTA + D_SCONV] = (rest[:, R_SB:R_SB + D_SCONV] * ys.reshape(rows, D_SCONV)
                                          * _silu(rest[:, R_GATE_S:R_GATE_S + D_SCONV]))

    uc = rest[:, R_GA:R_GA + D_CONF] * _sigmoid(rest[:, R_GB:R_GB + D_CONF])
    cbuf[:, CTAIL - (CONF_W - 1):CTAIL, :] = sc_ref[...]
    cbuf[:, CTAIL:, :] = uc.reshape(nb, pad, D_CONF)
    first = CTAIL + ntok - (CONF_W - 1)
    ct_ref[...] = cbuf[:, first:first + CONF_W - 1, :]
    yc = _conformer(cbuf, ccw_ref, ccb_ref, clg_ref, clb_ref, CTAIL, pad, lead=(slice(None),))
    obuf[:, D_DELTA + D_SCONV:] = yc.reshape(rows, D_CONF) * _silu(rest[:, R_GATE_C:R_GATE_C + D_CONF])

    ba = rest[:, R_BA:R_BA + LANES]
    beta_all = jnp.where(valid, _sigmoid(ba), 0.0)
    g_all = jnp.where(valid, -jnp.exp(alog_ref[...]) * _softplus(ba + dt_ref[...]), 0.0)
    gc = _chunk_cumsum(g_all, pad, row)
    gct = gc.T
    gl_all = jnp.broadcast_to(
        gc.reshape(nb, pad, LANES)[:, pad - 1:pad, :], (nb, pad, LANES)).reshape(rows, LANES)
    pk = _Packed(pad, rows, False)

    def conv_act(c0):
        acc = None
        for j in range(QK_CONV):
            term = (qbuf[:, pl.ds(SUBLANES - (QK_CONV - 1) + j, pad), c0:c0 + HEAD_DIM]
                    * cw_ref[j:j + 1, c0:c0 + HEAD_DIM].reshape(1, 1, HEAD_DIM))
            acc = term if acc is None else acc + term
        return jnp.where(valid1, _silu(acc.reshape(rows, HEAD_DIM)), 0.0)

    for pr in range(N_DHEADS // 2):
        h0, h1 = 2 * pr, 2 * pr + 1
        heads, kdecs, egls = [], [], []
        for hd in (h0, h1):
            q = conv_act(hd * HEAD_DIM)
            k = conv_act(D_DELTA + hd * HEAD_DIM)
            v = conv_act(2 * D_DELTA + hd * HEAD_DIM)
            q = q * lax.rsqrt(jnp.sum(q * q, axis=-1, keepdims=True) + EPS) * (HEAD_DIM ** -0.5)
            k = k * lax.rsqrt(jnp.sum(k * k, axis=-1, keepdims=True) + EPS)
            lane = N_DHEADS + hd
            gcol = gc[:, lane:lane + 1]
            heads.append((q, k, v, beta_all[:, hd:hd + 1], gcol, gct[lane:lane + 1, :]))
            gl = gl_all[:, lane:lane + 1]
            kdecs.append(k * jnp.exp(gl - gcol))
            egls.append(jnp.broadcast_to(jnp.exp(gl), (rows, HEAD_DIM)))
        (u0, w0, qg0, qkp0), (u1, w1, qg1, qkp1) = _run(pk.delta_prep(heads, pad // 2))
        (kd0, kd1), (egl0, egl1) = kdecs, egls
        ubuf[...] = jnp.concatenate([u0, u1], axis=1)
        wqbuf[:, 0:pad, :] = jnp.concatenate([w0, w1], axis=1).reshape(nb, pad, 2 * HEAD_DIM)
        wqbuf[:, pad:, :] = jnp.concatenate([qg0, qg1], axis=1).reshape(nb, pad, 2 * HEAD_DIM)
        kdbuf[:, 0:pad, :] = kd0.reshape(nb, pad, HEAD_DIM)
        kdbuf[:, pad:, :] = kd1.reshape(nb, pad, HEAD_DIM)
        glbuf[...] = jnp.concatenate([egl0, egl1], axis=1).reshape(nb, pad, 2 * HEAD_DIM)

        def body(it, carry):
            bis = [it * SAMPLE_SEQ_LOCKSTEP + t for t in range(SAMPLE_SEQ_LOCKSTEP)]
            r0s = [pl.multiple_of(bi * pad, pad) for bi in bis]
            s0s = [sd_ref[bi, h0] for bi in bis]
            s1s = [sd_ref[bi, h1] for bi in bis]
            rs = [_mm(wqbuf[bi], _pair_diag(s0, s1)) for bi, s0, s1 in zip(bis, s0s, s1s)]
            vns = [ubuf[pl.ds(r0, pad), :] - r[:pad] for r0, r in zip(r0s, rs)]
            upds = [_mm_tn(kdbuf[bi], _pair_diag(vn[:, :HEAD_DIM], vn[:, HEAD_DIM:]))
                    for bi, vn in zip(bis, vns)]
            for bi, r0, s0, s1, r, vn, upd in zip(bis, r0s, s0s, s1s, rs, vns, upds):
                ubuf[pl.ds(r0, pad), :] = vn
                oibuf[pl.ds(r0, pad), :] = r[pad:]
                egl = glbuf[bi][0:1, :]
                sd_out_ref[bi, h0] = s0 * egl[:, :HEAD_DIM] + upd[:, :HEAD_DIM]
                sd_out_ref[bi, h1] = s1 * egl[:, HEAD_DIM:] + upd[:, HEAD_DIM:]
            return carry

        lax.fori_loop(0, nb // SAMPLE_SEQ_LOCKSTEP, body, 0)
        for hh, qkp in ((h0, qkp0), (h1, qkp1)):
            ls = slice((hh - h0) * HEAD_DIM, (hh - h0 + 1) * HEAD_DIM)
            o = oibuf[:, ls] + _mm(pk.block_diag(qkp), ubuf[:, ls])
            o = _rms_rows(o, dng_ref[...])
            obuf[:, hh * HEAD_DIM:(hh + 1) * HEAD_DIM] = (
                o * _silu(rest[:, R_GATE_D + hh * HEAD_DIM:R_GATE_D + (hh + 1) * HEAD_DIM]))

    y = x + jnp.dot(obuf[...].astype(BF16), wout_ref[...], preferred_element_type=F32)
    xcarry[blk] = y.reshape(nb, pad, D_MODEL)
    is_last = layer == pl.num_programs(0) - 1
    y_ref[...] = jnp.where(is_last, _rms_rows(y, fg_ref[...]), y).reshape(nb, pad, D_MODEL)[:, 0:ntok, :]


def _sample_layers(x, sd, sq, ss, sc, wts):
    depth = sd.shape[0]
    b, ntok, _ = x.shape
    nb = SAMPLE_BATCH_BLOCK

    def bspec(shape):
        return pl.BlockSpec((nb,) + shape, lambda l, bi: (bi,) + (0,) * len(shape))

    def lspec(shape):
        return pl.BlockSpec((None, nb) + shape, lambda l, bi: (l, bi) + (0,) * len(shape))

    def lshape(shape):
        return jax.ShapeDtypeStruct((depth, b) + shape, F32)

    state_shapes = ((N_DHEADS, HEAD_DIM, HEAD_DIM), (QK_CONV - 1, 3 * D_DELTA),
                    (SCONV_W - 1, D_SCONV), (CONF_W - 1, D_CONF))
    out_shape = (lshape((ntok, D_MODEL)),) + tuple(lshape(s) for s in state_shapes)
    out_specs = (lspec((ntok, D_MODEL)),) + tuple(lspec(s) for s in state_shapes)
    rows = nb * SAMPLE_PAD
    scratch = [
        pltpu.VMEM((nb, 2 * SUBLANES, 3 * D_DELTA), F32),
        pltpu.VMEM((nb, 2 * SUBLANES, D_SCONV), F32),
        pltpu.VMEM((nb, CTAIL + SAMPLE_PAD, D_CONF), F32),
        pltpu.VMEM((rows, D_MODEL), F32),
        pltpu.VMEM((rows, 2 * HEAD_DIM), F32),
        pltpu.VMEM((nb, 2 * SAMPLE_PAD, 2 * HEAD_DIM), F32),
        pltpu.VMEM((nb, 2 * SAMPLE_PAD, HEAD_DIM), F32),
        pltpu.VMEM((nb, SAMPLE_PAD, 2 * HEAD_DIM), F32),
        pltpu.VMEM((rows, 2 * HEAD_DIM), F32),
        pltpu.VMEM((rows, R_WIDTH), F32),
        pltpu.VMEM((b // nb, nb, SAMPLE_PAD, D_MODEL), F32),
    ]
    in_specs = ([bspec((ntok, D_MODEL))] + [lspec(s) for s in state_shapes]
                + _weight_specs(wts, lambda l, bi: l))
    return pl.pallas_call(
        _sample_kernel,
        grid=(depth, b // nb),
        in_specs=in_specs,
        out_specs=out_specs,
        out_shape=out_shape,
        scratch_shapes=scratch,
        compiler_params=pltpu.CompilerParams(
            dimension_semantics=("arbitrary", "arbitrary"),
            vmem_limit_bytes=VMEM_LIMIT),
        name="sample_layers",
    )(x, sd, sq, ss, sc, *wts)


def _reorder_kernel(w_ref, o_ref):
    n_qkv = 3 * D_DELTA
    n_ba = 2 * N_DHEADS
    rows = w_ref.shape[0]
    o_ref[:, 0:n_qkv] = w_ref[:, 0:n_qkv].astype(BF16)
    o_ref[:, n_qkv:n_qkv + R_BA] = w_ref[:, n_qkv + n_ba:].astype(BF16)
    o_ref[:, n_qkv + R_BA:] = jnp.concatenate(
        [w_ref[:, n_qkv:n_qkv + n_ba], jnp.zeros((rows, LANES - n_ba), F32)], axis=1).astype(BF16)


def _reorder_in_proj(w_in):
    depth, d, d_in = w_in.shape
    rows = 256
    return pl.pallas_call(
        _reorder_kernel,
        grid=(depth, d // rows),
        in_specs=[pl.BlockSpec((None, rows, d_in), lambda l, r: (l, r, 0))],
        out_specs=pl.BlockSpec((None, rows, W_REST + R_WIDTH), lambda l, r: (l, r, 0)),
        out_shape=jax.ShapeDtypeStruct((depth, d, W_REST + R_WIDTH), BF16),
        compiler_params=pltpu.CompilerParams(dimension_semantics=("arbitrary", "arbitrary")),
        name="reorder_in_proj",
    )(w_in)


def _stacked_weights(norm_g, w_in, conv_qkv_w, a_log, dt_bias, delta_norm_g, sconv_w,
                     cconv_w, cconv_b, cln_g, cln_b, w_out, final_norm_g):
    depth = w_in.shape[0]
    w_all = _reorder_in_proj(w_in)
    lane_pad = ((0, 0), (N_DHEADS, LANES - 2 * N_DHEADS))
    return (
        norm_g.reshape(depth, 1, D_MODEL),
        w_all,
        conv_qkv_w,
        jnp.pad(a_log, lane_pad).reshape(depth, 1, LANES),
        jnp.pad(dt_bias, lane_pad).reshape(depth, 1, LANES),
        delta_norm_g.reshape(depth, 1, HEAD_DIM),
        sconv_w,
        cconv_w,
        cconv_b.reshape(depth, 1, D_CONF),
        cln_g.reshape(depth, 1, D_CONF),
        cln_b.reshape(depth, 1, D_CONF),
        w_out.astype(BF16),
        final_norm_g.reshape(1, D_MODEL),
    )


def kernel(x_prompt, x_sample, state_delta, state_qkv_conv, state_sconv, state_cconv, norm_g, w_in, conv_qkv_w, a_log, dt_bias, delta_norm_g, sconv_w, cconv_w, cconv_b, cln_g, cln_b, w_out, final_norm_g):
    depth = w_in.shape[0]
    dec_seq = x_sample.shape[1]
    assert x_prompt.shape[1] % TIME_TILE == 0
    assert dec_seq == SAMPLE_PAD // 2 and x_sample.shape[0] % SAMPLE_BATCH_BLOCK == 0

    wts = _stacked_weights(norm_g, w_in, conv_qkv_w, a_log, dt_bias, delta_norm_g, sconv_w,
                           cconv_w, cconv_b, cln_g, cln_b, w_out, final_norm_g)

    xp = x_prompt
    p_outs = [[] for _ in range(4)]
    for l in range(depth):
        xp, pd, pq, ps, pc = _prompt_layer_pipelined(xp, wts, l, l == depth - 1)
        for acc, o in zip(p_outs, (pd, pq, ps, pc)):
            acc.append(o)
    pd, pq, ps, pc = (jnp.stack(o) for o in p_outs)

    xs, sd, sq, ss, sc = _sample_layers(x_sample, state_delta, state_qkv_conv, state_sconv,
                                        state_cconv, wts)
    return (xp, xs[depth - 1],
            pd, pq[:, :, SUBLANES - (QK_CONV - 1):, :], ps[:, :, SUBLANES - (SCONV_W - 1):, :],
            pc[:, :, CTAIL - (CONF_W - 1):, :],
            sd, sq, ss, sc)
```

```python
import functools

import jax
import jax.numpy as jnp
from jax import lax
from jax.experimental import pallas as pl
from jax.experimental.pallas import tpu as pltpu

D_MODEL = 1024
N_DHEADS = 4
HEAD_DIM = 128
D_DELTA = N_DHEADS * HEAD_DIM
D_SCONV = 256
D_CONF = 256
QK_CONV = 4
SCONV_W = 3
CONF_W = 31
CHUNK = 64
EPS = 1e-6

R_GATE_D = 0
R_SB = 512
R_SC = 768
R_SX = 1024
R_GATE_S = 1280
R_GA = 1536
R_GB = 1792
R_GATE_C = 2048
R_BA = 2304
R_WIDTH = 2432
W_REST = 3 * D_DELTA

SUBLANES = 8
LANES = 128
TIME_TILE = 256
SAMPLE_PAD = 8
SAMPLE_BATCH_BLOCK = 16
SAMPLE_SEQ_LOCKSTEP = 8
CTAIL = 32
VMEM_LIMIT = 56 * 1024 * 1024

F32 = jnp.float32
BF16 = jnp.bfloat16


def _mm(a, b):
    return jnp.dot(a.astype(BF16), b.astype(BF16), preferred_element_type=F32)


def _mm_nt(a, b):
    return lax.dot_general(a.astype(BF16), b.astype(BF16), (((1,), (1,)), ((), ())),
                           preferred_element_type=F32)


def _mm_tn(a, b):
    return lax.dot_general(a.astype(BF16), b.astype(BF16), (((0,), (0,)), ((), ())),
                           preferred_element_type=F32)


def _sigmoid(x):
    return 1.0 / (1.0 + jnp.exp(-x))


def _silu(x):
    return x * _sigmoid(x)


def _softplus(x):
    return jnp.maximum(x, 0.0) + jnp.log1p(jnp.exp(-jnp.abs(x)))


def _rms_rows(x, g):
    return x * lax.rsqrt(jnp.mean(x * x, axis=-1, keepdims=True) + EPS) * g


def _chunk_cumsum(g, chunk, row):
    pos = row % chunk
    s = 1
    while s < chunk:
        g = g + jnp.where(pos >= s, pltpu.roll(g, s, axis=0), 0.0)
        s *= 2
    return g


def _run(staged):
    try:
        while True:
            next(staged)
    except StopIteration as stop:
        return stop.value


def _interleave(*staged, steps=None):
    live = list(staged)
    steps = dict(zip(live, steps or [1] * len(live)))
    while live:
        for s in list(live):
            try:
                for _ in range(steps[s]):
                    next(s)
            except StopIteration:
                live.remove(s)


def _pair_diag(a, b):
    z = jnp.zeros_like(a)
    return jnp.concatenate([jnp.concatenate([a, z], axis=1), jnp.concatenate([z, b], axis=1)], axis=0)


class _Packed:
    def __init__(self, chunk, rows, bf16_tile):
        self.chunk, self.rows, self.n = chunk, rows, rows // chunk
        self.rr = lax.broadcasted_iota(jnp.int32, (chunk, rows), 0)
        lane = lax.broadcasted_iota(jnp.int32, (chunk, rows), 1)
        self.jl = lane % chunk
        self.lane_blk = lane // chunk
        ii = lax.broadcasted_iota(jnp.int32, (rows, rows), 0)
        jj = lax.broadcasted_iota(jnp.int32, (rows, rows), 1)
        mask = jnp.where((ii // chunk) == (jj // chunk), 1.0, 0.0).astype(F32)
        self.bd_mask = mask.astype(BF16) if bf16_tile else mask

    def pack(self, g):
        out = g[0:self.chunk]
        for c in range(1, self.n):
            out = jnp.where(self.lane_blk == c, g[c * self.chunk:(c + 1) * self.chunk], out)
        return out

    def col(self, v):
        shape = (self.chunk, self.rows)
        out = jnp.broadcast_to(v[0:self.chunk], shape)
        for c in range(1, self.n):
            out = jnp.where(self.lane_blk == c,
                            jnp.broadcast_to(v[c * self.chunk:(c + 1) * self.chunk], shape), out)
        return out

    def block_diag(self, xp):
        if self.bd_mask.dtype == BF16:
            return jnp.concatenate([xp.astype(BF16)] * self.n, axis=0) * self.bd_mask
        return (jnp.concatenate([xp] * self.n, axis=0) * self.bd_mask).astype(BF16)

    def unit_lower_inverse(self, lps, nil):
        base = min(self.chunk, 16)
        same = (self.rr // base) == (self.jl // base)
        eye = jnp.where(self.rr == self.jl, 1.0, 0.0).astype(F32)
        ds = [jnp.where(same, lp, 0.0) for lp in lps]
        xs = [eye - d for d in ds]
        d_bds = [self.block_diag(d) for d in ds]
        p = 2
        while p < min(base, nil):
            ds = [_mm(d, d_bd) for d, d_bd in zip(ds, d_bds)]
            yield
            d_bds = [self.block_diag(d) for d in ds]
            xs = [x + _mm(x, d_bd) for x, d_bd in zip(xs, d_bds)]
            yield
            p *= 2
        size = base
        while size < self.chunk:
            big = (self.rr // (2 * size)) == (self.jl // (2 * size))
            off = jnp.logical_and(big, jnp.logical_not(same))
            xes = [_mm(x, self.block_diag(jnp.where(off, lp, 0.0))) for x, lp in zip(xs, lps)]
            yield
            xs = [x - _mm(xe, self.block_diag(x)) for x, xe in zip(xs, xes)]
            yield
            same = big
            size *= 2
        return xs

    def delta_prep(self, heads, nil):
        tril = self.rr >= self.jl
        strict = self.rr > self.jl
        decays, kbs, gs = [], [], []
        for q, k, v, beta, gc_col, gc_row in heads:
            diff = self.col(gc_col) - gc_row
            decays.append(jnp.where(tril, jnp.exp(jnp.where(tril, diff, 0.0)), 0.0))
            kbs.append(k * beta)
        for (q, k, *_), kb in zip(heads, kbs):
            gs.append(_mm_nt(jnp.concatenate([kb, q], axis=0), k))
        yield
        lps = [jnp.where(strict, self.pack(g[:self.rows]) * dec, 0.0) for g, dec in zip(gs, decays)]
        qkps = [self.pack(g[self.rows:]) * dec for g, dec in zip(gs, decays)]
        tinvs = yield from self.unit_lower_inverse(lps, nil)
        out = []
        for (q, k, v, beta, gc_col, _), kb, tinv, qkp in zip(heads, kbs, tinvs, qkps):
            eg = jnp.exp(gc_col)
            uw = _mm(self.block_diag(tinv), jnp.concatenate([v * beta, kb * eg], axis=1))
            out.append((uw[:, :HEAD_DIM], uw[:, HEAD_DIM:], q * eg, qkp))
        yield
        return out


def _branch_s(buf_ref, w_ref, tail, rows, lead=()):
    n = len(lead)
    acc = None
    for j in range(SCONV_W):
        idx = lead + (pl.ds(tail - (SCONV_W - 1) + j, rows), slice(None))
        term = buf_ref[idx] * w_ref[j:j + 1, :].reshape((1,) * n + (1, D_SCONV))
        acc = term if acc is None else acc + term
    return acc


def _conformer(ubuf_ref, w_ref, b_ref, g_ref, beta_ref, tail, rows, lead=()):
    n = len(lead)
    shp = (1,) * n + (1, D_CONF)
    acc = None
    for j in range(CONF_W):
        idx = lead + (pl.ds(tail - (CONF_W - 1) + j, rows), slice(None))
        term = ubuf_ref[idx] * w_ref[j:j + 1, :].reshape(shp)
        acc = term if acc is None else acc + term
    yc = acc + b_ref[...].reshape(shp)
    mu = jnp.mean(yc, axis=-1, keepdims=True)
    var = jnp.mean(jnp.square(yc - mu), axis=-1, keepdims=True)
    yc = (yc - mu) * lax.rsqrt(var + EPS) * g_ref[...].reshape(shp) + beta_ref[...].reshape(shp)
    return _silu(yc)


def _prompt_kernel(final_norm,
                   x_ref, ng_ref, wqkv_ref, wrest_ref, cw_ref, alog_ref, dt_ref, dng_ref,
                   sw_ref, ccw_ref, ccb_ref, clg_ref, clb_ref, wout_ref, fg_ref,
                   y_ref, s_out_ref, qt_ref, st_ref, ct_ref,
                   qbuf, sbuf, cbuf, cshift, s_scr, obuf, rest,
                   u_s, lhs_s, kdt_s, egl_s, qk_s):
    i = pl.program_id(1)
    nt = pl.num_programs(1)
    tt = TIME_TILE
    nchunk = tt // CHUNK

    @pl.when(i == 0)
    def _():
        qbuf[0:SUBLANES, :] = jnp.zeros((SUBLANES, 3 * D_DELTA), F32)
        sbuf[0:SUBLANES, :] = jnp.zeros((SUBLANES, D_SCONV), F32)
        cbuf[0:CTAIL, :] = jnp.zeros((CTAIL, D_CONF), F32)
        s_scr[...] = jnp.zeros(s_scr.shape, F32)

    x = x_ref[0]
    h = _rms_rows(x, ng_ref[...]).astype(BF16)
    qbuf[SUBLANES:SUBLANES + tt, :] = jnp.dot(h, wqkv_ref[...], preferred_element_type=F32)
    rest[...] = jnp.dot(h, wrest_ref[...], preferred_element_type=F32)

    sbuf[SUBLANES:SUBLANES + tt, :] = rest[:, R_SC:R_SC + D_SCONV] * rest[:, R_SX:R_SX + D_SCONV]
    ys = _branch_s(sbuf, sw_ref, SUBLANES, tt)
    obuf[:, D_DELTA:D_DELTA + D_SCONV] = (rest[:, R_SB:R_SB + D_SCONV] * ys
                                          * _silu(rest[:, R_GATE_S:R_GATE_S + D_SCONV]))

    cbuf[CTAIL:CTAIL + tt, :] = rest[:, R_GA:R_GA + D_CONF] * _sigmoid(rest[:, R_GB:R_GB + D_CONF])
    for sh in range(1, SUBLANES):
        cshift[sh - 1] = cbuf[pl.ds(sh, tt + CTAIL - SUBLANES), :]

    row = lax.broadcasted_iota(jnp.int32, (tt, LANES), 0)
    ba = rest[:, R_BA:R_BA + LANES]
    beta_all = _sigmoid(ba)
    g_all = -jnp.exp(alog_ref[...]) * _softplus(ba + dt_ref[...])
    gc = _chunk_cumsum(g_all, CHUNK, row)
    gct = gc.T
    gl_rows = jnp.concatenate(
        [jnp.broadcast_to(gc[(c + 1) * CHUNK - 1:(c + 1) * CHUNK, :], (CHUNK, LANES))
         for c in range(nchunk)], axis=0)
    pk = _Packed(CHUNK, tt, True)

    def conv_act(c0):
        acc = None
        for j in range(QK_CONV):
            term = (qbuf[pl.ds(SUBLANES - (QK_CONV - 1) + j, tt), c0:c0 + HEAD_DIM]
                    * cw_ref[j:j + 1, c0:c0 + HEAD_DIM])
            acc = term if acc is None else acc + term
        return _silu(acc)

    heads, kdecs = [], []
    for hd in range(N_DHEADS):
        q = conv_act(hd * HEAD_DIM)
        k = conv_act(D_DELTA + hd * HEAD_DIM)
        v = conv_act(2 * D_DELTA + hd * HEAD_DIM)
        q = q * lax.rsqrt(jnp.sum(q * q, axis=-1, keepdims=True) + EPS) * (HEAD_DIM ** -0.5)
        k = k * lax.rsqrt(jnp.sum(k * k, axis=-1, keepdims=True) + EPS)
        lane = N_DHEADS + hd
        gcol = gc[:, lane:lane + 1]
        heads.append((q, k, v, beta_all[:, hd:hd + 1], gcol, gct[lane:lane + 1, :]))
        kdecs.append(k * jnp.exp(gl_rows[:, lane:lane + 1] - gcol))
    prepped = pk.delta_prep(heads, CHUNK)

    for pr in range(N_DHEADS // 2):
        h0, h1 = 2 * pr, 2 * pr + 1
        (u0, w0, qg0, qkp0), (u1, w1, qg1, qkp1) = prepped[h0], prepped[h1]
        kd0, kd1 = kdecs[h0], kdecs[h1]
        u_s[pr] = jnp.concatenate([u0, u1], axis=1)
        w = jnp.concatenate([w0, w1], axis=1).astype(BF16)
        qg = jnp.concatenate([qg0, qg1], axis=1).astype(BF16)
        kst = jnp.concatenate(
            [kd[c * CHUNK:(c + 1) * CHUNK] for c in range(nchunk) for kd in (kd0, kd1)], axis=0)
        kdt = kst.T.astype(BF16)
        low = (lax.broadcasted_iota(jnp.int32, (CHUNK, tt), 1) % LANES) < CHUNK
        qk_even = jnp.where(low, qkp0, pltpu.roll(qkp1, CHUNK, axis=1)).astype(BF16)
        qk_odd = jnp.where(low, pltpu.roll(qkp0, tt - CHUNK, axis=1), qkp1).astype(BF16)
        for c in range(nchunk):
            cs = slice(c * CHUNK, (c + 1) * CHUNK)
            lhs_s[pr, c, 0:CHUNK, :] = w[cs]
            lhs_s[pr, c, CHUNK:, :] = qg[cs]
            kdt_s[pr, c] = kdt[:, 2 * c * CHUNK:2 * (c + 1) * CHUNK]
            tile = slice((c // 2) * LANES, (c // 2 + 1) * LANES)
            qk_s[pr, c] = (qk_even if c % 2 == 0 else qk_odd)[:, tile]
            ge = (c + 1) * CHUNK - 1
            egl = jnp.concatenate(
                [jnp.broadcast_to(jnp.exp(gc[ge:ge + 1, N_DHEADS + hh:N_DHEADS + hh + 1]),
                                  (SUBLANES, HEAD_DIM)) for hh in (h0, h1)], axis=1)
            egl_s[c, :, 2 * pr * HEAD_DIM:2 * (pr + 1) * HEAD_DIM] = egl

    def chunk_body(c, carry):
        r0 = pl.multiple_of(c * CHUNK, CHUNK)
        rows_c = pl.ds(r0, CHUNK)
        pairs = range(N_DHEADS // 2)
        ss = [s_scr[pr] for pr in pairs]
        rs = [jnp.dot(lhs_s[pr, c], _pair_diag(s[:, :HEAD_DIM], s[:, HEAD_DIM:]).astype(BF16),
                      preferred_element_type=F32) for pr, s in zip(pairs, ss)]
        vns = [u_s[pr, rows_c, :] - r[:CHUNK] for pr, r in zip(pairs, rs)]
        v_bds = [_pair_diag(vn[:, :HEAD_DIM], vn[:, HEAD_DIM:]).astype(BF16) for vn in vns]
        upds = [jnp.dot(kdt_s[pr, c], v_bd, preferred_element_type=F32) for pr, v_bd in zip(pairs, v_bds)]
        o_intra = [jnp.dot(qk_s[pr, c], v_bd, preferred_element_type=F32) for pr, v_bd in zip(pairs, v_bds)]
        for pr in pairs:
            egl = egl_s[c][0:1, 2 * pr * HEAD_DIM:2 * (pr + 1) * HEAD_DIM]
            s_scr[pr] = ss[pr] * egl + upds[pr]
            o_pair = rs[pr][CHUNK:] + o_intra[pr]
            for hl in range(2):
                hd = 2 * pr + hl
                o = _rms_rows(o_pair[:, hl * HEAD_DIM:(hl + 1) * HEAD_DIM], dng_ref[...])
                gate = rest[rows_c, R_GATE_D + hd * HEAD_DIM:R_GATE_D + (hd + 1) * HEAD_DIM]
                obuf[rows_c, hd * HEAD_DIM:(hd + 1) * HEAD_DIM] = o * _silu(gate)
        acc = None
        for j in range(CONF_W):
            off = CTAIL - (CONF_W - 1) + j
            start = pl.multiple_of(r0 + (off // SUBLANES) * SUBLANES, SUBLANES)
            sh = off % SUBLANES
            src = cbuf[pl.ds(start, CHUNK), :] if sh == 0 else cshift[sh - 1, pl.ds(start, CHUNK), :]
            term = src * ccw_ref[j:j + 1, :]
            acc = term if acc is None else acc + term
        yc = acc + ccb_ref[...]
        mu = jnp.mean(yc, axis=-1, keepdims=True)
        var = jnp.mean(jnp.square(yc - mu), axis=-1, keepdims=True)
        yc = (yc - mu) * lax.rsqrt(var + EPS) * clg_ref[...] + clb_ref[...]
        obuf[rows_c, D_DELTA + D_SCONV:] = _silu(yc) * _silu(rest[rows_c, R_GATE_C:R_GATE_C + D_CONF])
        return carry

    lax.fori_loop(0, nchunk, chunk_body, 0)

    y = x_ref[0] + jnp.dot(obuf[...].astype(BF16), wout_ref[...], preferred_element_type=F32)
    if final_norm:
        y = _rms_rows(y, fg_ref[...])
    y_ref[0] = y

    @pl.when(i == nt - 1)
    def _():
        for pr in range(N_DHEADS // 2):
            s_out_ref[0, 2 * pr] = s_scr[pr, :, :HEAD_DIM]
            s_out_ref[0, 2 * pr + 1] = s_scr[pr, :, HEAD_DIM:]
        qt_ref[0] = qbuf[tt:tt + SUBLANES, :]
        st_ref[0] = sbuf[tt:tt + SUBLANES, :]
        ct_ref[0] = cbuf[tt:tt + CTAIL, :]

    qbuf[0:SUBLANES, :] = qbuf[tt:tt + SUBLANES, :]
    sbuf[0:SUBLANES, :] = sbuf[tt:tt + SUBLANES, :]
    cbuf[0:CTAIL, :] = cbuf[tt:tt + CTAIL, :]


def _weight_specs(wts, layer_of, buffers=2):
    specs = []
    for w in wts[:-1]:
        tail = (0,) * (w.ndim - 1)
        specs.append(pl.BlockSpec((None,) + w.shape[1:], lambda *ids, tail=tail: (layer_of(*ids),) + tail,
                                  pipeline_mode=pl.Buffered(buffers)))
    specs.append(pl.BlockSpec(wts[-1].shape, lambda *ids: (0,) * wts[-1].ndim))
    return specs


def _prompt_layer(x, wts, layer, final_norm):
    b, t, d = x.shape
    nt = t // TIME_TILE
    npair = N_DHEADS // 2
    nchunk = TIME_TILE // CHUNK
    weight_specs = _weight_specs(wts, lambda *ids: layer)
    out_shape = (
        jax.ShapeDtypeStruct((b, t, d), F32),
        jax.ShapeDtypeStruct((b, N_DHEADS, HEAD_DIM, HEAD_DIM), F32),
        jax.ShapeDtypeStruct((b, SUBLANES, 3 * D_DELTA), F32),
        jax.ShapeDtypeStruct((b, SUBLANES, D_SCONV), F32),
        jax.ShapeDtypeStruct((b, CTAIL, D_CONF), F32),
    )
    out_specs = (
        pl.BlockSpec((1, TIME_TILE, d), lambda bi, ti: (bi, ti, 0)),
        pl.BlockSpec((1, N_DHEADS, HEAD_DIM, HEAD_DIM), lambda bi, ti: (bi, 0, 0, 0)),
        pl.BlockSpec((1, SUBLANES, 3 * D_DELTA), lambda bi, ti: (bi, 0, 0)),
        pl.BlockSpec((1, SUBLANES, D_SCONV), lambda bi, ti: (bi, 0, 0)),
        pl.BlockSpec((1, CTAIL, D_CONF), lambda bi, ti: (bi, 0, 0)),
    )
    scratch = [
        pltpu.VMEM((TIME_TILE + SUBLANES, 3 * D_DELTA), F32),
        pltpu.VMEM((TIME_TILE + SUBLANES, D_SCONV), F32),
        pltpu.VMEM((TIME_TILE + CTAIL, D_CONF), F32),
        pltpu.VMEM((SUBLANES - 1, TIME_TILE + CTAIL - SUBLANES, D_CONF), F32),
        pltpu.VMEM((npair, HEAD_DIM, 2 * HEAD_DIM), F32),
        pltpu.VMEM((TIME_TILE, D_MODEL), F32),
        pltpu.VMEM((TIME_TILE, R_WIDTH), F32),
        pltpu.VMEM((npair, TIME_TILE, 2 * HEAD_DIM), F32),
        pltpu.VMEM((npair, nchunk, 2 * CHUNK, 2 * HEAD_DIM), BF16),
        pltpu.VMEM((npair, nchunk, HEAD_DIM, 2 * CHUNK), BF16),
        pltpu.VMEM((nchunk, SUBLANES, N_DHEADS * HEAD_DIM), F32),
        pltpu.VMEM((npair, nchunk, CHUNK, 2 * CHUNK), BF16),
    ]
    return pl.pallas_call(
        functools.partial(_prompt_kernel, final_norm),
        grid=(b, nt),
        in_specs=[pl.BlockSpec((1, TIME_TILE, d), lambda bi, ti: (bi, ti, 0))] + weight_specs,
        out_specs=out_specs,
        out_shape=out_shape,
        scratch_shapes=scratch,
        compiler_params=pltpu.CompilerParams(
            dimension_semantics=("arbitrary", "arbitrary"),
            vmem_limit_bytes=VMEM_LIMIT),
        name="prompt_layer",
    )(x, *wts)


def _pipe_kernel(final_norm, nt,
                 x_ref, ng_ref, win_ref, cw_ref, alog_ref, dt_ref, dng_ref,
                 sw_ref, ccw_ref, ccb_ref, clg_ref, clb_ref, wout_ref, fg_ref,
                 y_ref, s_out_ref, qt_ref, st_ref, ct_ref,
                 qbuf, sbuf, s_scr, xs, qa, bg, rest, cbuf, cshift, obuf):
    g = pl.program_id(0)
    n_tiles = pl.num_programs(0) - 1
    tt = TIME_TILE
    nchunk = tt // CHUNK
    npair = N_DHEADS // 2
    t1 = jnp.minimum(g, n_tiles - 1) % nt
    t2 = jnp.maximum(g - 1, 0) % nt
    p = g % 2
    q = 1 - p

    @pl.when(g == 0)
    def _():
        xs[1] = jnp.zeros(xs.shape[1:], F32)
        qa[1] = jnp.zeros(qa.shape[1:], F32)
        bg[1] = jnp.zeros(bg.shape[1:], F32)
        rest[1] = jnp.zeros(rest.shape[1:], F32)
        cbuf[1] = jnp.zeros(cbuf.shape[1:], F32)
        cshift[1] = jnp.zeros(cshift.shape[1:], F32)
        obuf[1] = jnp.zeros(obuf.shape[1:], F32)

    @pl.when(t1 == 0)
    def _():
        qbuf[0:SUBLANES, :] = jnp.zeros((SUBLANES, 3 * D_DELTA), F32)
        sbuf[0:SUBLANES, :] = jnp.zeros((SUBLANES, D_SCONV), F32)

    @pl.when(t2 == 0)
    def _():
        s_scr[...] = jnp.zeros(s_scr.shape, F32)

    def stage1():
        x = x_ref[0]
        xs[p] = x
        h = _rms_rows(x, ng_ref[...]).astype(BF16)
        yield
        for c0 in range(0, 3 * D_DELTA, 2 * LANES):
            qbuf[SUBLANES:SUBLANES + tt, c0:c0 + 2 * LANES] = jnp.dot(
                h, win_ref[:, c0:c0 + 2 * LANES], preferred_element_type=F32)
            yield
        for c0 in range(0, R_WIDTH, 2 * LANES):
            c1 = min(c0 + 2 * LANES, R_WIDTH)
            rest[p, :, c0:c1] = jnp.dot(h, win_ref[:, W_REST + c0:W_REST + c1],
                                        preferred_element_type=F32)
            yield

        def conv_act(c0):
            acc = None
            for j in range(QK_CONV):
                term = (qbuf[pl.ds(SUBLANES - (QK_CONV - 1) + j, tt), c0:c0 + HEAD_DIM]
                        * cw_ref[j:j + 1, c0:c0 + HEAD_DIM])
                acc = term if acc is None else acc + term
            return _silu(acc)

        for hd in range(N_DHEADS):
            qh = conv_act(hd * HEAD_DIM)
            qa[p, :, hd * HEAD_DIM:(hd + 1) * HEAD_DIM] = (
                qh * lax.rsqrt(jnp.sum(qh * qh, axis=-1, keepdims=True) + EPS) * (HEAD_DIM ** -0.5))
            kh = conv_act(D_DELTA + hd * HEAD_DIM)
            qa[p, :, D_DELTA + hd * HEAD_DIM:D_DELTA + (hd + 1) * HEAD_DIM] = (
                kh * lax.rsqrt(jnp.sum(kh * kh, axis=-1, keepdims=True) + EPS))
            qa[p, :, 2 * D_DELTA + hd * HEAD_DIM:2 * D_DELTA + (hd + 1) * HEAD_DIM] = conv_act(
                2 * D_DELTA + hd * HEAD_DIM)
            yield

        row = lax.broadcasted_iota(jnp.int32, (tt, LANES), 0)
        ba = rest[p, :, R_BA:R_BA + LANES]
        bg[p, 0] = _sigmoid(ba)
        bg[p, 1] = _chunk_cumsum(-jnp.exp(alog_ref[...]) * _softplus(ba + dt_ref[...]), CHUNK, row)
        yield

        sbuf[SUBLANES:SUBLANES + tt, :] = (rest[p, :, R_SC:R_SC + D_SCONV]
                                           * rest[p, :, R_SX:R_SX + D_SCONV])
        obuf[p, :, D_DELTA:D_DELTA + D_SCONV] = (
            rest[p, :, R_SB:R_SB + D_SCONV] * _branch_s(sbuf, sw_ref, SUBLANES, tt)
            * _silu(rest[p, :, R_GATE_S:R_GATE_S + D_SCONV]))
        yield

        prev_tail = jnp.where(t1 == 0, 0.0, cbuf[q, tt:tt + CTAIL, :])
        cbuf[p, 0:CTAIL, :] = prev_tail
        cbuf[p, CTAIL:CTAIL + tt, :] = (rest[p, :, R_GA:R_GA + D_CONF]
                                        * _sigmoid(rest[p, :, R_GB:R_GB + D_CONF]))
        yield
        for sh in range(1, SUBLANES):
            cshift[p, sh - 1] = cbuf[p, pl.ds(sh, tt + CTAIL - SUBLANES), :]
        yield

        qbuf[0:SUBLANES, :] = qbuf[tt:tt + SUBLANES, :]
        sbuf[0:SUBLANES, :] = sbuf[tt:tt + SUBLANES, :]

    def stage2():
        beta_all = bg[q, 0]
        gc = bg[q, 1]
        gct = gc.T
        gl_rows = jnp.concatenate(
            [jnp.broadcast_to(gc[(c + 1) * CHUNK - 1:(c + 1) * CHUNK, :], (CHUNK, LANES))
             for c in range(nchunk)], axis=0)
        pk = _Packed(CHUNK, tt, True)
        heads, kdecs = [], []
        for hd in range(N_DHEADS):
            lane = N_DHEADS + hd
            gcol = gc[:, lane:lane + 1]
            kh = qa[q, :, D_DELTA + hd * HEAD_DIM:D_DELTA + (hd + 1) * HEAD_DIM]
            heads.append((qa[q, :, hd * HEAD_DIM:(hd + 1) * HEAD_DIM], kh,
                          qa[q, :, 2 * D_DELTA + hd * HEAD_DIM:2 * D_DELTA + (hd + 1) * HEAD_DIM],
                          beta_all[:, hd:hd + 1], gcol, gct[lane:lane + 1, :]))
            kdecs.append(kh * jnp.exp(gl_rows[:, lane:lane + 1] - gcol))
        yield
        prepped = yield from pk.delta_prep(heads, CHUNK)

        us, lhss, kdts, qks = [], [], [], []
        low = (lax.broadcasted_iota(jnp.int32, (CHUNK, tt), 1) % LANES) < CHUNK
        for pr in range(npair):
            (u0, w0, qg0, qkp0), (u1, w1, qg1, qkp1) = prepped[2 * pr], prepped[2 * pr + 1]
            us.append(jnp.concatenate([u0, u1], axis=1))
            w = jnp.concatenate([w0, w1], axis=1).astype(BF16)
            qg = jnp.concatenate([qg0, qg1], axis=1).astype(BF16)
            lhss.append([jnp.concatenate([w[c * CHUNK:(c + 1) * CHUNK], qg[c * CHUNK:(c + 1) * CHUNK]],
                                         axis=0) for c in range(nchunk)])
            kst = jnp.concatenate([kd[c * CHUNK:(c + 1) * CHUNK] for c in range(nchunk)
                                   for kd in (kdecs[2 * pr], kdecs[2 * pr + 1])], axis=0)
            kdts.append(kst.T.astype(BF16))
            qks.append((jnp.where(low, qkp0, pltpu.roll(qkp1, CHUNK, axis=1)).astype(BF16),
                        jnp.where(low, pltpu.roll(qkp0, tt - CHUNK, axis=1), qkp1).astype(BF16)))
        yield

        ss = [s_scr[pr] for pr in range(npair)]
        for c in range(nchunk):
            cs = slice(c * CHUNK, (c + 1) * CHUNK)
            rs = [jnp.dot(lhss[pr][c], _pair_diag(s[:, :HEAD_DIM], s[:, HEAD_DIM:]).astype(BF16),
                          preferred_element_type=F32) for pr, s in enumerate(ss)]
            yield
            vns = [us[pr][cs] - r[:CHUNK] for pr, r in enumerate(rs)]
            v_bds = [_pair_diag(vn[:, :HEAD_DIM], vn[:, HEAD_DIM:]).astype(BF16) for vn in vns]
            upds = [jnp.dot(kdts[pr][:, 2 * c * CHUNK:2 * (c + 1) * CHUNK], v_bd,
                            preferred_element_type=F32) for pr, v_bd in enumerate(v_bds)]
            tile = slice((c // 2) * LANES, (c // 2 + 1) * LANES)
            o_intra = [jnp.dot(qks[pr][c % 2][:, tile], v_bd, preferred_element_type=F32)
                       for pr, v_bd in enumerate(v_bds)]
            ge = (c + 1) * CHUNK - 1
            for pr in range(npair):
                egl = jnp.concatenate(
                    [jnp.broadcast_to(jnp.exp(gc[ge:ge + 1, N_DHEADS + hh:N_DHEADS + hh + 1]),
                                      (1, HEAD_DIM)) for hh in (2 * pr, 2 * pr + 1)], axis=1)
                ss[pr] = ss[pr] * egl + upds[pr]
                o_pair = rs[pr][CHUNK:] + o_intra[pr]
                for hl in range(2):
                    hd = 2 * pr + hl
                    o = _rms_rows(o_pair[:, hl * HEAD_DIM:(hl + 1) * HEAD_DIM], dng_ref[...])
                    gate = rest[q, cs, R_GATE_D + hd * HEAD_DIM:R_GATE_D + (hd + 1) * HEAD_DIM]
                    obuf[q, cs, hd * HEAD_DIM:(hd + 1) * HEAD_DIM] = o * _silu(gate)
            yield
            acc = None
            for j in range(CONF_W):
                off = CTAIL - (CONF_W - 1) + j
                start = c * CHUNK + (off // SUBLANES) * SUBLANES
                sh = off % SUBLANES
                src = (cbuf[q, start:start + CHUNK, :] if sh == 0
                       else cshift[q, sh - 1, start:start + CHUNK, :])
                term = src * ccw_ref[j:j + 1, :]
                acc = term if acc is None else acc + term
            yc = acc + ccb_ref[...]
            mu = jnp.mean(yc, axis=-1, keepdims=True)
            var = jnp.mean(jnp.square(yc - mu), axis=-1, keepdims=True)
            yc = (yc - mu) * lax.rsqrt(var + EPS) * clg_ref[...] + clb_ref[...]
            obuf[q, cs, D_DELTA + D_SCONV:] = _silu(yc) * _silu(rest[q, cs, R_GATE_C:R_GATE_C + D_CONF])
            yield
        for pr in range(npair):
            s_scr[pr] = ss[pr]

        y = xs[q] + jnp.dot(obuf[q].astype(BF16), wout_ref[...], preferred_element_type=F32)
        if final_norm:
            y = _rms_rows(y, fg_ref[...])
        y_ref[0] = y

    _interleave(stage1(), stage2(), steps=(4, 1))

    @pl.when(jnp.logical_and(g >= 1, t2 == nt - 1))
    def _():
        for pr in range(npair):
            s_out_ref[0, 2 * pr] = s_scr[pr, :, :HEAD_DIM]
            s_out_ref[0, 2 * pr + 1] = s_scr[pr, :, HEAD_DIM:]

    @pl.when(jnp.logical_and(g < n_tiles, t1 == nt - 1))
    def _():
        qt_ref[0] = qbuf[0:SUBLANES, :]
        st_ref[0] = sbuf[0:SUBLANES, :]
        ct_ref[0] = cbuf[p, tt:tt + CTAIL, :]


def _prompt_layer_pipelined(x, wts, layer, final_norm):
    b, t, d = x.shape
    nt = t // TIME_TILE
    n_tiles = b * nt
    npair = N_DHEADS // 2

    def tile1(g):
        g1 = jnp.minimum(g, n_tiles - 1)
        return g1 // nt, g1 % nt

    def tile2(g):
        g2 = jnp.maximum(g - 1, 0)
        return g2 // nt, g2 % nt

    out_shape = (
        jax.ShapeDtypeStruct((b, t, d), F32),
        jax.ShapeDtypeStruct((b, N_DHEADS, HEAD_DIM, HEAD_DIM), F32),
        jax.ShapeDtypeStruct((b, SUBLANES, 3 * D_DELTA), F32),
        jax.ShapeDtypeStruct((b, SUBLANES, D_SCONV), F32),
        jax.ShapeDtypeStruct((b, CTAIL, D_CONF), F32),
    )
    out_specs = (
        pl.BlockSpec((1, TIME_TILE, d), lambda g: tile2(g) + (0,)),
        pl.BlockSpec((1, N_DHEADS, HEAD_DIM, HEAD_DIM), lambda g: (tile2(g)[0], 0, 0, 0)),
        pl.BlockSpec((1, SUBLANES, 3 * D_DELTA), lambda g: (tile1(g)[0], 0, 0)),
        pl.BlockSpec((1, SUBLANES, D_SCONV), lambda g: (tile1(g)[0], 0, 0)),
        pl.BlockSpec((1, CTAIL, D_CONF), lambda g: (tile1(g)[0], 0, 0)),
    )
    scratch = [
        pltpu.VMEM((TIME_TILE + SUBLANES, 3 * D_DELTA), F32),
        pltpu.VMEM((TIME_TILE + SUBLANES, D_SCONV), F32),
        pltpu.VMEM((npair, HEAD_DIM, 2 * HEAD_DIM), F32),
        pltpu.VMEM((2, TIME_TILE, D_MODEL), F32),
        pltpu.VMEM((2, TIME_TILE, 3 * D_DELTA), F32),
        pltpu.VMEM((2, 2, TIME_TILE, LANES), F32),
        pltpu.VMEM((2, TIME_TILE, R_WIDTH), F32),
        pltpu.VMEM((2, TIME_TILE + CTAIL, D_CONF), F32),
        pltpu.VMEM((2, SUBLANES - 1, TIME_TILE + CTAIL - SUBLANES, D_CONF), F32),
        pltpu.VMEM((2, TIME_TILE, D_MODEL), F32),
    ]
    return pl.pallas_call(
        functools.partial(_pipe_kernel, final_norm, nt),
        grid=(n_tiles + 1,),
        in_specs=[pl.BlockSpec((1, TIME_TILE, d), lambda g: tile1(g) + (0,))]
        + _weight_specs(wts, lambda g: layer, buffers=1),
        out_specs=out_specs,
        out_shape=out_shape,
        scratch_shapes=scratch,
        compiler_params=pltpu.CompilerParams(
            dimension_semantics=("arbitrary",),
            vmem_limit_bytes=VMEM_LIMIT),
        name="prompt_layer",
    )(x, *wts)


def _sample_kernel(x_ref, sd_ref, sq_ref, ss_ref, sc_ref,
                   ng_ref, win_ref, cw_ref, alog_ref, dt_ref, dng_ref,
                   sw_ref, ccw_ref, ccb_ref, clg_ref, clb_ref, wout_ref, fg_ref,
                   y_ref, sd_out_ref, qt_ref, st_ref, ct_ref,
                   qbuf, sbuf, cbuf, obuf, ubuf, wqbuf, kdbuf, glbuf, oibuf, rest, xcarry):
    layer = pl.program_id(0)
    blk = pl.program_id(1)
    nb = SAMPLE_BATCH_BLOCK
    pad = SAMPLE_PAD
    ntok = pad // 2
    rows = nb * pad

    @pl.when(layer == 0)
    def _():
        xcarry[blk, :, 0:ntok, :] = x_ref[...]
        xcarry[blk, :, ntok:, :] = jnp.zeros((nb, pad - ntok, D_MODEL), F32)

    x = xcarry[blk].reshape(rows, D_MODEL)
    h = _rms_rows(x, ng_ref[...]).astype(BF16)
    qkv = jnp.dot(h, win_ref[:, :W_REST], preferred_element_type=F32)
    rest[...] = jnp.dot(h, win_ref[:, W_REST:], preferred_element_type=F32)
    row = lax.broadcasted_iota(jnp.int32, (rows, LANES), 0)
    valid = (row % pad) < (pad // 2)
    valid1 = valid[:, 0:1]

    qkv3 = qkv.reshape(nb, pad, 3 * D_DELTA)
    qt_ref[...] = qkv3[:, ntok - (QK_CONV - 1):ntok, :]
    qbuf[:, SUBLANES - (QK_CONV - 1):SUBLANES, :] = sq_ref[...]
    qbuf[:, SUBLANES:, :] = qkv3

    hs = (rest[:, R_SC:R_SC + D_SCONV] * rest[:, R_SX:R_SX + D_SCONV]).reshape(nb, pad, D_SCONV)
    st_ref[...] = hs[:, ntok - (SCONV_W - 1):ntok, :]
    sbuf[:, SUBLANES - (SCONV_W - 1):SUBLANES, :] = ss_ref[...]
    sbuf[:, SUBLANES:, :] = hs
    ys = _branch_s(sbuf, sw_ref, SUBLANES, pad, lead=(slice(None),))
    obuf[:, D_DELTA:D_DELTA + D_SCONV] = (rest[:, R_SB:R_SB + D_SCONV] * ys.reshape(rows, D_SCONV)
                                          * _silu(rest[:, R_GATE_S:R_GATE_S + D_SCONV]))

    uc = rest[:, R_GA:R_GA + D_CONF] * _sigmoid(rest[:, R_GB:R_GB + D_CONF])
    cbuf[:, CTAIL - (CONF_W - 1):CTAIL, :] = sc_ref[...]
    cbuf[:, CTAIL:, :] = uc.reshape(nb, pad, D_CONF)
    first = CTAIL + ntok - (CONF_W - 1)
    ct_ref[...] = cbuf[:, first:first + CONF_W - 1, :]
    yc = _conformer(cbuf, ccw_ref, ccb_ref, clg_ref, clb_ref, CTAIL, pad, lead=(slice(None),))
    obuf[:, D_DELTA + D_SCONV:] = yc.reshape(rows, D_CONF) * _silu(rest[:, R_GATE_C:R_GATE_C + D_CONF])

    ba = rest[:, R_BA:R_BA + LANES]
    beta_all = jnp.where(valid, _sigmoid(ba), 0.0)
    g_all = jnp.where(valid, -jnp.exp(alog_ref[...]) * _softplus(ba + dt_ref[...]), 0.0)
    gc = _chunk_cumsum(g_all, pad, row)
    gct = gc.T
    gl_all = jnp.broadcast_to(
        gc.reshape(nb, pad, LANES)[:, pad - 1:pad, :], (nb, pad, LANES)).reshape(rows, LANES)
    pk = _Packed(pad, rows, False)

    def conv_act(c0):
        acc = None
        for j in range(QK_CONV):
            term = (qbuf[:, pl.ds(SUBLANES - (QK_CONV - 1) + j, pad), c0:c0 + HEAD_DIM]
                    * cw_ref[j:j + 1, c0:c0 + HEAD_DIM].reshape(1, 1, HEAD_DIM))
            acc = term if acc is None else acc + term
        return jnp.where(valid1, _silu(acc.reshape(rows, HEAD_DIM)), 0.0)

    for pr in range(N_DHEADS // 2):
        h0, h1 = 2 * pr, 2 * pr + 1
        heads, kdecs, egls = [], [], []
        for hd in (h0, h1):
            q = conv_act(hd * HEAD_DIM)
            k = conv_act(D_DELTA + hd * HEAD_DIM)
            v = conv_act(2 * D_DELTA + hd * HEAD_DIM)
            q = q * lax.rsqrt(jnp.sum(q * q, axis=-1, keepdims=True) + EPS) * (HEAD_DIM ** -0.5)
            k = k * lax.rsqrt(jnp.sum(k * k, axis=-1, keepdims=True) + EPS)
            lane = N_DHEADS + hd
            gcol = gc[:, lane:lane + 1]
            heads.append((q, k, v, beta_all[:, hd:hd + 1], gcol, gct[lane:lane + 1, :]))
            gl = gl_all[:, lane:lane + 1]
            kdecs.append(k * jnp.exp(gl - gcol))
            egls.append(jnp.broadcast_to(jnp.exp(gl), (rows, HEAD_DIM)))
        (u0, w0, qg0, qkp0), (u1, w1, qg1, qkp1) = _run(pk.delta_prep(heads, pad // 2))
        (kd0, kd1), (egl0, egl1) = kdecs, egls
        ubuf[...] = jnp.concatenate([u0, u1], axis=1)
        wqbuf[:, 0:pad, :] = jnp.concatenate([w0, w1], axis=1).reshape(nb, pad, 2 * HEAD_DIM)
        wqbuf[:, pad:, :] = jnp.concatenate([qg0, qg1], axis=1).reshape(nb, pad, 2 * HEAD_DIM)
        kdbuf[:, 0:pad, :] = kd0.reshape(nb, pad, HEAD_DIM)
        kdbuf[:, pad:, :] = kd1.reshape(nb, pad, HEAD_DIM)
        glbuf[...] = jnp.concatenate([egl0, egl1], axis=1).reshape(nb, pad, 2 * HEAD_DIM)

        def body(it, carry):
            bis = [it * SAMPLE_SEQ_LOCKSTEP + t for t in range(SAMPLE_SEQ_LOCKSTEP)]
            r0s = [pl.multiple_of(bi * pad, pad) for bi in bis]
            s0s = [sd_ref[bi, h0] for bi in bis]
            s1s = [sd_ref[bi, h1] for bi in bis]
            rs = [_mm(wqbuf[bi], _pair_diag(s0, s1)) for bi, s0, s1 in zip(bis, s0s, s1s)]
            vns = [ubuf[pl.ds(r0, pad), :] - r[:pad] for r0, r in zip(r0s, rs)]
            upds = [_mm_tn(kdbuf[bi], _pair_diag(vn[:, :HEAD_DIM], vn[:, HEAD_DIM:]))
                    for bi, vn in zip(bis, vns)]
            for bi, r0, s0, s1, r, vn, upd in zip(bis, r0s, s0s, s1s, rs, vns, upds):
                ubuf[pl.ds(r0, pad), :] = vn
                oibuf[pl.ds(r0, pad), :] = r[pad:]
                egl = glbuf[bi][0:1, :]
                sd_out_ref[bi, h0] = s0 * egl[:, :HEAD_DIM] + upd[:, :HEAD_DIM]
                sd_out_ref[bi, h1] = s1 * egl[:, HEAD_DIM:] + upd[:, HEAD_DIM:]
            return carry

        lax.fori_loop(0, nb // SAMPLE_SEQ_LOCKSTEP, body, 0)
        for hh, qkp in ((h0, qkp0), (h1, qkp1)):
            ls = slice((hh - h0) * HEAD_DIM, (hh - h0 + 1) * HEAD_DIM)
            o = oibuf[:, ls] + _mm(pk.block_diag(qkp), ubuf[:, ls])
            o = _rms_rows(o, dng_ref[...])
            obuf[:, hh * HEAD_DIM:(hh + 1) * HEAD_DIM] = (
                o * _silu(rest[:, R_GATE_D + hh * HEAD_DIM:R_GATE_D + (hh + 1) * HEAD_DIM]))

    y = x + jnp.dot(obuf[...].astype(BF16), wout_ref[...], preferred_element_type=F32)
    xcarry[blk] = y.reshape(nb, pad, D_MODEL)
    is_last = layer == pl.num_programs(0) - 1
    y_ref[...] = jnp.where(is_last, _rms_rows(y, fg_ref[...]), y).reshape(nb, pad, D_MODEL)[:, 0:ntok, :]


def _sample_layers(x, sd, sq, ss, sc, wts):
    depth = sd.shape[0]
    b, ntok, _ = x.shape
    nb = SAMPLE_BATCH_BLOCK

    def bspec(shape):
        return pl.BlockSpec((nb,) + shape, lambda l, bi: (bi,) + (0,) * len(shape))

    def lspec(shape):
        return pl.BlockSpec((None, nb) + shape, lambda l, bi: (l, bi) + (0,) * len(shape))

    def lshape(shape):
        return jax.ShapeDtypeStruct((depth, b) + shape, F32)

    state_shapes = ((N_DHEADS, HEAD_DIM, HEAD_DIM), (QK_CONV - 1, 3 * D_DELTA),
                    (SCONV_W - 1, D_SCONV), (CONF_W - 1, D_CONF))
    out_shape = (lshape((ntok, D_MODEL)),) + tuple(lshape(s) for s in state_shapes)
    out_specs = (lspec((ntok, D_MODEL)),) + tuple(lspec(s) for s in state_shapes)
    rows = nb * SAMPLE_PAD
    scratch = [
        pltpu.VMEM((nb, 2 * SUBLANES, 3 * D_DELTA), F32),
        pltpu.VMEM((nb, 2 * SUBLANES, D_SCONV), F32),
        pltpu.VMEM((nb, CTAIL + SAMPLE_PAD, D_CONF), F32),
        pltpu.VMEM((rows, D_MODEL), F32),
        pltpu.VMEM((rows, 2 * HEAD_DIM), F32),
        pltpu.VMEM((nb, 2 * SAMPLE_PAD, 2 * HEAD_DIM), F32),
        pltpu.VMEM((nb, 2 * SAMPLE_PAD, HEAD_DIM), F32),
        pltpu.VMEM((nb, SAMPLE_PAD, 2 * HEAD_DIM), F32),
        pltpu.VMEM((rows, 2 * HEAD_DIM), F32),
        pltpu.VMEM((rows, R_WIDTH), F32),
        pltpu.VMEM((b // nb, nb, SAMPLE_PAD, D_MODEL), F32),
    ]
    in_specs = ([bspec((ntok, D_MODEL))] + [lspec(s) for s in state_shapes]
                + _weight_specs(wts, lambda l, bi: l))
    return pl.pallas_call(
        _sample_kernel,
        grid=(depth, b // nb),
        in_specs=in_specs,
        out_specs=out_specs,
        out_shape=out_shape,
        scratch_shapes=scratch,
        compiler_params=pltpu.CompilerParams(
            dimension_semantics=("arbitrary", "arbitrary"),
            vmem_limit_bytes=VMEM_LIMIT),
        name="sample_layers",
    )(x, sd, sq, ss, sc, *wts)


def _reorder_kernel(w_ref, o_ref):
    n_qkv = 3 * D_DELTA
    n_ba = 2 * N_DHEADS
    rows = w_ref.shape[0]
    o_ref[:, 0:n_qkv] = w_ref[:, 0:n_qkv].astype(BF16)
    o_ref[:, n_qkv:n_qkv + R_BA] = w_ref[:, n_qkv + n_ba:].astype(BF16)
    o_ref[:, n_qkv + R_BA:] = jnp.concatenate(
        [w_ref[:, n_qkv:n_qkv + n_ba], jnp.zeros((rows, LANES - n_ba), F32)], axis=1).astype(BF16)


def _reorder_in_proj(w_in):
    depth, d, d_in = w_in.shape
    rows = 256
    return pl.pallas_call(
        _reorder_kernel,
        grid=(depth, d // rows),
        in_specs=[pl.BlockSpec((None, rows, d_in), lambda l, r: (l, r, 0))],
        out_specs=pl.BlockSpec((None, rows, W_REST + R_WIDTH), lambda l, r: (l, r, 0)),
        out_shape=jax.ShapeDtypeStruct((depth, d, W_REST + R_WIDTH), BF16),
        compiler_params=pltpu.CompilerParams(dimension_semantics=("arbitrary", "arbitrary")),
        name="reorder_in_proj",
    )(w_in)


def _stacked_weights(norm_g, w_in, conv_qkv_w, a_log, dt_bias, delta_norm_g, sconv_w,
                     cconv_w, cconv_b, cln_g, cln_b, w_out, final_norm_g):
    depth = w_in.shape[0]
    w_all = _reorder_in_proj(w_in)
    lane_pad = ((0, 0), (N_DHEADS, LANES - 2 * N_DHEADS))
    return (
        norm_g.reshape(depth, 1, D_MODEL),
        w_all,
        conv_qkv_w,
        jnp.pad(a_log, lane_pad).reshape(depth, 1, LANES),
        jnp.pad(dt_bias, lane_pad).reshape(depth, 1, LANES),
        delta_norm_g.reshape(depth, 1, HEAD_DIM),
        sconv_w,
        cconv_w,
        cconv_b.reshape(depth, 1, D_CONF),
        cln_g.reshape(depth, 1, D_CONF),
        cln_b.reshape(depth, 1, D_CONF),
        w_out.astype(BF16),
        final_norm_g.reshape(1, D_MODEL),
    )


def kernel(x_prompt, x_sample, state_delta, state_qkv_conv, state_sconv, state_cconv, norm_g, w_in, conv_qkv_w, a_log, dt_bias, delta_norm_g, sconv_w, cconv_w, cconv_b, cln_g, cln_b, w_out, final_norm_g):
    depth = w_in.shape[0]
    dec_seq = x_sample.shape[1]
    assert x_prompt.shape[1] % TIME_TILE == 0
    assert dec_seq == SAMPLE_PAD // 2 and x_sample.shape[0] % SAMPLE_BATCH_BLOCK == 0

    wts = _stacked_weights(norm_g, w_in, conv_qkv_w, a_log, dt_bias, delta_norm_g, sconv_w,
                           cconv_w, cconv_b, cln_g, cln_b, w_out, final_norm_g)

    xp = x_prompt
    p_outs = [[] for _ in range(4)]
    for l in range(depth):
        xp, pd, pq, ps, pc = _prompt_layer_pipelined(xp, wts, l, l == depth - 1)
        for acc, o in zip(p_outs, (pd, pq, ps, pc)):
            acc.append(o)
    pd, pq, ps, pc = (jnp.stack(o) for o in p_outs)

    xs, sd, sq, ss, sc = _sample_layers(x_sample, state_delta, state_qkv_conv, state_sconv,
                                        state_cconv, wts)
    return (xp, xs[depth - 1],
            pd, pq[:, :, SUBLANES - (QK_CONV - 1):, :], ps[:, :, SUBLANES - (SCONV_W - 1):, :],
            pc[:, :, CTAIL - (CONF_W - 1):, :],
            sd, sq, ss, sc)
```

```python
import functools

import jax
import jax.numpy as jnp
from jax import lax
from jax.experimental import pallas as pl
from jax.experimental.pallas import tpu as pltpu

D_MODEL = 1024
N_DHEADS = 4
HEAD_DIM = 128
D_DELTA = N_DHEADS * HEAD_DIM
D_SCONV = 256
D_CONF = 256
QK_CONV = 4
SCONV_W = 3
CONF_W = 31
CHUNK = 64
EPS = 1e-6

R_GATE_D = 0
R_SB = 512
R_SC = 768
R_SX = 1024
R_GATE_S = 1280
R_GA = 1536
R_GB = 1792
R_GATE_C = 2048
R_BA = 2304
R_WIDTH = 2432
W_REST = 3 * D_DELTA

SUBLANES = 8
LANES = 128
TIME_TILE = 256
SAMPLE_PAD = 8
SAMPLE_BATCH_BLOCK = 16
SAMPLE_SEQ_LOCKSTEP = 8
CTAIL = 32
VMEM_LIMIT = 56 * 1024 * 1024

F32 = jnp.float32
BF16 = jnp.bfloat16


def _mm(a, b):
    return jnp.dot(a.astype(BF16), b.astype(BF16), preferred_element_type=F32)


def _mm_nt(a, b):
    return lax.dot_general(a.astype(BF16), b.astype(BF16), (((1,), (1,)), ((), ())),
                           preferred_element_type=F32)


def _mm_tn(a, b):
    return lax.dot_general(a.astype(BF16), b.astype(BF16), (((0,), (0,)), ((), ())),
                           preferred_element_type=F32)


def _sigmoid(x):
    return 1.0 / (1.0 + jnp.exp(-x))


def _silu(x):
    return x * _sigmoid(x)


def _softplus(x):
    return jnp.maximum(x, 0.0) + jnp.log1p(jnp.exp(-jnp.abs(x)))


def _rms_rows(x, g):
    return x * lax.rsqrt(jnp.mean(x * x, axis=-1, keepdims=True) + EPS) * g


def _chunk_cumsum(g, chunk, row):
    pos = row % chunk
    s = 1
    while s < chunk:
        g = g + jnp.where(pos >= s, pltpu.roll(g, s, axis=0), 0.0)
        s *= 2
    return g


def _run(staged):
    try:
        while True:
            next(staged)
    except StopIteration as stop:
        return stop.value


def _interleave(*staged, steps=None):
    live = list(staged)
    steps = dict(zip(live, steps or [1] * len(live)))
    while live:
        for s in list(live):
            try:
                for _ in range(steps[s]):
                    next(s)
            except StopIteration:
                live.remove(s)


def _pair_diag(a, b):
    z = jnp.zeros_like(a)
    return jnp.concatenate([jnp.concatenate([a, z], axis=1), jnp.concatenate([z, b], axis=1)], axis=0)


class _Packed:
    def __init__(self, chunk, rows, wide_chunks):
        self.chunk, self.rows, self.n, self.wide_chunks = chunk, rows, rows // chunk, wide_chunks
        self.rr = lax.broadcasted_iota(jnp.int32, (chunk, rows), 0)
        lane = lax.broadcasted_iota(jnp.int32, (chunk, rows), 1)
        self.jl = lane % chunk
        self.lane_blk = lane // chunk
        ii = lax.broadcasted_iota(jnp.int32, (rows, rows), 0)
        jj = lax.broadcasted_iota(jnp.int32, (rows, rows), 1)
        mask = jnp.where((ii // chunk) == (jj // chunk), 1.0, 0.0).astype(F32)
        self.bd_mask = mask.astype(BF16) if wide_chunks else mask

    def pack(self, g):
        out = g[0:self.chunk]
        for c in range(1, self.n):
            out = jnp.where(self.lane_blk == c, g[c * self.chunk:(c + 1) * self.chunk], out)
        return out

    def col(self, v):
        shape = (self.chunk, self.rows)
        out = jnp.broadcast_to(v[0:self.chunk], shape)
        for c in range(1, self.n):
            out = jnp.where(self.lane_blk == c,
                            jnp.broadcast_to(v[c * self.chunk:(c + 1) * self.chunk], shape), out)
        return out

    def block_diag(self, xp):
        if self.wide_chunks:
            return jnp.concatenate([xp.astype(BF16)] * self.n, axis=0) * self.bd_mask
        return (jnp.concatenate([xp] * self.n, axis=0) * self.bd_mask).astype(BF16)

    def unit_lower_inverse(self, lps, nil):
        base = min(self.chunk, 16)
        same = (self.rr // base) == (self.jl // base)
        eye = jnp.where(self.rr == self.jl, 1.0, 0.0).astype(F32)
        ds = [jnp.where(same, lp, 0.0) for lp in lps]
        xs = [eye - d for d in ds]
        d_bds = [self.block_diag(d) for d in ds]
        p = 2
        while p < min(base, nil):
            ds = [_mm(d, d_bd) for d, d_bd in zip(ds, d_bds)]
            yield
            d_bds = [self.block_diag(d) for d in ds]
            xs = [x + _mm(x, d_bd) for x, d_bd in zip(xs, d_bds)]
            yield
            p *= 2
        size = base
        while size < self.chunk:
            big = (self.rr // (2 * size)) == (self.jl // (2 * size))
            off = jnp.logical_and(big, jnp.logical_not(same))
            xes = [_mm(x, self.block_diag(jnp.where(off, lp, 0.0))) for x, lp in zip(xs, lps)]
            yield
            xs = [x - _mm(xe, self.block_diag(x)) for x, xe in zip(xs, xes)]
            yield
            same = big
            size *= 2
        return xs

    def delta_prep(self, heads, nil):
        tril = self.rr >= self.jl
        strict = self.rr > self.jl
        decays, kbs, gs = [], [], []
        for q, k, v, beta, gc_col, gc_row in heads:
            diff = self.col(gc_col) - gc_row
            decays.append(jnp.where(tril, jnp.exp(jnp.where(tril, diff, 0.0)), 0.0))
            kbs.append(k * beta)
        if self.wide_chunks:
            c, n = self.chunk, self.n
            zero = jnp.zeros((c, HEAD_DIM), BF16)
            for (q, k, *_), kb in zip(heads, kbs):
                kbq = jnp.concatenate([kb, q], axis=1).astype(BF16)
                lhs = jnp.concatenate(
                    [jnp.concatenate([kbq[i * c:(i + 1) * c, :HEAD_DIM] for i in range(n)], axis=1),
                     jnp.concatenate([kbq[i * c:(i + 1) * c, HEAD_DIM:] for i in range(n)], axis=1)],
                    axis=0)
                kb16 = k.astype(BF16)
                rhs = jnp.concatenate(
                    [jnp.concatenate([kb16[i * c:(i + 1) * c] if j == i else zero for j in range(n)],
                                     axis=1) for i in range(n)], axis=0)
                gs.append(_mm_nt(lhs, rhs))
            yield
            lps = [jnp.where(strict, g[:c] * dec, 0.0) for g, dec in zip(gs, decays)]
            qkps = [g[c:] * dec for g, dec in zip(gs, decays)]
        else:
            for (q, k, *_), kb in zip(heads, kbs):
                gs.append(_mm_nt(jnp.concatenate([kb, q], axis=0), k))
            yield
            lps = [jnp.where(strict, self.pack(g[:self.rows]) * dec, 0.0) for g, dec in zip(gs, decays)]
            qkps = [self.pack(g[self.rows:]) * dec for g, dec in zip(gs, decays)]
        tinvs = yield from self.unit_lower_inverse(lps, nil)
        out = []
        for (q, k, v, beta, gc_col, _), kb, tinv, qkp in zip(heads, kbs, tinvs, qkps):
            eg = jnp.exp(gc_col)
            uw = _mm(self.block_diag(tinv), jnp.concatenate([v * beta, kb * eg], axis=1))
            out.append((uw[:, :HEAD_DIM], uw[:, HEAD_DIM:], q * eg, qkp))
        yield
        return out


def _branch_s(buf_ref, w_ref, tail, rows, lead=()):
    n = len(lead)
    acc = None
    for j in range(SCONV_W):
        idx = lead + (pl.ds(tail - (SCONV_W - 1) + j, rows), slice(None))
        term = buf_ref[idx] * w_ref[j:j + 1, :].reshape((1,) * n + (1, D_SCONV))
        acc = term if acc is None else acc + term
    return acc


def _conformer(ubuf_ref, w_ref, b_ref, g_ref, beta_ref, tail, rows, lead=()):
    n = len(lead)
    shp = (1,) * n + (1, D_CONF)
    acc = None
    for j in range(CONF_W):
        idx = lead + (pl.ds(tail - (CONF_W - 1) + j, rows), slice(None))
        term = ubuf_ref[idx] * w_ref[j:j + 1, :].reshape(shp)
        acc = term if acc is None else acc + term
    yc = acc + b_ref[...].reshape(shp)
    mu = jnp.mean(yc, axis=-1, keepdims=True)
    var = jnp.mean(jnp.square(yc - mu), axis=-1, keepdims=True)
    yc = (yc - mu) * lax.rsqrt(var + EPS) * g_ref[...].reshape(shp) + beta_ref[...].reshape(shp)
    return _silu(yc)


def _weight_specs(wts, layer_of, buffers=2):
    specs = []
    for w in wts[:-1]:
        tail = (0,) * (w.ndim - 1)
        specs.append(pl.BlockSpec((None,) + w.shape[1:], lambda *ids, tail=tail: (layer_of(*ids),) + tail,
                                  pipeline_mode=pl.Buffered(buffers)))
    specs.append(pl.BlockSpec(wts[-1].shape, lambda *ids: (0,) * wts[-1].ndim))
    return specs


def _pipe_kernel(final_norm, nt,
                 x_ref, ng_ref, win_ref, cw_ref, alog_ref, dt_ref, dng_ref,
                 sw_ref, ccw_ref, ccb_ref, clg_ref, clb_ref, wout_ref, fg_ref,
                 y_ref, s_out_ref, qt_ref, st_ref, ct_ref,
                 qbuf, sbuf, s_scr, xs, qa, bg, rest, cbuf, cshift, obuf):
    g = pl.program_id(0)
    n_tiles = pl.num_programs(0) - 1
    tt = TIME_TILE
    nchunk = tt // CHUNK
    npair = N_DHEADS // 2
    t1 = jnp.minimum(g, n_tiles - 1) % nt
    t2 = jnp.maximum(g - 1, 0) % nt
    p = g % 2
    q = 1 - p

    @pl.when(g == 0)
    def _():
        xs[1] = jnp.zeros(xs.shape[1:], F32)
        qa[1] = jnp.zeros(qa.shape[1:], F32)
        bg[1] = jnp.zeros(bg.shape[1:], F32)
        rest[1] = jnp.zeros(rest.shape[1:], F32)
        cbuf[1] = jnp.zeros(cbuf.shape[1:], F32)
        cshift[1] = jnp.zeros(cshift.shape[1:], F32)
        obuf[1] = jnp.zeros(obuf.shape[1:], F32)

    @pl.when(t1 == 0)
    def _():
        qbuf[0:SUBLANES, :] = jnp.zeros((SUBLANES, 3 * D_DELTA), F32)
        sbuf[0:SUBLANES, :] = jnp.zeros((SUBLANES, D_SCONV), F32)

    @pl.when(t2 == 0)
    def _():
        s_scr[...] = jnp.zeros(s_scr.shape, F32)

    def stage1():
        x = x_ref[0]
        xs[p] = x
        h = _rms_rows(x, ng_ref[...]).astype(BF16)
        yield
        for c0 in range(0, 3 * D_DELTA, 2 * LANES):
            qbuf[SUBLANES:SUBLANES + tt, c0:c0 + 2 * LANES] = jnp.dot(
                h, win_ref[:, c0:c0 + 2 * LANES], preferred_element_type=F32)
            yield
        for c0 in range(0, R_WIDTH, 2 * LANES):
            c1 = min(c0 + 2 * LANES, R_WIDTH)
            rest[p, :, c0:c1] = jnp.dot(h, win_ref[:, W_REST + c0:W_REST + c1],
                                        preferred_element_type=F32)
            yield

        def conv_act(c0):
            full = qbuf[:, c0:c0 + HEAD_DIM]
            acc = full[SUBLANES:] * cw_ref[QK_CONV - 1:QK_CONV, c0:c0 + HEAD_DIM]
            for s in range(1, QK_CONV):
                acc = acc + (pltpu.roll(full, s, axis=0)[SUBLANES:]
                             * cw_ref[QK_CONV - 1 - s:QK_CONV - s, c0:c0 + HEAD_DIM])
            return _silu(acc)

        for hd in range(N_DHEADS):
            qh = conv_act(hd * HEAD_DIM)
            qa[p, :, hd * HEAD_DIM:(hd + 1) * HEAD_DIM] = (
                qh * lax.rsqrt(jnp.sum(qh * qh, axis=-1, keepdims=True) + EPS) * (HEAD_DIM ** -0.5))
            kh = conv_act(D_DELTA + hd * HEAD_DIM)
            qa[p, :, D_DELTA + hd * HEAD_DIM:D_DELTA + (hd + 1) * HEAD_DIM] = (
                kh * lax.rsqrt(jnp.sum(kh * kh, axis=-1, keepdims=True) + EPS))
            qa[p, :, 2 * D_DELTA + hd * HEAD_DIM:2 * D_DELTA + (hd + 1) * HEAD_DIM] = conv_act(
                2 * D_DELTA + hd * HEAD_DIM)
            yield

        row = lax.broadcasted_iota(jnp.int32, (tt, LANES), 0)
        ba = rest[p, :, R_BA:R_BA + LANES]
        bg[p, 0] = _sigmoid(ba)
        bg[p, 1] = _chunk_cumsum(-jnp.exp(alog_ref[...]) * _softplus(ba + dt_ref[...]), CHUNK, row)
        yield

        sbuf[SUBLANES:SUBLANES + tt, :] = (rest[p, :, R_SC:R_SC + D_SCONV]
                                           * rest[p, :, R_SX:R_SX + D_SCONV])
        obuf[p, :, D_DELTA:D_DELTA + D_SCONV] = (
            rest[p, :, R_SB:R_SB + D_SCONV] * _branch_s(sbuf, sw_ref, SUBLANES, tt)
            * _silu(rest[p, :, R_GATE_S:R_GATE_S + D_SCONV]))
        yield

        prev_tail = jnp.where(t1 == 0, 0.0, cbuf[q, tt:tt + CTAIL, :])
        cbuf[p, 0:CTAIL, :] = prev_tail
        cbuf[p, CTAIL:CTAIL + tt, :] = (rest[p, :, R_GA:R_GA + D_CONF]
                                        * _sigmoid(rest[p, :, R_GB:R_GB + D_CONF]))
        yield
        for sh in range(1, SUBLANES):
            cshift[p, sh - 1] = cbuf[p, pl.ds(sh, tt + CTAIL - SUBLANES), :]
        yield

        qbuf[0:SUBLANES, :] = qbuf[tt:tt + SUBLANES, :]
        sbuf[0:SUBLANES, :] = sbuf[tt:tt + SUBLANES, :]

    def stage2():
        beta_all = bg[q, 0]
        gc = bg[q, 1]
        gct = gc.T
        gl_rows = jnp.concatenate(
            [jnp.broadcast_to(gc[(c + 1) * CHUNK - 1:(c + 1) * CHUNK, :], (CHUNK, LANES))
             for c in range(nchunk)], axis=0)
        pk = _Packed(CHUNK, tt, True)
        heads, kdecs = [], []
        for hd in range(N_DHEADS):
            lane = N_DHEADS + hd
            gcol = gc[:, lane:lane + 1]
            kh = qa[q, :, D_DELTA + hd * HEAD_DIM:D_DELTA + (hd + 1) * HEAD_DIM]
            heads.append((qa[q, :, hd * HEAD_DIM:(hd + 1) * HEAD_DIM], kh,
                          qa[q, :, 2 * D_DELTA + hd * HEAD_DIM:2 * D_DELTA + (hd + 1) * HEAD_DIM],
                          beta_all[:, hd:hd + 1], gcol, gct[lane:lane + 1, :]))
            kdecs.append(kh * jnp.exp(gl_rows[:, lane:lane + 1] - gcol))
        yield
        prepped = yield from pk.delta_prep(heads, CHUNK)

        us, lhss, kdts, qks = [], [], [], []
        low = (lax.broadcasted_iota(jnp.int32, (CHUNK, tt), 1) % LANES) < CHUNK
        for pr in range(npair):
            (u0, w0, qg0, qkp0), (u1, w1, qg1, qkp1) = prepped[2 * pr], prepped[2 * pr + 1]
            us.append(jnp.concatenate([u0, u1], axis=1))
            w = jnp.concatenate([w0, w1], axis=1).astype(BF16)
            qg = jnp.concatenate([qg0, qg1], axis=1).astype(BF16)
            lhss.append([jnp.concatenate([w[c * CHUNK:(c + 1) * CHUNK], qg[c * CHUNK:(c + 1) * CHUNK]],
                                         axis=0) for c in range(nchunk)])
            kst = jnp.concatenate([kd[c * CHUNK:(c + 1) * CHUNK] for c in range(nchunk)
                                   for kd in (kdecs[2 * pr], kdecs[2 * pr + 1])], axis=0)
            kdts.append(kst.T.astype(BF16))
            qks.append((jnp.where(low, qkp0, pltpu.roll(qkp1, CHUNK, axis=1)).astype(BF16),
                        jnp.where(low, pltpu.roll(qkp0, tt - CHUNK, axis=1), qkp1).astype(BF16)))
        yield

        ss = [s_scr[pr] for pr in range(npair)]
        for c in range(nchunk):
            cs = slice(c * CHUNK, (c + 1) * CHUNK)
            rs = [jnp.dot(lhss[pr][c], _pair_diag(s[:, :HEAD_DIM], s[:, HEAD_DIM:]).astype(BF16),
                          preferred_element_type=F32) for pr, s in enumerate(ss)]
            yield
            vns = [us[pr][cs] - r[:CHUNK] for pr, r in enumerate(rs)]
            v_bds = [_pair_diag(vn[:, :HEAD_DIM], vn[:, HEAD_DIM:]).astype(BF16) for vn in vns]
            upds = [jnp.dot(kdts[pr][:, 2 * c * CHUNK:2 * (c + 1) * CHUNK], v_bd,
                            preferred_element_type=F32) for pr, v_bd in enumerate(v_bds)]
            tile = slice((c // 2) * LANES, (c // 2 + 1) * LANES)
            o_intra = [jnp.dot(qks[pr][c % 2][:, tile], v_bd, preferred_element_type=F32)
                       for pr, v_bd in enumerate(v_bds)]
            ge = (c + 1) * CHUNK - 1
            for pr in range(npair):
                egl = jnp.concatenate(
                    [jnp.broadcast_to(jnp.exp(gc[ge:ge + 1, N_DHEADS + hh:N_DHEADS + hh + 1]),
                                      (1, HEAD_DIM)) for hh in (2 * pr, 2 * pr + 1)], axis=1)
                ss[pr] = ss[pr] * egl + upds[pr]
                o_pair = rs[pr][CHUNK:] + o_intra[pr]
                for hl in range(2):
                    hd = 2 * pr + hl
                    o = _rms_rows(o_pair[:, hl * HEAD_DIM:(hl + 1) * HEAD_DIM], dng_ref[...])
                    gate = rest[q, cs, R_GATE_D + hd * HEAD_DIM:R_GATE_D + (hd + 1) * HEAD_DIM]
                    obuf[q, cs, hd * HEAD_DIM:(hd + 1) * HEAD_DIM] = o * _silu(gate)
            yield
            acc = None
            for j in range(CONF_W):
                off = CTAIL - (CONF_W - 1) + j
                start = c * CHUNK + (off // SUBLANES) * SUBLANES
                sh = off % SUBLANES
                src = (cbuf[q, start:start + CHUNK, :] if sh == 0
                       else cshift[q, sh - 1, start:start + CHUNK, :])
                term = src * ccw_ref[j:j + 1, :]
                acc = term if acc is None else acc + term
            yc = acc + ccb_ref[...]
            mu = jnp.mean(yc, axis=-1, keepdims=True)
            var = jnp.mean(jnp.square(yc - mu), axis=-1, keepdims=True)
            yc = (yc - mu) * lax.rsqrt(var + EPS) * clg_ref[...] + clb_ref[...]
            obuf[q, cs, D_DELTA + D_SCONV:] = _silu(yc) * _silu(rest[q, cs, R_GATE_C:R_GATE_C + D_CONF])
            yield
        for pr in range(npair):
            s_scr[pr] = ss[pr]

        y = xs[q] + jnp.dot(obuf[q].astype(BF16), wout_ref[...], preferred_element_type=F32)
        if final_norm:
            y = _rms_rows(y, fg_ref[...])
        y_ref[0] = y

    _interleave(stage1(), stage2(), steps=(2, 1))

    @pl.when(jnp.logical_and(g >= 1, t2 == nt - 1))
    def _():
        for pr in range(npair):
            s_out_ref[0, 2 * pr] = s_scr[pr, :, :HEAD_DIM]
            s_out_ref[0, 2 * pr + 1] = s_scr[pr, :, HEAD_DIM:]

    @pl.when(jnp.logical_and(g < n_tiles, t1 == nt - 1))
    def _():
        qt_ref[0] = qbuf[0:SUBLANES, :]
        st_ref[0] = sbuf[0:SUBLANES, :]
        ct_ref[0] = cbuf[p, tt:tt + CTAIL, :]


def _prompt_layer_pipelined(x, wts, layer, final_norm):
    b, t, d = x.shape
    nt = t // TIME_TILE
    n_tiles = b * nt
    npair = N_DHEADS // 2

    def tile1(g):
        g1 = jnp.minimum(g, n_tiles - 1)
        return g1 // nt, g1 % nt

    def tile2(g):
        g2 = jnp.maximum(g - 1, 0)
        return g2 // nt, g2 % nt

    out_shape = (
        jax.ShapeDtypeStruct((b, t, d), F32),
        jax.ShapeDtypeStruct((b, N_DHEADS, HEAD_DIM, HEAD_DIM), F32),
        jax.ShapeDtypeStruct((b, SUBLANES, 3 * D_DELTA), F32),
        jax.ShapeDtypeStruct((b, SUBLANES, D_SCONV), F32),
        jax.ShapeDtypeStruct((b, CTAIL, D_CONF), F32),
    )
    out_specs = (
        pl.BlockSpec((1, TIME_TILE, d), lambda g: tile2(g) + (0,)),
        pl.BlockSpec((1, N_DHEADS, HEAD_DIM, HEAD_DIM), lambda g: (tile2(g)[0], 0, 0, 0)),
        pl.BlockSpec((1, SUBLANES, 3 * D_DELTA), lambda g: (tile1(g)[0], 0, 0)),
        pl.BlockSpec((1, SUBLANES, D_SCONV), lambda g: (tile1(g)[0], 0, 0)),
        pl.BlockSpec((1, CTAIL, D_CONF), lambda g: (tile1(g)[0], 0, 0)),
    )
    scratch = [
        pltpu.VMEM((TIME_TILE + SUBLANES, 3 * D_DELTA), F32),
        pltpu.VMEM((TIME_TILE + SUBLANES, D_SCONV), F32),
        pltpu.VMEM((npair, HEAD_DIM, 2 * HEAD_DIM), F32),
        pltpu.VMEM((2, TIME_TILE, D_MODEL), F32),
        pltpu.VMEM((2, TIME_TILE, 3 * D_DELTA), F32),
        pltpu.VMEM((2, 2, TIME_TILE, LANES), F32),
        pltpu.VMEM((2, TIME_TILE, R_WIDTH), F32),
        pltpu.VMEM((2, TIME_TILE + CTAIL, D_CONF), F32),
        pltpu.VMEM((2, SUBLANES - 1, TIME_TILE + CTAIL - SUBLANES, D_CONF), F32),
        pltpu.VMEM((2, TIME_TILE, D_MODEL), F32),
    ]
    return pl.pallas_call(
        functools.partial(_pipe_kernel, final_norm, nt),
        grid=(n_tiles + 1,),
        in_specs=[pl.BlockSpec((1, TIME_TILE, d), lambda g: tile1(g) + (0,))]
        + _weight_specs(wts, lambda g: layer, buffers=1),
        out_specs=out_specs,
        out_shape=out_shape,
        scratch_shapes=scratch,
        compiler_params=pltpu.CompilerParams(
            dimension_semantics=("arbitrary",),
            vmem_limit_bytes=VMEM_LIMIT),
        name="prompt_layer",
    )(x, *wts)


def _sample_kernel(x_ref, sd_ref, sq_ref, ss_ref, sc_ref,
                   ng_ref, win_ref, cw_ref, alog_ref, dt_ref, dng_ref,
                   sw_ref, ccw_ref, ccb_ref, clg_ref, clb_ref, wout_ref, fg_ref,
                   y_ref, sd_out_ref, qt_ref, st_ref, ct_ref,
                   qbuf, sbuf, cbuf, obuf, ubuf, wqbuf, kdbuf, glbuf, oibuf, rest, xcarry):
    layer = pl.program_id(0)
    blk = pl.program_id(1)
    nb = SAMPLE_BATCH_BLOCK
    pad = SAMPLE_PAD
    ntok = pad // 2
    rows = nb * pad

    @pl.when(layer == 0)
    def _():
        xcarry[blk, :, 0:ntok, :] = x_ref[...]
        xcarry[blk, :, ntok:, :] = jnp.zeros((nb, pad - ntok, D_MODEL), F32)

    x = xcarry[blk].reshape(rows, D_MODEL)
    h = _rms_rows(x, ng_ref[...]).astype(BF16)
    qkv = jnp.dot(h, win_ref[:, :W_REST], preferred_element_type=F32)
    rest[...] = jnp.dot(h, win_ref[:, W_REST:], preferred_element_type=F32)
    row = lax.broadcasted_iota(jnp.int32, (rows, LANES), 0)
    valid = (row % pad) < (pad // 2)
    valid1 = valid[:, 0:1]

    qkv3 = qkv.reshape(nb, pad, 3 * D_DELTA)
    qt_ref[...] = qkv3[:, ntok - (QK_CONV - 1):ntok, :]
    qbuf[:, SUBLANES - (QK_CONV - 1):SUBLANES, :] = sq_ref[...]
    qbuf[:, SUBLANES:, :] = qkv3

    hs = (rest[:, R_SC:R_SC + D_SCONV] * rest[:, R_SX:R_SX + D_SCONV]).reshape(nb, pad, D_SCONV)
    st_ref[...] = hs[:, ntok - (SCONV_W - 1):ntok, :]
    sbuf[:, SUBLANES - (SCONV_W - 1):SUBLANES, :] = ss_ref[...]
    sbuf[:, SUBLANES:, :] = hs
    ys = _branch_s(sbuf, sw_ref, SUBLANES, pad, lead=(slice(None),))
    obuf[:, D_DELTA:D_DELTA + D_SCONV] = (rest[:, R_SB:R_SB + D_SCONV] * ys.reshape(rows, D_SCONV)
                                          * _silu(rest[:, R_GATE_S:R_GATE_S + D_SCONV]))

    uc = rest[:, R_GA:R_GA + D_CONF] * _sigmoid(rest[:, R_GB:R_GB + D_CONF])
    cbuf[:, CTAIL - (CONF_W - 1):CTAIL, :] = sc_ref[...]
    cbuf[:, CTAIL:, :] = uc.reshape(nb, pad, D_CONF)
    first = CTAIL + ntok - (CONF_W - 1)
    ct_ref[...] = cbuf[:, first:first + CONF_W - 1, :]
    yc = _conformer(cbuf, ccw_ref, ccb_ref, clg_ref, clb_ref, CTAIL, pad, lead=(slice(None),))
    obuf[:, D_DELTA + D_SCONV:] = yc.reshape(rows, D_CONF) * _silu(rest[:, R_GATE_C:R_GATE_C + D_CONF])

    ba = rest[:, R_BA:R_BA + LANES]
    beta_all = jnp.where(valid, _sigmoid(ba), 0.0)
    g_all = jnp.where(valid, -jnp.exp(alog_ref[...]) * _softplus(ba + dt_ref[...]), 0.0)
    gc = _chunk_cumsum(g_all, pad, row)
    gct = gc.T
    gl_all = jnp.broadcast_to(
        gc.reshape(nb, pad, LANES)[:, pad - 1:pad, :], (nb, pad, LANES)).reshape(rows, LANES)
    pk = _Packed(pad, rows, False)

    def conv_act(c0):
        acc = None
        for j in range(QK_CONV):
            term = (qbuf[:, pl.ds(SUBLANES - (QK_CONV - 1) + j, pad), c0:c0 + HEAD_DIM]
                    * cw_ref[j:j + 1, c0:c0 + HEAD_DIM].reshape(1, 1, HEAD_DIM))
            acc = term if acc is None else acc + term
        return jnp.where(valid1, _silu(acc.reshape(rows, HEAD_DIM)), 0.0)

    for pr in range(N_DHEADS // 2):
        h0, h1 = 2 * pr, 2 * pr + 1
        heads, kdecs, egls = [], [], []
        for hd in (h0, h1):
            q = conv_act(hd * HEAD_DIM)
            k = conv_act(D_DELTA + hd * HEAD_DIM)
            v = conv_act(2 * D_DELTA + hd * HEAD_DIM)
            q = q * lax.rsqrt(jnp.sum(q * q, axis=-1, keepdims=True) + EPS) * (HEAD_DIM ** -0.5)
            k = k * lax.rsqrt(jnp.sum(k * k, axis=-1, keepdims=True) + EPS)
            lane = N_DHEADS + hd
            gcol = gc[:, lane:lane + 1]
            heads.append((q, k, v, beta_all[:, hd:hd + 1], gcol, gct[lane:lane + 1, :]))
            gl = gl_all[:, lane:lane + 1]
            kdecs.append(k * jnp.exp(gl - gcol))
            egls.append(jnp.broadcast_to(jnp.exp(gl), (rows, HEAD_DIM)))
        (u0, w0, qg0, qkp0), (u1, w1, qg1, qkp1) = _run(pk.delta_prep(heads, pad // 2))
        (kd0, kd1), (egl0, egl1) = kdecs, egls
        ubuf[...] = jnp.concatenate([u0, u1], axis=1)
        wqbuf[:, 0:pad, :] = jnp.concatenate([w0, w1], axis=1).reshape(nb, pad, 2 * HEAD_DIM)
        wqbuf[:, pad:, :] = jnp.concatenate([qg0, qg1], axis=1).reshape(nb, pad, 2 * HEAD_DIM)
        kdbuf[:, 0:pad, :] = kd0.reshape(nb, pad, HEAD_DIM)
        kdbuf[:, pad:, :] = kd1.reshape(nb, pad, HEAD_DIM)
        glbuf[...] = jnp.concatenate([egl0, egl1], axis=1).reshape(nb, pad, 2 * HEAD_DIM)

        def body(it, carry):
            bis = [it * SAMPLE_SEQ_LOCKSTEP + t for t in range(SAMPLE_SEQ_LOCKSTEP)]
            r0s = [pl.multiple_of(bi * pad, pad) for bi in bis]
            s0s = [sd_ref[bi, h0] for bi in bis]
            s1s = [sd_ref[bi, h1] for bi in bis]
            rs = [_mm(wqbuf[bi], _pair_diag(s0, s1)) for bi, s0, s1 in zip(bis, s0s, s1s)]
            vns = [ubuf[pl.ds(r0, pad), :] - r[:pad] for r0, r in zip(r0s, rs)]
            upds = [_mm_tn(kdbuf[bi], _pair_diag(vn[:, :HEAD_DIM], vn[:, HEAD_DIM:]))
                    for bi, vn in zip(bis, vns)]
            for bi, r0, s0, s1, r, vn, upd in zip(bis, r0s, s0s, s1s, rs, vns, upds):
                ubuf[pl.ds(r0, pad), :] = vn
                oibuf[pl.ds(r0, pad), :] = r[pad:]
                egl = glbuf[bi][0:1, :]
                sd_out_ref[bi, h0] = s0 * egl[:, :HEAD_DIM] + upd[:, :HEAD_DIM]
                sd_out_ref[bi, h1] = s1 * egl[:, HEAD_DIM:] + upd[:, HEAD_DIM:]
            return carry

        lax.fori_loop(0, nb // SAMPLE_SEQ_LOCKSTEP, body, 0)
        for hh, qkp in ((h0, qkp0), (h1, qkp1)):
            ls = slice((hh - h0) * HEAD_DIM, (hh - h0 + 1) * HEAD_DIM)
            o = oibuf[:, ls] + _mm(pk.block_diag(qkp), ubuf[:, ls])
            o = _rms_rows(o, dng_ref[...])
            obuf[:, hh * HEAD_DIM:(hh + 1) * HEAD_DIM] = (
                o * _silu(rest[:, R_GATE_D + hh * HEAD_DIM:R_GATE_D + (hh + 1) * HEAD_DIM]))

    y = x + jnp.dot(obuf[...].astype(BF16), wout_ref[...], preferred_element_type=F32)
    xcarry[blk] = y.reshape(nb, pad, D_MODEL)
    is_last = layer == pl.num_programs(0) - 1
    y_ref[...] = jnp.where(is_last, _rms_rows(y, fg_ref[...]), y).reshape(nb, pad, D_MODEL)[:, 0:ntok, :]


def _sample_layers(x, sd, sq, ss, sc, wts):
    depth = sd.shape[0]
    b, ntok, _ = x.shape
    nb = SAMPLE_BATCH_BLOCK

    def bspec(shape):
        return pl.BlockSpec((nb,) + shape, lambda l, bi: (bi,) + (0,) * len(shape))

    def lspec(shape):
        return pl.BlockSpec((None, nb) + shape, lambda l, bi: (l, bi) + (0,) * len(shape))

    def lshape(shape):
        return jax.ShapeDtypeStruct((depth, b) + shape, F32)

    state_shapes = ((N_DHEADS, HEAD_DIM, HEAD_DIM), (QK_CONV - 1, 3 * D_DELTA),
                    (SCONV_W - 1, D_SCONV), (CONF_W - 1, D_CONF))
    out_shape = (lshape((ntok, D_MODEL)),) + tuple(lshape(s) for s in state_shapes)
    out_specs = (lspec((ntok, D_MODEL)),) + tuple(lspec(s) for s in state_shapes)
    rows = nb * SAMPLE_PAD
    scratch = [
        pltpu.VMEM((nb, 2 * SUBLANES, 3 * D_DELTA), F32),
        pltpu.VMEM((nb, 2 * SUBLANES, D_SCONV), F32),
        pltpu.VMEM((nb, CTAIL + SAMPLE_PAD, D_CONF), F32),
        pltpu.VMEM((rows, D_MODEL), F32),
        pltpu.VMEM((rows, 2 * HEAD_DIM), F32),
        pltpu.VMEM((nb, 2 * SAMPLE_PAD, 2 * HEAD_DIM), F32),
        pltpu.VMEM((nb, 2 * SAMPLE_PAD, HEAD_DIM), F32),
        pltpu.VMEM((nb, SAMPLE_PAD, 2 * HEAD_DIM), F32),
        pltpu.VMEM((rows, 2 * HEAD_DIM), F32),
        pltpu.VMEM((rows, R_WIDTH), F32),
        pltpu.VMEM((b // nb, nb, SAMPLE_PAD, D_MODEL), F32),
    ]
    in_specs = ([bspec((ntok, D_MODEL))] + [lspec(s) for s in state_shapes]
                + _weight_specs(wts, lambda l, bi: l))
    return pl.pallas_call(
        _sample_kernel,
        grid=(depth, b // nb),
        in_specs=in_specs,
        out_specs=out_specs,
        out_shape=out_shape,
        scratch_shapes=scratch,
        compiler_params=pltpu.CompilerParams(
            dimension_semantics=("arbitrary", "arbitrary"),
            vmem_limit_bytes=VMEM_LIMIT),
        name="sample_layers",
    )(x, sd, sq, ss, sc, *wts)


def _reorder_kernel(w_ref, o_ref):
    n_qkv = 3 * D_DELTA
    n_ba = 2 * N_DHEADS
    rows = w_ref.shape[0]
    o_ref[:, 0:n_qkv] = w_ref[:, 0:n_qkv].astype(BF16)
    o_ref[:, n_qkv:n_qkv + R_BA] = w_ref[:, n_qkv + n_ba:].astype(BF16)
    o_ref[:, n_qkv + R_BA:] = jnp.concatenate(
        [w_ref[:, n_qkv:n_qkv + n_ba], jnp.zeros((rows, LANES - n_ba), F32)], axis=1).astype(BF16)


def _reorder_in_proj(w_in):
    depth, d, d_in = w_in.shape
    rows = 256
    return pl.pallas_call(
        _reorder_kernel,
        grid=(depth, d // rows),
        in_specs=[pl.BlockSpec((None, rows, d_in), lambda l, r: (l, r, 0))],
        out_specs=pl.BlockSpec((None, rows, W_REST + R_WIDTH), lambda l, r: (l, r, 0)),
        out_shape=jax.ShapeDtypeStruct((depth, d, W_REST + R_WIDTH), BF16),
        compiler_params=pltpu.CompilerParams(dimension_semantics=("arbitrary", "arbitrary")),
        name="reorder_in_proj",
    )(w_in)


def _stacked_weights(norm_g, w_in, conv_qkv_w, a_log, dt_bias, delta_norm_g, sconv_w,
                     cconv_w, cconv_b, cln_g, cln_b, w_out, final_norm_g):
    depth = w_in.shape[0]
    w_all = _reorder_in_proj(w_in)
    lane_pad = ((0, 0), (N_DHEADS, LANES - 2 * N_DHEADS))
    return (
        norm_g.reshape(depth, 1, D_MODEL),
        w_all,
        conv_qkv_w,
        jnp.pad(a_log, lane_pad).reshape(depth, 1, LANES),
        jnp.pad(dt_bias, lane_pad).reshape(depth, 1, LANES),
        delta_norm_g.reshape(depth, 1, HEAD_DIM),
        sconv_w,
        cconv_w,
        cconv_b.reshape(depth, 1, D_CONF),
        cln_g.reshape(depth, 1, D_CONF),
        cln_b.reshape(depth, 1, D_CONF),
        w_out.astype(BF16),
        final_norm_g.reshape(1, D_MODEL),
    )


def kernel(x_prompt, x_sample, state_delta, state_qkv_conv, state_sconv, state_cconv, norm_g, w_in, conv_qkv_w, a_log, dt_bias, delta_norm_g, sconv_w, cconv_w, cconv_b, cln_g, cln_b, w_out, final_norm_g):
    depth = w_in.shape[0]
    dec_seq = x_sample.shape[1]
    assert x_prompt.shape[1] % TIME_TILE == 0
    assert dec_seq == SAMPLE_PAD // 2 and x_sample.shape[0] % SAMPLE_BATCH_BLOCK == 0

    wts = _stacked_weights(norm_g, w_in, conv_qkv_w, a_log, dt_bias, delta_norm_g, sconv_w,
                           cconv_w, cconv_b, cln_g, cln_b, w_out, final_norm_g)

    xp = x_prompt
    p_outs = [[] for _ in range(4)]
    for l in range(depth):
        xp, pd, pq, ps, pc = _prompt_layer_pipelined(xp, wts, l, l == depth - 1)
        for acc, o in zip(p_outs, (pd, pq, ps, pc)):
            acc.append(o)
    pd, pq, ps, pc = (jnp.stack(o) for o in p_outs)

    xs, sd, sq, ss, sc = _sample_layers(x_sample, state_delta, state_qkv_conv, state_sconv,
                                        state_cconv, wts)
    return (xp, xs[depth - 1],
            pd, pq[:, :, SUBLANES - (QK_CONV - 1):, :], ps[:, :, SUBLANES - (SCONV_W - 1):, :],
            pc[:, :, CTAIL - (CONF_W - 1):, :],
            sd, sq, ss, sc)
```

```python
import functools

import jax
import jax.numpy as jnp
from jax import lax
from jax.experimental import pallas as pl
from jax.experimental.pallas import tpu as pltpu

D_MODEL = 1024
N_DHEADS = 4
HEAD_DIM = 128
D_DELTA = N_DHEADS * HEAD_DIM
D_SCONV = 256
D_CONF = 256
QK_CONV = 4
SCONV_W = 3
CONF_W = 31
CHUNK = 64
EPS = 1e-6

R_GATE_D = 0
R_SB = 512
R_SC = 768
R_SX = 1024
R_GATE_S = 1280
R_GA = 1536
R_GB = 1792
R_GATE_C = 2048
R_BA = 2304
R_WIDTH = 2432
W_REST = 3 * D_DELTA

SUBLANES = 8
LANES = 128
TIME_TILE = 256
SAMPLE_PAD = 8
SAMPLE_BATCH_BLOCK = 16
SAMPLE_SEQ_LOCKSTEP = 16
CTAIL = 32
VMEM_LIMIT = 56 * 1024 * 1024

F32 = jnp.float32
BF16 = jnp.bfloat16


def _mm(a, b):
    return jnp.dot(a.astype(BF16), b.astype(BF16), preferred_element_type=F32)


def _mm_nt(a, b):
    return lax.dot_general(a.astype(BF16), b.astype(BF16), (((1,), (1,)), ((), ())),
                           preferred_element_type=F32)


def _mm_tn(a, b):
    return lax.dot_general(a.astype(BF16), b.astype(BF16), (((0,), (0,)), ((), ())),
                           preferred_element_type=F32)


def _sigmoid(x):
    return 1.0 / (1.0 + jnp.exp(-x))


def _silu(x):
    return x * _sigmoid(x)


def _softplus(x):
    return jnp.maximum(x, 0.0) + jnp.log1p(jnp.exp(-jnp.abs(x)))


def _rms_rows(x, g):
    return x * lax.rsqrt(jnp.mean(x * x, axis=-1, keepdims=True) + EPS) * g


def _chunk_cumsum(g, chunk, row):
    pos = row % chunk
    s = 1
    while s < chunk:
        g = g + jnp.where(pos >= s, pltpu.roll(g, s, axis=0), 0.0)
        s *= 2
    return g


def _run(staged):
    try:
        while True:
            next(staged)
    except StopIteration as stop:
        return stop.value


def _interleave(*staged, steps=None):
    live = list(staged)
    steps = dict(zip(live, steps or [1] * len(live)))
    while live:
        for s in list(live):
            try:
                for _ in range(steps[s]):
                    next(s)
            except StopIteration:
                live.remove(s)


def _pair_diag(a, b):
    z = jnp.zeros_like(a)
    return jnp.concatenate([jnp.concatenate([a, z], axis=1), jnp.concatenate([z, b], axis=1)], axis=0)


class _Packed:
    def __init__(self, chunk, rows, wide_chunks):
        self.chunk, self.rows, self.n, self.wide_chunks = chunk, rows, rows // chunk, wide_chunks
        self.rr = lax.broadcasted_iota(jnp.int32, (chunk, rows), 0)
        lane = lax.broadcasted_iota(jnp.int32, (chunk, rows), 1)
        self.jl = lane % chunk
        self.lane_blk = lane // chunk
        ii = lax.broadcasted_iota(jnp.int32, (rows, rows), 0)
        jj = lax.broadcasted_iota(jnp.int32, (rows, rows), 1)
        mask = jnp.where((ii // chunk) == (jj // chunk), 1.0, 0.0).astype(F32)
        self.bd_mask = mask.astype(BF16) if wide_chunks else mask

    def pack(self, g):
        out = g[0:self.chunk]
        for c in range(1, self.n):
            out = jnp.where(self.lane_blk == c, g[c * self.chunk:(c + 1) * self.chunk], out)
        return out

    def col(self, v):
        shape = (self.chunk, self.rows)
        out = jnp.broadcast_to(v[0:self.chunk], shape)
        for c in range(1, self.n):
            out = jnp.where(self.lane_blk == c,
                            jnp.broadcast_to(v[c * self.chunk:(c + 1) * self.chunk], shape), out)
        return out

    def block_diag(self, xp):
        if self.wide_chunks:
            return jnp.concatenate([xp.astype(BF16)] * self.n, axis=0) * self.bd_mask
        return (jnp.concatenate([xp] * self.n, axis=0) * self.bd_mask).astype(BF16)

    def unit_lower_inverse(self, lps, nil):
        base = min(self.chunk, 16)
        same = (self.rr // base) == (self.jl // base)
        eye = jnp.where(self.rr == self.jl, 1.0, 0.0).astype(F32)
        ds = [jnp.where(same, lp, 0.0) for lp in lps]
        xs = [eye - d for d in ds]
        d_bds = [self.block_diag(d) for d in ds]
        p = 2
        while p < min(base, nil):
            ds = [_mm(d, d_bd) for d, d_bd in zip(ds, d_bds)]
            yield
            d_bds = [self.block_diag(d) for d in ds]
            xs = [x + _mm(x, d_bd) for x, d_bd in zip(xs, d_bds)]
            yield
            p *= 2
        size = base
        while size < self.chunk:
            big = (self.rr // (2 * size)) == (self.jl // (2 * size))
            off = jnp.logical_and(big, jnp.logical_not(same))
            xes = [_mm(x, self.block_diag(jnp.where(off, lp, 0.0))) for x, lp in zip(xs, lps)]
            yield
            xs = [x - _mm(xe, self.block_diag(x)) for x, xe in zip(xs, xes)]
            yield
            same = big
            size *= 2
        return xs

    def delta_prep(self, heads, nil):
        tril = self.rr >= self.jl
        strict = self.rr > self.jl
        decays, kbs, gs = [], [], []
        for q, k, v, beta, gc_col, gc_row in heads:
            diff = self.col(gc_col) - gc_row
            decays.append(jnp.where(tril, jnp.exp(jnp.where(tril, diff, 0.0)), 0.0))
            kbs.append(k * beta)
        if self.wide_chunks:
            c, n = self.chunk, self.n
            zero = jnp.zeros((c, HEAD_DIM), BF16)
            for (q, k, *_), kb in zip(heads, kbs):
                kbq = jnp.concatenate([kb, q], axis=1).astype(BF16)
                lhs = jnp.concatenate(
                    [jnp.concatenate([kbq[i * c:(i + 1) * c, :HEAD_DIM] for i in range(n)], axis=1),
                     jnp.concatenate([kbq[i * c:(i + 1) * c, HEAD_DIM:] for i in range(n)], axis=1)],
                    axis=0)
                kb16 = k.astype(BF16)
                rhs = jnp.concatenate(
                    [jnp.concatenate([kb16[i * c:(i + 1) * c] if j == i else zero for j in range(n)],
                                     axis=1) for i in range(n)], axis=0)
                gs.append(_mm_nt(lhs, rhs))
            yield
            lps = [jnp.where(strict, g[:c] * dec, 0.0) for g, dec in zip(gs, decays)]
            qkps = [g[c:] * dec for g, dec in zip(gs, decays)]
        else:
            for (q, k, *_), kb in zip(heads, kbs):
                gs.append(_mm_nt(jnp.concatenate([kb, q], axis=0), k))
            yield
            lps = [jnp.where(strict, self.pack(g[:self.rows]) * dec, 0.0) for g, dec in zip(gs, decays)]
            qkps = [self.pack(g[self.rows:]) * dec for g, dec in zip(gs, decays)]
        tinvs = yield from self.unit_lower_inverse(lps, nil)
        out = []
        for (q, k, v, beta, gc_col, _), kb, tinv, qkp in zip(heads, kbs, tinvs, qkps):
            eg = jnp.exp(gc_col)
            uw = _mm(self.block_diag(tinv), jnp.concatenate([v * beta, kb * eg], axis=1))
            out.append((uw[:, :HEAD_DIM], uw[:, HEAD_DIM:], q * eg, qkp))
        yield
        return out


def _branch_s(buf_ref, w_ref, tail, rows, lead=()):
    n = len(lead)
    acc = None
    for j in range(SCONV_W):
        idx = lead + (pl.ds(tail - (SCONV_W - 1) + j, rows), slice(None))
        term = buf_ref[idx] * w_ref[j:j + 1, :].reshape((1,) * n + (1, D_SCONV))
        acc = term if acc is None else acc + term
    return acc


def _conformer(ubuf_ref, w_ref, b_ref, g_ref, beta_ref, tail, rows, lead=()):
    n = len(lead)
    shp = (1,) * n + (1, D_CONF)
    acc = None
    for j in range(CONF_W):
        idx = lead + (pl.ds(tail - (CONF_W - 1) + j, rows), slice(None))
        term = ubuf_ref[idx] * w_ref[j:j + 1, :].reshape(shp)
        acc = term if acc is None else acc + term
    yc = acc + b_ref[...].reshape(shp)
    mu = jnp.mean(yc, axis=-1, keepdims=True)
    var = jnp.mean(jnp.square(yc - mu), axis=-1, keepdims=True)
    yc = (yc - mu) * lax.rsqrt(var + EPS) * g_ref[...].reshape(shp) + beta_ref[...].reshape(shp)
    return _silu(yc)


def _weight_specs(wts, layer_of, buffers=2):
    specs = []
    for w in wts[:-1]:
        tail = (0,) * (w.ndim - 1)
        specs.append(pl.BlockSpec((None,) + w.shape[1:], lambda *ids, tail=tail: (layer_of(*ids),) + tail,
                                  pipeline_mode=pl.Buffered(buffers)))
    specs.append(pl.BlockSpec(wts[-1].shape, lambda *ids: (0,) * wts[-1].ndim))
    return specs


def _pipe_kernel(final_norm, nt,
                 x_ref, ng_ref, win_ref, cw_ref, alog_ref, dt_ref, dng_ref,
                 sw_ref, ccw_ref, ccb_ref, clg_ref, clb_ref, wout_ref, fg_ref,
                 y_ref, s_out_ref, qt_ref, st_ref, ct_ref,
                 qbuf, sbuf, s_scr, xs, qa, bg, rest, cbuf, cshift, obuf):
    g = pl.program_id(0)
    n_tiles = pl.num_programs(0) - 1
    tt = TIME_TILE
    nchunk = tt // CHUNK
    npair = N_DHEADS // 2
    t1 = jnp.minimum(g, n_tiles - 1) % nt
    t2 = jnp.maximum(g - 1, 0) % nt
    p = g % 2
    q = 1 - p

    @pl.when(g == 0)
    def _():
        xs[1] = jnp.zeros(xs.shape[1:], F32)
        qa[1] = jnp.zeros(qa.shape[1:], F32)
        bg[1] = jnp.zeros(bg.shape[1:], F32)
        rest[1] = jnp.zeros(rest.shape[1:], F32)
        cbuf[1] = jnp.zeros(cbuf.shape[1:], F32)
        cshift[1] = jnp.zeros(cshift.shape[1:], F32)
        obuf[1] = jnp.zeros(obuf.shape[1:], F32)

    @pl.when(t1 == 0)
    def _():
        qbuf[0:SUBLANES, :] = jnp.zeros((SUBLANES, 3 * D_DELTA), F32)
        sbuf[0:SUBLANES, :] = jnp.zeros((SUBLANES, D_SCONV), F32)

    @pl.when(t2 == 0)
    def _():
        s_scr[...] = jnp.zeros(s_scr.shape, F32)

    def stage1():
        x = x_ref[0]
        xs[p] = x
        h = _rms_rows(x, ng_ref[...]).astype(BF16)
        yield
        for c0 in range(0, 3 * D_DELTA, 2 * LANES):
            qbuf[SUBLANES:SUBLANES + tt, c0:c0 + 2 * LANES] = jnp.dot(
                h, win_ref[:, c0:c0 + 2 * LANES], preferred_element_type=F32)
            yield
        for c0 in range(0, R_WIDTH, 2 * LANES):
            c1 = min(c0 + 2 * LANES, R_WIDTH)
            rest[p, :, c0:c1] = jnp.dot(h, win_ref[:, W_REST + c0:W_REST + c1],
                                        preferred_element_type=F32)
            yield

        def conv_act(c0):
            full = qbuf[:, c0:c0 + HEAD_DIM]
            acc = full[SUBLANES:] * cw_ref[QK_CONV - 1:QK_CONV, c0:c0 + HEAD_DIM]
            for s in range(1, QK_CONV):
                acc = acc + (pltpu.roll(full, s, axis=0)[SUBLANES:]
                             * cw_ref[QK_CONV - 1 - s:QK_CONV - s, c0:c0 + HEAD_DIM])
            return _silu(acc)

        for hd in range(N_DHEADS):
            qh = conv_act(hd * HEAD_DIM)
            qa[p, :, hd * HEAD_DIM:(hd + 1) * HEAD_DIM] = (
                qh * lax.rsqrt(jnp.sum(qh * qh, axis=-1, keepdims=True) + EPS) * (HEAD_DIM ** -0.5))
            kh = conv_act(D_DELTA + hd * HEAD_DIM)
            qa[p, :, D_DELTA + hd * HEAD_DIM:D_DELTA + (hd + 1) * HEAD_DIM] = (
                kh * lax.rsqrt(jnp.sum(kh * kh, axis=-1, keepdims=True) + EPS))
            qa[p, :, 2 * D_DELTA + hd * HEAD_DIM:2 * D_DELTA + (hd + 1) * HEAD_DIM] = conv_act(
                2 * D_DELTA + hd * HEAD_DIM)
            yield

        row = lax.broadcasted_iota(jnp.int32, (tt, LANES), 0)
        ba = rest[p, :, R_BA:R_BA + LANES]
        bg[p, 0] = _sigmoid(ba)
        bg[p, 1] = _chunk_cumsum(-jnp.exp(alog_ref[...]) * _softplus(ba + dt_ref[...]), CHUNK, row)
        yield

        sbuf[SUBLANES:SUBLANES + tt, :] = (rest[p, :, R_SC:R_SC + D_SCONV]
                                           * rest[p, :, R_SX:R_SX + D_SCONV])
        obuf[p, :, D_DELTA:D_DELTA + D_SCONV] = (
            rest[p, :, R_SB:R_SB + D_SCONV] * _branch_s(sbuf, sw_ref, SUBLANES, tt)
            * _silu(rest[p, :, R_GATE_S:R_GATE_S + D_SCONV]))
        yield

        prev_tail = jnp.where(t1 == 0, 0.0, cbuf[q, tt:tt + CTAIL, :])
        cbuf[p, 0:CTAIL, :] = prev_tail
        cbuf[p, CTAIL:CTAIL + tt, :] = (rest[p, :, R_GA:R_GA + D_CONF]
                                        * _sigmoid(rest[p, :, R_GB:R_GB + D_CONF]))
        yield
        for sh in range(1, SUBLANES):
            cshift[p, sh - 1] = cbuf[p, pl.ds(sh, tt + CTAIL - SUBLANES), :]
        yield

        qbuf[0:SUBLANES, :] = qbuf[tt:tt + SUBLANES, :]
        sbuf[0:SUBLANES, :] = sbuf[tt:tt + SUBLANES, :]

    def stage2():
        beta_all = bg[q, 0]
        gc = bg[q, 1]
        gct = gc.T
        gl_rows = jnp.concatenate(
            [jnp.broadcast_to(gc[(c + 1) * CHUNK - 1:(c + 1) * CHUNK, :], (CHUNK, LANES))
             for c in range(nchunk)], axis=0)
        pk = _Packed(CHUNK, tt, True)
        heads, kdecs = [], []
        for hd in range(N_DHEADS):
            lane = N_DHEADS + hd
            gcol = gc[:, lane:lane + 1]
            kh = qa[q, :, D_DELTA + hd * HEAD_DIM:D_DELTA + (hd + 1) * HEAD_DIM]
            heads.append((qa[q, :, hd * HEAD_DIM:(hd + 1) * HEAD_DIM], kh,
                          qa[q, :, 2 * D_DELTA + hd * HEAD_DIM:2 * D_DELTA + (hd + 1) * HEAD_DIM],
                          beta_all[:, hd:hd + 1], gcol, gct[lane:lane + 1, :]))
            kdecs.append(kh * jnp.exp(gl_rows[:, lane:lane + 1] - gcol))
        yield
        prepped = yield from pk.delta_prep(heads, CHUNK)

        us, lhss, kdts, qks = [], [], [], []
        low = (lax.broadcasted_iota(jnp.int32, (CHUNK, tt), 1) % LANES) < CHUNK
        for pr in range(npair):
            (u0, w0, qg0, qkp0), (u1, w1, qg1, qkp1) = prepped[2 * pr], prepped[2 * pr + 1]
            us.append(jnp.concatenate([u0, u1], axis=1))
            w = jnp.concatenate([w0, w1], axis=1).astype(BF16)
            qg = jnp.concatenate([qg0, qg1], axis=1).astype(BF16)
            lhss.append([jnp.concatenate([w[c * CHUNK:(c + 1) * CHUNK], qg[c * CHUNK:(c + 1) * CHUNK]],
                                         axis=0) for c in range(nchunk)])
            kst = jnp.concatenate([kd[c * CHUNK:(c + 1) * CHUNK] for c in range(nchunk)
                                   for kd in (kdecs[2 * pr], kdecs[2 * pr + 1])], axis=0)
            kdts.append(kst.T.astype(BF16))
            qks.append((jnp.where(low, qkp0, pltpu.roll(qkp1, CHUNK, axis=1)).astype(BF16),
                        jnp.where(low, pltpu.roll(qkp0, tt - CHUNK, axis=1), qkp1).astype(BF16)))
        yield

        ss = [s_scr[pr] for pr in range(npair)]
        for c in range(nchunk):
            cs = slice(c * CHUNK, (c + 1) * CHUNK)
            rs = [jnp.dot(lhss[pr][c], _pair_diag(s[:, :HEAD_DIM], s[:, HEAD_DIM:]).astype(BF16),
                          preferred_element_type=F32) for pr, s in enumerate(ss)]
            yield
            vns = [us[pr][cs] - r[:CHUNK] for pr, r in enumerate(rs)]
            v_bds = [_pair_diag(vn[:, :HEAD_DIM], vn[:, HEAD_DIM:]).astype(BF16) for vn in vns]
            upds = [jnp.dot(kdts[pr][:, 2 * c * CHUNK:2 * (c + 1) * CHUNK], v_bd,
                            preferred_element_type=F32) for pr, v_bd in enumerate(v_bds)]
            tile = slice((c // 2) * LANES, (c // 2 + 1) * LANES)
            o_intra = [jnp.dot(qks[pr][c % 2][:, tile], v_bd, preferred_element_type=F32)
                       for pr, v_bd in enumerate(v_bds)]
            ge = (c + 1) * CHUNK - 1
            for pr in range(npair):
                egl = jnp.concatenate(
                    [jnp.broadcast_to(jnp.exp(gc[ge:ge + 1, N_DHEADS + hh:N_DHEADS + hh + 1]),
                                      (1, HEAD_DIM)) for hh in (2 * pr, 2 * pr + 1)], axis=1)
                ss[pr] = ss[pr] * egl + upds[pr]
                o_pair = rs[pr][CHUNK:] + o_intra[pr]
                for hl in range(2):
                    hd = 2 * pr + hl
                    o = _rms_rows(o_pair[:, hl * HEAD_DIM:(hl + 1) * HEAD_DIM], dng_ref[...])
                    gate = rest[q, cs, R_GATE_D + hd * HEAD_DIM:R_GATE_D + (hd + 1) * HEAD_DIM]
                    obuf[q, cs, hd * HEAD_DIM:(hd + 1) * HEAD_DIM] = o * _silu(gate)
            yield
            acc = None
            for j in range(CONF_W):
                off = CTAIL - (CONF_W - 1) + j
                start = c * CHUNK + (off // SUBLANES) * SUBLANES
                sh = off % SUBLANES
                src = (cbuf[q, start:start + CHUNK, :] if sh == 0
                       else cshift[q, sh - 1, start:start + CHUNK, :])
                term = src * ccw_ref[j:j + 1, :]
                acc = term if acc is None else acc + term
            yc = acc + ccb_ref[...]
            mu = jnp.mean(yc, axis=-1, keepdims=True)
            var = jnp.mean(jnp.square(yc - mu), axis=-1, keepdims=True)
            yc = (yc - mu) * lax.rsqrt(var + EPS) * clg_ref[...] + clb_ref[...]
            obuf[q, cs, D_DELTA + D_SCONV:] = _silu(yc) * _silu(rest[q, cs, R_GATE_C:R_GATE_C + D_CONF])
            yield
        for pr in range(npair):
            s_scr[pr] = ss[pr]

        y = xs[q] + jnp.dot(obuf[q].astype(BF16), wout_ref[...], preferred_element_type=F32)
        if final_norm:
            y = _rms_rows(y, fg_ref[...])
        y_ref[0] = y

    _interleave(stage1(), stage2(), steps=(2, 1))

    @pl.when(jnp.logical_and(g >= 1, t2 == nt - 1))
    def _():
        for pr in range(npair):
            s_out_ref[0, 2 * pr] = s_scr[pr, :, :HEAD_DIM]
            s_out_ref[0, 2 * pr + 1] = s_scr[pr, :, HEAD_DIM:]

    @pl.when(jnp.logical_and(g < n_tiles, t1 == nt - 1))
    def _():
        qt_ref[0] = qbuf[0:SUBLANES, :]
        st_ref[0] = sbuf[0:SUBLANES, :]
        ct_ref[0] = cbuf[p, tt:tt + CTAIL, :]


def _prompt_layer_pipelined(x, wts, layer, final_norm):
    b, t, d = x.shape
    nt = t // TIME_TILE
    n_tiles = b * nt
    npair = N_DHEADS // 2

    def tile1(g):
        g1 = jnp.minimum(g, n_tiles - 1)
        return g1 // nt, g1 % nt

    def tile2(g):
        g2 = jnp.maximum(g - 1, 0)
        return g2 // nt, g2 % nt

    out_shape = (
        jax.ShapeDtypeStruct((b, t, d), F32),
        jax.ShapeDtypeStruct((b, N_DHEADS, HEAD_DIM, HEAD_DIM), F32),
        jax.ShapeDtypeStruct((b, SUBLANES, 3 * D_DELTA), F32),
        jax.ShapeDtypeStruct((b, SUBLANES, D_SCONV), F32),
        jax.ShapeDtypeStruct((b, CTAIL, D_CONF), F32),
    )
    out_specs = (
        pl.BlockSpec((1, TIME_TILE, d), lambda g: tile2(g) + (0,)),
        pl.BlockSpec((1, N_DHEADS, HEAD_DIM, HEAD_DIM), lambda g: (tile2(g)[0], 0, 0, 0)),
        pl.BlockSpec((1, SUBLANES, 3 * D_DELTA), lambda g: (tile1(g)[0], 0, 0)),
        pl.BlockSpec((1, SUBLANES, D_SCONV), lambda g: (tile1(g)[0], 0, 0)),
        pl.BlockSpec((1, CTAIL, D_CONF), lambda g: (tile1(g)[0], 0, 0)),
    )
    scratch = [
        pltpu.VMEM((TIME_TILE + SUBLANES, 3 * D_DELTA), F32),
        pltpu.VMEM((TIME_TILE + SUBLANES, D_SCONV), F32),
        pltpu.VMEM((npair, HEAD_DIM, 2 * HEAD_DIM), F32),
        pltpu.VMEM((2, TIME_TILE, D_MODEL), F32),
        pltpu.VMEM((2, TIME_TILE, 3 * D_DELTA), F32),
        pltpu.VMEM((2, 2, TIME_TILE, LANES), F32),
        pltpu.VMEM((2, TIME_TILE, R_WIDTH), F32),
        pltpu.VMEM((2, TIME_TILE + CTAIL, D_CONF), F32),
        pltpu.VMEM((2, SUBLANES - 1, TIME_TILE + CTAIL - SUBLANES, D_CONF), F32),
        pltpu.VMEM((2, TIME_TILE, D_MODEL), F32),
    ]
    return pl.pallas_call(
        functools.partial(_pipe_kernel, final_norm, nt),
        grid=(n_tiles + 1,),
        in_specs=[pl.BlockSpec((1, TIME_TILE, d), lambda g: tile1(g) + (0,))]
        + _weight_specs(wts, lambda g: layer, buffers=1),
        out_specs=out_specs,
        out_shape=out_shape,
        scratch_shapes=scratch,
        compiler_params=pltpu.CompilerParams(
            dimension_semantics=("arbitrary",),
            vmem_limit_bytes=VMEM_LIMIT),
        name="prompt_layer",
    )(x, *wts)


def _sample_kernel(x_ref, sd_ref, sq_ref, ss_ref, sc_ref,
                   ng_ref, win_ref, cw_ref, alog_ref, dt_ref, dng_ref,
                   sw_ref, ccw_ref, ccb_ref, clg_ref, clb_ref, wout_ref, fg_ref,
                   y_ref, sd_out_ref, qt_ref, st_ref, ct_ref,
                   qbuf, sbuf, cbuf, obuf, ubuf, wqbuf, kdbuf, glbuf, oibuf, rest, xcarry):
    layer = pl.program_id(0)
    blk = pl.program_id(1)
    nb = SAMPLE_BATCH_BLOCK
    pad = SAMPLE_PAD
    ntok = pad // 2
    rows = nb * pad

    @pl.when(layer == 0)
    def _():
        xcarry[blk, :, 0:ntok, :] = x_ref[...]
        xcarry[blk, :, ntok:, :] = jnp.zeros((nb, pad - ntok, D_MODEL), F32)

    x = xcarry[blk].reshape(rows, D_MODEL)
    h = _rms_rows(x, ng_ref[...]).astype(BF16)
    qkv = jnp.dot(h, win_ref[:, :W_REST], preferred_element_type=F32)
    rest[...] = jnp.dot(h, win_ref[:, W_REST:], preferred_element_type=F32)
    row = lax.broadcasted_iota(jnp.int32, (rows, LANES), 0)
    valid = (row % pad) < (pad // 2)
    valid1 = valid[:, 0:1]

    qkv3 = qkv.reshape(nb, pad, 3 * D_DELTA)
    qt_ref[...] = qkv3[:, ntok - (QK_CONV - 1):ntok, :]
    qbuf[:, SUBLANES - (QK_CONV - 1):SUBLANES, :] = sq_ref[...]
    qbuf[:, SUBLANES:, :] = qkv3

    hs = (rest[:, R_SC:R_SC + D_SCONV] * rest[:, R_SX:R_SX + D_SCONV]).reshape(nb, pad, D_SCONV)
    st_ref[...] = hs[:, ntok - (SCONV_W - 1):ntok, :]
    sbuf[:, SUBLANES - (SCONV_W - 1):SUBLANES, :] = ss_ref[...]
    sbuf[:, SUBLANES:, :] = hs
    ys = _branch_s(sbuf, sw_ref, SUBLANES, pad, lead=(slice(None),))
    obuf[:, D_DELTA:D_DELTA + D_SCONV] = (rest[:, R_SB:R_SB + D_SCONV] * ys.reshape(rows, D_SCONV)
                                          * _silu(rest[:, R_GATE_S:R_GATE_S + D_SCONV]))

    uc = rest[:, R_GA:R_GA + D_CONF] * _sigmoid(rest[:, R_GB:R_GB + D_CONF])
    cbuf[:, CTAIL - (CONF_W - 1):CTAIL, :] = sc_ref[...]
    cbuf[:, CTAIL:, :] = uc.reshape(nb, pad, D_CONF)
    first = CTAIL + ntok - (CONF_W - 1)
    ct_ref[...] = cbuf[:, first:first + CONF_W - 1, :]
    yc = _conformer(cbuf, ccw_ref, ccb_ref, clg_ref, clb_ref, CTAIL, pad, lead=(slice(None),))
    obuf[:, D_DELTA + D_SCONV:] = yc.reshape(rows, D_CONF) * _silu(rest[:, R_GATE_C:R_GATE_C + D_CONF])

    ba = rest[:, R_BA:R_BA + LANES]
    beta_all = jnp.where(valid, _sigmoid(ba), 0.0)
    g_all = jnp.where(valid, -jnp.exp(alog_ref[...]) * _softplus(ba + dt_ref[...]), 0.0)
    gc = _chunk_cumsum(g_all, pad, row)
    gct = gc.T
    gl_all = jnp.broadcast_to(
        gc.reshape(nb, pad, LANES)[:, pad - 1:pad, :], (nb, pad, LANES)).reshape(rows, LANES)
    pk = _Packed(pad, rows, False)

    def conv_act(c0):
        acc = None
        for j in range(QK_CONV):
            term = (qbuf[:, pl.ds(SUBLANES - (QK_CONV - 1) + j, pad), c0:c0 + HEAD_DIM]
                    * cw_ref[j:j + 1, c0:c0 + HEAD_DIM].reshape(1, 1, HEAD_DIM))
            acc = term if acc is None else acc + term
        return jnp.where(valid1, _silu(acc.reshape(rows, HEAD_DIM)), 0.0)

    for pr in range(N_DHEADS // 2):
        h0, h1 = 2 * pr, 2 * pr + 1
        heads, kdecs, egls = [], [], []
        for hd in (h0, h1):
            q = conv_act(hd * HEAD_DIM)
            k = conv_act(D_DELTA + hd * HEAD_DIM)
            v = conv_act(2 * D_DELTA + hd * HEAD_DIM)
            q = q * lax.rsqrt(jnp.sum(q * q, axis=-1, keepdims=True) + EPS) * (HEAD_DIM ** -0.5)
            k = k * lax.rsqrt(jnp.sum(k * k, axis=-1, keepdims=True) + EPS)
            lane = N_DHEADS + hd
            gcol = gc[:, lane:lane + 1]
            heads.append((q, k, v, beta_all[:, hd:hd + 1], gcol, gct[lane:lane + 1, :]))
            gl = gl_all[:, lane:lane + 1]
            kdecs.append(k * jnp.exp(gl - gcol))
            egls.append(jnp.broadcast_to(jnp.exp(gl), (rows, HEAD_DIM)))
        (u0, w0, qg0, qkp0), (u1, w1, qg1, qkp1) = _run(pk.delta_prep(heads, pad // 2))
        (kd0, kd1), (egl0, egl1) = kdecs, egls
        ubuf[...] = jnp.concatenate([u0, u1], axis=1)
        wqbuf[:, 0:pad, :] = jnp.concatenate([w0, w1], axis=1).reshape(nb, pad, 2 * HEAD_DIM)
        wqbuf[:, pad:, :] = jnp.concatenate([qg0, qg1], axis=1).reshape(nb, pad, 2 * HEAD_DIM)
        kdbuf[:, 0:pad, :] = kd0.reshape(nb, pad, HEAD_DIM)
        kdbuf[:, pad:, :] = kd1.reshape(nb, pad, HEAD_DIM)
        glbuf[...] = jnp.concatenate([egl0, egl1], axis=1).reshape(nb, pad, 2 * HEAD_DIM)

        def body(it, carry):
            bis = [it * SAMPLE_SEQ_LOCKSTEP + t for t in range(SAMPLE_SEQ_LOCKSTEP)]
            r0s = [pl.multiple_of(bi * pad, pad) for bi in bis]
            s0s = [sd_ref[bi, h0] for bi in bis]
            s1s = [sd_ref[bi, h1] for bi in bis]
            rs = [_mm(wqbuf[bi], _pair_diag(s0, s1)) for bi, s0, s1 in zip(bis, s0s, s1s)]
            vns = [ubuf[pl.ds(r0, pad), :] - r[:pad] for r0, r in zip(r0s, rs)]
            upds = [_mm_tn(kdbuf[bi], _pair_diag(vn[:, :HEAD_DIM], vn[:, HEAD_DIM:]))
                    for bi, vn in zip(bis, vns)]
            for bi, r0, s0, s1, r, vn, upd in zip(bis, r0s, s0s, s1s, rs, vns, upds):
                ubuf[pl.ds(r0, pad), :] = vn
                oibuf[pl.ds(r0, pad), :] = r[pad:]
                egl = glbuf[bi][0:1, :]
                sd_out_ref[bi, h0] = s0 * egl[:, :HEAD_DIM] + upd[:, :HEAD_DIM]
                sd_out_ref[bi, h1] = s1 * egl[:, HEAD_DIM:] + upd[:, HEAD_DIM:]
            return carry

        lax.fori_loop(0, nb // SAMPLE_SEQ_LOCKSTEP, body, 0)
        for hh, qkp in ((h0, qkp0), (h1, qkp1)):
            ls = slice((hh - h0) * HEAD_DIM, (hh - h0 + 1) * HEAD_DIM)
            o = oibuf[:, ls] + _mm(pk.block_diag(qkp), ubuf[:, ls])
            o = _rms_rows(o, dng_ref[...])
            obuf[:, hh * HEAD_DIM:(hh + 1) * HEAD_DIM] = (
                o * _silu(rest[:, R_GATE_D + hh * HEAD_DIM:R_GATE_D + (hh + 1) * HEAD_DIM]))

    y = x + jnp.dot(obuf[...].astype(BF16), wout_ref[...], preferred_element_type=F32)
    xcarry[blk] = y.reshape(nb, pad, D_MODEL)
    is_last = layer == pl.num_programs(0) - 1
    y_ref[...] = jnp.where(is_last, _rms_rows(y, fg_ref[...]), y).reshape(nb, pad, D_MODEL)[:, 0:ntok, :]


def _sample_layers(x, sd, sq, ss, sc, wts):
    depth = sd.shape[0]
    b, ntok, _ = x.shape
    nb = SAMPLE_BATCH_BLOCK

    def bspec(shape):
        return pl.BlockSpec((nb,) + shape, lambda l, bi: (bi,) + (0,) * len(shape))

    def lspec(shape):
        return pl.BlockSpec((None, nb) + shape, lambda l, bi: (l, bi) + (0,) * len(shape))

    def lshape(shape):
        return jax.ShapeDtypeStruct((depth, b) + shape, F32)

    state_shapes = ((N_DHEADS, HEAD_DIM, HEAD_DIM), (QK_CONV - 1, 3 * D_DELTA),
                    (SCONV_W - 1, D_SCONV), (CONF_W - 1, D_CONF))
    out_shape = (lshape((ntok, D_MODEL)),) + tuple(lshape(s) for s in state_shapes)
    out_specs = (lspec((ntok, D_MODEL)),) + tuple(lspec(s) for s in state_shapes)
    rows = nb * SAMPLE_PAD
    scratch = [
        pltpu.VMEM((nb, 2 * SUBLANES, 3 * D_DELTA), F32),
        pltpu.VMEM((nb, 2 * SUBLANES, D_SCONV), F32),
        pltpu.VMEM((nb, CTAIL + SAMPLE_PAD, D_CONF), F32),
        pltpu.VMEM((rows, D_MODEL), F32),
        pltpu.VMEM((rows, 2 * HEAD_DIM), F32),
        pltpu.VMEM((nb, 2 * SAMPLE_PAD, 2 * HEAD_DIM), F32),
        pltpu.VMEM((nb, 2 * SAMPLE_PAD, HEAD_DIM), F32),
        pltpu.VMEM((nb, SAMPLE_PAD, 2 * HEAD_DIM), F32),
        pltpu.VMEM((rows, 2 * HEAD_DIM), F32),
        pltpu.VMEM((rows, R_WIDTH), F32),
        pltpu.VMEM((b // nb, nb, SAMPLE_PAD, D_MODEL), F32),
    ]
    in_specs = ([bspec((ntok, D_MODEL))] + [lspec(s) for s in state_shapes]
                + _weight_specs(wts, lambda l, bi: l))
    return pl.pallas_call(
        _sample_kernel,
        grid=(depth, b // nb),
        in_specs=in_specs,
        out_specs=out_specs,
        out_shape=out_shape,
        scratch_shapes=scratch,
        compiler_params=pltpu.CompilerParams(
            dimension_semantics=("arbitrary", "arbitrary"),
            vmem_limit_bytes=VMEM_LIMIT),
        name="sample_layers",
    )(x, sd, sq, ss, sc, *wts)


def _reorder_kernel(w_ref, o_ref):
    n_qkv = 3 * D_DELTA
    n_ba = 2 * N_DHEADS
    rows = w_ref.shape[0]
    o_ref[:, 0:n_qkv] = w_ref[:, 0:n_qkv].astype(BF16)
    o_ref[:, n_qkv:n_qkv + R_BA] = w_ref[:, n_qkv + n_ba:].astype(BF16)
    o_ref[:, n_qkv + R_BA:] = jnp.concatenate(
        [w_ref[:, n_qkv:n_qkv + n_ba], jnp.zeros((rows, LANES - n_ba), F32)], axis=1).astype(BF16)


def _reorder_in_proj(w_in):
    depth, d, d_in = w_in.shape
    rows = 256
    return pl.pallas_call(
        _reorder_kernel,
        grid=(depth, d // rows),
        in_specs=[pl.BlockSpec((None, rows, d_in), lambda l, r: (l, r, 0))],
        out_specs=pl.BlockSpec((None, rows, W_REST + R_WIDTH), lambda l, r: (l, r, 0)),
        out_shape=jax.ShapeDtypeStruct((depth, d, W_REST + R_WIDTH), BF16),
        compiler_params=pltpu.CompilerParams(dimension_semantics=("arbitrary", "arbitrary")),
        name="reorder_in_proj",
    )(w_in)


def _stacked_weights(norm_g, w_in, conv_qkv_w, a_log, dt_bias, delta_norm_g, sconv_w,
                     cconv_w, cconv_b, cln_g, cln_b, w_out, final_norm_g):
    depth = w_in.shape[0]
    w_all = _reorder_in_proj(w_in)
    lane_pad = ((0, 0), (N_DHEADS, LANES - 2 * N_DHEADS))
    return (
        norm_g.reshape(depth, 1, D_MODEL),
        w_all,
        conv_qkv_w,
        jnp.pad(a_log, lane_pad).reshape(depth, 1, LANES),
        jnp.pad(dt_bias, lane_pad).reshape(depth, 1, LANES),
        delta_norm_g.reshape(depth, 1, HEAD_DIM),
        sconv_w,
        cconv_w,
        cconv_b.reshape(depth, 1, D_CONF),
        cln_g.reshape(depth, 1, D_CONF),
        cln_b.reshape(depth, 1, D_CONF),
        w_out.astype(BF16),
        final_norm_g.reshape(1, D_MODEL),
    )


def kernel(x_prompt, x_sample, state_delta, state_qkv_conv, state_sconv, state_cconv, norm_g, w_in, conv_qkv_w, a_log, dt_bias, delta_norm_g, sconv_w, cconv_w, cconv_b, cln_g, cln_b, w_out, final_norm_g):
    depth = w_in.shape[0]
    dec_seq = x_sample.shape[1]
    assert x_prompt.shape[1] % TIME_TILE == 0
    assert dec_seq == SAMPLE_PAD // 2 and x_sample.shape[0] % SAMPLE_BATCH_BLOCK == 0

    wts = _stacked_weights(norm_g, w_in, conv_qkv_w, a_log, dt_bias, delta_norm_g, sconv_w,
                           cconv_w, cconv_b, cln_g, cln_b, w_out, final_norm_g)

    xp = x_prompt
    p_outs = [[] for _ in range(4)]
    for l in range(depth):
        xp, pd, pq, ps, pc = _prompt_layer_pipelined(xp, wts, l, l == depth - 1)
        for acc, o in zip(p_outs, (pd, pq, ps, pc)):
            acc.append(o)
    pd, pq, ps, pc = (jnp.stack(o) for o in p_outs)

    xs, sd, sq, ss, sc = _sample_layers(x_sample, state_delta, state_qkv_conv, state_sconv,
                                        state_cconv, wts)
    return (xp, xs[depth - 1],
            pd, pq[:, :, SUBLANES - (QK_CONV - 1):, :], ps[:, :, SUBLANES - (SCONV_W - 1):, :],
            pc[:, :, CTAIL - (CONF_W - 1):, :],
            sd, sq, ss, sc)
```

```python
import functools

import jax
import jax.numpy as jnp
from jax import lax
from jax.experimental import pallas as pl
from jax.experimental.pallas import tpu as pltpu

D_MODEL = 1024
N_DHEADS = 4
HEAD_DIM = 128
D_DELTA = N_DHEADS * HEAD_DIM
D_SCONV = 256
D_CONF = 256
QK_CONV = 4
SCONV_W = 3
CONF_W = 31
CHUNK = 64
EPS = 1e-6

R_GATE_D = 0
R_SB = 512
R_SC = 768
R_SX = 1024
R_GATE_S = 1280
R_GA = 1536
R_GB = 1792
R_GATE_C = 2048
R_BA = 2304
R_WIDTH = 2432
W_REST = 3 * D_DELTA

SUBLANES = 8
LANES = 128
TIME_TILE = 256
SAMPLE_PAD = 8
SAMPLE_BATCH_BLOCK = 16
SAMPLE_SEQ_LOCKSTEP = 16
CTAIL = 32
VMEM_LIMIT = 56 * 1024 * 1024

F32 = jnp.float32
BF16 = jnp.bfloat16


def _mm(a, b):
    return jnp.dot(a.astype(BF16), b.astype(BF16), preferred_element_type=F32)


def _mm_nt(a, b):
    return lax.dot_general(a.astype(BF16), b.astype(BF16), (((1,), (1,)), ((), ())),
                           preferred_element_type=F32)


def _mm_tn(a, b):
    return lax.dot_general(a.astype(BF16), b.astype(BF16), (((0,), (0,)), ((), ())),
                           preferred_element_type=F32)


def _sigmoid(x):
    return 1.0 / (1.0 + jnp.exp(-x))


def _silu(x):
    h = 0.5 * x
    return h + h * jnp.tanh(h)


def _softplus(x):
    return jnp.maximum(x, 0.0) + jnp.log1p(jnp.exp(-jnp.abs(x)))


def _rms_rows(x, g):
    return x * lax.rsqrt(jnp.mean(x * x, axis=-1, keepdims=True) + EPS) * g


def _chunk_cumsum(g, chunk, row):
    pos = row % chunk
    s = 1
    while s < chunk:
        g = g + jnp.where(pos >= s, pltpu.roll(g, s, axis=0), 0.0)
        s *= 2
    return g


def _run(staged):
    try:
        while True:
            next(staged)
    except StopIteration as stop:
        return stop.value


def _interleave(*staged, steps=None):
    live = list(staged)
    steps = dict(zip(live, steps or [1] * len(live)))
    while live:
        for s in list(live):
            try:
                for _ in range(steps[s]):
                    next(s)
            except StopIteration:
                live.remove(s)


def _pair_diag(a, b):
    z = jnp.zeros_like(a)
    return jnp.concatenate([jnp.concatenate([a, z], axis=1), jnp.concatenate([z, b], axis=1)], axis=0)


class _Packed:
    def __init__(self, chunk, rows, wide_chunks):
        self.chunk, self.rows, self.n, self.wide_chunks = chunk, rows, rows // chunk, wide_chunks
        self.rr = lax.broadcasted_iota(jnp.int32, (chunk, rows), 0)
        lane = lax.broadcasted_iota(jnp.int32, (chunk, rows), 1)
        self.jl = lane % chunk
        self.lane_blk = lane // chunk
        ii = lax.broadcasted_iota(jnp.int32, (rows, rows), 0)
        jj = lax.broadcasted_iota(jnp.int32, (rows, rows), 1)
        mask = jnp.where((ii // chunk) == (jj // chunk), 1.0, 0.0).astype(F32)
        self.bd_mask = mask.astype(BF16) if wide_chunks else mask

    def pack(self, g):
        out = g[0:self.chunk]
        for c in range(1, self.n):
            out = jnp.where(self.lane_blk == c, g[c * self.chunk:(c + 1) * self.chunk], out)
        return out

    def col(self, v):
        shape = (self.chunk, self.rows)
        out = jnp.broadcast_to(v[0:self.chunk], shape)
        for c in range(1, self.n):
            out = jnp.where(self.lane_blk == c,
                            jnp.broadcast_to(v[c * self.chunk:(c + 1) * self.chunk], shape), out)
        return out

    def block_diag(self, xp):
        if self.wide_chunks:
            return jnp.concatenate([xp.astype(BF16)] * self.n, axis=0) * self.bd_mask
        return (jnp.concatenate([xp] * self.n, axis=0) * self.bd_mask).astype(BF16)

    def unit_lower_inverse(self, lps, nil):
        base = min(self.chunk, 16)
        same = (self.rr // base) == (self.jl // base)
        eye = jnp.where(self.rr == self.jl, 1.0, 0.0).astype(F32)
        ds = [jnp.where(same, lp, 0.0) for lp in lps]
        xs = [eye - d for d in ds]
        d_bds = [self.block_diag(d) for d in ds]
        p = 2
        while p < min(base, nil):
            ds = [_mm(d, d_bd) for d, d_bd in zip(ds, d_bds)]
            yield
            d_bds = [self.block_diag(d) for d in ds]
            xs = [x + _mm(x, d_bd) for x, d_bd in zip(xs, d_bds)]
            yield
            p *= 2
        size = base
        while size < self.chunk:
            big = (self.rr // (2 * size)) == (self.jl // (2 * size))
            off = jnp.logical_and(big, jnp.logical_not(same))
            xes = [_mm(x, self.block_diag(jnp.where(off, lp, 0.0))) for x, lp in zip(xs, lps)]
            yield
            xs = [x - _mm(xe, self.block_diag(x)) for x, xe in zip(xs, xes)]
            yield
            same = big
            size *= 2
        return xs

    def delta_prep(self, heads, nil):
        tril = self.rr >= self.jl
        strict = self.rr > self.jl
        decays, kbs, gs = [], [], []
        for q, k, v, beta, gc_col, gc_row in heads:
            diff = self.col(gc_col) - gc_row
            decays.append(jnp.where(tril, jnp.exp(jnp.where(tril, diff, 0.0)), 0.0))
            kbs.append(k * beta)
        if self.wide_chunks:
            c, n = self.chunk, self.n
            zero = jnp.zeros((c, HEAD_DIM), BF16)
            for (q, k, *_), kb in zip(heads, kbs):
                kbq = jnp.concatenate([kb, q], axis=1).astype(BF16)
                lhs = jnp.concatenate(
                    [jnp.concatenate([kbq[i * c:(i + 1) * c, :HEAD_DIM] for i in range(n)], axis=1),
                     jnp.concatenate([kbq[i * c:(i + 1) * c, HEAD_DIM:] for i in range(n)], axis=1)],
                    axis=0)
                kb16 = k.astype(BF16)
                rhs = jnp.concatenate(
                    [jnp.concatenate([kb16[i * c:(i + 1) * c] if j == i else zero for j in range(n)],
                                     axis=1) for i in range(n)], axis=0)
                gs.append(_mm_nt(lhs, rhs))
            yield
            lps = [jnp.where(strict, g[:c] * dec, 0.0) for g, dec in zip(gs, decays)]
            qkps = [g[c:] * dec for g, dec in zip(gs, decays)]
        else:
            for (q, k, *_), kb in zip(heads, kbs):
                gs.append(_mm_nt(jnp.concatenate([kb, q], axis=0), k))
            yield
            lps = [jnp.where(strict, self.pack(g[:self.rows]) * dec, 0.0) for g, dec in zip(gs, decays)]
            qkps = [self.pack(g[self.rows:]) * dec for g, dec in zip(gs, decays)]
        tinvs = yield from self.unit_lower_inverse(lps, nil)
        out = []
        for (q, k, v, beta, gc_col, _), kb, tinv, qkp in zip(heads, kbs, tinvs, qkps):
            eg = jnp.exp(gc_col)
            uw = _mm(self.block_diag(tinv), jnp.concatenate([v * beta, kb * eg], axis=1))
            out.append((uw[:, :HEAD_DIM], uw[:, HEAD_DIM:], q * eg, qkp))
        yield
        return out


def _branch_s(buf_ref, w_ref, tail, rows, lead=()):
    n = len(lead)
    acc = None
    for j in range(SCONV_W):
        idx = lead + (pl.ds(tail - (SCONV_W - 1) + j, rows), slice(None))
        term = buf_ref[idx] * w_ref[j:j + 1, :].reshape((1,) * n + (1, D_SCONV))
        acc = term if acc is None else acc + term
    return acc


def _conformer(ubuf_ref, w_ref, b_ref, g_ref, beta_ref, tail, rows, lead=()):
    n = len(lead)
    shp = (1,) * n + (1, D_CONF)
    acc = None
    for j in range(CONF_W):
        idx = lead + (pl.ds(tail - (CONF_W - 1) + j, rows), slice(None))
        term = ubuf_ref[idx] * w_ref[j:j + 1, :].reshape(shp)
        acc = term if acc is None else acc + term
    yc = acc + b_ref[...].reshape(shp)
    mu = jnp.mean(yc, axis=-1, keepdims=True)
    var = jnp.mean(jnp.square(yc - mu), axis=-1, keepdims=True)
    yc = (yc - mu) * lax.rsqrt(var + EPS) * g_ref[...].reshape(shp) + beta_ref[...].reshape(shp)
    return _silu(yc)


def _weight_specs(wts, layer_of, buffers=2):
    specs = []
    for w in wts[:-1]:
        tail = (0,) * (w.ndim - 1)
        specs.append(pl.BlockSpec((None,) + w.shape[1:], lambda *ids, tail=tail: (layer_of(*ids),) + tail,
                                  pipeline_mode=pl.Buffered(buffers)))
    specs.append(pl.BlockSpec(wts[-1].shape, lambda *ids: (0,) * wts[-1].ndim))
    return specs


def _pipe_kernel(final_norm, nt,
                 x_ref, x2_ref, ng_ref, win_ref, cw_ref, alog_ref, dt_ref, dng_ref,
                 sw_ref, ccw_ref, ccb_ref, clg_ref, clb_ref, wout_ref, fg_ref,
                 y_ref, s_out_ref, qt_ref, st_ref, ct_ref,
                 qbuf, sbuf, s_scr, qa, bg, rest, cbuf, cshift, obuf):
    g = pl.program_id(0)
    n_tiles = pl.num_programs(0) - 1
    tt = TIME_TILE
    nchunk = tt // CHUNK
    npair = N_DHEADS // 2
    t1 = jnp.minimum(g, n_tiles - 1) % nt
    t2 = jnp.maximum(g - 1, 0) % nt
    p = g % 2
    q = 1 - p

    @pl.when(g == 0)
    def _():
        qa[1] = jnp.zeros(qa.shape[1:], F32)
        bg[1] = jnp.zeros(bg.shape[1:], F32)
        rest[1] = jnp.zeros(rest.shape[1:], F32)
        cbuf[1] = jnp.zeros(cbuf.shape[1:], F32)
        cshift[1] = jnp.zeros(cshift.shape[1:], F32)
        obuf[1] = jnp.zeros(obuf.shape[1:], F32)

    @pl.when(t1 == 0)
    def _():
        qbuf[0:SUBLANES, :] = jnp.zeros((SUBLANES, 3 * D_DELTA), F32)
        sbuf[0:SUBLANES, :] = jnp.zeros((SUBLANES, D_SCONV), F32)

    @pl.when(t2 == 0)
    def _():
        s_scr[...] = jnp.zeros(s_scr.shape, F32)

    def stage1():
        h = _rms_rows(x_ref[0], ng_ref[...]).astype(BF16)
        yield
        for c0 in range(0, 3 * D_DELTA, 2 * LANES):
            qbuf[SUBLANES:SUBLANES + tt, c0:c0 + 2 * LANES] = jnp.dot(
                h, win_ref[:, c0:c0 + 2 * LANES], preferred_element_type=F32)
            yield
        for c0 in range(0, R_WIDTH, 2 * LANES):
            c1 = min(c0 + 2 * LANES, R_WIDTH)
            rest[p, :, c0:c1] = jnp.dot(h, win_ref[:, W_REST + c0:W_REST + c1],
                                        preferred_element_type=F32)
            yield

        def conv_act(c0):
            full = qbuf[:, c0:c0 + HEAD_DIM]
            acc = full[SUBLANES:] * cw_ref[QK_CONV - 1:QK_CONV, c0:c0 + HEAD_DIM]
            for s in range(1, QK_CONV):
                acc = acc + (pltpu.roll(full, s, axis=0)[SUBLANES:]
                             * cw_ref[QK_CONV - 1 - s:QK_CONV - s, c0:c0 + HEAD_DIM])
            return _silu(acc)

        for hd in range(N_DHEADS):
            qh = conv_act(hd * HEAD_DIM)
            qa[p, :, hd * HEAD_DIM:(hd + 1) * HEAD_DIM] = (
                qh * lax.rsqrt(jnp.sum(qh * qh, axis=-1, keepdims=True) + EPS) * (HEAD_DIM ** -0.5))
            kh = conv_act(D_DELTA + hd * HEAD_DIM)
            qa[p, :, D_DELTA + hd * HEAD_DIM:D_DELTA + (hd + 1) * HEAD_DIM] = (
                kh * lax.rsqrt(jnp.sum(kh * kh, axis=-1, keepdims=True) + EPS))
            qa[p, :, 2 * D_DELTA + hd * HEAD_DIM:2 * D_DELTA + (hd + 1) * HEAD_DIM] = conv_act(
                2 * D_DELTA + hd * HEAD_DIM)
            yield

        row = lax.broadcasted_iota(jnp.int32, (tt, LANES), 0)
        ba = rest[p, :, R_BA:R_BA + LANES]
        bg[p, 0] = _sigmoid(ba)
        bg[p, 1] = _chunk_cumsum(-jnp.exp(alog_ref[...]) * _softplus(ba + dt_ref[...]), CHUNK, row)
        yield

        sbuf[SUBLANES:SUBLANES + tt, :] = (rest[p, :, R_SC:R_SC + D_SCONV]
                                           * rest[p, :, R_SX:R_SX + D_SCONV])
        obuf[p, :, D_DELTA:D_DELTA + D_SCONV] = (
            rest[p, :, R_SB:R_SB + D_SCONV] * _branch_s(sbuf, sw_ref, SUBLANES, tt)
            * _silu(rest[p, :, R_GATE_S:R_GATE_S + D_SCONV]))
        yield

        prev_tail = jnp.where(t1 == 0, 0.0, cbuf[q, tt:tt + CTAIL, :])
        cbuf[p, 0:CTAIL, :] = prev_tail
        cbuf[p, CTAIL:CTAIL + tt, :] = (rest[p, :, R_GA:R_GA + D_CONF]
                                        * _sigmoid(rest[p, :, R_GB:R_GB + D_CONF]))
        yield
        for sh in range(1, SUBLANES):
            cshift[p, sh - 1] = cbuf[p, pl.ds(sh, tt + CTAIL - SUBLANES), :]
        yield

        qbuf[0:SUBLANES, :] = qbuf[tt:tt + SUBLANES, :]
        sbuf[0:SUBLANES, :] = sbuf[tt:tt + SUBLANES, :]

    def stage2():
        beta_all = bg[q, 0]
        gc = bg[q, 1]
        gct = gc.T
        gl_rows = jnp.concatenate(
            [jnp.broadcast_to(gc[(c + 1) * CHUNK - 1:(c + 1) * CHUNK, :], (CHUNK, LANES))
             for c in range(nchunk)], axis=0)
        pk = _Packed(CHUNK, tt, True)
        heads, kdecs = [], []
        for hd in range(N_DHEADS):
            lane = N_DHEADS + hd
            gcol = gc[:, lane:lane + 1]
            kh = qa[q, :, D_DELTA + hd * HEAD_DIM:D_DELTA + (hd + 1) * HEAD_DIM]
            heads.append((qa[q, :, hd * HEAD_DIM:(hd + 1) * HEAD_DIM], kh,
                          qa[q, :, 2 * D_DELTA + hd * HEAD_DIM:2 * D_DELTA + (hd + 1) * HEAD_DIM],
                          beta_all[:, hd:hd + 1], gcol, gct[lane:lane + 1, :]))
            kdecs.append(kh * jnp.exp(gl_rows[:, lane:lane + 1] - gcol))
        yield
        prepped = yield from pk.delta_prep(heads, CHUNK)

        us, lhss, kdts, qks = [], [], [], []
        low = (lax.broadcasted_iota(jnp.int32, (CHUNK, tt), 1) % LANES) < CHUNK
        for pr in range(npair):
            (u0, w0, qg0, qkp0), (u1, w1, qg1, qkp1) = prepped[2 * pr], prepped[2 * pr + 1]
            us.append(jnp.concatenate([u0, u1], axis=1))
            w = jnp.concatenate([w0, w1], axis=1).astype(BF16)
            qg = jnp.concatenate([qg0, qg1], axis=1).astype(BF16)
            lhss.append([jnp.concatenate([w[c * CHUNK:(c + 1) * CHUNK], qg[c * CHUNK:(c + 1) * CHUNK]],
                                         axis=0) for c in range(nchunk)])
            kst = jnp.concatenate([kd[c * CHUNK:(c + 1) * CHUNK] for c in range(nchunk)
                                   for kd in (kdecs[2 * pr], kdecs[2 * pr + 1])], axis=0)
            kdts.append(kst.T.astype(BF16))
            qks.append((jnp.where(low, qkp0, pltpu.roll(qkp1, CHUNK, axis=1)).astype(BF16),
                        jnp.where(low, pltpu.roll(qkp0, tt - CHUNK, axis=1), qkp1).astype(BF16)))
        yield

        ss = [s_scr[pr] for pr in range(npair)]
        for c in range(nchunk):
            cs = slice(c * CHUNK, (c + 1) * CHUNK)
            rs = [jnp.dot(lhss[pr][c], _pair_diag(s[:, :HEAD_DIM], s[:, HEAD_DIM:]).astype(BF16),
                          preferred_element_type=F32) for pr, s in enumerate(ss)]
            yield
            vns = [us[pr][cs] - r[:CHUNK] for pr, r in enumerate(rs)]
            v_bds = [_pair_diag(vn[:, :HEAD_DIM], vn[:, HEAD_DIM:]).astype(BF16) for vn in vns]
            upds = [jnp.dot(kdts[pr][:, 2 * c * CHUNK:2 * (c + 1) * CHUNK], v_bd,
                            preferred_element_type=F32) for pr, v_bd in enumerate(v_bds)]
            tile = slice((c // 2) * LANES, (c // 2 + 1) * LANES)
            o_intra = [jnp.dot(qks[pr][c % 2][:, tile], v_bd, preferred_element_type=F32)
                       for pr, v_bd in enumerate(v_bds)]
            ge = (c + 1) * CHUNK - 1
            for pr in range(npair):
                egl = jnp.concatenate(
                    [jnp.broadcast_to(jnp.exp(gc[ge:ge + 1, N_DHEADS + hh:N_DHEADS + hh + 1]),
                                      (1, HEAD_DIM)) for hh in (2 * pr, 2 * pr + 1)], axis=1)
                ss[pr] = ss[pr] * egl + upds[pr]
                o_pair = rs[pr][CHUNK:] + o_intra[pr]
                for hl in range(2):
                    hd = 2 * pr + hl
                    o = _rms_rows(o_pair[:, hl * HEAD_DIM:(hl + 1) * HEAD_DIM], dng_ref[...])
                    gate = rest[q, cs, R_GATE_D + hd * HEAD_DIM:R_GATE_D + (hd + 1) * HEAD_DIM]
                    obuf[q, cs, hd * HEAD_DIM:(hd + 1) * HEAD_DIM] = o * _silu(gate)
            yield
            acc = None
            for j in range(CONF_W):
                off = CTAIL - (CONF_W - 1) + j
                start = c * CHUNK + (off // SUBLANES) * SUBLANES
                sh = off % SUBLANES
                src = (cbuf[q, start:start + CHUNK, :] if sh == 0
                       else cshift[q, sh - 1, start:start + CHUNK, :])
                term = src * ccw_ref[j:j + 1, :]
                acc = term if acc is None else acc + term
            yc = acc + ccb_ref[...]
            mu = jnp.mean(yc, axis=-1, keepdims=True)
            var = jnp.mean(jnp.square(yc - mu), axis=-1, keepdims=True)
            yc = (yc - mu) * lax.rsqrt(var + EPS) * clg_ref[...] + clb_ref[...]
            obuf[q, cs, D_DELTA + D_SCONV:] = _silu(yc) * _silu(rest[q, cs, R_GATE_C:R_GATE_C + D_CONF])
            yield
        for pr in range(npair):
            s_scr[pr] = ss[pr]

        y = x2_ref[0] + jnp.dot(obuf[q].astype(BF16), wout_ref[...], preferred_element_type=F32)
        if final_norm:
            y = _rms_rows(y, fg_ref[...])
        y_ref[0] = y

    _interleave(stage1(), stage2(), steps=(2, 1))

    @pl.when(jnp.logical_and(g >= 1, t2 == nt - 1))
    def _():
        for pr in range(npair):
            s_out_ref[0, 2 * pr] = s_scr[pr, :, :HEAD_DIM]
            s_out_ref[0, 2 * pr + 1] = s_scr[pr, :, HEAD_DIM:]

    @pl.when(jnp.logical_and(g < n_tiles, t1 == nt - 1))
    def _():
        qt_ref[0] = qbuf[0:SUBLANES, :]
        st_ref[0] = sbuf[0:SUBLANES, :]
        ct_ref[0] = cbuf[p, tt:tt + CTAIL, :]


def _prompt_layer_pipelined(x, wts, layer, final_norm):
    b, t, d = x.shape
    nt = t // TIME_TILE
    n_tiles = b * nt
    npair = N_DHEADS // 2

    def tile1(g):
        g1 = jnp.minimum(g, n_tiles - 1)
        return g1 // nt, g1 % nt

    def tile2(g):
        g2 = jnp.maximum(g - 1, 0)
        return g2 // nt, g2 % nt

    out_shape = (
        jax.ShapeDtypeStruct((b, t, d), F32),
        jax.ShapeDtypeStruct((b, N_DHEADS, HEAD_DIM, HEAD_DIM), F32),
        jax.ShapeDtypeStruct((b, SUBLANES, 3 * D_DELTA), F32),
        jax.ShapeDtypeStruct((b, SUBLANES, D_SCONV), F32),
        jax.ShapeDtypeStruct((b, CTAIL, D_CONF), F32),
    )
    out_specs = (
        pl.BlockSpec((1, TIME_TILE, d), lambda g: tile2(g) + (0,)),
        pl.BlockSpec((1, N_DHEADS, HEAD_DIM, HEAD_DIM), lambda g: (tile2(g)[0], 0, 0, 0)),
        pl.BlockSpec((1, SUBLANES, 3 * D_DELTA), lambda g: (tile1(g)[0], 0, 0)),
        pl.BlockSpec((1, SUBLANES, D_SCONV), lambda g: (tile1(g)[0], 0, 0)),
        pl.BlockSpec((1, CTAIL, D_CONF), lambda g: (tile1(g)[0], 0, 0)),
    )
    scratch = [
        pltpu.VMEM((TIME_TILE + SUBLANES, 3 * D_DELTA), F32),
        pltpu.VMEM((TIME_TILE + SUBLANES, D_SCONV), F32),
        pltpu.VMEM((npair, HEAD_DIM, 2 * HEAD_DIM), F32),
        pltpu.VMEM((2, TIME_TILE, 3 * D_DELTA), F32),
        pltpu.VMEM((2, 2, TIME_TILE, LANES), F32),
        pltpu.VMEM((2, TIME_TILE, R_WIDTH), F32),
        pltpu.VMEM((2, TIME_TILE + CTAIL, D_CONF), F32),
        pltpu.VMEM((2, SUBLANES - 1, TIME_TILE + CTAIL - SUBLANES, D_CONF), F32),
        pltpu.VMEM((2, TIME_TILE, D_MODEL), F32),
    ]
    return pl.pallas_call(
        functools.partial(_pipe_kernel, final_norm, nt),
        grid=(n_tiles + 1,),
        in_specs=[pl.BlockSpec((1, TIME_TILE, d), lambda g: tile1(g) + (0,)),
                  pl.BlockSpec((1, TIME_TILE, d), lambda g: tile2(g) + (0,))]
        + _weight_specs(wts, lambda g: layer, buffers=1),
        out_specs=out_specs,
        out_shape=out_shape,
        scratch_shapes=scratch,
        compiler_params=pltpu.CompilerParams(
            dimension_semantics=("arbitrary",),
            vmem_limit_bytes=VMEM_LIMIT),
        name="prompt_layer",
    )(x, x, *wts)


def _sample_kernel(x_ref, sd_ref, sq_ref, ss_ref, sc_ref,
                   ng_ref, win_ref, cw_ref, alog_ref, dt_ref, dng_ref,
                   sw_ref, ccw_ref, ccb_ref, clg_ref, clb_ref, wout_ref, fg_ref,
                   y_ref, sd_out_ref, qt_ref, st_ref, ct_ref,
                   qbuf, sbuf, cbuf, obuf, ubuf, wqbuf, kdbuf, glbuf, oibuf, rest, xcarry):
    layer = pl.program_id(0)
    blk = pl.program_id(1)
    nb = SAMPLE_BATCH_BLOCK
    pad = SAMPLE_PAD
    ntok = pad // 2
    rows = nb * pad

    @pl.when(layer == 0)
    def _():
        xcarry[blk, :, 0:ntok, :] = x_ref[...]
        xcarry[blk, :, ntok:, :] = jnp.zeros((nb, pad - ntok, D_MODEL), F32)

    x = xcarry[blk].reshape(rows, D_MODEL)
    h = _rms_rows(x, ng_ref[...]).astype(BF16)
    qkv = jnp.dot(h, win_ref[:, :W_REST], preferred_element_type=F32)
    rest[...] = jnp.dot(h, win_ref[:, W_REST:], preferred_element_type=F32)
    row = lax.broadcasted_iota(jnp.int32, (rows, LANES), 0)
    valid = (row % pad) < (pad // 2)
    valid1 = valid[:, 0:1]

    qkv3 = qkv.reshape(nb, pad, 3 * D_DELTA)
    qt_ref[...] = qkv3[:, ntok - (QK_CONV - 1):ntok, :]
    qbuf[:, SUBLANES - (QK_CONV - 1):SUBLANES, :] = sq_ref[...]
    qbuf[:, SUBLANES:, :] = qkv3

    hs = (rest[:, R_SC:R_SC + D_SCONV] * rest[:, R_SX:R_SX + D_SCONV]).reshape(nb, pad, D_SCONV)
    st_ref[...] = hs[:, ntok - (SCONV_W - 1):ntok, :]
    sbuf[:, SUBLANES - (SCONV_W - 1):SUBLANES, :] = ss_ref[...]
    sbuf[:, SUBLANES:, :] = hs
    ys = _branch_s(sbuf, sw_ref, SUBLANES, pad, lead=(slice(None),))
    obuf[:, D_DELTA:D_DELTA + D_SCONV] = (rest[:, R_SB:R_SB + D_SCONV] * ys.reshape(rows, D_SCONV)
                                          * _silu(rest[:, R_GATE_S:R_GATE_S + D_SCONV]))

    uc = rest[:, R_GA:R_GA + D_CONF] * _sigmoid(rest[:, R_GB:R_GB + D_CONF])
    cbuf[:, CTAIL - (CONF_W - 1):CTAIL, :] = sc_ref[...]
    cbuf[:, CTAIL:, :] = uc.reshape(nb, pad, D_CONF)
    first = CTAIL + ntok - (CONF_W - 1)
    ct_ref[...] = cbuf[:, first:first + CONF_W - 1, :]
    yc = _conformer(cbuf, ccw_ref, ccb_ref, clg_ref, clb_ref, CTAIL, pad, lead=(slice(None),))
    obuf[:, D_DELTA + D_SCONV:] = yc.reshape(rows, D_CONF) * _silu(rest[:, R_GATE_C:R_GATE_C + D_CONF])

    ba = rest[:, R_BA:R_BA + LANES]
    beta_all = jnp.where(valid, _sigmoid(ba), 0.0)
    g_all = jnp.where(valid, -jnp.exp(alog_ref[...]) * _softplus(ba + dt_ref[...]), 0.0)
    gc = _chunk_cumsum(g_all, pad, row)
    gct = gc.T
    gl_all = jnp.broadcast_to(
        gc.reshape(nb, pad, LANES)[:, pad - 1:pad, :], (nb, pad, LANES)).reshape(rows, LANES)
    pk = _Packed(pad, rows, False)

    def conv_act(c0):
        acc = None
        for j in range(QK_CONV):
            term = (qbuf[:, pl.ds(SUBLANES - (QK_CONV - 1) + j, pad), c0:c0 + HEAD_DIM]
                    * cw_ref[j:j + 1, c0:c0 + HEAD_DIM].reshape(1, 1, HEAD_DIM))
            acc = term if acc is None else acc + term
        return jnp.where(valid1, _silu(acc.reshape(rows, HEAD_DIM)), 0.0)

    for pr in range(N_DHEADS // 2):
        h0, h1 = 2 * pr, 2 * pr + 1
        heads, kdecs, egls = [], [], []
        for hd in (h0, h1):
            q = conv_act(hd * HEAD_DIM)
            k = conv_act(D_DELTA + hd * HEAD_DIM)
            v = conv_act(2 * D_DELTA + hd * HEAD_DIM)
            q = q * lax.rsqrt(jnp.sum(q * q, axis=-1, keepdims=True) + EPS) * (HEAD_DIM ** -0.5)
            k = k * lax.rsqrt(jnp.sum(k * k, axis=-1, keepdims=True) + EPS)
            lane = N_DHEADS + hd
            gcol = gc[:, lane:lane + 1]
            heads.append((q, k, v, beta_all[:, hd:hd + 1], gcol, gct[lane:lane + 1, :]))
            gl = gl_all[:, lane:lane + 1]
            kdecs.append(k * jnp.exp(gl - gcol))
            egls.append(jnp.broadcast_to(jnp.exp(gl), (rows, HEAD_DIM)))
        (u0, w0, qg0, qkp0), (u1, w1, qg1, qkp1) = _run(pk.delta_prep(heads, pad // 2))
        (kd0, kd1), (egl0, egl1) = kdecs, egls
        ubuf[...] = jnp.concatenate([u0, u1], axis=1)
        wqbuf[:, 0:pad, :] = jnp.concatenate([w0, w1], axis=1).reshape(nb, pad, 2 * HEAD_DIM)
        wqbuf[:, pad:, :] = jnp.concatenate([qg0, qg1], axis=1).reshape(nb, pad, 2 * HEAD_DIM)
        kdbuf[:, 0:pad, :] = kd0.reshape(nb, pad, HEAD_DIM)
        kdbuf[:, pad:, :] = kd1.reshape(nb, pad, HEAD_DIM)
        glbuf[...] = jnp.concatenate([egl0, egl1], axis=1).reshape(nb, pad, 2 * HEAD_DIM)

        def body(it, carry):
            bis = [it * SAMPLE_SEQ_LOCKSTEP + t for t in range(SAMPLE_SEQ_LOCKSTEP)]
            r0s = [pl.multiple_of(bi * pad, pad) for bi in bis]
            s0s = [sd_ref[bi, h0] for bi in bis]
            s1s = [sd_ref[bi, h1] for bi in bis]
            rs = [_mm(wqbuf[bi], _pair_diag(s0, s1)) for bi, s0, s1 in zip(bis, s0s, s1s)]
            vns = [ubuf[pl.ds(r0, pad), :] - r[:pad] for r0, r in zip(r0s, rs)]
            upds = [_mm_tn(kdbuf[bi], _pair_diag(vn[:, :HEAD_DIM], vn[:, HEAD_DIM:]))
                    for bi, vn in zip(bis, vns)]
            for bi, r0, s0, s1, r, vn, upd in zip(bis, r0s, s0s, s1s, rs, vns, upds):
                ubuf[pl.ds(r0, pad), :] = vn
                oibuf[pl.ds(r0, pad), :] = r[pad:]
                egl = glbuf[bi][0:1, :]
                sd_out_ref[bi, h0] = s0 * egl[:, :HEAD_DIM] + upd[:, :HEAD_DIM]
                sd_out_ref[bi, h1] = s1 * egl[:, HEAD_DIM:] + upd[:, HEAD_DIM:]
            return carry

        lax.fori_loop(0, nb // SAMPLE_SEQ_LOCKSTEP, body, 0)
        for hh, qkp in ((h0, qkp0), (h1, qkp1)):
            ls = slice((hh - h0) * HEAD_DIM, (hh - h0 + 1) * HEAD_DIM)
            o = oibuf[:, ls] + _mm(pk.block_diag(qkp), ubuf[:, ls])
            o = _rms_rows(o, dng_ref[...])
            obuf[:, hh * HEAD_DIM:(hh + 1) * HEAD_DIM] = (
                o * _silu(rest[:, R_GATE_D + hh * HEAD_DIM:R_GATE_D + (hh + 1) * HEAD_DIM]))

    y = x + jnp.dot(obuf[...].astype(BF16), wout_ref[...], preferred_element_type=F32)
    xcarry[blk] = y.reshape(nb, pad, D_MODEL)
    is_last = layer == pl.num_programs(0) - 1
    y_ref[...] = jnp.where(is_last, _rms_rows(y, fg_ref[...]), y).reshape(nb, pad, D_MODEL)[:, 0:ntok, :]


def _sample_layers(x, sd, sq, ss, sc, wts):
    depth = sd.shape[0]
    b, ntok, _ = x.shape
    nb = SAMPLE_BATCH_BLOCK

    def bspec(shape):
        return pl.BlockSpec((nb,) + shape, lambda l, bi: (bi,) + (0,) * len(shape))

    def lspec(shape):
        return pl.BlockSpec((None, nb) + shape, lambda l, bi: (l, bi) + (0,) * len(shape))

    def lshape(shape):
        return jax.ShapeDtypeStruct((depth, b) + shape, F32)

    state_shapes = ((N_DHEADS, HEAD_DIM, HEAD_DIM), (QK_CONV - 1, 3 * D_DELTA),
                    (SCONV_W - 1, D_SCONV), (CONF_W - 1, D_CONF))
    out_shape = (lshape((ntok, D_MODEL)),) + tuple(lshape(s) for s in state_shapes)
    out_specs = (lspec((ntok, D_MODEL)),) + tuple(lspec(s) for s in state_shapes)
    rows = nb * SAMPLE_PAD
    scratch = [
        pltpu.VMEM((nb, 2 * SUBLANES, 3 * D_DELTA), F32),
        pltpu.VMEM((nb, 2 * SUBLANES, D_SCONV), F32),
        pltpu.VMEM((nb, CTAIL + SAMPLE_PAD, D_CONF), F32),
        pltpu.VMEM((rows, D_MODEL), F32),
        pltpu.VMEM((rows, 2 * HEAD_DIM), F32),
        pltpu.VMEM((nb, 2 * SAMPLE_PAD, 2 * HEAD_DIM), F32),
        pltpu.VMEM((nb, 2 * SAMPLE_PAD, HEAD_DIM), F32),
        pltpu.VMEM((nb, SAMPLE_PAD, 2 * HEAD_DIM), F32),
        pltpu.VMEM((rows, 2 * HEAD_DIM), F32),
        pltpu.VMEM((rows, R_WIDTH), F32),
        pltpu.VMEM((b // nb, nb, SAMPLE_PAD, D_MODEL), F32),
    ]
    in_specs = ([bspec((ntok, D_MODEL))] + [lspec(s) for s in state_shapes]
                + _weight_specs(wts, lambda l, bi: l))
    return pl.pallas_call(
        _sample_kernel,
        grid=(depth, b // nb),
        in_specs=in_specs,
        out_specs=out_specs,
        out_shape=out_shape,
        scratch_shapes=scratch,
        compiler_params=pltpu.CompilerParams(
            dimension_semantics=("arbitrary", "arbitrary"),
            vmem_limit_bytes=VMEM_LIMIT),
        name="sample_layers",
    )(x, sd, sq, ss, sc, *wts)


def _reorder_kernel(w_ref, o_ref):
    n_qkv = 3 * D_DELTA
    n_ba = 2 * N_DHEADS
    rows = w_ref.shape[0]
    o_ref[:, 0:n_qkv] = w_ref[:, 0:n_qkv].astype(BF16)
    o_ref[:, n_qkv:n_qkv + R_BA] = w_ref[:, n_qkv + n_ba:].astype(BF16)
    o_ref[:, n_qkv + R_BA:] = jnp.concatenate(
        [w_ref[:, n_qkv:n_qkv + n_ba], jnp.zeros((rows, LANES - n_ba), F32)], axis=1).astype(BF16)


def _reorder_in_proj(w_in):
    depth, d, d_in = w_in.shape
    rows = 256
    return pl.pallas_call(
        _reorder_kernel,
        grid=(depth, d // rows),
        in_specs=[pl.BlockSpec((None, rows, d_in), lambda l, r: (l, r, 0))],
        out_specs=pl.BlockSpec((None, rows, W_REST + R_WIDTH), lambda l, r: (l, r, 0)),
        out_shape=jax.ShapeDtypeStruct((depth, d, W_REST + R_WIDTH), BF16),
        compiler_params=pltpu.CompilerParams(dimension_semantics=("arbitrary", "arbitrary")),
        name="reorder_in_proj",
    )(w_in)


def _stacked_weights(norm_g, w_in, conv_qkv_w, a_log, dt_bias, delta_norm_g, sconv_w,
                     cconv_w, cconv_b, cln_g, cln_b, w_out, final_norm_g):
    depth = w_in.shape[0]
    w_all = _reorder_in_proj(w_in)
    lane_pad = ((0, 0), (N_DHEADS, LANES - 2 * N_DHEADS))
    return (
        norm_g.reshape(depth, 1, D_MODEL),
        w_all,
        conv_qkv_w,
        jnp.pad(a_log, lane_pad).reshape(depth, 1, LANES),
        jnp.pad(dt_bias, lane_pad).reshape(depth, 1, LANES),
        delta_norm_g.reshape(depth, 1, HEAD_DIM),
        sconv_w,
        cconv_w,
        cconv_b.reshape(depth, 1, D_CONF),
        cln_g.reshape(depth, 1, D_CONF),
        cln_b.reshape(depth, 1, D_CONF),
        w_out.astype(BF16),
        final_norm_g.reshape(1, D_MODEL),
    )


def kernel(x_prompt, x_sample, state_delta, state_qkv_conv, state_sconv, state_cconv, norm_g, w_in, conv_qkv_w, a_log, dt_bias, delta_norm_g, sconv_w, cconv_w, cconv_b, cln_g, cln_b, w_out, final_norm_g):
    depth = w_in.shape[0]
    dec_seq = x_sample.shape[1]
    assert x_prompt.shape[1] % TIME_TILE == 0
    assert dec_seq == SAMPLE_PAD // 2 and x_sample.shape[0] % SAMPLE_BATCH_BLOCK == 0

    wts = _stacked_weights(norm_g, w_in, conv_qkv_w, a_log, dt_bias, delta_norm_g, sconv_w,
                           cconv_w, cconv_b, cln_g, cln_b, w_out, final_norm_g)

    xp = x_prompt
    p_outs = [[] for _ in range(4)]
    for l in range(depth):
        xp, pd, pq, ps, pc = _prompt_layer_pipelined(xp, wts, l, l == depth - 1)
        for acc, o in zip(p_outs, (pd, pq, ps, pc)):
            acc.append(o)
    pd, pq, ps, pc = (jnp.stack(o) for o in p_outs)

    xs, sd, sq, ss, sc = _sample_layers(x_sample, state_delta, state_qkv_conv, state_sconv,
                                        state_cconv, wts)
    return (xp, xs[depth - 1],
            pd, pq[:, :, SUBLANES - (QK_CONV - 1):, :], ps[:, :, SUBLANES - (SCONV_W - 1):, :],
            pc[:, :, CTAIL - (CONF_W - 1):, :],
            sd, sq, ss, sc)
```

```python
import functools

import jax
import jax.numpy as jnp
from jax import lax
from jax.experimental import pallas as pl
from jax.experimental.pallas import tpu as pltpu

D_MODEL = 1024
N_DHEADS = 4
HEAD_DIM = 128
D_DELTA = N_DHEADS * HEAD_DIM
D_SCONV = 256
D_CONF = 256
QK_CONV = 4
SCONV_W = 3
CONF_W = 31
CHUNK = 64
EPS = 1e-6

R_GATE_D = 0
R_SB = 512
R_SC = 768
R_SX = 1024
R_GATE_S = 1280
R_GA = 1536
R_GB = 1792
R_GATE_C = 2048
R_BA = 2304
R_WIDTH = 2432
W_REST = 3 * D_DELTA

SUBLANES = 8
LANES = 128
TIME_TILE = 256
SAMPLE_PAD = 8
SAMPLE_BATCH_BLOCK = 16
SAMPLE_SEQ_LOCKSTEP = 16
CTAIL = 32
REORDER_ROW_BLOCK = 256
MXU_COLS = 256
V7X_VMEM_BYTES = 64 * 1024 * 1024
VMEM_LIMIT = V7X_VMEM_BYTES - 8 * 1024 * 1024

F32 = jnp.float32
BF16 = jnp.bfloat16


def _mm(a, b):
    return jnp.dot(a.astype(BF16), b.astype(BF16), preferred_element_type=F32)


def _mm_nt(a, b):
    return lax.dot_general(a.astype(BF16), b.astype(BF16), (((1,), (1,)), ((), ())),
                           preferred_element_type=F32)


def _mm_tn(a, b):
    return lax.dot_general(a.astype(BF16), b.astype(BF16), (((0,), (0,)), ((), ())),
                           preferred_element_type=F32)


def _sigmoid(x):
    return 1.0 / (1.0 + jnp.exp(-x))


def _silu(x):
    h = 0.5 * x
    return h + h * jnp.tanh(h)


def _softplus(x):
    return jnp.maximum(x, 0.0) + jnp.log1p(jnp.exp(-jnp.abs(x)))


def _rms_rows(x, g):
    return x * lax.rsqrt(jnp.mean(x * x, axis=-1, keepdims=True) + EPS) * g


def _chunk_cumsum(g, chunk, row):
    pos = row % chunk
    s = 1
    while s < chunk:
        g = g + jnp.where(pos >= s, pltpu.roll(g, s, axis=0), 0.0)
        s *= 2
    return g


def _run(staged):
    try:
        while True:
            next(staged)
    except StopIteration as stop:
        return stop.value


def _interleave(*staged, steps=None):
    live = list(staged)
    steps = dict(zip(live, steps or [1] * len(live)))
    while live:
        for s in list(live):
            try:
                for _ in range(steps[s]):
                    next(s)
            except StopIteration:
                live.remove(s)


def _pair_diag(a, b):
    z = jnp.zeros_like(a)
    return jnp.concatenate([jnp.concatenate([a, z], axis=1), jnp.concatenate([z, b], axis=1)], axis=0)


class _Packed:
    def __init__(self, chunk, rows, wide_chunks):
        self.chunk, self.rows, self.n, self.wide_chunks = chunk, rows, rows // chunk, wide_chunks
        self.rr = lax.broadcasted_iota(jnp.int32, (chunk, rows), 0)
        lane = lax.broadcasted_iota(jnp.int32, (chunk, rows), 1)
        self.jl = lane % chunk
        self.lane_blk = lane // chunk
        ii = lax.broadcasted_iota(jnp.int32, (rows, rows), 0)
        jj = lax.broadcasted_iota(jnp.int32, (rows, rows), 1)
        mask = jnp.where((ii // chunk) == (jj // chunk), 1.0, 0.0).astype(F32)
        self.bd_mask = mask.astype(BF16) if wide_chunks else mask

    def pack(self, g):
        out = g[0:self.chunk]
        for c in range(1, self.n):
            out = jnp.where(self.lane_blk == c, g[c * self.chunk:(c + 1) * self.chunk], out)
        return out

    def col(self, v):
        shape = (self.chunk, self.rows)
        out = jnp.broadcast_to(v[0:self.chunk], shape)
        for c in range(1, self.n):
            out = jnp.where(self.lane_blk == c,
                            jnp.broadcast_to(v[c * self.chunk:(c + 1) * self.chunk], shape), out)
        return out

    def block_diag(self, xp):
        if self.wide_chunks:
            return jnp.concatenate([xp.astype(BF16)] * self.n, axis=0) * self.bd_mask
        return (jnp.concatenate([xp] * self.n, axis=0) * self.bd_mask).astype(BF16)

    def unit_lower_inverse(self, lps, nil):
        base = min(self.chunk, 16)
        same = (self.rr // base) == (self.jl // base)
        eye = jnp.where(self.rr == self.jl, 1.0, 0.0).astype(F32)
        ds = [jnp.where(same, lp, 0.0) for lp in lps]
        xs = [eye - d for d in ds]
        d_bds = [self.block_diag(d) for d in ds]
        p = 2
        while p < min(base, nil):
            ds = [_mm(d, d_bd) for d, d_bd in zip(ds, d_bds)]
            yield
            d_bds = [self.block_diag(d) for d in ds]
            xs = [x + _mm(x, d_bd) for x, d_bd in zip(xs, d_bds)]
            yield
            p *= 2
        size = base
        while size < self.chunk:
            big = (self.rr // (2 * size)) == (self.jl // (2 * size))
            off = jnp.logical_and(big, jnp.logical_not(same))
            xes = [_mm(x, self.block_diag(jnp.where(off, lp, 0.0))) for x, lp in zip(xs, lps)]
            yield
            xs = [x - _mm(xe, self.block_diag(x)) for x, xe in zip(xs, xes)]
            yield
            same = big
            size *= 2
        return xs

    def delta_prep(self, heads, nil):
        tril = self.rr >= self.jl
        strict = self.rr > self.jl
        decays, kbs, gs = [], [], []
        for q, k, v, beta, gc_col, gc_row in heads:
            diff = self.col(gc_col) - gc_row
            decays.append(jnp.where(tril, jnp.exp(jnp.where(tril, diff, 0.0)), 0.0))
            kbs.append(k * beta)
        if self.wide_chunks:
            c, n = self.chunk, self.n
            zero = jnp.zeros((c, HEAD_DIM), BF16)
            for (q, k, *_), kb in zip(heads, kbs):
                kbq = jnp.concatenate([kb, q], axis=1).astype(BF16)
                lhs = jnp.concatenate(
                    [jnp.concatenate([kbq[i * c:(i + 1) * c, :HEAD_DIM] for i in range(n)], axis=1),
                     jnp.concatenate([kbq[i * c:(i + 1) * c, HEAD_DIM:] for i in range(n)], axis=1)],
                    axis=0)
                kb16 = k.astype(BF16)
                rhs = jnp.concatenate(
                    [jnp.concatenate([kb16[i * c:(i + 1) * c] if j == i else zero for j in range(n)],
                                     axis=1) for i in range(n)], axis=0)
                gs.append(_mm_nt(lhs, rhs))
            yield
            lps = [jnp.where(strict, g[:c] * dec, 0.0) for g, dec in zip(gs, decays)]
            qkps = [g[c:] * dec for g, dec in zip(gs, decays)]
        else:
            for (q, k, *_), kb in zip(heads, kbs):
                gs.append(_mm_nt(jnp.concatenate([kb, q], axis=0), k))
            yield
            lps = [jnp.where(strict, self.pack(g[:self.rows]) * dec, 0.0) for g, dec in zip(gs, decays)]
            qkps = [self.pack(g[self.rows:]) * dec for g, dec in zip(gs, decays)]
        tinvs = yield from self.unit_lower_inverse(lps, nil)
        out = []
        for (q, k, v, beta, gc_col, _), kb, tinv, qkp in zip(heads, kbs, tinvs, qkps):
            eg = jnp.exp(gc_col)
            uw = _mm(self.block_diag(tinv), jnp.concatenate([v * beta, kb * eg], axis=1))
            out.append((uw[:, :HEAD_DIM], uw[:, HEAD_DIM:], q * eg, qkp))
        yield
        return out


def _branch_s(buf_ref, w_ref, tail, rows, lead=()):
    n = len(lead)
    acc = None
    for j in range(SCONV_W):
        idx = lead + (pl.ds(tail - (SCONV_W - 1) + j, rows), slice(None))
        term = buf_ref[idx] * w_ref[j:j + 1, :].reshape((1,) * n + (1, D_SCONV))
        acc = term if acc is None else acc + term
    return acc


def _conformer(ubuf_ref, w_ref, b_ref, g_ref, beta_ref, tail, rows, lead=()):
    n = len(lead)
    shp = (1,) * n + (1, D_CONF)
    acc = None
    for j in range(CONF_W):
        idx = lead + (pl.ds(tail - (CONF_W - 1) + j, rows), slice(None))
        term = ubuf_ref[idx] * w_ref[j:j + 1, :].reshape(shp)
        acc = term if acc is None else acc + term
    yc = acc + b_ref[...].reshape(shp)
    mu = jnp.mean(yc, axis=-1, keepdims=True)
    var = jnp.mean(jnp.square(yc - mu), axis=-1, keepdims=True)
    yc = (yc - mu) * lax.rsqrt(var + EPS) * g_ref[...].reshape(shp) + beta_ref[...].reshape(shp)
    return _silu(yc)


def _weight_specs(wts, layer_of, buffers=2):
    specs = []
    for w in wts[:-1]:
        tail = (0,) * (w.ndim - 1)
        specs.append(pl.BlockSpec((None,) + w.shape[1:], lambda *ids, tail=tail: (layer_of(*ids),) + tail,
                                  pipeline_mode=pl.Buffered(buffers)))
    specs.append(pl.BlockSpec(wts[-1].shape, lambda *ids: (0,) * wts[-1].ndim))
    return specs


def _pipe_kernel(final_norm, nt,
                 x_ref, x2_ref, ng_ref, win_ref, cw_ref, alog_ref, dt_ref, dng_ref,
                 sw_ref, ccw_ref, ccb_ref, clg_ref, clb_ref, wout_ref, fg_ref,
                 y_ref, s_out_ref, qt_ref, st_ref, ct_ref,
                 qbuf, sbuf, s_scr, qa, bg, rest, cbuf, cshift, obuf):
    g = pl.program_id(0)
    n_tiles = pl.num_programs(0) - 1
    tt = TIME_TILE
    nchunk = tt // CHUNK
    npair = N_DHEADS // 2
    t1 = jnp.minimum(g, n_tiles - 1) % nt
    t2 = jnp.maximum(g - 1, 0) % nt
    p = g % 2
    q = 1 - p

    @pl.when(g == 0)
    def _():
        qa[1] = jnp.zeros(qa.shape[1:], F32)
        bg[1] = jnp.zeros(bg.shape[1:], F32)
        rest[1] = jnp.zeros(rest.shape[1:], F32)
        cbuf[1] = jnp.zeros(cbuf.shape[1:], F32)
        cshift[1] = jnp.zeros(cshift.shape[1:], F32)
        obuf[1] = jnp.zeros(obuf.shape[1:], F32)

    @pl.when(t1 == 0)
    def _():
        qbuf[0:SUBLANES, :] = jnp.zeros((SUBLANES, 3 * D_DELTA), F32)
        sbuf[0:SUBLANES, :] = jnp.zeros((SUBLANES, D_SCONV), F32)

    @pl.when(t2 == 0)
    def _():
        s_scr[...] = jnp.zeros(s_scr.shape, F32)

    def stage1():
        h = _rms_rows(x_ref[0], ng_ref[...]).astype(BF16)
        yield
        for c0 in range(0, 3 * D_DELTA, MXU_COLS):
            qbuf[SUBLANES:SUBLANES + tt, c0:c0 + MXU_COLS] = jnp.dot(
                h, win_ref[:, c0:c0 + MXU_COLS], preferred_element_type=F32)
            yield
        for c0 in range(0, R_WIDTH, MXU_COLS):
            c1 = min(c0 + MXU_COLS, R_WIDTH)
            rest[p, :, c0:c1] = jnp.dot(h, win_ref[:, W_REST + c0:W_REST + c1],
                                        preferred_element_type=F32)
            yield

        def conv_act(c0):
            full = qbuf[:, c0:c0 + HEAD_DIM]
            acc = full[SUBLANES:] * cw_ref[QK_CONV - 1:QK_CONV, c0:c0 + HEAD_DIM]
            for s in range(1, QK_CONV):
                acc = acc + (pltpu.roll(full, s, axis=0)[SUBLANES:]
                             * cw_ref[QK_CONV - 1 - s:QK_CONV - s, c0:c0 + HEAD_DIM])
            return _silu(acc)

        for hd in range(N_DHEADS):
            qh = conv_act(hd * HEAD_DIM)
            qa[p, :, hd * HEAD_DIM:(hd + 1) * HEAD_DIM] = (
                qh * lax.rsqrt(jnp.sum(qh * qh, axis=-1, keepdims=True) + EPS) * (HEAD_DIM ** -0.5))
            kh = conv_act(D_DELTA + hd * HEAD_DIM)
            qa[p, :, D_DELTA + hd * HEAD_DIM:D_DELTA + (hd + 1) * HEAD_DIM] = (
                kh * lax.rsqrt(jnp.sum(kh * kh, axis=-1, keepdims=True) + EPS))
            qa[p, :, 2 * D_DELTA + hd * HEAD_DIM:2 * D_DELTA + (hd + 1) * HEAD_DIM] = conv_act(
                2 * D_DELTA + hd * HEAD_DIM)
            yield

        row = lax.broadcasted_iota(jnp.int32, (tt, LANES), 0)
        ba = rest[p, :, R_BA:R_BA + LANES]
        bg[p, 0] = _sigmoid(ba)
        bg[p, 1] = _chunk_cumsum(-jnp.exp(alog_ref[...]) * _softplus(ba + dt_ref[...]), CHUNK, row)
        yield

        sbuf[SUBLANES:SUBLANES + tt, :] = (rest[p, :, R_SC:R_SC + D_SCONV]
                                           * rest[p, :, R_SX:R_SX + D_SCONV])
        obuf[p, :, D_DELTA:D_DELTA + D_SCONV] = (
            rest[p, :, R_SB:R_SB + D_SCONV] * _branch_s(sbuf, sw_ref, SUBLANES, tt)
            * _silu(rest[p, :, R_GATE_S:R_GATE_S + D_SCONV]))
        yield

        prev_tail = jnp.where(t1 == 0, 0.0, cbuf[q, tt:tt + CTAIL, :])
        cbuf[p, 0:CTAIL, :] = prev_tail
        cbuf[p, CTAIL:CTAIL + tt, :] = (rest[p, :, R_GA:R_GA + D_CONF]
                                        * _sigmoid(rest[p, :, R_GB:R_GB + D_CONF]))
        yield
        for sh in range(1, SUBLANES):
            cshift[p, sh - 1] = cbuf[p, pl.ds(sh, tt + CTAIL - SUBLANES), :]
        yield

        qbuf[0:SUBLANES, :] = qbuf[tt:tt + SUBLANES, :]
        sbuf[0:SUBLANES, :] = sbuf[tt:tt + SUBLANES, :]

    def stage2():
        beta_all = bg[q, 0]
        gc = bg[q, 1]
        gct = gc.T
        gl_rows = jnp.concatenate(
            [jnp.broadcast_to(gc[(c + 1) * CHUNK - 1:(c + 1) * CHUNK, :], (CHUNK, LANES))
             for c in range(nchunk)], axis=0)
        pk = _Packed(CHUNK, tt, True)
        heads, kdecs = [], []
        for hd in range(N_DHEADS):
            lane = N_DHEADS + hd
            gcol = gc[:, lane:lane + 1]
            kh = qa[q, :, D_DELTA + hd * HEAD_DIM:D_DELTA + (hd + 1) * HEAD_DIM]
            heads.append((qa[q, :, hd * HEAD_DIM:(hd + 1) * HEAD_DIM], kh,
                          qa[q, :, 2 * D_DELTA + hd * HEAD_DIM:2 * D_DELTA + (hd + 1) * HEAD_DIM],
                          beta_all[:, hd:hd + 1], gcol, gct[lane:lane + 1, :]))
            kdecs.append(kh * jnp.exp(gl_rows[:, lane:lane + 1] - gcol))
        yield
        prepped = yield from pk.delta_prep(heads, CHUNK)

        us, lhss, kdts, qks = [], [], [], []
        low = (lax.broadcasted_iota(jnp.int32, (CHUNK, tt), 1) % LANES) < CHUNK
        for pr in range(npair):
            (u0, w0, qg0, qkp0), (u1, w1, qg1, qkp1) = prepped[2 * pr], prepped[2 * pr + 1]
            us.append(jnp.concatenate([u0, u1], axis=1))
            w = jnp.concatenate([w0, w1], axis=1).astype(BF16)
            qg = jnp.concatenate([qg0, qg1], axis=1).astype(BF16)
            lhss.append([jnp.concatenate([w[c * CHUNK:(c + 1) * CHUNK], qg[c * CHUNK:(c + 1) * CHUNK]],
                                         axis=0) for c in range(nchunk)])
            kst = jnp.concatenate([kd[c * CHUNK:(c + 1) * CHUNK] for c in range(nchunk)
                                   for kd in (kdecs[2 * pr], kdecs[2 * pr + 1])], axis=0)
            kdts.append(kst.T.astype(BF16))
            qks.append((jnp.where(low, qkp0, pltpu.roll(qkp1, CHUNK, axis=1)).astype(BF16),
                        jnp.where(low, pltpu.roll(qkp0, tt - CHUNK, axis=1), qkp1).astype(BF16)))
        yield

        ss = [s_scr[pr] for pr in range(npair)]
        for c in range(nchunk):
            cs = slice(c * CHUNK, (c + 1) * CHUNK)
            rs = [jnp.dot(lhss[pr][c], _pair_diag(s[:, :HEAD_DIM], s[:, HEAD_DIM:]).astype(BF16),
                          preferred_element_type=F32) for pr, s in enumerate(ss)]
            yield
            vns = [us[pr][cs] - r[:CHUNK] for pr, r in enumerate(rs)]
            v_bds = [_pair_diag(vn[:, :HEAD_DIM], vn[:, HEAD_DIM:]).astype(BF16) for vn in vns]
            upds = [jnp.dot(kdts[pr][:, 2 * c * CHUNK:2 * (c + 1) * CHUNK], v_bd,
                            preferred_element_type=F32) for pr, v_bd in enumerate(v_bds)]
            tile = slice((c // 2) * LANES, (c // 2 + 1) * LANES)
            o_intra = [jnp.dot(qks[pr][c % 2][:, tile], v_bd, preferred_element_type=F32)
                       for pr, v_bd in enumerate(v_bds)]
            ge = (c + 1) * CHUNK - 1
            for pr in range(npair):
                egl = jnp.concatenate(
                    [jnp.broadcast_to(jnp.exp(gc[ge:ge + 1, N_DHEADS + hh:N_DHEADS + hh + 1]),
                                      (1, HEAD_DIM)) for hh in (2 * pr, 2 * pr + 1)], axis=1)
                ss[pr] = ss[pr] * egl + upds[pr]
                o_pair = rs[pr][CHUNK:] + o_intra[pr]
                for hl in range(2):
                    hd = 2 * pr + hl
                    o = _rms_rows(o_pair[:, hl * HEAD_DIM:(hl + 1) * HEAD_DIM], dng_ref[...])
                    gate = rest[q, cs, R_GATE_D + hd * HEAD_DIM:R_GATE_D + (hd + 1) * HEAD_DIM]
                    obuf[q, cs, hd * HEAD_DIM:(hd + 1) * HEAD_DIM] = o * _silu(gate)
            yield
            acc = None
            for j in range(CONF_W):
                off = CTAIL - (CONF_W - 1) + j
                start = c * CHUNK + (off // SUBLANES) * SUBLANES
                sh = off % SUBLANES
                src = (cbuf[q, start:start + CHUNK, :] if sh == 0
                       else cshift[q, sh - 1, start:start + CHUNK, :])
                term = src * ccw_ref[j:j + 1, :]
                acc = term if acc is None else acc + term
            yc = acc + ccb_ref[...]
            mu = jnp.mean(yc, axis=-1, keepdims=True)
            var = jnp.mean(jnp.square(yc - mu), axis=-1, keepdims=True)
            yc = (yc - mu) * lax.rsqrt(var + EPS) * clg_ref[...] + clb_ref[...]
            obuf[q, cs, D_DELTA + D_SCONV:] = _silu(yc) * _silu(rest[q, cs, R_GATE_C:R_GATE_C + D_CONF])
            yield
        for pr in range(npair):
            s_scr[pr] = ss[pr]

        y = x2_ref[0] + jnp.dot(obuf[q].astype(BF16), wout_ref[...], preferred_element_type=F32)
        if final_norm:
            y = _rms_rows(y, fg_ref[...])
        y_ref[0] = y

    _interleave(stage1(), stage2(), steps=(2, 1))

    @pl.when(jnp.logical_and(g >= 1, t2 == nt - 1))
    def _():
        for pr in range(npair):
            s_out_ref[0, 2 * pr] = s_scr[pr, :, :HEAD_DIM]
            s_out_ref[0, 2 * pr + 1] = s_scr[pr, :, HEAD_DIM:]

    @pl.when(jnp.logical_and(g < n_tiles, t1 == nt - 1))
    def _():
        qt_ref[0] = qbuf[0:SUBLANES, :]
        st_ref[0] = sbuf[0:SUBLANES, :]
        ct_ref[0] = cbuf[p, tt:tt + CTAIL, :]


def _prompt_layer_pipelined(x, wts, layer, final_norm):
    b, t, d = x.shape
    nt = t // TIME_TILE
    n_tiles = b * nt
    npair = N_DHEADS // 2

    def tile1(g):
        g1 = jnp.minimum(g, n_tiles - 1)
        return g1 // nt, g1 % nt

    def tile2(g):
        g2 = jnp.maximum(g - 1, 0)
        return g2 // nt, g2 % nt

    out_shape = (
        jax.ShapeDtypeStruct((b, t, d), F32),
        jax.ShapeDtypeStruct((b, N_DHEADS, HEAD_DIM, HEAD_DIM), F32),
        jax.ShapeDtypeStruct((b, SUBLANES, 3 * D_DELTA), F32),
        jax.ShapeDtypeStruct((b, SUBLANES, D_SCONV), F32),
        jax.ShapeDtypeStruct((b, CTAIL, D_CONF), F32),
    )
    out_specs = (
        pl.BlockSpec((1, TIME_TILE, d), lambda g: tile2(g) + (0,)),
        pl.BlockSpec((1, N_DHEADS, HEAD_DIM, HEAD_DIM), lambda g: (tile2(g)[0], 0, 0, 0)),
        pl.BlockSpec((1, SUBLANES, 3 * D_DELTA), lambda g: (tile1(g)[0], 0, 0)),
        pl.BlockSpec((1, SUBLANES, D_SCONV), lambda g: (tile1(g)[0], 0, 0)),
        pl.BlockSpec((1, CTAIL, D_CONF), lambda g: (tile1(g)[0], 0, 0)),
    )
    scratch = [
        pltpu.VMEM((TIME_TILE + SUBLANES, 3 * D_DELTA), F32),
        pltpu.VMEM((TIME_TILE + SUBLANES, D_SCONV), F32),
        pltpu.VMEM((npair, HEAD_DIM, 2 * HEAD_DIM), F32),
        pltpu.VMEM((2, TIME_TILE, 3 * D_DELTA), F32),
        pltpu.VMEM((2, 2, TIME_TILE, LANES), F32),
        pltpu.VMEM((2, TIME_TILE, R_WIDTH), F32),
        pltpu.VMEM((2, TIME_TILE + CTAIL, D_CONF), F32),
        pltpu.VMEM((2, SUBLANES - 1, TIME_TILE + CTAIL - SUBLANES, D_CONF), F32),
        pltpu.VMEM((2, TIME_TILE, D_MODEL), F32),
    ]
    return pl.pallas_call(
        functools.partial(_pipe_kernel, final_norm, nt),
        grid=(n_tiles + 1,),
        in_specs=[pl.BlockSpec((1, TIME_TILE, d), lambda g: tile1(g) + (0,)),
                  pl.BlockSpec((1, TIME_TILE, d), lambda g: tile2(g) + (0,))]
        + _weight_specs(wts, lambda g: layer, buffers=1),
        out_specs=out_specs,
        out_shape=out_shape,
        scratch_shapes=scratch,
        compiler_params=pltpu.CompilerParams(
            dimension_semantics=("arbitrary",),
            vmem_limit_bytes=VMEM_LIMIT),
        name="prompt_layer",
    )(x, x, *wts)


def _sample_kernel(x_ref, sd_ref, sq_ref, ss_ref, sc_ref,
                   ng_ref, win_ref, cw_ref, alog_ref, dt_ref, dng_ref,
                   sw_ref, ccw_ref, ccb_ref, clg_ref, clb_ref, wout_ref, fg_ref,
                   y_ref, sd_out_ref, qt_ref, st_ref, ct_ref,
                   qbuf, sbuf, cbuf, obuf, ubuf, wqbuf, kdbuf, glbuf, oibuf, rest, xcarry):
    layer = pl.program_id(0)
    blk = pl.program_id(1)
    nb = SAMPLE_BATCH_BLOCK
    pad = SAMPLE_PAD
    ntok = pad // 2
    rows = nb * pad

    @pl.when(layer == 0)
    def _():
        xcarry[blk, :, 0:ntok, :] = x_ref[...]
        xcarry[blk, :, ntok:, :] = jnp.zeros((nb, pad - ntok, D_MODEL), F32)

    x = xcarry[blk].reshape(rows, D_MODEL)
    h = _rms_rows(x, ng_ref[...]).astype(BF16)
    qkv = jnp.dot(h, win_ref[:, :W_REST], preferred_element_type=F32)
    rest[...] = jnp.dot(h, win_ref[:, W_REST:], preferred_element_type=F32)
    row = lax.broadcasted_iota(jnp.int32, (rows, LANES), 0)
    valid = (row % pad) < (pad // 2)
    valid1 = valid[:, 0:1]

    qkv3 = qkv.reshape(nb, pad, 3 * D_DELTA)
    qt_ref[...] = qkv3[:, ntok - (QK_CONV - 1):ntok, :]
    qbuf[:, SUBLANES - (QK_CONV - 1):SUBLANES, :] = sq_ref[...]
    qbuf[:, SUBLANES:, :] = qkv3

    hs = (rest[:, R_SC:R_SC + D_SCONV] * rest[:, R_SX:R_SX + D_SCONV]).reshape(nb, pad, D_SCONV)
    st_ref[...] = hs[:, ntok - (SCONV_W - 1):ntok, :]
    sbuf[:, SUBLANES - (SCONV_W - 1):SUBLANES, :] = ss_ref[...]
    sbuf[:, SUBLANES:, :] = hs
    ys = _branch_s(sbuf, sw_ref, SUBLANES, pad, lead=(slice(None),))
    obuf[:, D_DELTA:D_DELTA + D_SCONV] = (rest[:, R_SB:R_SB + D_SCONV] * ys.reshape(rows, D_SCONV)
                                          * _silu(rest[:, R_GATE_S:R_GATE_S + D_SCONV]))

    uc = rest[:, R_GA:R_GA + D_CONF] * _sigmoid(rest[:, R_GB:R_GB + D_CONF])
    cbuf[:, CTAIL - (CONF_W - 1):CTAIL, :] = sc_ref[...]
    cbuf[:, CTAIL:, :] = uc.reshape(nb, pad, D_CONF)
    first = CTAIL + ntok - (CONF_W - 1)
    ct_ref[...] = cbuf[:, first:first + CONF_W - 1, :]
    yc = _conformer(cbuf, ccw_ref, ccb_ref, clg_ref, clb_ref, CTAIL, pad, lead=(slice(None),))
    obuf[:, D_DELTA + D_SCONV:] = yc.reshape(rows, D_CONF) * _silu(rest[:, R_GATE_C:R_GATE_C + D_CONF])

    ba = rest[:, R_BA:R_BA + LANES]
    beta_all = jnp.where(valid, _sigmoid(ba), 0.0)
    g_all = jnp.where(valid, -jnp.exp(alog_ref[...]) * _softplus(ba + dt_ref[...]), 0.0)
    gc = _chunk_cumsum(g_all, pad, row)
    gct = gc.T
    gl_all = jnp.broadcast_to(
        gc.reshape(nb, pad, LANES)[:, pad - 1:pad, :], (nb, pad, LANES)).reshape(rows, LANES)
    pk = _Packed(pad, rows, False)

    def conv_act(c0):
        acc = None
        for j in range(QK_CONV):
            term = (qbuf[:, pl.ds(SUBLANES - (QK_CONV - 1) + j, pad), c0:c0 + HEAD_DIM]
                    * cw_ref[j:j + 1, c0:c0 + HEAD_DIM].reshape(1, 1, HEAD_DIM))
            acc = term if acc is None else acc + term
        return jnp.where(valid1, _silu(acc.reshape(rows, HEAD_DIM)), 0.0)

    for pr in range(N_DHEADS // 2):
        h0, h1 = 2 * pr, 2 * pr + 1
        heads, kdecs, egls = [], [], []
        for hd in (h0, h1):
            q = conv_act(hd * HEAD_DIM)
            k = conv_act(D_DELTA + hd * HEAD_DIM)
            v = conv_act(2 * D_DELTA + hd * HEAD_DIM)
            q = q * lax.rsqrt(jnp.sum(q * q, axis=-1, keepdims=True) + EPS) * (HEAD_DIM ** -0.5)
            k = k * lax.rsqrt(jnp.sum(k * k, axis=-1, keepdims=True) + EPS)
            lane = N_DHEADS + hd
            gcol = gc[:, lane:lane + 1]
            heads.append((q, k, v, beta_all[:, hd:hd + 1], gcol, gct[lane:lane + 1, :]))
            gl = gl_all[:, lane:lane + 1]
            kdecs.append(k * jnp.exp(gl - gcol))
            egls.append(jnp.broadcast_to(jnp.exp(gl), (rows, HEAD_DIM)))
        (u0, w0, qg0, qkp0), (u1, w1, qg1, qkp1) = _run(pk.delta_prep(heads, pad // 2))
        (kd0, kd1), (egl0, egl1) = kdecs, egls
        ubuf[...] = jnp.concatenate([u0, u1], axis=1)
        wqbuf[:, 0:pad, :] = jnp.concatenate([w0, w1], axis=1).reshape(nb, pad, 2 * HEAD_DIM)
        wqbuf[:, pad:, :] = jnp.concatenate([qg0, qg1], axis=1).reshape(nb, pad, 2 * HEAD_DIM)
        kdbuf[:, 0:pad, :] = kd0.reshape(nb, pad, HEAD_DIM)
        kdbuf[:, pad:, :] = kd1.reshape(nb, pad, HEAD_DIM)
        glbuf[...] = jnp.concatenate([egl0, egl1], axis=1).reshape(nb, pad, 2 * HEAD_DIM)

        def body(it, carry):
            bis = [it * SAMPLE_SEQ_LOCKSTEP + t for t in range(SAMPLE_SEQ_LOCKSTEP)]
            r0s = [pl.multiple_of(bi * pad, pad) for bi in bis]
            s0s = [sd_ref[bi, h0] for bi in bis]
            s1s = [sd_ref[bi, h1] for bi in bis]
            rs = [_mm(wqbuf[bi], _pair_diag(s0, s1)) for bi, s0, s1 in zip(bis, s0s, s1s)]
            vns = [ubuf[pl.ds(r0, pad), :] - r[:pad] for r0, r in zip(r0s, rs)]
            upds = [_mm_tn(kdbuf[bi], _pair_diag(vn[:, :HEAD_DIM], vn[:, HEAD_DIM:]))
                    for bi, vn in zip(bis, vns)]
            for bi, r0, s0, s1, r, vn, upd in zip(bis, r0s, s0s, s1s, rs, vns, upds):
                ubuf[pl.ds(r0, pad), :] = vn
                oibuf[pl.ds(r0, pad), :] = r[pad:]
                egl = glbuf[bi][0:1, :]
                sd_out_ref[bi, h0] = s0 * egl[:, :HEAD_DIM] + upd[:, :HEAD_DIM]
                sd_out_ref[bi, h1] = s1 * egl[:, HEAD_DIM:] + upd[:, HEAD_DIM:]
            return carry

        lax.fori_loop(0, nb // SAMPLE_SEQ_LOCKSTEP, body, 0)
        for hh, qkp in ((h0, qkp0), (h1, qkp1)):
            ls = slice((hh - h0) * HEAD_DIM, (hh - h0 + 1) * HEAD_DIM)
            o = oibuf[:, ls] + _mm(pk.block_diag(qkp), ubuf[:, ls])
            o = _rms_rows(o, dng_ref[...])
            obuf[:, hh * HEAD_DIM:(hh + 1) * HEAD_DIM] = (
                o * _silu(rest[:, R_GATE_D + hh * HEAD_DIM:R_GATE_D + (hh + 1) * HEAD_DIM]))

    y = x + jnp.dot(obuf[...].astype(BF16), wout_ref[...], preferred_element_type=F32)
    xcarry[blk] = y.reshape(nb, pad, D_MODEL)
    is_last = layer == pl.num_programs(0) - 1
    y_ref[...] = jnp.where(is_last, _rms_rows(y, fg_ref[...]), y).reshape(nb, pad, D_MODEL)[:, 0:ntok, :]


def _sample_layers(x, sd, sq, ss, sc, wts):
    depth = sd.shape[0]
    b, ntok, _ = x.shape
    nb = SAMPLE_BATCH_BLOCK

    def bspec(shape):
        return pl.BlockSpec((nb,) + shape, lambda l, bi: (bi,) + (0,) * len(shape))

    def lspec(shape):
        return pl.BlockSpec((None, nb) + shape, lambda l, bi: (l, bi) + (0,) * len(shape))

    def lshape(shape):
        return jax.ShapeDtypeStruct((depth, b) + shape, F32)

    state_shapes = ((N_DHEADS, HEAD_DIM, HEAD_DIM), (QK_CONV - 1, 3 * D_DELTA),
                    (SCONV_W - 1, D_SCONV), (CONF_W - 1, D_CONF))
    out_shape = (lshape((ntok, D_MODEL)),) + tuple(lshape(s) for s in state_shapes)
    out_specs = (lspec((ntok, D_MODEL)),) + tuple(lspec(s) for s in state_shapes)
    rows = nb * SAMPLE_PAD
    scratch = [
        pltpu.VMEM((nb, 2 * SUBLANES, 3 * D_DELTA), F32),
        pltpu.VMEM((nb, 2 * SUBLANES, D_SCONV), F32),
        pltpu.VMEM((nb, CTAIL + SAMPLE_PAD, D_CONF), F32),
        pltpu.VMEM((rows, D_MODEL), F32),
        pltpu.VMEM((rows, 2 * HEAD_DIM), F32),
        pltpu.VMEM((nb, 2 * SAMPLE_PAD, 2 * HEAD_DIM), F32),
        pltpu.VMEM((nb, 2 * SAMPLE_PAD, HEAD_DIM), F32),
        pltpu.VMEM((nb, SAMPLE_PAD, 2 * HEAD_DIM), F32),
        pltpu.VMEM((rows, 2 * HEAD_DIM), F32),
        pltpu.VMEM((rows, R_WIDTH), F32),
        pltpu.VMEM((b // nb, nb, SAMPLE_PAD, D_MODEL), F32),
    ]
    in_specs = ([bspec((ntok, D_MODEL))] + [lspec(s) for s in state_shapes]
                + _weight_specs(wts, lambda l, bi: l))
    return pl.pallas_call(
        _sample_kernel,
        grid=(depth, b // nb),
        in_specs=in_specs,
        out_specs=out_specs,
        out_shape=out_shape,
        scratch_shapes=scratch,
        compiler_params=pltpu.CompilerParams(
            dimension_semantics=("arbitrary", "arbitrary"),
            vmem_limit_bytes=VMEM_LIMIT),
        name="sample_layers",
    )(x, sd, sq, ss, sc, *wts)


def _reorder_kernel(w_ref, o_ref):
    n_qkv = 3 * D_DELTA
    n_ba = 2 * N_DHEADS
    rows = w_ref.shape[0]
    o_ref[:, 0:n_qkv] = w_ref[:, 0:n_qkv].astype(BF16)
    o_ref[:, n_qkv:n_qkv + R_BA] = w_ref[:, n_qkv + n_ba:].astype(BF16)
    o_ref[:, n_qkv + R_BA:] = jnp.concatenate(
        [w_ref[:, n_qkv:n_qkv + n_ba], jnp.zeros((rows, LANES - n_ba), F32)], axis=1).astype(BF16)


def _reorder_in_proj(w_in):
    depth, d, d_in = w_in.shape
    rows = REORDER_ROW_BLOCK
    return pl.pallas_call(
        _reorder_kernel,
        grid=(depth, d // rows),
        in_specs=[pl.BlockSpec((None, rows, d_in), lambda l, r: (l, r, 0))],
        out_specs=pl.BlockSpec((None, rows, W_REST + R_WIDTH), lambda l, r: (l, r, 0)),
        out_shape=jax.ShapeDtypeStruct((depth, d, W_REST + R_WIDTH), BF16),
        compiler_params=pltpu.CompilerParams(dimension_semantics=("arbitrary", "arbitrary")),
        name="reorder_in_proj",
    )(w_in)


def _stacked_weights(norm_g, w_in, conv_qkv_w, a_log, dt_bias, delta_norm_g, sconv_w,
                     cconv_w, cconv_b, cln_g, cln_b, w_out, final_norm_g):
    depth = w_in.shape[0]
    w_all = _reorder_in_proj(w_in)
    lane_pad = ((0, 0), (N_DHEADS, LANES - 2 * N_DHEADS))
    return (
        norm_g.reshape(depth, 1, D_MODEL),
        w_all,
        conv_qkv_w,
        jnp.pad(a_log, lane_pad).reshape(depth, 1, LANES),
        jnp.pad(dt_bias, lane_pad).reshape(depth, 1, LANES),
        delta_norm_g.reshape(depth, 1, HEAD_DIM),
        sconv_w,
        cconv_w,
        cconv_b.reshape(depth, 1, D_CONF),
        cln_g.reshape(depth, 1, D_CONF),
        cln_b.reshape(depth, 1, D_CONF),
        w_out.astype(BF16),
        final_norm_g.reshape(1, D_MODEL),
    )


def kernel(x_prompt, x_sample, state_delta, state_qkv_conv, state_sconv, state_cconv, norm_g, w_in, conv_qkv_w, a_log, dt_bias, delta_norm_g, sconv_w, cconv_w, cconv_b, cln_g, cln_b, w_out, final_norm_g):
    depth = w_in.shape[0]
    dec_seq = x_sample.shape[1]
    assert x_prompt.shape[1] % TIME_TILE == 0
    assert dec_seq == SAMPLE_PAD // 2 and x_sample.shape[0] % SAMPLE_BATCH_BLOCK == 0

    wts = _stacked_weights(norm_g, w_in, conv_qkv_w, a_log, dt_bias, delta_norm_g, sconv_w,
                           cconv_w, cconv_b, cln_g, cln_b, w_out, final_norm_g)

    xp = x_prompt
    p_outs = [[] for _ in range(4)]
    for l in range(depth):
        xp, pd, pq, ps, pc = _prompt_layer_pipelined(xp, wts, l, l == depth - 1)
        for acc, o in zip(p_outs, (pd, pq, ps, pc)):
            acc.append(o)
    pd, pq, ps, pc = (jnp.stack(o) for o in p_outs)

    xs, sd, sq, ss, sc = _sample_layers(x_sample, state_delta, state_qkv_conv, state_sconv,
                                        state_cconv, wts)
    return (xp, xs[depth - 1],
            pd, pq[:, :, SUBLANES - (QK_CONV - 1):, :], ps[:, :, SUBLANES - (SCONV_W - 1):, :],
            pc[:, :, CTAIL - (CONF_W - 1):, :],
            sd, sq, ss, sc)
```

```python
import functools

import jax
import jax.numpy as jnp
from jax import lax
from jax.experimental import pallas as pl
from jax.experimental.pallas import tpu as pltpu

D_MODEL = 1024
N_DHEADS = 4
HEAD_DIM = 128
D_DELTA = N_DHEADS * HEAD_DIM
D_SCONV = 256
D_CONF = 256
QK_CONV = 4
SCONV_W = 3
CONF_W = 31
CHUNK = 64
EPS = 1e-6

R_GATE_D = 0
R_SB = 512
R_SC = 768
R_SX = 1024
R_GATE_S = 1280
R_GA = 1536
R_GB = 1792
R_GATE_C = 2048
R_BA = 2304
R_WIDTH = 2432
W_REST = 3 * D_DELTA

SUBLANES = 8
LANES = 128
TIME_TILE = 256
SAMPLE_PAD = 8
SAMPLE_BATCH_BLOCK = 16
SAMPLE_SEQ_LOCKSTEP = 16
CTAIL = 32
REORDER_ROW_BLOCK = 256
MXU_COLS = 256
V7X_VMEM_BYTES = 64 * 1024 * 1024
VMEM_LIMIT = V7X_VMEM_BYTES - 8 * 1024 * 1024

F32 = jnp.float32
BF16 = jnp.bfloat16


def _mm(a, b):
    return jnp.dot(a.astype(BF16), b.astype(BF16), preferred_element_type=F32)


def _mm_nt(a, b):
    return lax.dot_general(a.astype(BF16), b.astype(BF16), (((1,), (1,)), ((), ())),
                           preferred_element_type=F32)


def _mm_tn(a, b):
    return lax.dot_general(a.astype(BF16), b.astype(BF16), (((0,), (0,)), ((), ())),
                           preferred_element_type=F32)


def _sigmoid(x):
    return 1.0 / (1.0 + jnp.exp(-x))


def _silu(x):
    h = 0.5 * x
    return h + h * jnp.tanh(h)


def _softplus(x):
    return jnp.maximum(x, 0.0) + jnp.log1p(jnp.exp(-jnp.abs(x)))


def _rms_rows(x, g):
    return x * lax.rsqrt(jnp.mean(x * x, axis=-1, keepdims=True) + EPS) * g


def _chunk_cumsum(g, chunk, row):
    pos = row % chunk
    s = 1
    while s < chunk:
        g = g + jnp.where(pos >= s, pltpu.roll(g, s, axis=0), 0.0)
        s *= 2
    return g


def _run(staged):
    try:
        while True:
            next(staged)
    except StopIteration as stop:
        return stop.value


def _interleave(*staged, steps=None):
    live = list(staged)
    steps = dict(zip(live, steps or [1] * len(live)))
    while live:
        for s in list(live):
            try:
                for _ in range(steps[s]):
                    next(s)
            except StopIteration:
                live.remove(s)


def _pair_diag(a, b):
    z = jnp.zeros_like(a)
    return jnp.concatenate([jnp.concatenate([a, z], axis=1), jnp.concatenate([z, b], axis=1)], axis=0)


class _Packed:
    def __init__(self, chunk, rows, wide_chunks):
        self.chunk, self.rows, self.n, self.wide_chunks = chunk, rows, rows // chunk, wide_chunks
        self.rr = lax.broadcasted_iota(jnp.int32, (chunk, rows), 0)
        lane = lax.broadcasted_iota(jnp.int32, (chunk, rows), 1)
        self.jl = lane % chunk
        self.lane_blk = lane // chunk
        ii = lax.broadcasted_iota(jnp.int32, (rows, rows), 0)
        jj = lax.broadcasted_iota(jnp.int32, (rows, rows), 1)
        mask = jnp.where((ii // chunk) == (jj // chunk), 1.0, 0.0).astype(F32)
        self.bd_mask = mask.astype(BF16) if wide_chunks else mask

    def pack(self, g):
        out = g[0:self.chunk]
        for c in range(1, self.n):
            out = jnp.where(self.lane_blk == c, g[c * self.chunk:(c + 1) * self.chunk], out)
        return out

    def col(self, v):
        shape = (self.chunk, self.rows)
        out = jnp.broadcast_to(v[0:self.chunk], shape)
        for c in range(1, self.n):
            out = jnp.where(self.lane_blk == c,
                            jnp.broadcast_to(v[c * self.chunk:(c + 1) * self.chunk], shape), out)
        return out

    def block_diag(self, xp):
        if self.wide_chunks:
            return jnp.concatenate([xp.astype(BF16)] * self.n, axis=0) * self.bd_mask
        return (jnp.concatenate([xp] * self.n, axis=0) * self.bd_mask).astype(BF16)

    def unit_lower_inverse(self, lps, nil):
        base = min(self.chunk, 16)
        same = (self.rr // base) == (self.jl // base)
        eye = jnp.where(self.rr == self.jl, 1.0, 0.0).astype(F32)
        ds = [jnp.where(same, lp, 0.0) for lp in lps]
        xs = [eye - d for d in ds]
        d_bds = [self.block_diag(d) for d in ds]
        p = 2
        while p < min(base, nil):
            ds = [_mm(d, d_bd) for d, d_bd in zip(ds, d_bds)]
            yield
            d_bds = [self.block_diag(d) for d in ds]
            xs = [x + _mm(x, d_bd) for x, d_bd in zip(xs, d_bds)]
            yield
            p *= 2
        size = base
        while size < self.chunk:
            big = (self.rr // (2 * size)) == (self.jl // (2 * size))
            off = jnp.logical_and(big, jnp.logical_not(same))
            xes = [_mm(x, self.block_diag(jnp.where(off, lp, 0.0))) for x, lp in zip(xs, lps)]
            yield
            xs = [x - _mm(xe, self.block_diag(x)) for x, xe in zip(xs, xes)]
            yield
            same = big
            size *= 2
        return xs

    def delta_prep(self, heads, nil):
        tril = self.rr >= self.jl
        strict = self.rr > self.jl
        decays, kbs, gs = [], [], []
        for q, k, v, beta, gc_col, gc_row in heads:
            diff = self.col(gc_col) - gc_row
            decays.append(jnp.where(tril, jnp.exp(jnp.where(tril, diff, 0.0)), 0.0))
            kbs.append(k * beta)
        if self.wide_chunks:
            c, n = self.chunk, self.n
            zero = jnp.zeros((c, HEAD_DIM), BF16)
            for (q, k, *_), kb in zip(heads, kbs):
                kbq = jnp.concatenate([kb, q], axis=1).astype(BF16)
                lhs = jnp.concatenate(
                    [jnp.concatenate([kbq[i * c:(i + 1) * c, :HEAD_DIM] for i in range(n)], axis=1),
                     jnp.concatenate([kbq[i * c:(i + 1) * c, HEAD_DIM:] for i in range(n)], axis=1)],
                    axis=0)
                kb16 = k.astype(BF16)
                rhs = jnp.concatenate(
                    [jnp.concatenate([kb16[i * c:(i + 1) * c] if j == i else zero for j in range(n)],
                                     axis=1) for i in range(n)], axis=0)
                gs.append(_mm_nt(lhs, rhs))
            yield
            lps = [jnp.where(strict, g[:c] * dec, 0.0) for g, dec in zip(gs, decays)]
            qkps = [g[c:] * dec for g, dec in zip(gs, decays)]
        else:
            for (q, k, *_), kb in zip(heads, kbs):
                gs.append(_mm_nt(jnp.concatenate([kb, q], axis=0), k))
            yield
            lps = [jnp.where(strict, self.pack(g[:self.rows]) * dec, 0.0) for g, dec in zip(gs, decays)]
            qkps = [self.pack(g[self.rows:]) * dec for g, dec in zip(gs, decays)]
        tinvs = yield from self.unit_lower_inverse(lps, nil)
        out = []
        for (q, k, v, beta, gc_col, _), kb, tinv, qkp in zip(heads, kbs, tinvs, qkps):
            eg = jnp.exp(gc_col)
            uw = _mm(self.block_diag(tinv), jnp.concatenate([v * beta, kb * eg], axis=1))
            out.append((uw[:, :HEAD_DIM], uw[:, HEAD_DIM:], q * eg, qkp))
        yield
        return out


def _branch_s(buf_ref, w_ref, tail, rows, lead=()):
    n = len(lead)
    acc = None
    for j in range(SCONV_W):
        idx = lead + (pl.ds(tail - (SCONV_W - 1) + j, rows), slice(None))
        term = buf_ref[idx] * w_ref[j:j + 1, :].reshape((1,) * n + (1, D_SCONV))
        acc = term if acc is None else acc + term
    return acc


def _conformer(ubuf_ref, w_ref, b_ref, g_ref, beta_ref, tail, rows, lead=()):
    n = len(lead)
    shp = (1,) * n + (1, D_CONF)
    acc = None
    for j in range(CONF_W):
        idx = lead + (pl.ds(tail - (CONF_W - 1) + j, rows), slice(None))
        term = ubuf_ref[idx] * w_ref[j:j + 1, :].reshape(shp)
        acc = term if acc is None else acc + term
    yc = acc + b_ref[...].reshape(shp)
    mu = jnp.mean(yc, axis=-1, keepdims=True)
    var = jnp.mean(jnp.square(yc - mu), axis=-1, keepdims=True)
    yc = (yc - mu) * lax.rsqrt(var + EPS) * g_ref[...].reshape(shp) + beta_ref[...].reshape(shp)
    return _silu(yc)


def _weight_specs(wts, layer_of, buffers=2):
    specs = []
    for w in wts[:-1]:
        tail = (0,) * (w.ndim - 1)
        specs.append(pl.BlockSpec((None,) + w.shape[1:], lambda *ids, tail=tail: (layer_of(*ids),) + tail,
                                  pipeline_mode=pl.Buffered(buffers)))
    specs.append(pl.BlockSpec(wts[-1].shape, lambda *ids: (0,) * wts[-1].ndim))
    return specs


def _pipe_kernel(final_norm, nt,
                 x_ref, x2_ref, ng_ref, win_ref, cw_ref, alog_ref, dt_ref, dng_ref,
                 sw_ref, ccw_ref, ccb_ref, clg_ref, clb_ref, wout_ref, fg_ref,
                 y_ref, s_out_ref, qt_ref, st_ref, ct_ref,
                 qbuf, sbuf, s_scr, qa, bg, rest, cbuf, cshift, obuf):
    g = pl.program_id(0)
    n_tiles = pl.num_programs(0) - 1
    tt = TIME_TILE
    nchunk = tt // CHUNK
    npair = N_DHEADS // 2
    t1 = jnp.minimum(g, n_tiles - 1) % nt
    t2 = jnp.maximum(g - 1, 0) % nt
    p = g % 2
    q = 1 - p

    @pl.when(g == 0)
    def _():
        qa[1] = jnp.zeros(qa.shape[1:], F32)
        bg[1] = jnp.zeros(bg.shape[1:], F32)
        rest[1] = jnp.zeros(rest.shape[1:], F32)
        cbuf[1] = jnp.zeros(cbuf.shape[1:], F32)
        obuf[1] = jnp.zeros(obuf.shape[1:], F32)

    @pl.when(t1 == 0)
    def _():
        qbuf[0:SUBLANES, :] = jnp.zeros((SUBLANES, 3 * D_DELTA), F32)
        sbuf[0:SUBLANES, :] = jnp.zeros((SUBLANES, D_SCONV), F32)

    @pl.when(t2 == 0)
    def _():
        s_scr[...] = jnp.zeros(s_scr.shape, F32)

    def stage1():
        h = _rms_rows(x_ref[0], ng_ref[...]).astype(BF16)
        yield
        for c0 in range(0, 3 * D_DELTA, MXU_COLS):
            qbuf[SUBLANES:SUBLANES + tt, c0:c0 + MXU_COLS] = jnp.dot(
                h, win_ref[:, c0:c0 + MXU_COLS], preferred_element_type=F32)
            yield
        for c0 in range(0, R_WIDTH, MXU_COLS):
            c1 = min(c0 + MXU_COLS, R_WIDTH)
            rest[p, :, c0:c1] = jnp.dot(h, win_ref[:, W_REST + c0:W_REST + c1],
                                        preferred_element_type=F32)
            yield

        def conv_act(c0):
            full = qbuf[:, c0:c0 + HEAD_DIM]
            acc = full[SUBLANES:] * cw_ref[QK_CONV - 1:QK_CONV, c0:c0 + HEAD_DIM]
            for s in range(1, QK_CONV):
                acc = acc + (pltpu.roll(full, s, axis=0)[SUBLANES:]
                             * cw_ref[QK_CONV - 1 - s:QK_CONV - s, c0:c0 + HEAD_DIM])
            return _silu(acc)

        for hd in range(N_DHEADS):
            qh = conv_act(hd * HEAD_DIM)
            qa[p, :, hd * HEAD_DIM:(hd + 1) * HEAD_DIM] = (
                qh * lax.rsqrt(jnp.sum(qh * qh, axis=-1, keepdims=True) + EPS) * (HEAD_DIM ** -0.5))
            kh = conv_act(D_DELTA + hd * HEAD_DIM)
            qa[p, :, D_DELTA + hd * HEAD_DIM:D_DELTA + (hd + 1) * HEAD_DIM] = (
                kh * lax.rsqrt(jnp.sum(kh * kh, axis=-1, keepdims=True) + EPS))
            qa[p, :, 2 * D_DELTA + hd * HEAD_DIM:2 * D_DELTA + (hd + 1) * HEAD_DIM] = conv_act(
                2 * D_DELTA + hd * HEAD_DIM)
            yield

        row = lax.broadcasted_iota(jnp.int32, (tt, LANES), 0)
        ba = rest[p, :, R_BA:R_BA + LANES]
        bg[p, 0] = _sigmoid(ba)
        bg[p, 1] = _chunk_cumsum(-jnp.exp(alog_ref[...]) * _softplus(ba + dt_ref[...]), CHUNK, row)
        yield

        sbuf[SUBLANES:SUBLANES + tt, :] = (rest[p, :, R_SC:R_SC + D_SCONV]
                                           * rest[p, :, R_SX:R_SX + D_SCONV])
        obuf[p, :, D_DELTA:D_DELTA + D_SCONV] = (
            rest[p, :, R_SB:R_SB + D_SCONV] * _branch_s(sbuf, sw_ref, SUBLANES, tt)
            * _silu(rest[p, :, R_GATE_S:R_GATE_S + D_SCONV]))
        yield

        prev_tail = jnp.where(t1 == 0, 0.0, cbuf[q, tt:tt + CTAIL, :])
        cbuf[p, 0:CTAIL, :] = prev_tail
        cbuf[p, CTAIL:CTAIL + tt, :] = (rest[p, :, R_GA:R_GA + D_CONF]
                                        * _sigmoid(rest[p, :, R_GB:R_GB + D_CONF]))
        yield
        for sh in range(1, SUBLANES):
            cshift[sh - 1] = cbuf[p, pl.ds(sh, tt + CTAIL - SUBLANES), :]
        yield

        for c in range(nchunk):
            cs = slice(c * CHUNK, (c + 1) * CHUNK)
            acc = None
            for j in range(CONF_W):
                off = CTAIL - (CONF_W - 1) + j
                start = c * CHUNK + (off // SUBLANES) * SUBLANES
                sh = off % SUBLANES
                src = (cbuf[p, start:start + CHUNK, :] if sh == 0
                       else cshift[sh - 1, start:start + CHUNK, :])
                term = src * ccw_ref[j:j + 1, :]
                acc = term if acc is None else acc + term
            yc = acc + ccb_ref[...]
            mu = jnp.mean(yc, axis=-1, keepdims=True)
            var = jnp.mean(jnp.square(yc - mu), axis=-1, keepdims=True)
            yc = (yc - mu) * lax.rsqrt(var + EPS) * clg_ref[...] + clb_ref[...]
            obuf[p, cs, D_DELTA + D_SCONV:] = _silu(yc) * _silu(rest[p, cs, R_GATE_C:R_GATE_C + D_CONF])
            yield

        qbuf[0:SUBLANES, :] = qbuf[tt:tt + SUBLANES, :]
        sbuf[0:SUBLANES, :] = sbuf[tt:tt + SUBLANES, :]

    def stage2():
        beta_all = bg[q, 0]
        gc = bg[q, 1]
        gct = gc.T
        gl_rows = jnp.concatenate(
            [jnp.broadcast_to(gc[(c + 1) * CHUNK - 1:(c + 1) * CHUNK, :], (CHUNK, LANES))
             for c in range(nchunk)], axis=0)
        pk = _Packed(CHUNK, tt, True)
        heads, kdecs = [], []
        for hd in range(N_DHEADS):
            lane = N_DHEADS + hd
            gcol = gc[:, lane:lane + 1]
            kh = qa[q, :, D_DELTA + hd * HEAD_DIM:D_DELTA + (hd + 1) * HEAD_DIM]
            heads.append((qa[q, :, hd * HEAD_DIM:(hd + 1) * HEAD_DIM], kh,
                          qa[q, :, 2 * D_DELTA + hd * HEAD_DIM:2 * D_DELTA + (hd + 1) * HEAD_DIM],
                          beta_all[:, hd:hd + 1], gcol, gct[lane:lane + 1, :]))
            kdecs.append(kh * jnp.exp(gl_rows[:, lane:lane + 1] - gcol))
        yield
        prepped = yield from pk.delta_prep(heads, CHUNK)

        us, lhss, kdts, qks = [], [], [], []
        low = (lax.broadcasted_iota(jnp.int32, (CHUNK, tt), 1) % LANES) < CHUNK
        for pr in range(npair):
            (u0, w0, qg0, qkp0), (u1, w1, qg1, qkp1) = prepped[2 * pr], prepped[2 * pr + 1]
            us.append(jnp.concatenate([u0, u1], axis=1))
            w = jnp.concatenate([w0, w1], axis=1).astype(BF16)
            qg = jnp.concatenate([qg0, qg1], axis=1).astype(BF16)
            lhss.append([jnp.concatenate([w[c * CHUNK:(c + 1) * CHUNK], qg[c * CHUNK:(c + 1) * CHUNK]],
                                         axis=0) for c in range(nchunk)])
            kst = jnp.concatenate([kd[c * CHUNK:(c + 1) * CHUNK] for c in range(nchunk)
                                   for kd in (kdecs[2 * pr], kdecs[2 * pr + 1])], axis=0)
            kdts.append(kst.T.astype(BF16))
            qks.append((jnp.where(low, qkp0, pltpu.roll(qkp1, CHUNK, axis=1)).astype(BF16),
                        jnp.where(low, pltpu.roll(qkp0, tt - CHUNK, axis=1), qkp1).astype(BF16)))
        yield

        ss = [s_scr[pr] for pr in range(npair)]
        for c in range(nchunk):
            cs = slice(c * CHUNK, (c + 1) * CHUNK)
            rs = [jnp.dot(lhss[pr][c], _pair_diag(s[:, :HEAD_DIM], s[:, HEAD_DIM:]).astype(BF16),
                          preferred_element_type=F32) for pr, s in enumerate(ss)]
            yield
            vns = [us[pr][cs] - r[:CHUNK] for pr, r in enumerate(rs)]
            v_bds = [_pair_diag(vn[:, :HEAD_DIM], vn[:, HEAD_DIM:]).astype(BF16) for vn in vns]
            upds = [jnp.dot(kdts[pr][:, 2 * c * CHUNK:2 * (c + 1) * CHUNK], v_bd,
                            preferred_element_type=F32) for pr, v_bd in enumerate(v_bds)]
            tile = slice((c // 2) * LANES, (c // 2 + 1) * LANES)
            o_intra = [jnp.dot(qks[pr][c % 2][:, tile], v_bd, preferred_element_type=F32)
                       for pr, v_bd in enumerate(v_bds)]
            ge = (c + 1) * CHUNK - 1
            for pr in range(npair):
                egl = jnp.concatenate(
                    [jnp.broadcast_to(jnp.exp(gc[ge:ge + 1, N_DHEADS + hh:N_DHEADS + hh + 1]),
                                      (1, HEAD_DIM)) for hh in (2 * pr, 2 * pr + 1)], axis=1)
                ss[pr] = ss[pr] * egl + upds[pr]
                o_pair = rs[pr][CHUNK:] + o_intra[pr]
                for hl in range(2):
                    hd = 2 * pr + hl
                    o = _rms_rows(o_pair[:, hl * HEAD_DIM:(hl + 1) * HEAD_DIM], dng_ref[...])
                    gate = rest[q, cs, R_GATE_D + hd * HEAD_DIM:R_GATE_D + (hd + 1) * HEAD_DIM]
                    obuf[q, cs, hd * HEAD_DIM:(hd + 1) * HEAD_DIM] = o * _silu(gate)
            yield
        for pr in range(npair):
            s_scr[pr] = ss[pr]

        y = x2_ref[0] + jnp.dot(obuf[q].astype(BF16), wout_ref[...], preferred_element_type=F32)
        if final_norm:
            y = _rms_rows(y, fg_ref[...])
        y_ref[0] = y

    _interleave(stage1(), stage2(), steps=(2, 1))

    @pl.when(jnp.logical_and(g >= 1, t2 == nt - 1))
    def _():
        for pr in range(npair):
            s_out_ref[0, 2 * pr] = s_scr[pr, :, :HEAD_DIM]
            s_out_ref[0, 2 * pr + 1] = s_scr[pr, :, HEAD_DIM:]

    @pl.when(jnp.logical_and(g < n_tiles, t1 == nt - 1))
    def _():
        qt_ref[0] = qbuf[0:SUBLANES, :]
        st_ref[0] = sbuf[0:SUBLANES, :]
        ct_ref[0] = cbuf[p, tt:tt + CTAIL, :]


def _prompt_layer_pipelined(x, wts, layer, final_norm):
    b, t, d = x.shape
    nt = t // TIME_TILE
    n_tiles = b * nt
    npair = N_DHEADS // 2

    def tile1(g):
        g1 = jnp.minimum(g, n_tiles - 1)
        return g1 // nt, g1 % nt

    def tile2(g):
        g2 = jnp.maximum(g - 1, 0)
        return g2 // nt, g2 % nt

    out_shape = (
        jax.ShapeDtypeStruct((b, t, d), F32),
        jax.ShapeDtypeStruct((b, N_DHEADS, HEAD_DIM, HEAD_DIM), F32),
        jax.ShapeDtypeStruct((b, SUBLANES, 3 * D_DELTA), F32),
        jax.ShapeDtypeStruct((b, SUBLANES, D_SCONV), F32),
        jax.ShapeDtypeStruct((b, CTAIL, D_CONF), F32),
    )
    out_specs = (
        pl.BlockSpec((1, TIME_TILE, d), lambda g: tile2(g) + (0,)),
        pl.BlockSpec((1, N_DHEADS, HEAD_DIM, HEAD_DIM), lambda g: (tile2(g)[0], 0, 0, 0)),
        pl.BlockSpec((1, SUBLANES, 3 * D_DELTA), lambda g: (tile1(g)[0], 0, 0)),
        pl.BlockSpec((1, SUBLANES, D_SCONV), lambda g: (tile1(g)[0], 0, 0)),
        pl.BlockSpec((1, CTAIL, D_CONF), lambda g: (tile1(g)[0], 0, 0)),
    )
    scratch = [
        pltpu.VMEM((TIME_TILE + SUBLANES, 3 * D_DELTA), F32),
        pltpu.VMEM((TIME_TILE + SUBLANES, D_SCONV), F32),
        pltpu.VMEM((npair, HEAD_DIM, 2 * HEAD_DIM), F32),
        pltpu.VMEM((2, TIME_TILE, 3 * D_DELTA), F32),
        pltpu.VMEM((2, 2, TIME_TILE, LANES), F32),
        pltpu.VMEM((2, TIME_TILE, R_WIDTH), F32),
        pltpu.VMEM((2, TIME_TILE + CTAIL, D_CONF), F32),
        pltpu.VMEM((SUBLANES - 1, TIME_TILE + CTAIL - SUBLANES, D_CONF), F32),
        pltpu.VMEM((2, TIME_TILE, D_MODEL), F32),
    ]
    return pl.pallas_call(
        functools.partial(_pipe_kernel, final_norm, nt),
        grid=(n_tiles + 1,),
        in_specs=[pl.BlockSpec((1, TIME_TILE, d), lambda g: tile1(g) + (0,)),
                  pl.BlockSpec((1, TIME_TILE, d), lambda g: tile2(g) + (0,))]
        + _weight_specs(wts, lambda g: layer, buffers=1),
        out_specs=out_specs,
        out_shape=out_shape,
        scratch_shapes=scratch,
        compiler_params=pltpu.CompilerParams(
            dimension_semantics=("arbitrary",),
            vmem_limit_bytes=VMEM_LIMIT),
        name="prompt_layer",
    )(x, x, *wts)


def _sample_kernel(x_ref, sd_ref, sq_ref, ss_ref, sc_ref,
                   ng_ref, win_ref, cw_ref, alog_ref, dt_ref, dng_ref,
                   sw_ref, ccw_ref, ccb_ref, clg_ref, clb_ref, wout_ref, fg_ref,
                   y_ref, sd_out_ref, qt_ref, st_ref, ct_ref,
                   qbuf, sbuf, cbuf, obuf, ubuf, wqbuf, kdbuf, glbuf, oibuf, rest, xcarry):
    layer = pl.program_id(0)
    blk = pl.program_id(1)
    nb = SAMPLE_BATCH_BLOCK
    pad = SAMPLE_PAD
    ntok = pad // 2
    rows = nb * pad

    @pl.when(layer == 0)
    def _():
        xcarry[blk, :, 0:ntok, :] = x_ref[...]
        xcarry[blk, :, ntok:, :] = jnp.zeros((nb, pad - ntok, D_MODEL), F32)

    x = xcarry[blk].reshape(rows, D_MODEL)
    h = _rms_rows(x, ng_ref[...]).astype(BF16)
    qkv = jnp.dot(h, win_ref[:, :W_REST], preferred_element_type=F32)
    rest[...] = jnp.dot(h, win_ref[:, W_REST:], preferred_element_type=F32)
    row = lax.broadcasted_iota(jnp.int32, (rows, LANES), 0)
    valid = (row % pad) < (pad // 2)
    valid1 = valid[:, 0:1]

    qkv3 = qkv.reshape(nb, pad, 3 * D_DELTA)
    qt_ref[...] = qkv3[:, ntok - (QK_CONV - 1):ntok, :]
    qbuf[:, SUBLANES - (QK_CONV - 1):SUBLANES, :] = sq_ref[...]
    qbuf[:, SUBLANES:, :] = qkv3

    hs = (rest[:, R_SC:R_SC + D_SCONV] * rest[:, R_SX:R_SX + D_SCONV]).reshape(nb, pad, D_SCONV)
    st_ref[...] = hs[:, ntok - (SCONV_W - 1):ntok, :]
    sbuf[:, SUBLANES - (SCONV_W - 1):SUBLANES, :] = ss_ref[...]
    sbuf[:, SUBLANES:, :] = hs
    ys = _branch_s(sbuf, sw_ref, SUBLANES, pad, lead=(slice(None),))
    obuf[:, D_DELTA:D_DELTA + D_SCONV] = (rest[:, R_SB:R_SB + D_SCONV] * ys.reshape(rows, D_SCONV)
                                          * _silu(rest[:, R_GATE_S:R_GATE_S + D_SCONV]))

    uc = rest[:, R_GA:R_GA + D_CONF] * _sigmoid(rest[:, R_GB:R_GB + D_CONF])
    cbuf[:, CTAIL - (CONF_W - 1):CTAIL, :] = sc_ref[...]
    cbuf[:, CTAIL:, :] = uc.reshape(nb, pad, D_CONF)
    first = CTAIL + ntok - (CONF_W - 1)
    ct_ref[...] = cbuf[:, first:first + CONF_W - 1, :]
    yc = _conformer(cbuf, ccw_ref, ccb_ref, clg_ref, clb_ref, CTAIL, pad, lead=(slice(None),))
    obuf[:, D_DELTA + D_SCONV:] = yc.reshape(rows, D_CONF) * _silu(rest[:, R_GATE_C:R_GATE_C + D_CONF])

    ba = rest[:, R_BA:R_BA + LANES]
    beta_all = jnp.where(valid, _sigmoid(ba), 0.0)
    g_all = jnp.where(valid, -jnp.exp(alog_ref[...]) * _softplus(ba + dt_ref[...]), 0.0)
    gc = _chunk_cumsum(g_all, pad, row)
    gct = gc.T
    gl_all = jnp.broadcast_to(
        gc.reshape(nb, pad, LANES)[:, pad - 1:pad, :], (nb, pad, LANES)).reshape(rows, LANES)
    pk = _Packed(pad, rows, False)

    def conv_act(c0):
        acc = None
        for j in range(QK_CONV):
            term = (qbuf[:, pl.ds(SUBLANES - (QK_CONV - 1) + j, pad), c0:c0 + HEAD_DIM]
                    * cw_ref[j:j + 1, c0:c0 + HEAD_DIM].reshape(1, 1, HEAD_DIM))
            acc = term if acc is None else acc + term
        return jnp.where(valid1, _silu(acc.reshape(rows, HEAD_DIM)), 0.0)

    for pr in range(N_DHEADS // 2):
        h0, h1 = 2 * pr, 2 * pr + 1
        heads, kdecs, egls = [], [], []
        for hd in (h0, h1):
            q = conv_act(hd * HEAD_DIM)
            k = conv_act(D_DELTA + hd * HEAD_DIM)
            v = conv_act(2 * D_DELTA + hd * HEAD_DIM)
            q = q * lax.rsqrt(jnp.sum(q * q, axis=-1, keepdims=True) + EPS) * (HEAD_DIM ** -0.5)
            k = k * lax.rsqrt(jnp.sum(k * k, axis=-1, keepdims=True) + EPS)
            lane = N_DHEADS + hd
            gcol = gc[:, lane:lane + 1]
            heads.append((q, k, v, beta_all[:, hd:hd + 1], gcol, gct[lane:lane + 1, :]))
            gl = gl_all[:, lane:lane + 1]
            kdecs.append(k * jnp.exp(gl - gcol))
            egls.append(jnp.broadcast_to(jnp.exp(gl), (rows, HEAD_DIM)))
        (u0, w0, qg0, qkp0), (u1, w1, qg1, qkp1) = _run(pk.delta_prep(heads, pad // 2))
        (kd0, kd1), (egl0, egl1) = kdecs, egls
        ubuf[...] = jnp.concatenate([u0, u1], axis=1)
        wqbuf[:, 0:pad, :] = jnp.concatenate([w0, w1], axis=1).reshape(nb, pad, 2 * HEAD_DIM)
        wqbuf[:, pad:, :] = jnp.concatenate([qg0, qg1], axis=1).reshape(nb, pad, 2 * HEAD_DIM)
        kdbuf[:, 0:pad, :] = kd0.reshape(nb, pad, HEAD_DIM)
        kdbuf[:, pad:, :] = kd1.reshape(nb, pad, HEAD_DIM)
        glbuf[...] = jnp.concatenate([egl0, egl1], axis=1).reshape(nb, pad, 2 * HEAD_DIM)

        def body(it, carry):
            bis = [it * SAMPLE_SEQ_LOCKSTEP + t for t in range(SAMPLE_SEQ_LOCKSTEP)]
            r0s = [pl.multiple_of(bi * pad, pad) for bi in bis]
            s0s = [sd_ref[bi, h0] for bi in bis]
            s1s = [sd_ref[bi, h1] for bi in bis]
            rs = [_mm(wqbuf[bi], _pair_diag(s0, s1)) for bi, s0, s1 in zip(bis, s0s, s1s)]
            vns = [ubuf[pl.ds(r0, pad), :] - r[:pad] for r0, r in zip(r0s, rs)]
            upds = [_mm_tn(kdbuf[bi], _pair_diag(vn[:, :HEAD_DIM], vn[:, HEAD_DIM:]))
                    for bi, vn in zip(bis, vns)]
            for bi, r0, s0, s1, r, vn, upd in zip(bis, r0s, s0s, s1s, rs, vns, upds):
                ubuf[pl.ds(r0, pad), :] = vn
                oibuf[pl.ds(r0, pad), :] = r[pad:]
                egl = glbuf[bi][0:1, :]
                sd_out_ref[bi, h0] = s0 * egl[:, :HEAD_DIM] + upd[:, :HEAD_DIM]
                sd_out_ref[bi, h1] = s1 * egl[:, HEAD_DIM:] + upd[:, HEAD_DIM:]
            return carry

        lax.fori_loop(0, nb // SAMPLE_SEQ_LOCKSTEP, body, 0)
        for hh, qkp in ((h0, qkp0), (h1, qkp1)):
            ls = slice((hh - h0) * HEAD_DIM, (hh - h0 + 1) * HEAD_DIM)
            o = oibuf[:, ls] + _mm(pk.block_diag(qkp), ubuf[:, ls])
            o = _rms_rows(o, dng_ref[...])
            obuf[:, hh * HEAD_DIM:(hh + 1) * HEAD_DIM] = (
                o * _silu(rest[:, R_GATE_D + hh * HEAD_DIM:R_GATE_D + (hh + 1) * HEAD_DIM]))

    y = x + jnp.dot(obuf[...].astype(BF16), wout_ref[...], preferred_element_type=F32)
    xcarry[blk] = y.reshape(nb, pad, D_MODEL)
    is_last = layer == pl.num_programs(0) - 1
    y_ref[...] = jnp.where(is_last, _rms_rows(y, fg_ref[...]), y).reshape(nb, pad, D_MODEL)[:, 0:ntok, :]


def _sample_layers(x, sd, sq, ss, sc, wts):
    depth = sd.shape[0]
    b, ntok, _ = x.shape
    nb = SAMPLE_BATCH_BLOCK

    def bspec(shape):
        return pl.BlockSpec((nb,) + shape, lambda l, bi: (bi,) + (0,) * len(shape))

    def lspec(shape):
        return pl.BlockSpec((None, nb) + shape, lambda l, bi: (l, bi) + (0,) * len(shape))

    def lshape(shape):
        return jax.ShapeDtypeStruct((depth, b) + shape, F32)

    state_shapes = ((N_DHEADS, HEAD_DIM, HEAD_DIM), (QK_CONV - 1, 3 * D_DELTA),
                    (SCONV_W - 1, D_SCONV), (CONF_W - 1, D_CONF))
    out_shape = (lshape((ntok, D_MODEL)),) + tuple(lshape(s) for s in state_shapes)
    out_specs = (lspec((ntok, D_MODEL)),) + tuple(lspec(s) for s in state_shapes)
    rows = nb * SAMPLE_PAD
    scratch = [
        pltpu.VMEM((nb, 2 * SUBLANES, 3 * D_DELTA), F32),
        pltpu.VMEM((nb, 2 * SUBLANES, D_SCONV), F32),
        pltpu.VMEM((nb, CTAIL + SAMPLE_PAD, D_CONF), F32),
        pltpu.VMEM((rows, D_MODEL), F32),
        pltpu.VMEM((rows, 2 * HEAD_DIM), F32),
        pltpu.VMEM((nb, 2 * SAMPLE_PAD, 2 * HEAD_DIM), F32),
        pltpu.VMEM((nb, 2 * SAMPLE_PAD, HEAD_DIM), F32),
        pltpu.VMEM((nb, SAMPLE_PAD, 2 * HEAD_DIM), F32),
        pltpu.VMEM((rows, 2 * HEAD_DIM), F32),
        pltpu.VMEM((rows, R_WIDTH), F32),
        pltpu.VMEM((b // nb, nb, SAMPLE_PAD, D_MODEL), F32),
    ]
    in_specs = ([bspec((ntok, D_MODEL))] + [lspec(s) for s in state_shapes]
                + _weight_specs(wts, lambda l, bi: l))
    return pl.pallas_call(
        _sample_kernel,
        grid=(depth, b // nb),
        in_specs=in_specs,
        out_specs=out_specs,
        out_shape=out_shape,
        scratch_shapes=scratch,
        compiler_params=pltpu.CompilerParams(
            dimension_semantics=("arbitrary", "arbitrary"),
            vmem_limit_bytes=VMEM_LIMIT),
        name="sample_layers",
    )(x, sd, sq, ss, sc, *wts)


def _reorder_kernel(w_ref, o_ref):
    n_qkv = 3 * D_DELTA
    n_ba = 2 * N_DHEADS
    rows = w_ref.shape[0]
    o_ref[:, 0:n_qkv] = w_ref[:, 0:n_qkv].astype(BF16)
    o_ref[:, n_qkv:n_qkv + R_BA] = w_ref[:, n_qkv + n_ba:].astype(BF16)
    o_ref[:, n_qkv + R_BA:] = jnp.concatenate(
        [w_ref[:, n_qkv:n_qkv + n_ba], jnp.zeros((rows, LANES - n_ba), F32)], axis=1).astype(BF16)


def _reorder_in_proj(w_in):
    depth, d, d_in = w_in.shape
    rows = REORDER_ROW_BLOCK
    return pl.pallas_call(
        _reorder_kernel,
        grid=(depth, d // rows),
        in_specs=[pl.BlockSpec((None, rows, d_in), lambda l, r: (l, r, 0))],
        out_specs=pl.BlockSpec((None, rows, W_REST + R_WIDTH), lambda l, r: (l, r, 0)),
        out_shape=jax.ShapeDtypeStruct((depth, d, W_REST + R_WIDTH), BF16),
        compiler_params=pltpu.CompilerParams(dimension_semantics=("arbitrary", "arbitrary")),
        name="reorder_in_proj",
    )(w_in)


def _stacked_weights(norm_g, w_in, conv_qkv_w, a_log, dt_bias, delta_norm_g, sconv_w,
                     cconv_w, cconv_b, cln_g, cln_b, w_out, final_norm_g):
    depth = w_in.shape[0]
    w_all = _reorder_in_proj(w_in)
    lane_pad = ((0, 0), (N_DHEADS, LANES - 2 * N_DHEADS))
    return (
        norm_g.reshape(depth, 1, D_MODEL),
        w_all,
        conv_qkv_w,
        jnp.pad(a_log, lane_pad).reshape(depth, 1, LANES),
        jnp.pad(dt_bias, lane_pad).reshape(depth, 1, LANES),
        delta_norm_g.reshape(depth, 1, HEAD_DIM),
        sconv_w,
        cconv_w,
        cconv_b.reshape(depth, 1, D_CONF),
        cln_g.reshape(depth, 1, D_CONF),
        cln_b.reshape(depth, 1, D_CONF),
        w_out.astype(BF16),
        final_norm_g.reshape(1, D_MODEL),
    )


def kernel(x_prompt, x_sample, state_delta, state_qkv_conv, state_sconv, state_cconv, norm_g, w_in, conv_qkv_w, a_log, dt_bias, delta_norm_g, sconv_w, cconv_w, cconv_b, cln_g, cln_b, w_out, final_norm_g):
    depth = w_in.shape[0]
    dec_seq = x_sample.shape[1]
    assert x_prompt.shape[1] % TIME_TILE == 0
    assert dec_seq == SAMPLE_PAD // 2 and x_sample.shape[0] % SAMPLE_BATCH_BLOCK == 0

    wts = _stacked_weights(norm_g, w_in, conv_qkv_w, a_log, dt_bias, delta_norm_g, sconv_w,
                           cconv_w, cconv_b, cln_g, cln_b, w_out, final_norm_g)

    xp = x_prompt
    p_outs = [[] for _ in range(4)]
    for l in range(depth):
        xp, pd, pq, ps, pc = _prompt_layer_pipelined(xp, wts, l, l == depth - 1)
        for acc, o in zip(p_outs, (pd, pq, ps, pc)):
            acc.append(o)
    pd, pq, ps, pc = (jnp.stack(o) for o in p_outs)

    xs, sd, sq, ss, sc = _sample_layers(x_sample, state_delta, state_qkv_conv, state_sconv,
                                        state_cconv, wts)
    return (xp, xs[depth - 1],
            pd, pq[:, :, SUBLANES - (QK_CONV - 1):, :], ps[:, :, SUBLANES - (SCONV_W - 1):, :],
            pc[:, :, CTAIL - (CONF_W - 1):, :],
            sd, sq, ss, sc)
```

```python
import functools

import jax
import jax.numpy as jnp
from jax import lax
from jax.experimental import pallas as pl
from jax.experimental.pallas import tpu as pltpu

D_MODEL = 1024
N_DHEADS = 4
HEAD_DIM = 128
D_DELTA = N_DHEADS * HEAD_DIM
D_SCONV = 256
D_CONF = 256
QK_CONV = 4
SCONV_W = 3
CONF_W = 31
CHUNK = 64
EPS = 1e-6

R_GATE_D = 0
R_SB = 512
R_SC = 768
R_SX = 1024
R_GATE_S = 1280
R_GA = 1536
R_GB = 1792
R_GATE_C = 2048
R_BA = 2304
R_WIDTH = 2432
W_REST = 3 * D_DELTA

SUBLANES = 8
LANES = 128
TIME_TILE = 256
SAMPLE_PAD = 8
SAMPLE_BATCH_BLOCK = 16
SAMPLE_SEQ_LOCKSTEP = 16
CTAIL = 32
REORDER_ROW_BLOCK = 256
MXU_COLS = 256
V7X_VMEM_BYTES = 64 * 1024 * 1024
VMEM_LIMIT = V7X_VMEM_BYTES - 8 * 1024 * 1024

F32 = jnp.float32
BF16 = jnp.bfloat16


def _mm(a, b):
    return jnp.dot(a.astype(BF16), b.astype(BF16), preferred_element_type=F32)


def _mm_nt(a, b):
    return lax.dot_general(a.astype(BF16), b.astype(BF16), (((1,), (1,)), ((), ())),
                           preferred_element_type=F32)


def _mm_tn(a, b):
    return lax.dot_general(a.astype(BF16), b.astype(BF16), (((0,), (0,)), ((), ())),
                           preferred_element_type=F32)


def _sigmoid(x):
    return 1.0 / (1.0 + jnp.exp(-x))


def _silu(x):
    h = 0.5 * x
    return h + h * jnp.tanh(h)


def _softplus(x):
    return jnp.maximum(x, 0.0) + jnp.log1p(jnp.exp(-jnp.abs(x)))


def _rms_rows(x, g):
    return x * lax.rsqrt(jnp.mean(x * x, axis=-1, keepdims=True) + EPS) * g


def _chunk_cumsum(g, chunk, row):
    pos = row % chunk
    s = 1
    while s < chunk:
        g = g + jnp.where(pos >= s, pltpu.roll(g, s, axis=0), 0.0)
        s *= 2
    return g


def _run(staged):
    try:
        while True:
            next(staged)
    except StopIteration as stop:
        return stop.value


def _interleave(*staged, steps=None):
    live = list(staged)
    steps = dict(zip(live, steps or [1] * len(live)))
    while live:
        for s in list(live):
            try:
                for _ in range(steps[s]):
                    next(s)
            except StopIteration:
                live.remove(s)


def _pair_diag(a, b):
    z = jnp.zeros_like(a)
    return jnp.concatenate([jnp.concatenate([a, z], axis=1), jnp.concatenate([z, b], axis=1)], axis=0)


class _Packed:
    def __init__(self, chunk, rows, wide_chunks):
        self.chunk, self.rows, self.n, self.wide_chunks = chunk, rows, rows // chunk, wide_chunks
        self.rr = lax.broadcasted_iota(jnp.int32, (chunk, rows), 0)
        lane = lax.broadcasted_iota(jnp.int32, (chunk, rows), 1)
        self.jl = lane % chunk
        self.lane_blk = lane // chunk
        ii = lax.broadcasted_iota(jnp.int32, (rows, rows), 0)
        jj = lax.broadcasted_iota(jnp.int32, (rows, rows), 1)
        mask = jnp.where((ii // chunk) == (jj // chunk), 1.0, 0.0).astype(F32)
        self.bd_mask = mask.astype(BF16) if wide_chunks else mask

    def pack(self, g):
        out = g[0:self.chunk]
        for c in range(1, self.n):
            out = jnp.where(self.lane_blk == c, g[c * self.chunk:(c + 1) * self.chunk], out)
        return out

    def col(self, v):
        shape = (self.chunk, self.rows)
        out = jnp.broadcast_to(v[0:self.chunk], shape)
        for c in range(1, self.n):
            out = jnp.where(self.lane_blk == c,
                            jnp.broadcast_to(v[c * self.chunk:(c + 1) * self.chunk], shape), out)
        return out

    def block_diag(self, xp):
        if self.wide_chunks:
            return jnp.concatenate([xp.astype(BF16)] * self.n, axis=0) * self.bd_mask
        return (jnp.concatenate([xp] * self.n, axis=0) * self.bd_mask).astype(BF16)

    def unit_lower_inverse(self, lps, nil):
        base = min(self.chunk, 16)
        same = (self.rr // base) == (self.jl // base)
        eye = jnp.where(self.rr == self.jl, 1.0, 0.0).astype(F32)
        ds = [jnp.where(same, lp, 0.0) for lp in lps]
        xs = [eye - d for d in ds]
        d_bds = [self.block_diag(d) for d in ds]
        p = 2
        while p < min(base, nil):
            ds = [_mm(d, d_bd) for d, d_bd in zip(ds, d_bds)]
            yield
            d_bds = [self.block_diag(d) for d in ds]
            xs = [x + _mm(x, d_bd) for x, d_bd in zip(xs, d_bds)]
            yield
            p *= 2
        size = base
        while size < self.chunk:
            big = (self.rr // (2 * size)) == (self.jl // (2 * size))
            off = jnp.logical_and(big, jnp.logical_not(same))
            xes = [_mm(x, self.block_diag(jnp.where(off, lp, 0.0))) for x, lp in zip(xs, lps)]
            yield
            xs = [x - _mm(xe, self.block_diag(x)) for x, xe in zip(xs, xes)]
            yield
            same = big
            size *= 2
        return xs

    def delta_prep(self, heads, nil):
        tril = self.rr >= self.jl
        strict = self.rr > self.jl
        decays, kbs, gs = [], [], []
        for q, k, v, beta, gc_col, gc_row in heads:
            diff = self.col(gc_col) - gc_row
            decays.append(jnp.where(tril, jnp.exp(jnp.where(tril, diff, 0.0)), 0.0))
            kbs.append(k * beta)
        if self.wide_chunks:
            c, n = self.chunk, self.n
            zero = jnp.zeros((c, HEAD_DIM), BF16)
            for (q, k, *_), kb in zip(heads, kbs):
                kbq = jnp.concatenate([kb, q], axis=1).astype(BF16)
                lhs = jnp.concatenate(
                    [jnp.concatenate([kbq[i * c:(i + 1) * c, :HEAD_DIM] for i in range(n)], axis=1),
                     jnp.concatenate([kbq[i * c:(i + 1) * c, HEAD_DIM:] for i in range(n)], axis=1)],
                    axis=0)
                kb16 = k.astype(BF16)
                rhs = jnp.concatenate(
                    [jnp.concatenate([kb16[i * c:(i + 1) * c] if j == i else zero for j in range(n)],
                                     axis=1) for i in range(n)], axis=0)
                gs.append(_mm_nt(lhs, rhs))
            yield
            lps = [jnp.where(strict, g[:c] * dec, 0.0) for g, dec in zip(gs, decays)]
            qkps = [g[c:] * dec for g, dec in zip(gs, decays)]
        else:
            for (q, k, *_), kb in zip(heads, kbs):
                gs.append(_mm_nt(jnp.concatenate([kb, q], axis=0), k))
            yield
            lps = [jnp.where(strict, self.pack(g[:self.rows]) * dec, 0.0) for g, dec in zip(gs, decays)]
            qkps = [self.pack(g[self.rows:]) * dec for g, dec in zip(gs, decays)]
        tinvs = yield from self.unit_lower_inverse(lps, nil)
        out = []
        for (q, k, v, beta, gc_col, _), kb, tinv, qkp in zip(heads, kbs, tinvs, qkps):
            eg = jnp.exp(gc_col)
            uw = _mm(self.block_diag(tinv), jnp.concatenate([v * beta, kb * eg], axis=1))
            out.append((uw[:, :HEAD_DIM], uw[:, HEAD_DIM:], q * eg, qkp))
        yield
        return out


def _branch_s(buf_ref, w_ref, tail, rows, lead=()):
    n = len(lead)
    acc = None
    for j in range(SCONV_W):
        idx = lead + (pl.ds(tail - (SCONV_W - 1) + j, rows), slice(None))
        term = buf_ref[idx] * w_ref[j:j + 1, :].reshape((1,) * n + (1, D_SCONV))
        acc = term if acc is None else acc + term
    return acc


def _conformer(ubuf_ref, w_ref, b_ref, g_ref, beta_ref, tail, rows, lead=()):
    n = len(lead)
    shp = (1,) * n + (1, D_CONF)
    acc = None
    for j in range(CONF_W):
        idx = lead + (pl.ds(tail - (CONF_W - 1) + j, rows), slice(None))
        term = ubuf_ref[idx] * w_ref[j:j + 1, :].reshape(shp)
        acc = term if acc is None else acc + term
    yc = acc + b_ref[...].reshape(shp)
    mu = jnp.mean(yc, axis=-1, keepdims=True)
    var = jnp.mean(jnp.square(yc - mu), axis=-1, keepdims=True)
    yc = (yc - mu) * lax.rsqrt(var + EPS) * g_ref[...].reshape(shp) + beta_ref[...].reshape(shp)
    return _silu(yc)


def _weight_specs(wts, layer_of, buffers=2):
    specs = []
    for w in wts[:-1]:
        tail = (0,) * (w.ndim - 1)
        specs.append(pl.BlockSpec((None,) + w.shape[1:], lambda *ids, tail=tail: (layer_of(*ids),) + tail,
                                  pipeline_mode=pl.Buffered(buffers)))
    specs.append(pl.BlockSpec(wts[-1].shape, lambda *ids: (0,) * wts[-1].ndim))
    return specs


def _pipe_kernel(final_norm, nt,
                 x_ref, x2_ref, ng_ref, win_ref, cw_ref, alog_ref, dt_ref, dng_ref,
                 sw_ref, ccw_ref, ccb_ref, clg_ref, clb_ref, wout_ref, fg_ref,
                 y_ref, s_out_ref, qt_ref, st_ref, ct_ref,
                 qbuf, sbuf, s_scr, qa, bg, rest, cbuf, cshift, obuf, kd):
    g = pl.program_id(0)
    n_tiles = pl.num_programs(0) - 1
    tt = TIME_TILE
    nchunk = tt // CHUNK
    npair = N_DHEADS // 2
    t1 = jnp.minimum(g, n_tiles - 1) % nt
    t2 = jnp.maximum(g - 1, 0) % nt
    p = g % 2
    q = 1 - p

    @pl.when(g == 0)
    def _():
        qa[1] = jnp.zeros(qa.shape[1:], F32)
        bg[1] = jnp.zeros(bg.shape[1:], F32)
        rest[1] = jnp.zeros(rest.shape[1:], F32)
        cbuf[1] = jnp.zeros(cbuf.shape[1:], F32)
        obuf[1] = jnp.zeros(obuf.shape[1:], F32)
        kd[1] = jnp.zeros(kd.shape[1:], F32)

    @pl.when(t1 == 0)
    def _():
        qbuf[0:SUBLANES, :] = jnp.zeros((SUBLANES, 3 * D_DELTA), F32)
        sbuf[0:SUBLANES, :] = jnp.zeros((SUBLANES, D_SCONV), F32)

    @pl.when(t2 == 0)
    def _():
        s_scr[...] = jnp.zeros(s_scr.shape, F32)

    def stage1():
        h = _rms_rows(x_ref[0], ng_ref[...]).astype(BF16)
        yield
        for c0 in range(0, 3 * D_DELTA, MXU_COLS):
            qbuf[SUBLANES:SUBLANES + tt, c0:c0 + MXU_COLS] = jnp.dot(
                h, win_ref[:, c0:c0 + MXU_COLS], preferred_element_type=F32)
            yield
        for c0 in range(0, R_WIDTH, MXU_COLS):
            c1 = min(c0 + MXU_COLS, R_WIDTH)
            rest[p, :, c0:c1] = jnp.dot(h, win_ref[:, W_REST + c0:W_REST + c1],
                                        preferred_element_type=F32)
            yield

        def conv_act(c0):
            full = qbuf[:, c0:c0 + HEAD_DIM]
            acc = full[SUBLANES:] * cw_ref[QK_CONV - 1:QK_CONV, c0:c0 + HEAD_DIM]
            for s in range(1, QK_CONV):
                acc = acc + (pltpu.roll(full, s, axis=0)[SUBLANES:]
                             * cw_ref[QK_CONV - 1 - s:QK_CONV - s, c0:c0 + HEAD_DIM])
            return _silu(acc)

        for hd in range(N_DHEADS):
            qh = conv_act(hd * HEAD_DIM)
            qa[p, :, hd * HEAD_DIM:(hd + 1) * HEAD_DIM] = (
                qh * lax.rsqrt(jnp.sum(qh * qh, axis=-1, keepdims=True) + EPS) * (HEAD_DIM ** -0.5))
            kh = conv_act(D_DELTA + hd * HEAD_DIM)
            qa[p, :, D_DELTA + hd * HEAD_DIM:D_DELTA + (hd + 1) * HEAD_DIM] = (
                kh * lax.rsqrt(jnp.sum(kh * kh, axis=-1, keepdims=True) + EPS))
            qa[p, :, 2 * D_DELTA + hd * HEAD_DIM:2 * D_DELTA + (hd + 1) * HEAD_DIM] = conv_act(
                2 * D_DELTA + hd * HEAD_DIM)
            yield

        row = lax.broadcasted_iota(jnp.int32, (tt, LANES), 0)
        ba = rest[p, :, R_BA:R_BA + LANES]
        bg[p, 0] = _sigmoid(ba)
        gcum = _chunk_cumsum(-jnp.exp(alog_ref[...]) * _softplus(ba + dt_ref[...]), CHUNK, row)
        bg[p, 1] = gcum
        yield

        g_end = jnp.concatenate(
            [jnp.broadcast_to(gcum[(c + 1) * CHUNK - 1:(c + 1) * CHUNK, :], (CHUNK, LANES))
             for c in range(nchunk)], axis=0)
        for hd in range(N_DHEADS):
            lane = N_DHEADS + hd
            kcols = slice(D_DELTA + hd * HEAD_DIM, D_DELTA + (hd + 1) * HEAD_DIM)
            kd[p, :, hd * HEAD_DIM:(hd + 1) * HEAD_DIM] = (
                qa[p, :, kcols] * jnp.exp(g_end[:, lane:lane + 1] - gcum[:, lane:lane + 1]))
        yield

        sbuf[SUBLANES:SUBLANES + tt, :] = (rest[p, :, R_SC:R_SC + D_SCONV]
                                           * rest[p, :, R_SX:R_SX + D_SCONV])
        obuf[p, :, D_DELTA:D_DELTA + D_SCONV] = (
            rest[p, :, R_SB:R_SB + D_SCONV] * _branch_s(sbuf, sw_ref, SUBLANES, tt)
            * _silu(rest[p, :, R_GATE_S:R_GATE_S + D_SCONV]))
        yield

        prev_tail = jnp.where(t1 == 0, 0.0, cbuf[q, tt:tt + CTAIL, :])
        cbuf[p, 0:CTAIL, :] = prev_tail
        cbuf[p, CTAIL:CTAIL + tt, :] = (rest[p, :, R_GA:R_GA + D_CONF]
                                        * _sigmoid(rest[p, :, R_GB:R_GB + D_CONF]))
        yield
        for sh in range(1, SUBLANES):
            cshift[sh - 1] = cbuf[p, pl.ds(sh, tt + CTAIL - SUBLANES), :]
        yield

        for c in range(nchunk):
            cs = slice(c * CHUNK, (c + 1) * CHUNK)
            acc = None
            for j in range(CONF_W):
                off = CTAIL - (CONF_W - 1) + j
                start = c * CHUNK + (off // SUBLANES) * SUBLANES
                sh = off % SUBLANES
                src = (cbuf[p, start:start + CHUNK, :] if sh == 0
                       else cshift[sh - 1, start:start + CHUNK, :])
                term = src * ccw_ref[j:j + 1, :]
                acc = term if acc is None else acc + term
            yc = acc + ccb_ref[...]
            mu = jnp.mean(yc, axis=-1, keepdims=True)
            var = jnp.mean(jnp.square(yc - mu), axis=-1, keepdims=True)
            yc = (yc - mu) * lax.rsqrt(var + EPS) * clg_ref[...] + clb_ref[...]
            obuf[p, cs, D_DELTA + D_SCONV:] = _silu(yc) * _silu(rest[p, cs, R_GATE_C:R_GATE_C + D_CONF])
            yield

        qbuf[0:SUBLANES, :] = qbuf[tt:tt + SUBLANES, :]
        sbuf[0:SUBLANES, :] = sbuf[tt:tt + SUBLANES, :]

    def stage2():
        beta_all = bg[q, 0]
        gc = bg[q, 1]
        gct = gc.T
        pk = _Packed(CHUNK, tt, True)
        heads, kdecs = [], []
        for hd in range(N_DHEADS):
            lane = N_DHEADS + hd
            gcol = gc[:, lane:lane + 1]
            kh = qa[q, :, D_DELTA + hd * HEAD_DIM:D_DELTA + (hd + 1) * HEAD_DIM]
            heads.append((qa[q, :, hd * HEAD_DIM:(hd + 1) * HEAD_DIM], kh,
                          qa[q, :, 2 * D_DELTA + hd * HEAD_DIM:2 * D_DELTA + (hd + 1) * HEAD_DIM],
                          beta_all[:, hd:hd + 1], gcol, gct[lane:lane + 1, :]))
            kdecs.append(kd[q, :, hd * HEAD_DIM:(hd + 1) * HEAD_DIM])
        yield
        prepped = yield from pk.delta_prep(heads, CHUNK)

        us, lhss, kdts, qks = [], [], [], []
        low = (lax.broadcasted_iota(jnp.int32, (CHUNK, tt), 1) % LANES) < CHUNK
        for pr in range(npair):
            (u0, w0, qg0, qkp0), (u1, w1, qg1, qkp1) = prepped[2 * pr], prepped[2 * pr + 1]
            us.append(jnp.concatenate([u0, u1], axis=1))
            w = jnp.concatenate([w0, w1], axis=1).astype(BF16)
            qg = jnp.concatenate([qg0, qg1], axis=1).astype(BF16)
            lhss.append([jnp.concatenate([w[c * CHUNK:(c + 1) * CHUNK], qg[c * CHUNK:(c + 1) * CHUNK]],
                                         axis=0) for c in range(nchunk)])
            kst = jnp.concatenate([kd[c * CHUNK:(c + 1) * CHUNK] for c in range(nchunk)
                                   for kd in (kdecs[2 * pr], kdecs[2 * pr + 1])], axis=0)
            kdts.append(kst.T.astype(BF16))
            qks.append((jnp.where(low, qkp0, pltpu.roll(qkp1, CHUNK, axis=1)).astype(BF16),
                        jnp.where(low, pltpu.roll(qkp0, tt - CHUNK, axis=1), qkp1).astype(BF16)))
        yield

        ss = [s_scr[pr] for pr in range(npair)]
        for c in range(nchunk):
            cs = slice(c * CHUNK, (c + 1) * CHUNK)
            rs = [jnp.dot(lhss[pr][c], _pair_diag(s[:, :HEAD_DIM], s[:, HEAD_DIM:]).astype(BF16),
                          preferred_element_type=F32) for pr, s in enumerate(ss)]
            yield
            vns = [us[pr][cs] - r[:CHUNK] for pr, r in enumerate(rs)]
            v_bds = [_pair_diag(vn[:, :HEAD_DIM], vn[:, HEAD_DIM:]).astype(BF16) for vn in vns]
            upds = [jnp.dot(kdts[pr][:, 2 * c * CHUNK:2 * (c + 1) * CHUNK], v_bd,
                            preferred_element_type=F32) for pr, v_bd in enumerate(v_bds)]
            tile = slice((c // 2) * LANES, (c // 2 + 1) * LANES)
            o_intra = [jnp.dot(qks[pr][c % 2][:, tile], v_bd, preferred_element_type=F32)
                       for pr, v_bd in enumerate(v_bds)]
            ge = (c + 1) * CHUNK - 1
            for pr in range(npair):
                egl = jnp.concatenate(
                    [jnp.broadcast_to(jnp.exp(gc[ge:ge + 1, N_DHEADS + hh:N_DHEADS + hh + 1]),
                                      (1, HEAD_DIM)) for hh in (2 * pr, 2 * pr + 1)], axis=1)
                ss[pr] = ss[pr] * egl + upds[pr]
                o_pair = rs[pr][CHUNK:] + o_intra[pr]
                for hl in range(2):
                    hd = 2 * pr + hl
                    o = _rms_rows(o_pair[:, hl * HEAD_DIM:(hl + 1) * HEAD_DIM], dng_ref[...])
                    gate = rest[q, cs, R_GATE_D + hd * HEAD_DIM:R_GATE_D + (hd + 1) * HEAD_DIM]
                    obuf[q, cs, hd * HEAD_DIM:(hd + 1) * HEAD_DIM] = o * _silu(gate)
            yield
        for pr in range(npair):
            s_scr[pr] = ss[pr]

        y = x2_ref[0] + jnp.dot(obuf[q].astype(BF16), wout_ref[...], preferred_element_type=F32)
        if final_norm:
            y = _rms_rows(y, fg_ref[...])
        y_ref[0] = y

    _interleave(stage1(), stage2(), steps=(2, 1))

    @pl.when(jnp.logical_and(g >= 1, t2 == nt - 1))
    def _():
        for pr in range(npair):
            s_out_ref[0, 2 * pr] = s_scr[pr, :, :HEAD_DIM]
            s_out_ref[0, 2 * pr + 1] = s_scr[pr, :, HEAD_DIM:]

    @pl.when(jnp.logical_and(g < n_tiles, t1 == nt - 1))
    def _():
        qt_ref[0] = qbuf[0:SUBLANES, :]
        st_ref[0] = sbuf[0:SUBLANES, :]
        ct_ref[0] = cbuf[p, tt:tt + CTAIL, :]


def _prompt_layer_pipelined(x, wts, layer, final_norm):
    b, t, d = x.shape
    nt = t // TIME_TILE
    n_tiles = b * nt
    npair = N_DHEADS // 2

    def tile1(g):
        g1 = jnp.minimum(g, n_tiles - 1)
        return g1 // nt, g1 % nt

    def tile2(g):
        g2 = jnp.maximum(g - 1, 0)
        return g2 // nt, g2 % nt

    out_shape = (
        jax.ShapeDtypeStruct((b, t, d), F32),
        jax.ShapeDtypeStruct((b, N_DHEADS, HEAD_DIM, HEAD_DIM), F32),
        jax.ShapeDtypeStruct((b, SUBLANES, 3 * D_DELTA), F32),
        jax.ShapeDtypeStruct((b, SUBLANES, D_SCONV), F32),
        jax.ShapeDtypeStruct((b, CTAIL, D_CONF), F32),
    )
    out_specs = (
        pl.BlockSpec((1, TIME_TILE, d), lambda g: tile2(g) + (0,)),
        pl.BlockSpec((1, N_DHEADS, HEAD_DIM, HEAD_DIM), lambda g: (tile2(g)[0], 0, 0, 0)),
        pl.BlockSpec((1, SUBLANES, 3 * D_DELTA), lambda g: (tile1(g)[0], 0, 0)),
        pl.BlockSpec((1, SUBLANES, D_SCONV), lambda g: (tile1(g)[0], 0, 0)),
        pl.BlockSpec((1, CTAIL, D_CONF), lambda g: (tile1(g)[0], 0, 0)),
    )
    scratch = [
        pltpu.VMEM((TIME_TILE + SUBLANES, 3 * D_DELTA), F32),
        pltpu.VMEM((TIME_TILE + SUBLANES, D_SCONV), F32),
        pltpu.VMEM((npair, HEAD_DIM, 2 * HEAD_DIM), F32),
        pltpu.VMEM((2, TIME_TILE, 3 * D_DELTA), F32),
        pltpu.VMEM((2, 2, TIME_TILE, LANES), F32),
        pltpu.VMEM((2, TIME_TILE, R_WIDTH), F32),
        pltpu.VMEM((2, TIME_TILE + CTAIL, D_CONF), F32),
        pltpu.VMEM((SUBLANES - 1, TIME_TILE + CTAIL - SUBLANES, D_CONF), F32),
        pltpu.VMEM((2, TIME_TILE, D_MODEL), F32),
        pltpu.VMEM((2, TIME_TILE, D_DELTA), F32),
    ]
    return pl.pallas_call(
        functools.partial(_pipe_kernel, final_norm, nt),
        grid=(n_tiles + 1,),
        in_specs=[pl.BlockSpec((1, TIME_TILE, d), lambda g: tile1(g) + (0,)),
                  pl.BlockSpec((1, TIME_TILE, d), lambda g: tile2(g) + (0,))]
        + _weight_specs(wts, lambda g: layer, buffers=1),
        out_specs=out_specs,
        out_shape=out_shape,
        scratch_shapes=scratch,
        compiler_params=pltpu.CompilerParams(
            dimension_semantics=("arbitrary",),
            vmem_limit_bytes=VMEM_LIMIT),
        name="prompt_layer",
    )(x, x, *wts)


def _sample_kernel(x_ref, sd_ref, sq_ref, ss_ref, sc_ref,
                   ng_ref, win_ref, cw_ref, alog_ref, dt_ref, dng_ref,
                   sw_ref, ccw_ref, ccb_ref, clg_ref, clb_ref, wout_ref, fg_ref,
                   y_ref, sd_out_ref, qt_ref, st_ref, ct_ref,
                   qbuf, sbuf, cbuf, obuf, ubuf, wqbuf, kdbuf, glbuf, oibuf, rest, xcarry):
    layer = pl.program_id(0)
    blk = pl.program_id(1)
    nb = SAMPLE_BATCH_BLOCK
    pad = SAMPLE_PAD
    ntok = pad // 2
    rows = nb * pad

    @pl.when(layer == 0)
    def _():
        xcarry[blk, :, 0:ntok, :] = x_ref[...]
        xcarry[blk, :, ntok:, :] = jnp.zeros((nb, pad - ntok, D_MODEL), F32)

    x = xcarry[blk].reshape(rows, D_MODEL)
    h = _rms_rows(x, ng_ref[...]).astype(BF16)
    qkv = jnp.dot(h, win_ref[:, :W_REST], preferred_element_type=F32)
    rest[...] = jnp.dot(h, win_ref[:, W_REST:], preferred_element_type=F32)
    row = lax.broadcasted_iota(jnp.int32, (rows, LANES), 0)
    valid = (row % pad) < (pad // 2)
    valid1 = valid[:, 0:1]

    qkv3 = qkv.reshape(nb, pad, 3 * D_DELTA)
    qt_ref[...] = qkv3[:, ntok - (QK_CONV - 1):ntok, :]
    qbuf[:, SUBLANES - (QK_CONV - 1):SUBLANES, :] = sq_ref[...]
    qbuf[:, SUBLANES:, :] = qkv3

    hs = (rest[:, R_SC:R_SC + D_SCONV] * rest[:, R_SX:R_SX + D_SCONV]).reshape(nb, pad, D_SCONV)
    st_ref[...] = hs[:, ntok - (SCONV_W - 1):ntok, :]
    sbuf[:, SUBLANES - (SCONV_W - 1):SUBLANES, :] = ss_ref[...]
    sbuf[:, SUBLANES:, :] = hs
    ys = _branch_s(sbuf, sw_ref, SUBLANES, pad, lead=(slice(None),))
    obuf[:, D_DELTA:D_DELTA + D_SCONV] = (rest[:, R_SB:R_SB + D_SCONV] * ys.reshape(rows, D_SCONV)
                                          * _silu(rest[:, R_GATE_S:R_GATE_S + D_SCONV]))

    uc = rest[:, R_GA:R_GA + D_CONF] * _sigmoid(rest[:, R_GB:R_GB + D_CONF])
    cbuf[:, CTAIL - (CONF_W - 1):CTAIL, :] = sc_ref[...]
    cbuf[:, CTAIL:, :] = uc.reshape(nb, pad, D_CONF)
    first = CTAIL + ntok - (CONF_W - 1)
    ct_ref[...] = cbuf[:, first:first + CONF_W - 1, :]
    yc = _conformer(cbuf, ccw_ref, ccb_ref, clg_ref, clb_ref, CTAIL, pad, lead=(slice(None),))
    obuf[:, D_DELTA + D_SCONV:] = yc.reshape(rows, D_CONF) * _silu(rest[:, R_GATE_C:R_GATE_C + D_CONF])

    ba = rest[:, R_BA:R_BA + LANES]
    beta_all = jnp.where(valid, _sigmoid(ba), 0.0)
    g_all = jnp.where(valid, -jnp.exp(alog_ref[...]) * _softplus(ba + dt_ref[...]), 0.0)
    gc = _chunk_cumsum(g_all, pad, row)
    gct = gc.T
    gl_all = jnp.broadcast_to(
        gc.reshape(nb, pad, LANES)[:, pad - 1:pad, :], (nb, pad, LANES)).reshape(rows, LANES)
    pk = _Packed(pad, rows, False)

    def conv_act(c0):
        acc = None
        for j in range(QK_CONV):
            term = (qbuf[:, pl.ds(SUBLANES - (QK_CONV - 1) + j, pad), c0:c0 + HEAD_DIM]
                    * cw_ref[j:j + 1, c0:c0 + HEAD_DIM].reshape(1, 1, HEAD_DIM))
            acc = term if acc is None else acc + term
        return jnp.where(valid1, _silu(acc.reshape(rows, HEAD_DIM)), 0.0)

    for pr in range(N_DHEADS // 2):
        h0, h1 = 2 * pr, 2 * pr + 1
        heads, kdecs, egls = [], [], []
        for hd in (h0, h1):
            q = conv_act(hd * HEAD_DIM)
            k = conv_act(D_DELTA + hd * HEAD_DIM)
            v = conv_act(2 * D_DELTA + hd * HEAD_DIM)
            q = q * lax.rsqrt(jnp.sum(q * q, axis=-1, keepdims=True) + EPS) * (HEAD_DIM ** -0.5)
            k = k * lax.rsqrt(jnp.sum(k * k, axis=-1, keepdims=True) + EPS)
            lane = N_DHEADS + hd
            gcol = gc[:, lane:lane + 1]
            heads.append((q, k, v, beta_all[:, hd:hd + 1], gcol, gct[lane:lane + 1, :]))
            gl = gl_all[:, lane:lane + 1]
            kdecs.append(k * jnp.exp(gl - gcol))
            egls.append(jnp.broadcast_to(jnp.exp(gl), (rows, HEAD_DIM)))
        (u0, w0, qg0, qkp0), (u1, w1, qg1, qkp1) = _run(pk.delta_prep(heads, pad // 2))
        (kd0, kd1), (egl0, egl1) = kdecs, egls
        ubuf[...] = jnp.concatenate([u0, u1], axis=1)
        wqbuf[:, 0:pad, :] = jnp.concatenate([w0, w1], axis=1).reshape(nb, pad, 2 * HEAD_DIM)
        wqbuf[:, pad:, :] = jnp.concatenate([qg0, qg1], axis=1).reshape(nb, pad, 2 * HEAD_DIM)
        kdbuf[:, 0:pad, :] = kd0.reshape(nb, pad, HEAD_DIM)
        kdbuf[:, pad:, :] = kd1.reshape(nb, pad, HEAD_DIM)
        glbuf[...] = jnp.concatenate([egl0, egl1], axis=1).reshape(nb, pad, 2 * HEAD_DIM)

        def body(it, carry):
            bis = [it * SAMPLE_SEQ_LOCKSTEP + t for t in range(SAMPLE_SEQ_LOCKSTEP)]
            r0s = [pl.multiple_of(bi * pad, pad) for bi in bis]
            s0s = [sd_ref[bi, h0] for bi in bis]
            s1s = [sd_ref[bi, h1] for bi in bis]
            rs = [_mm(wqbuf[bi], _pair_diag(s0, s1)) for bi, s0, s1 in zip(bis, s0s, s1s)]
            vns = [ubuf[pl.ds(r0, pad), :] - r[:pad] for r0, r in zip(r0s, rs)]
            upds = [_mm_tn(kdbuf[bi], _pair_diag(vn[:, :HEAD_DIM], vn[:, HEAD_DIM:]))
                    for bi, vn in zip(bis, vns)]
            for bi, r0, s0, s1, r, vn, upd in zip(bis, r0s, s0s, s1s, rs, vns, upds):
                ubuf[pl.ds(r0, pad), :] = vn
                oibuf[pl.ds(r0, pad), :] = r[pad:]
                egl = glbuf[bi][0:1, :]
                sd_out_ref[bi, h0] = s0 * egl[:, :HEAD_DIM] + upd[:, :HEAD_DIM]
                sd_out_ref[bi, h1] = s1 * egl[:, HEAD_DIM:] + upd[:, HEAD_DIM:]
            return carry

        lax.fori_loop(0, nb // SAMPLE_SEQ_LOCKSTEP, body, 0)
        for hh, qkp in ((h0, qkp0), (h1, qkp1)):
            ls = slice((hh - h0) * HEAD_DIM, (hh - h0 + 1) * HEAD_DIM)
            o = oibuf[:, ls] + _mm(pk.block_diag(qkp), ubuf[:, ls])
            o = _rms_rows(o, dng_ref[...])
            obuf[:, hh * HEAD_DIM:(hh + 1) * HEAD_DIM] = (
                o * _silu(rest[:, R_GATE_D + hh * HEAD_DIM:R_GATE_D + (hh + 1) * HEAD_DIM]))

    y = x + jnp.dot(obuf[...].astype(BF16), wout_ref[...], preferred_element_type=F32)
    xcarry[blk] = y.reshape(nb, pad, D_MODEL)
    is_last = layer == pl.num_programs(0) - 1
    y_ref[...] = jnp.where(is_last, _rms_rows(y, fg_ref[...]), y).reshape(nb, pad, D_MODEL)[:, 0:ntok, :]


def _sample_layers(x, sd, sq, ss, sc, wts):
    depth = sd.shape[0]
    b, ntok, _ = x.shape
    nb = SAMPLE_BATCH_BLOCK

    def bspec(shape):
        return pl.BlockSpec((nb,) + shape, lambda l, bi: (bi,) + (0,) * len(shape))

    def lspec(shape):
        return pl.BlockSpec((None, nb) + shape, lambda l, bi: (l, bi) + (0,) * len(shape))

    def lshape(shape):
        return jax.ShapeDtypeStruct((depth, b) + shape, F32)

    state_shapes = ((N_DHEADS, HEAD_DIM, HEAD_DIM), (QK_CONV - 1, 3 * D_DELTA),
                    (SCONV_W - 1, D_SCONV), (CONF_W - 1, D_CONF))
    out_shape = (lshape((ntok, D_MODEL)),) + tuple(lshape(s) for s in state_shapes)
    out_specs = (lspec((ntok, D_MODEL)),) + tuple(lspec(s) for s in state_shapes)
    rows = nb * SAMPLE_PAD
    scratch = [
        pltpu.VMEM((nb, 2 * SUBLANES, 3 * D_DELTA), F32),
        pltpu.VMEM((nb, 2 * SUBLANES, D_SCONV), F32),
        pltpu.VMEM((nb, CTAIL + SAMPLE_PAD, D_CONF), F32),
        pltpu.VMEM((rows, D_MODEL), F32),
        pltpu.VMEM((rows, 2 * HEAD_DIM), F32),
        pltpu.VMEM((nb, 2 * SAMPLE_PAD, 2 * HEAD_DIM), F32),
        pltpu.VMEM((nb, 2 * SAMPLE_PAD, HEAD_DIM), F32),
        pltpu.VMEM((nb, SAMPLE_PAD, 2 * HEAD_DIM), F32),
        pltpu.VMEM((rows, 2 * HEAD_DIM), F32),
        pltpu.VMEM((rows, R_WIDTH), F32),
        pltpu.VMEM((b // nb, nb, SAMPLE_PAD, D_MODEL), F32),
    ]
    in_specs = ([bspec((ntok, D_MODEL))] + [lspec(s) for s in state_shapes]
                + _weight_specs(wts, lambda l, bi: l))
    return pl.pallas_call(
        _sample_kernel,
        grid=(depth, b // nb),
        in_specs=in_specs,
        out_specs=out_specs,
        out_shape=out_shape,
        scratch_shapes=scratch,
        compiler_params=pltpu.CompilerParams(
            dimension_semantics=("arbitrary", "arbitrary"),
            vmem_limit_bytes=VMEM_LIMIT),
        name="sample_layers",
    )(x, sd, sq, ss, sc, *wts)


def _reorder_kernel(w_ref, o_ref):
    n_qkv = 3 * D_DELTA
    n_ba = 2 * N_DHEADS
    rows = w_ref.shape[0]
    o_ref[:, 0:n_qkv] = w_ref[:, 0:n_qkv].astype(BF16)
    o_ref[:, n_qkv:n_qkv + R_BA] = w_ref[:, n_qkv + n_ba:].astype(BF16)
    o_ref[:, n_qkv + R_BA:] = jnp.concatenate(
        [w_ref[:, n_qkv:n_qkv + n_ba], jnp.zeros((rows, LANES - n_ba), F32)], axis=1).astype(BF16)


def _reorder_in_proj(w_in):
    depth, d, d_in = w_in.shape
    rows = REORDER_ROW_BLOCK
    return pl.pallas_call(
        _reorder_kernel,
        grid=(depth, d // rows),
        in_specs=[pl.BlockSpec((None, rows, d_in), lambda l, r: (l, r, 0))],
        out_specs=pl.BlockSpec((None, rows, W_REST + R_WIDTH), lambda l, r: (l, r, 0)),
        out_shape=jax.ShapeDtypeStruct((depth, d, W_REST + R_WIDTH), BF16),
        compiler_params=pltpu.CompilerParams(dimension_semantics=("arbitrary", "arbitrary")),
        name="reorder_in_proj",
    )(w_in)


def _stacked_weights(norm_g, w_in, conv_qkv_w, a_log, dt_bias, delta_norm_g, sconv_w,
                     cconv_w, cconv_b, cln_g, cln_b, w_out, final_norm_g):
    depth = w_in.shape[0]
    w_all = _reorder_in_proj(w_in)
    lane_pad = ((0, 0), (N_DHEADS, LANES - 2 * N_DHEADS))
    return (
        norm_g.reshape(depth, 1, D_MODEL),
        w_all,
        conv_qkv_w,
        jnp.pad(a_log, lane_pad).reshape(depth, 1, LANES),
        jnp.pad(dt_bias, lane_pad).reshape(depth, 1, LANES),
        delta_norm_g.reshape(depth, 1, HEAD_DIM),
        sconv_w,
        cconv_w,
        cconv_b.reshape(depth, 1, D_CONF),
        cln_g.reshape(depth, 1, D_CONF),
        cln_b.reshape(depth, 1, D_CONF),
        w_out.astype(BF16),
        final_norm_g.reshape(1, D_MODEL),
    )


def kernel(x_prompt, x_sample, state_delta, state_qkv_conv, state_sconv, state_cconv, norm_g, w_in, conv_qkv_w, a_log, dt_bias, delta_norm_g, sconv_w, cconv_w, cconv_b, cln_g, cln_b, w_out, final_norm_g):
    depth = w_in.shape[0]
    dec_seq = x_sample.shape[1]
    assert x_prompt.shape[1] % TIME_TILE == 0
    assert dec_seq == SAMPLE_PAD // 2 and x_sample.shape[0] % SAMPLE_BATCH_BLOCK == 0

    wts = _stacked_weights(norm_g, w_in, conv_qkv_w, a_log, dt_bias, delta_norm_g, sconv_w,
                           cconv_w, cconv_b, cln_g, cln_b, w_out, final_norm_g)

    xp = x_prompt
    p_outs = [[] for _ in range(4)]
    for l in range(depth):
        xp, pd, pq, ps, pc = _prompt_layer_pipelined(xp, wts, l, l == depth - 1)
        for acc, o in zip(p_outs, (pd, pq, ps, pc)):
            acc.append(o)
    pd, pq, ps, pc = (jnp.stack(o) for o in p_outs)

    xs, sd, sq, ss, sc = _sample_layers(x_sample, state_delta, state_qkv_conv, state_sconv,
                                        state_cconv, wts)
    return (xp, xs[depth - 1],
            pd, pq[:, :, SUBLANES - (QK_CONV - 1):, :], ps[:, :, SUBLANES - (SCONV_W - 1):, :],
            pc[:, :, CTAIL - (CONF_W - 1):, :],
            sd, sq, ss, sc)
```
